```python
import math
import jax, jax.numpy as jnp
from jax import lax
import numpy as np

D_MODEL = 1024
BATCH = 4
SEQ = 8192
DEPTH = 2

PLE_DIM = 256
NORM_EPS = 1e-6

MLA_HEADS = 8
MLA_Q_LORA = 384
MLA_KV_LORA = 256
MLA_NOPE = 64
MLA_ROPE = 32
MLA_V = 64
ROPE_THETA = 10000.0
Q_BLOCK = 128

DIL_GROUPS = ((128, 1), (512, 4), (2048, 16))
DIL_HEADS = 4
DIL_HEAD_DIM = 64

N_EXPERTS = 32
TOP_K = 4
D_EXPERT = 1024
SWIGLU_LIMIT = 7.0
SWIGLU_ALPHA = 1.702
EXPERT_ROW_BLOCK = 256

N_GATES = 2
COLS_MLA = MLA_Q_LORA + MLA_KV_LORA + MLA_ROPE
COLS_DIL = len(DIL_GROUPS) * 3 * DIL_HEADS * DIL_HEAD_DIM
COLS_GATE = N_GATES * D_MODEL
IN_COLS = COLS_MLA + COLS_DIL + COLS_GATE

kernel_name = "hybrid_mla_dilated_moe_ple"


def rmsnorm(x, g):
    xf = x.astype(jnp.float32)
    y = xf * lax.rsqrt(jnp.mean(xf * xf, axis=-1, keepdims=True) + NORM_EPS)
    return (y * g.astype(jnp.float32)).astype(x.dtype)


def rope(t):
    S, R = t.shape[1], t.shape[-1]
    half = R // 2
    inv_freq = ROPE_THETA ** (-jnp.arange(half, dtype=jnp.float32) * 2.0 / R)
    ang = jnp.arange(S, dtype=jnp.float32)[:, None] * inv_freq[None, :]
    cos = jnp.cos(ang)[None, :, None, :]
    sin = jnp.sin(ang)[None, :, None, :]
    tf = t.astype(jnp.float32)
    t1, t2 = tf[..., :half], tf[..., half:]
    return jnp.concatenate([t1 * cos - t2 * sin, t1 * sin + t2 * cos], axis=-1).astype(t.dtype)


def alibi_slopes(n):
    def pow2(m):
        start = 2.0 ** (-8.0 / m)
        return [start ** (i + 1) for i in range(m)]
    if math.log2(n).is_integer():
        s = pow2(n)
    else:
        c = 2 ** int(math.floor(math.log2(n)))
        s = pow2(c) + pow2(2 * c)[0::2][: n - c]
    return np.array(sorted(s, reverse=True), dtype=np.float32)


def causal_attention(q, k, v):
    Bb, S, H, dqk = q.shape
    scale = dqk ** -0.5
    outs = []
    for start in range(0, S, Q_BLOCK):
        end = min(start + Q_BLOCK, S)
        s = jnp.einsum('bqhd,bkhd->bhqk', q[:, start:end], k[:, :end]).astype(jnp.float32) * scale
        mask = (start + jnp.arange(end - start))[:, None] >= jnp.arange(end)[None, :]
        s = jnp.where(mask, s, -jnp.inf)
        pr = jax.nn.softmax(s, axis=-1).astype(v.dtype)
        outs.append(jnp.einsum('bhqk,bkhd->bqhd', pr, v[:, :end]))
    return jnp.concatenate(outs, axis=1)


def banded_causal_attention(q, k, v, step_slopes, steps):
    G, N, H, dh = q.shape
    nb = N // steps
    qb = q.reshape(G, nb, steps, H, dh)

    def band(t):
        tb = t.reshape(G, nb, steps, H, dh)
        prev = jnp.pad(tb[:, :-1], ((0, 0), (1, 0), (0, 0), (0, 0), (0, 0)))
        return jnp.concatenate([prev, tb], axis=2)

    s = jnp.einsum('gnqhd,gnkhd->gnhqk', qb, band(k)).astype(jnp.float32) * (dh ** -0.5)
    qi = jnp.arange(steps)[:, None]
    ki = jnp.arange(2 * steps)[None, :]
    dist = qi + steps - ki
    key_pos = (jnp.arange(nb)[:, None, None] - 1) * steps + ki[None]
    valid = (dist >= 0) & (dist <= steps) & (key_pos >= 0)
    s = s - step_slopes[:, None, None] * dist.astype(jnp.float32)
    s = jnp.where(valid[None, :, None], s, -jnp.inf)
    lse = jax.nn.logsumexp(s, axis=-1)
    pr = jnp.exp(s - lse[..., None]).astype(v.dtype)
    o = jnp.einsum('gnhqk,gnkhd->gnqhd', pr, band(v))
    return o.reshape(G, N, H, dh), lse.transpose(0, 1, 3, 2).reshape(G, N, H)


def dilated_attention(q, k, v, slopes, window, dil):
    steps = window // dil
    Bb, S, H, dh = q.shape
    unit = dil * steps
    Sp = -(-S // unit) * unit

    def to_residue(t):
        t = jnp.pad(t, ((0, 0), (0, Sp - S), (0, 0), (0, 0)))
        t = t.reshape(Bb, Sp // dil, dil, H, dh).transpose(0, 2, 1, 3, 4)
        return t.reshape(Bb * dil, Sp // dil, H, dh)

    step_slopes = jnp.asarray(slopes * dil)
    o, lse = banded_causal_attention(to_residue(q), to_residue(k), to_residue(v), step_slopes, steps)
    o = o.reshape(Bb, dil, Sp // dil, H, dh).transpose(0, 2, 1, 3, 4).reshape(Bb, Sp, H, dh)[:, :S]
    lse = lse.reshape(Bb, dil, Sp // dil, H).transpose(0, 2, 1, 3).reshape(Bb, Sp, H)[:, :S]
    return o, lse


def token_mixer(h, w_in, q_norm, w_uq, kv_norm, w_ukv, w_branch_a, w_branch_b, w_out):
    Bb, S, _ = h.shape
    z = h @ w_in
    splits = np.cumsum([MLA_Q_LORA, MLA_KV_LORA, MLA_ROPE, COLS_DIL]).tolist()
    c_q, c_kv, k_r, z_dil, z_gate = jnp.split(z, splits, axis=-1)

    q = (rmsnorm(c_q, q_norm) @ w_uq).reshape(Bb, S, MLA_HEADS, MLA_NOPE + MLA_ROPE)
    kv = (rmsnorm(c_kv, kv_norm) @ w_ukv).reshape(Bb, S, MLA_HEADS, MLA_NOPE + MLA_V)
    k_rope = jnp.broadcast_to(rope(k_r[:, :, None, :]), (Bb, S, MLA_HEADS, MLA_ROPE))
    q = jnp.concatenate([q[..., :MLA_NOPE], rope(q[..., MLA_NOPE:])], axis=-1)
    k = jnp.concatenate([kv[..., :MLA_NOPE], k_rope], axis=-1)
    v = kv[..., MLA_NOPE:]
    o_a = causal_attention(q, k, v).reshape(Bb, S, MLA_HEADS * MLA_V) @ w_branch_a

    n_groups = len(DIL_GROUPS)
    zd = z_dil.reshape(Bb, S, n_groups, 3, DIL_HEADS, DIL_HEAD_DIM)
    slopes = alibi_slopes(n_groups * DIL_HEADS).reshape(n_groups, DIL_HEADS)
    outs, lses = [], []
    for gi, (window, dil) in enumerate(DIL_GROUPS):
        o_g, lse_g = dilated_attention(zd[:, :, gi, 0], zd[:, :, gi, 1], zd[:, :, gi, 2], slopes[gi], window, dil)
        outs.append(o_g)
        lses.append(lse_g)
    wts = jax.nn.softmax(jnp.stack(lses), axis=0)
    o_b = jnp.sum(wts[..., None] * jnp.stack(outs).astype(jnp.float32), axis=0).astype(h.dtype)
    o_b = o_b.reshape(Bb, S, DIL_HEADS * DIL_HEAD_DIM) @ w_branch_b

    gates = jax.nn.sigmoid(z_gate.reshape(Bb, S, N_GATES, D_MODEL))
    mixed = gates[:, :, 0] * o_a + gates[:, :, 1] * o_b
    return mixed @ w_out


def moe_ffn(h, w_router, b_router, w_gate, b_gate, w_up, b_up, w_down, b_down):
    Bb, S, D = h.shape
    T = Bb * S
    hf = h.reshape(T, D)
    logits = (hf @ w_router + b_router).astype(jnp.float32)
    top_val, top_idx = lax.top_k(logits, TOP_K)
    top_w = jax.nn.softmax(top_val, axis=-1)

    n_assign = T * TOP_K
    flat_e = top_idx.reshape(n_assign)
    flat_tok = jnp.repeat(jnp.arange(T, dtype=jnp.int32), TOP_K)
    flat_w = top_w.reshape(n_assign)
    order = jnp.argsort(flat_e)
    sorted_e = flat_e[order]
    counts = jnp.bincount(flat_e, length=N_EXPERTS)
    starts = jnp.cumsum(counts) - counts
    padded = (counts + EXPERT_ROW_BLOCK - 1) // EXPERT_ROW_BLOCK * EXPERT_ROW_BLOCK
    padded_ends = jnp.cumsum(padded)
    dest = padded_ends[sorted_e] - padded[sorted_e] + jnp.arange(n_assign) - starts[sorted_e]
    n_blocks = -(-(n_assign + N_EXPERTS * (EXPERT_ROW_BLOCK - 1)) // EXPERT_ROW_BLOCK)
    n_rows = n_blocks * EXPERT_ROW_BLOCK
    row_tok = jnp.zeros((n_rows,), jnp.int32).at[dest].set(flat_tok[order])
    row_w = jnp.zeros((n_rows,), jnp.float32).at[dest].set(flat_w[order])
    block_e = jnp.minimum(
        jnp.searchsorted(padded_ends, jnp.arange(n_blocks) * EXPERT_ROW_BLOCK, side='right'),
        N_EXPERTS - 1)

    def expert_block(args):
        tok, wt, e = args
        xb = hf[tok]
        a = jnp.minimum(xb @ w_gate[e] + b_gate[e], SWIGLU_LIMIT)
        u = jnp.clip(xb @ w_up[e] + b_up[e], -SWIGLU_LIMIT, SWIGLU_LIMIT)
        y = (a * jax.nn.sigmoid(SWIGLU_ALPHA * a)) * (u + 1.0)
        y = y @ w_down[e] + b_down[e]
        return y * wt[:, None].astype(y.dtype)

    ys = lax.map(expert_block, (row_tok.reshape(n_blocks, EXPERT_ROW_BLOCK),
                                row_w.reshape(n_blocks, EXPERT_ROW_BLOCK), block_e))
    out = jax.ops.segment_sum(ys.reshape(n_rows, D), row_tok, num_segments=T)
    return out.reshape(Bb, S, D)


def setup_inputs(seed: int = 0) -> dict:
    key = jax.random.key(seed)
    ks = iter(jax.random.split(key, 32))
    L = DEPTH

    def nrm(shape, fan_in):
        return jax.random.normal(next(ks), shape, jnp.float32) * (fan_in ** -0.5)

    def gain(shape):
        return 1.0 + 0.05 * jax.random.normal(next(ks), shape, jnp.float32)

    def bias(shape, scale):
        return scale * jax.random.normal(next(ks), shape, jnp.float32)

    return {
        "x": jax.random.normal(next(ks), (BATCH, SEQ, D_MODEL), jnp.float32),
        "p": jax.random.normal(next(ks), (DEPTH, BATCH, SEQ, PLE_DIM), jnp.float32),
        "attn_norm": gain((L, D_MODEL)),
        "w_in": nrm((L, D_MODEL, IN_COLS), D_MODEL),
        "q_norm": gain((L, MLA_Q_LORA)),
        "w_uq": nrm((L, MLA_Q_LORA, MLA_HEADS * (MLA_NOPE + MLA_ROPE)), MLA_Q_LORA),
        "kv_norm": gain((L, MLA_KV_LORA)),
        "w_ukv": nrm((L, MLA_KV_LORA, MLA_HEADS * (MLA_NOPE + MLA_V)), MLA_KV_LORA),
        "w_branch_a": nrm((L, MLA_HEADS * MLA_V, D_MODEL), MLA_HEADS * MLA_V),
        "w_branch_b": nrm((L, DIL_HEADS * DIL_HEAD_DIM, D_MODEL), DIL_HEADS * DIL_HEAD_DIM),
        "w_out": nrm((L, D_MODEL, D_MODEL), D_MODEL),
        "ffn_norm": gain((L, D_MODEL)),
        "w_router": nrm((L, D_MODEL, N_EXPERTS), D_MODEL),
        "b_router": bias((L, N_EXPERTS), 0.01),
        "w_gate": nrm((L, N_EXPERTS, D_MODEL, D_EXPERT), D_MODEL),
        "b_gate": bias((L, N_EXPERTS, D_EXPERT), 0.02),
        "w_up": nrm((L, N_EXPERTS, D_MODEL, D_EXPERT), D_MODEL),
        "b_up": bias((L, N_EXPERTS, D_EXPERT), 0.02),
        "w_down": nrm((L, N_EXPERTS, D_EXPERT, D_MODEL), D_EXPERT),
        "b_down": bias((L, N_EXPERTS, D_MODEL), 0.02),
        "ple_norm": gain((L, D_MODEL)),
        "w_ple_gate": nrm((L, D_MODEL, D_MODEL), D_MODEL),
        "w_ple_proj": nrm((L, PLE_DIM, D_MODEL), PLE_DIM),
        "final_norm": gain((D_MODEL,)),
    }


def reference(x, p, attn_norm, w_in, q_norm, w_uq, kv_norm, w_ukv, w_branch_a, w_branch_b, w_out,
              ffn_norm, w_router, b_router, w_gate, b_gate, w_up, b_up, w_down, b_down,
              ple_norm, w_ple_gate, w_ple_proj, final_norm):
    for i in range(DEPTH):
        h = rmsnorm(x, attn_norm[i])
        x = x + token_mixer(h, w_in[i], q_norm[i], w_uq[i], kv_norm[i], w_ukv[i],
                            w_branch_a[i], w_branch_b[i], w_out[i])
        h = rmsnorm(x, ffn_norm[i])
        x = x + moe_ffn(h, w_router[i], b_router[i], w_gate[i], b_gate[i], w_up[i], b_up[i],
                        w_down[i], b_down[i])
        ple_gate = jax.nn.sigmoid(rmsnorm(x, ple_norm[i]) @ w_ple_gate[i])
        x = x + ple_gate * (p[i] @ w_ple_proj[i])
    return rmsnorm(x, final_norm)
```

```python
import functools
import math

import jax
import jax.numpy as jnp
import numpy as np
from jax import lax
from jax.experimental import pallas as pl
from jax.experimental.pallas import tpu as pltpu
from jax.experimental.pallas import tpu_sc as plsc

F32 = jnp.float32
BF16 = jnp.bfloat16
I32 = jnp.int32

D_MODEL = 1024
PLE_DIM = 256
NORM_EPS = 1e-6

MLA_HEADS = 8
MLA_Q_LORA = 384
MLA_KV_LORA = 256
MLA_NOPE = 64
MLA_ROPE = 32
MLA_V = 64
ROPE_THETA = 10000.0
HEAD_PAD = 128
HALF_ROPE = MLA_ROPE // 2

DIL_GROUPS = ((128, 1), (512, 4), (2048, 16))
DIL_HEADS = 4
DIL_HEAD_DIM = 64
DIL_STEPS = 128
DIL_COLS = 3 * DIL_HEADS * DIL_HEAD_DIM
DIL_OUT = DIL_HEADS * DIL_HEAD_DIM

N_EXPERTS = 32
TOP_K = 4
D_EXPERT = 1024
SWIGLU_LIMIT = 7.0
SWIGLU_ALPHA = 1.702
ROW_BLOCK = 256

TOKEN_TILE = 512
ATTN_TILE = 512
HALF = D_MODEL // 2
NEG = -1e30
HI_MASK = -65536

SC_CORES = 2
SC_SUBCORES = 16
SC_WORKERS = SC_CORES * SC_SUBCORES
SC_WINDOW = 64

VMEM_LIMIT = 56 * 1024 * 1024


def _cparams(sem):
    return pltpu.CompilerParams(dimension_semantics=sem, vmem_limit_bytes=VMEM_LIMIT)


def _rms(x, g):
    return x * lax.rsqrt(jnp.mean(x * x, axis=-1, keepdims=True) + NORM_EPS) * g


def _dot(a, b):
    return jnp.dot(a, b, preferred_element_type=F32)


def _dot_nt(a, b):
    return lax.dot_general(a, b, (((1,), (1,)), ((), ())), preferred_element_type=F32)


def _pack_halves(lo, hi):
    lo_i = lax.bitcast_convert_type(lo.astype(BF16).astype(F32), I32)
    hi_i = lax.bitcast_convert_type(hi.astype(BF16).astype(F32), I32)
    return (hi_i & HI_MASK) | lax.shift_right_logical(lo_i, 16)


def _unpack_halves(w):
    lo = lax.bitcast_convert_type(lax.shift_left(w, 16), F32)
    hi = lax.bitcast_convert_type(w & HI_MASK, F32)
    return lo, hi


def _inproj_kernel(x_ref, g_ref, wmla_ref, wdil_ref, wgate_ref, qn_ref, kvn_ref, wuq_ref, wuk_ref,
                   wuv_ref, cos_ref, sina_ref, sinb_ref,
                   q_ref, k_ref, v_ref, zd0_ref, zd1_ref, zd2_ref, gate_ref):
    h = _rms(x_ref[...], g_ref[...]).astype(BF16)
    zm = _dot(h, wmla_ref[...])
    cq = _rms(zm[:, :MLA_Q_LORA], qn_ref[...]).astype(BF16)
    ckv = _rms(zm[:, MLA_Q_LORA:MLA_Q_LORA + MLA_KV_LORA], kvn_ref[...]).astype(BF16)
    kr = zm[:, MLA_Q_LORA + MLA_KV_LORA:]
    cos, sina, sinb = cos_ref[...], sina_ref[...], sinb_ref[...]

    def rope(t):
        return (t * cos + pltpu.roll(t, HALF_ROPE, 1) * sina
                + pltpu.roll(t, HEAD_PAD - HALF_ROPE, 1) * sinb)

    kr_rot = rope(kr)
    qraw = _dot(cq, wuq_ref[...])
    kraw = _dot(ckv, wuk_ref[...])
    v_ref[...] = _dot(ckv, wuv_ref[...]).astype(BF16)
    scale = (MLA_NOPE + MLA_ROPE) ** -0.5
    for hd in range(MLA_HEADS):
        sl = slice(hd * HEAD_PAD, (hd + 1) * HEAD_PAD)
        q_ref[:, sl] = (rope(qraw[:, sl]) * scale).astype(BF16)
        k_ref[:, sl] = (kraw[:, sl] + kr_rot).astype(BF16)
    for gi, zd_ref in enumerate((zd0_ref, zd1_ref, zd2_ref)):
        zd_ref[...] = _dot(h, wdil_ref[:, gi * DIL_COLS:(gi + 1) * DIL_COLS]).astype(BF16)
    for c in range(2):
        sl = slice(c * D_MODEL, (c + 1) * D_MODEL)
        gate_ref[:, sl] = jax.nn.sigmoid(_dot(h, wgate_ref[:, sl])).astype(BF16)


def _inproj(x2d, seq, g, wmla, wdil, wgate, qn, kvn, wuq, wuk, wuv, cos_t, sina_t, sinb_t):
    t = x2d.shape[0]
    tm = TOKEN_TILE
    n_seq_tiles = seq // tm

    def row(i):
        return (i, 0)

    def const(i):
        return (0, 0)

    def pos(i):
        return (i % n_seq_tiles, 0)

    def full(a):
        return pl.BlockSpec(a.shape, const)

    out_shape = [
        jax.ShapeDtypeStruct((t, MLA_HEADS * HEAD_PAD), BF16),
        jax.ShapeDtypeStruct((t, MLA_HEADS * HEAD_PAD), BF16),
        jax.ShapeDtypeStruct((t, MLA_HEADS * MLA_V), BF16),
        jax.ShapeDtypeStruct((t, DIL_COLS), BF16),
        jax.ShapeDtypeStruct((t, DIL_COLS), BF16),
        jax.ShapeDtypeStruct((t, DIL_COLS), BF16),
        jax.ShapeDtypeStruct((t, 2 * D_MODEL), BF16),
    ]
    return pl.pallas_call(
        _inproj_kernel,
        out_shape=out_shape,
        grid=(t // tm,),
        in_specs=[pl.BlockSpec((tm, D_MODEL), row), full(g), full(wmla), full(wdil), full(wgate),
                  full(qn), full(kvn), full(wuq), full(wuk), full(wuv),
                  pl.BlockSpec((tm, HEAD_PAD), pos), pl.BlockSpec((tm, HEAD_PAD), pos),
                  pl.BlockSpec((tm, HEAD_PAD), pos)],
        out_specs=[pl.BlockSpec((tm, s.shape[1]), row) for s in out_shape],
        compiler_params=_cparams(("parallel",)),
        name="inproj",
    )(x2d, g, wmla, wdil, wgate, qn, kvn, wuq, wuk, wuv, cos_t, sina_t, sinb_t)


def _mla_kernel(qi_ref, kj_ref, q_ref, k_ref, v_ref, o_ref, m_sc, l_sc, acc_sc):
    p = pl.program_id(2)
    i = qi_ref[p]
    j = kj_ref[p]
    tq = q_ref.shape[0]
    tk = k_ref.shape[0]

    @pl.when(j == 0)
    def _():
        m_sc[...] = jnp.full(m_sc.shape, NEG, F32)
        l_sc[...] = jnp.zeros(l_sc.shape, F32)
        acc_sc[...] = jnp.zeros(acc_sc.shape, F32)

    def step(diagonal):
        v = v_ref[...]
        for hh in range(2):
            sl = slice(hh * HEAD_PAD, (hh + 1) * HEAD_PAD)
            s = _dot_nt(q_ref[:, sl], k_ref[:, sl])
            if diagonal:
                r = lax.broadcasted_iota(I32, (tq, tk), 0)
                c = lax.broadcasted_iota(I32, (tq, tk), 1)
                s = jnp.where(r >= c, s, NEG)
            m_prev = m_sc[hh]
            m_new = jnp.maximum(m_prev, jnp.max(s, axis=1, keepdims=True))
            alpha = jnp.exp(m_prev - m_new)
            pe = jnp.exp(s - m_new)
            l_sc[hh] = alpha * l_sc[hh] + jnp.sum(pe, axis=1, keepdims=True)
            acc_sc[hh] = alpha * acc_sc[hh] + _dot(pe.astype(BF16), v)
            m_sc[hh] = m_new

    @pl.when(j < i)
    def _():
        step(False)

    @pl.when(j == i)
    def _():
        step(True)
        lane = lax.broadcasted_iota(I32, (tq, 2 * MLA_V), 1)
        o = jnp.where(lane < MLA_V, acc_sc[0] / l_sc[0], acc_sc[1] / l_sc[1])
        o_ref[...] = o.astype(BF16)


def _mla_attention(q, k, v):
    b, s, _ = q.shape
    tq = ATTN_TILE
    nq = s // tq
    pairs = [(i, j) for i in range(nq) for j in range(i + 1)]
    qi = jnp.asarray([p[0] for p in pairs], I32)
    kj = jnp.asarray([p[1] for p in pairs], I32)
    grid_spec = pltpu.PrefetchScalarGridSpec(
        num_scalar_prefetch=2,
        grid=(b, MLA_HEADS // 2, len(pairs)),
        in_specs=[
            pl.BlockSpec((None, tq, 2 * HEAD_PAD), lambda bb, hp, p, qi, kj: (bb, qi[p], hp)),
            pl.BlockSpec((None, tq, 2 * HEAD_PAD), lambda bb, hp, p, qi, kj: (bb, kj[p], hp)),
            pl.BlockSpec((None, tq, 2 * MLA_V), lambda bb, hp, p, qi, kj: (bb, kj[p], hp)),
        ],
        out_specs=pl.BlockSpec((None, tq, 2 * MLA_V), lambda bb, hp, p, qi, kj: (bb, qi[p], hp)),
        scratch_shapes=[pltpu.VMEM((2, tq, 1), F32), pltpu.VMEM((2, tq, 1), F32),
                        pltpu.VMEM((2, tq, 2 * MLA_V), F32)],
    )
    return pl.pallas_call(
        _mla_kernel,
        out_shape=jax.ShapeDtypeStruct((b, s, MLA_HEADS * MLA_V), BF16),
        grid_spec=grid_spec,
        compiler_params=_cparams(("parallel", "parallel", "arbitrary")),
        name="mla_attention",
    )(qi, kj, q, k, v)


def _alibi_slopes(n):
    def pow2(m):
        start = 2.0 ** (-8.0 / m)
        return [start ** (i + 1) for i in range(m)]
    if math.log2(n).is_integer():
        s = pow2(n)
    else:
        c = 2 ** int(math.floor(math.log2(n)))
        s = pow2(c) + pow2(2 * c)[0::2][: n - c]
    return np.array(sorted(s, reverse=True), dtype=np.float32)


def _dilated_kernel(cur_ref, prev_ref, o_ref, lse_ref, *, step_slopes):
    mi = pl.program_id(2)
    n = DIL_STEPS
    hw = DIL_HEADS * DIL_HEAD_DIM
    cur = cur_ref[...]
    prev = prev_ref[...]
    qi = lax.broadcasted_iota(I32, (n, 2 * n), 0)
    ki = lax.broadcasted_iota(I32, (n, 2 * n), 1)
    dist = qi + n - ki
    valid = (dist >= 0) & (dist <= n) & ((ki >= n) | (mi > 0))
    distf = dist.astype(F32)
    lane = lax.broadcasted_iota(I32, (n, 2 * DIL_HEAD_DIM), 1)
    low = lane < DIL_HEAD_DIM
    scale = DIL_HEAD_DIM ** -0.5
    for pair in range(DIL_HEADS // 2):
        sl = slice(pair * 2 * DIL_HEAD_DIM, (pair + 1) * 2 * DIL_HEAD_DIM)
        q = cur[:, sl]
        kk = jnp.concatenate([prev[:, hw:2 * hw][:, sl], cur[:, hw:2 * hw][:, sl]], axis=0)
        vv = jnp.concatenate([prev[:, 2 * hw:][:, sl], cur[:, 2 * hw:][:, sl]], axis=0)
        outs, lses = [], []
        for hh in range(2):
            keep = low if hh == 0 else jnp.logical_not(low)
            qm = jnp.where(keep, q, jnp.zeros_like(q))
            s = _dot_nt(qm, kk) * scale - step_slopes[pair * 2 + hh] * distf
            s = jnp.where(valid, s, NEG)
            m = jnp.max(s, axis=1, keepdims=True)
            pe = jnp.exp(s - m)
            l = jnp.sum(pe, axis=1, keepdims=True)
            outs.append(_dot(pe.astype(BF16), vv) / l)
            lses.append(m + jnp.log(l))
        o_ref[:, sl] = jnp.where(low, outs[0], outs[1])
        lse_ref[:, sl] = jnp.where(low, lses[0], lses[1])


def _dilated_attention(zd, gi):
    window, dil = DIL_GROUPS[gi]
    b, s, _ = zd.shape
    n_sub = s // dil
    nb = n_sub // DIL_STEPS
    slopes = _alibi_slopes(len(DIL_GROUPS) * DIL_HEADS).reshape(len(DIL_GROUPS), DIL_HEADS)[gi] * dil
    view = zd.reshape(b, n_sub, dil * DIL_COLS)
    o, lse = pl.pallas_call(
        functools.partial(_dilated_kernel, step_slopes=tuple(float(x) for x in slopes)),
        out_shape=[jax.ShapeDtypeStruct((b, n_sub, dil * DIL_OUT), F32)] * 2,
        grid=(b, dil, nb),
        in_specs=[pl.BlockSpec((None, DIL_STEPS, DIL_COLS), lambda bb, r, m: (bb, m, r)),
                  pl.BlockSpec((None, DIL_STEPS, DIL_COLS),
                               lambda bb, r, m: (bb, jnp.maximum(m - 1, 0), r))],
        out_specs=[pl.BlockSpec((None, DIL_STEPS, DIL_OUT), lambda bb, r, m: (bb, m, r))] * 2,
        compiler_params=_cparams(("parallel", "parallel", "arbitrary")),
        name=f"dilated_attention_{gi}",
    )(view, view)
    return o.reshape(b * s, DIL_OUT), lse.reshape(b * s, DIL_OUT)


def _merge_kernel(x_ref, oa_ref, o0_ref, o1_ref, o2_ref, l0_ref, l1_ref, l2_ref, gate_ref,
                  wa_ref, wb_ref, wo_ref, g_ref, wr_ref, br_ref,
                  x1_ref, hp_ref, topi_ref, topw_ref, wcol_ref):
    tm = x_ref.shape[0]
    l0, l1, l2 = l0_ref[...], l1_ref[...], l2_ref[...]
    lmax = jnp.maximum(jnp.maximum(l0, l1), l2)
    e0, e1, e2 = jnp.exp(l0 - lmax), jnp.exp(l1 - lmax), jnp.exp(l2 - lmax)
    ob = (e0 * o0_ref[...] + e1 * o1_ref[...] + e2 * o2_ref[...]) / (e0 + e1 + e2)
    ya = _dot(oa_ref[...], wa_ref[...])
    yb = _dot(ob.astype(BF16), wb_ref[...])
    mixed = gate_ref[:, :D_MODEL].astype(F32) * ya + gate_ref[:, D_MODEL:].astype(F32) * yb
    x1 = x_ref[...] + _dot(mixed.astype(BF16), wo_ref[...])
    x1_ref[...] = x1
    h2 = _rms(x1, g_ref[...])
    hp_ref[...] = _pack_halves(h2[:, :HALF], h2[:, HALF:])

    logits = _dot_nt(wr_ref[...], h2.astype(BF16)) + br_ref[...]
    eidx = lax.broadcasted_iota(I32, (N_EXPERTS, tm), 0)
    vals, idxs = [], []
    for _ in range(TOP_K):
        m = jnp.max(logits, axis=0, keepdims=True)
        idx = jnp.min(jnp.where(logits == m, eidx, N_EXPERTS), axis=0, keepdims=True)
        vals.append(m)
        idxs.append(idx)
        logits = jnp.where(eidx == idx, -jnp.inf, logits)
    exps = [jnp.exp(vk - vals[0]) for vk in vals]
    den = exps[0] + exps[1] + exps[2] + exps[3]
    row8 = lax.broadcasted_iota(I32, (8, tm), 0)
    row128 = lax.broadcasted_iota(I32, (HEAD_PAD, tm), 0)
    topi = jnp.zeros((8, tm), I32)
    topw = jnp.zeros((8, tm), F32)
    wide = jnp.zeros((HEAD_PAD, tm), F32)
    for kk in range(TOP_K):
        wk = exps[kk] / den
        topi = jnp.where(row8 == kk, idxs[kk], topi)
        topw = jnp.where(row8 == kk, wk, topw)
        wide = jnp.where(row128 == kk, wk, wide)
    topi_ref[...] = topi
    topw_ref[...] = topw
    wcol_ref[...] = wide.T


def _merge(x2d, oa, obs, lses, gates, wa, wb, wo, g, wr_t, br_col):
    t = x2d.shape[0]
    tm = TOKEN_TILE

    def row(i):
        return (i, 0)

    def col(i):
        return (0, i)

    def full(a):
        return pl.BlockSpec(a.shape, lambda i: (0, 0))

    def rows(width):
        return pl.BlockSpec((tm, width), row)

    out_shape = [
        jax.ShapeDtypeStruct((t, D_MODEL), F32),
        jax.ShapeDtypeStruct((t, HALF), I32),
        jax.ShapeDtypeStruct((8, t), I32),
        jax.ShapeDtypeStruct((8, t), F32),
        jax.ShapeDtypeStruct((t, HEAD_PAD), F32),
    ]
    return pl.pallas_call(
        _merge_kernel,
        out_shape=out_shape,
        grid=(t // tm,),
        in_specs=[rows(D_MODEL), rows(MLA_HEADS * MLA_V)] + [rows(DIL_OUT)] * 6 + [rows(2 * D_MODEL)]
        + [full(wa), full(wb), full(wo), full(g), full(wr_t), full(br_col)],
        out_specs=[rows(D_MODEL), rows(HALF), pl.BlockSpec((8, tm), col), pl.BlockSpec((8, tm), col),
                   rows(HEAD_PAD)],
        compiler_params=_cparams(("parallel",)),
        name="merge_router",
    )(x2d, oa, *obs, *lses, gates, wa, wb, wo, g, wr_t, br_col)


def _positions_kernel(topi_ref, dest_ref, meta_ref, cnt_sc, carry_sc, start_sc):
    ps = pl.program_id(0)
    i = pl.program_id(1)
    tm = topi_ref.shape[1]
    eidx = lax.broadcasted_iota(I32, (N_EXPERTS, tm), 0)
    topi = topi_ref[...]
    hits = [eidx == topi[kk:kk + 1, :] for kk in range(TOP_K)]
    member = (hits[0] | hits[1] | hits[2] | hits[3])
    tile_cnt = jnp.sum(member.astype(F32), axis=1, keepdims=True)

    @pl.when((ps == 0) & (i == 0))
    def _():
        cnt_sc[...] = jnp.zeros(cnt_sc.shape, F32)

    @pl.when(ps == 0)
    def _():
        cnt_sc[...] += tile_cnt

    @pl.when((ps == 1) & (i == 0))
    def _():
        cnt = cnt_sc[...].astype(I32)
        padded = lax.shift_left(lax.shift_right_logical(cnt + (ROW_BLOCK - 1), 8), 8)
        sub = lax.broadcasted_iota(I32, (N_EXPERTS, HEAD_PAD), 0)
        lane = lax.broadcasted_iota(I32, (N_EXPERTS, HEAD_PAD), 1)
        padded_row = jnp.sum(jnp.where(sub == lane, padded, 0), axis=0, keepdims=True)
        start = jnp.sum(jnp.where(lane < sub, padded_row, 0), axis=1, keepdims=True)
        start_sc[...] = start.astype(F32)
        carry_sc[...] = jnp.zeros(carry_sc.shape, F32)
        cnt_row = jnp.sum(jnp.where(sub == lane, cnt, 0), axis=0, keepdims=True)
        start_row = jnp.sum(jnp.where(sub == lane, start, 0), axis=0, keepdims=True)
        row8 = lax.broadcasted_iota(I32, (8, HEAD_PAD), 0)
        meta = jnp.where(row8 == 0, cnt_row, 0)
        meta = jnp.where(row8 == 1, start_row, meta)
        meta = jnp.where(row8 == 2, start_row + padded_row, meta)
        meta_ref[...] = meta

    @pl.when(ps == 1)
    def _():
        tr = lax.broadcasted_iota(I32, (tm, tm), 0)
        tc = lax.broadcasted_iota(I32, (tm, tm), 1)
        before = (tr < tc).astype(BF16)
        prefix = _dot(member.astype(BF16), before)
        base = prefix + carry_sc[...] + start_sc[...]
        row8 = lax.broadcasted_iota(I32, (8, tm), 0)
        dest = jnp.zeros((8, tm), I32)
        for kk in range(TOP_K):
            dk = jnp.sum(jnp.where(hits[kk], base, 0.0), axis=0, keepdims=True).astype(I32)
            dest = jnp.where(row8 == kk, dk, dest)
        dest_ref[...] = dest
        carry_sc[...] += tile_cnt


def _positions(topi_t):
    t = topi_t.shape[1]
    tm = TOKEN_TILE
    return pl.pallas_call(
        _positions_kernel,
        out_shape=[jax.ShapeDtypeStruct((8, t), I32), jax.ShapeDtypeStruct((8, HEAD_PAD), I32)],
        grid=(2, t // tm),
        in_specs=[pl.BlockSpec((8, tm), lambda ps, i: (0, i))],
        out_specs=[pl.BlockSpec((8, tm), lambda ps, i: (0, i * ps)),
                   pl.BlockSpec((8, HEAD_PAD), lambda ps, i: (0, 0))],
        scratch_shapes=[pltpu.VMEM((N_EXPERTS, 1), F32)] * 3,
        compiler_params=_cparams(("arbitrary", "arbitrary")),
        name="routing_positions",
    )(topi_t)


def _sc_mesh():
    return plsc.VectorSubcoreMesh(core_axis_name="c", subcore_axis_name="s")


def _dispatch_rows(table, dest_flat, n_rows):
    t, c = table.shape
    n_slots = dest_flat.shape[0] // t
    per_w = t // SC_WORKERS
    assert per_w * SC_WORKERS == t and per_w % SC_WINDOW == 0
    n_chunks = per_w // SC_WINDOW

    @functools.partial(
        pl.kernel, mesh=_sc_mesh(),
        out_type=jax.ShapeDtypeStruct((n_rows, c), table.dtype),
        scratch_types=[pltpu.VMEM((SC_WINDOW,), I32), pltpu.VMEM((SC_WINDOW, c), table.dtype),
                       pltpu.SemaphoreType.DMA],
        name="dispatch_rows",
    )
    def k(table_hbm, dest_hbm, out_hbm, idx_v, rows_v, sem):
        wid = lax.axis_index("s") * SC_CORES + lax.axis_index("c")
        base = wid * per_w

        @pl.loop(0, n_chunks)
        def _(i):
            off = pl.multiple_of(base + i * SC_WINDOW, SC_WINDOW)
            pltpu.sync_copy(table_hbm.at[pl.ds(off, SC_WINDOW)], rows_v)
            for kk in range(n_slots):
                src = pl.multiple_of(kk * t + off, SC_WINDOW)
                pltpu.sync_copy(dest_hbm.at[pl.ds(src, SC_WINDOW)], idx_v)
                pltpu.async_copy(rows_v, out_hbm.at[idx_v], sem).wait()

    return k(table, dest_flat)


def _gather_rows(table, idx):
    n = idx.shape[0]
    c = table.shape[1]
    per_w = n // SC_WORKERS
    assert per_w * SC_WORKERS == n and per_w % SC_WINDOW == 0
    n_chunks = per_w // SC_WINDOW

    @functools.partial(
        pl.kernel, mesh=_sc_mesh(),
        out_type=jax.ShapeDtypeStruct((n, c), table.dtype),
        scratch_types=[pltpu.VMEM((SC_WINDOW,), I32), pltpu.VMEM((SC_WINDOW, c), table.dtype),
                       pltpu.SemaphoreType.DMA],
        name="gather_rows",
    )
    def k(table_hbm, idx_hbm, out_hbm, idx_v, rows_v, sem):
        wid = lax.axis_index("s") * SC_CORES + lax.axis_index("c")
        base = wid * per_w

        @pl.loop(0, n_chunks)
        def _(i):
            off = pl.multiple_of(base + i * SC_WINDOW, SC_WINDOW)
            pltpu.sync_copy(idx_hbm.at[pl.ds(off, SC_WINDOW)], idx_v)
            pltpu.async_copy(table_hbm.at[idx_v], rows_v, sem).wait()
            pltpu.sync_copy(rows_v, out_hbm.at[pl.ds(off, SC_WINDOW)])

    return k(table, idx)


def _expert_kernel(be_ref, nused_ref, xs_ref, wg_ref, bg_ref, wu_ref, bu_ref, wd_ref, bd_ref, ys_ref):
    b = pl.program_id(0)

    @pl.when(b < nused_ref[0])
    def _():
        lo, hi = _unpack_halves(xs_ref[...])
        lo = lo.astype(BF16)
        hi = hi.astype(BF16)
        a = _dot(lo, wg_ref[:HALF, :]) + _dot(hi, wg_ref[HALF:, :]) + bg_ref[...]
        u = _dot(lo, wu_ref[:HALF, :]) + _dot(hi, wu_ref[HALF:, :]) + bu_ref[...]
        a = jnp.minimum(a, SWIGLU_LIMIT)
        u = jnp.clip(u, -SWIGLU_LIMIT, SWIGLU_LIMIT)
        y = (a * jax.nn.sigmoid(SWIGLU_ALPHA * a)) * (u + 1.0)
        out = _dot(y.astype(BF16), wd_ref[...]) + bd_ref[...]
        ys_ref[...] = _pack_halves(out[:, :HALF], out[:, HALF:])

    @pl.when(b >= nused_ref[0])
    def _():
        ys_ref[...] = jnp.zeros(ys_ref.shape, I32)


def _expert_ffn(xs, block_e, n_used, wg, bg, wu, bu, wd, bd):
    n_rows = xs.shape[0]
    n_blocks = n_rows // ROW_BLOCK

    def rows(b, be, nu):
        return (b, 0)

    def expert(b, be, nu):
        return (be[b], 0, 0)

    grid_spec = pltpu.PrefetchScalarGridSpec(
        num_scalar_prefetch=2,
        grid=(n_blocks,),
        in_specs=[pl.BlockSpec((ROW_BLOCK, HALF), rows),
                  pl.BlockSpec((None, D_MODEL, D_EXPERT), expert),
                  pl.BlockSpec((None, 1, D_EXPERT), expert),
                  pl.BlockSpec((None, D_MODEL, D_EXPERT), expert),
                  pl.BlockSpec((None, 1, D_EXPERT), expert),
                  pl.BlockSpec((None, D_EXPERT, D_MODEL), expert),
                  pl.BlockSpec((None, 1, D_MODEL), expert)],
        out_specs=pl.BlockSpec((ROW_BLOCK, HALF), rows),
    )
    return pl.pallas_call(
        _expert_kernel,
        out_shape=jax.ShapeDtypeStruct((n_rows, HALF), I32),
        grid_spec=grid_spec,
        compiler_params=_cparams(("arbitrary",)),
        name="expert_ffn",
    )(block_e, n_used, xs, wg, bg, wu, bu, wd, bd)


def _combine_kernel(x1_ref, yg_ref, wcol_ref, p_ref, gple_ref, wpg_ref, wpp_ref, gout_ref, o_ref,
                    *, final):
    x1 = x1_ref[...]
    acc_lo = x1[:, :HALF]
    acc_hi = x1[:, HALF:]
    wcol = wcol_ref[...]
    for kk in range(TOP_K):
        lo, hi = _unpack_halves(yg_ref[kk])
        wk = wcol[:, kk:kk + 1]
        acc_lo = acc_lo + wk * lo
        acc_hi = acc_hi + wk * hi
    x2 = jnp.concatenate([acc_lo, acc_hi], axis=1)
    gate = jax.nn.sigmoid(_dot(_rms(x2, gple_ref[...]).astype(BF16), wpg_ref[...]))
    x3 = x2 + gate * _dot(p_ref[...].astype(BF16), wpp_ref[...])
    o_ref[...] = _rms(x3, gout_ref[...]) if final else x3


def _combine(x1, yg, wcol, p2d, gple, wpg, wpp, gout, final):
    t = x1.shape[0]
    tm = TOKEN_TILE

    def row(i):
        return (i, 0)

    def full(a):
        return pl.BlockSpec(a.shape, lambda i: (0, 0))

    return pl.pallas_call(
        functools.partial(_combine_kernel, final=final),
        out_shape=jax.ShapeDtypeStruct((t, D_MODEL), F32),
        grid=(t // tm,),
        in_specs=[pl.BlockSpec((tm, D_MODEL), row),
                  pl.BlockSpec((TOP_K, tm, HALF), lambda i: (0, i, 0)),
                  pl.BlockSpec((tm, HEAD_PAD), row),
                  pl.BlockSpec((tm, PLE_DIM), row),
                  full(gple), full(wpg), full(wpp), full(gout)],
        out_specs=pl.BlockSpec((tm, D_MODEL), row),
        compiler_params=_cparams(("parallel",)),
        name="combine_ple",
    )(x1, yg, wcol, p2d, gple, wpg, wpp, gout)


def _rope_tables(seq):
    inv_freq = ROPE_THETA ** (-jnp.arange(HALF_ROPE, dtype=F32) * 2.0 / MLA_ROPE)
    ang = jnp.arange(seq, dtype=F32)[:, None] * inv_freq[None, :]
    cos, sin = jnp.cos(ang), jnp.sin(ang)
    ones = jnp.ones((seq, MLA_NOPE), F32)
    zeros16 = jnp.zeros((seq, HALF_ROPE), F32)
    zeros64 = jnp.zeros((seq, MLA_NOPE), F32)
    tail = jnp.ones((seq, HEAD_PAD - MLA_NOPE - MLA_ROPE), F32)
    ztail = jnp.zeros_like(tail)
    cos_t = jnp.concatenate([ones, cos, cos, tail], axis=1)
    sina_t = jnp.concatenate([zeros64, zeros16, sin, ztail], axis=1)
    sinb_t = jnp.concatenate([zeros64, -sin, zeros16, ztail], axis=1)
    return cos_t, sina_t, sinb_t


def _prep_mixer_weights(w_in, w_uq, w_ukv):
    c0 = MLA_Q_LORA + MLA_KV_LORA
    c1 = c0 + MLA_ROPE
    c2 = c1 + len(DIL_GROUPS) * DIL_COLS
    kr_pad = jnp.zeros((D_MODEL, HEAD_PAD), F32).at[:, MLA_NOPE:MLA_NOPE + MLA_ROPE].set(w_in[:, c0:c1])
    wmla = jnp.concatenate([w_in[:, :c0], kr_pad], axis=1).astype(BF16)
    wdil = w_in[:, c1:c2].astype(BF16)
    wgate = w_in[:, c2:].astype(BF16)
    pad = HEAD_PAD - MLA_NOPE - MLA_ROPE
    wuq_h = w_uq.reshape(MLA_Q_LORA, MLA_HEADS, MLA_NOPE + MLA_ROPE)
    wuq = jnp.pad(wuq_h, ((0, 0), (0, 0), (0, pad))).reshape(MLA_Q_LORA, MLA_HEADS * HEAD_PAD).astype(BF16)
    wukv_h = w_ukv.reshape(MLA_KV_LORA, MLA_HEADS, MLA_NOPE + MLA_V)
    wuk = jnp.pad(wukv_h[:, :, :MLA_NOPE], ((0, 0), (0, 0), (0, HEAD_PAD - MLA_NOPE)))
    wuk = wuk.reshape(MLA_KV_LORA, MLA_HEADS * HEAD_PAD).astype(BF16)
    wuv = wukv_h[:, :, MLA_NOPE:].reshape(MLA_KV_LORA, MLA_HEADS * MLA_V).astype(BF16)
    return wmla, wdil, wgate, wuq, wuk, wuv


def kernel(x, p, attn_norm, w_in, q_norm, w_uq, kv_norm, w_ukv, w_branch_a, w_branch_b, w_out, ffn_norm, w_router, b_router, w_gate, b_gate, w_up, b_up, w_down, b_down, ple_norm, w_ple_gate, w_ple_proj, final_norm):
    b, s, d = x.shape
    depth = w_in.shape[0]
    t = b * s
    assert d == D_MODEL and s % (DIL_GROUPS[-1][0]) == 0 and t % (SC_WORKERS * SC_WINDOW) == 0
    n_assign = t * TOP_K
    n_blocks = -(-(n_assign + N_EXPERTS * (ROW_BLOCK - 1)) // ROW_BLOCK)
    n_rows = n_blocks * ROW_BLOCK
    cos_t, sina_t, sinb_t = _rope_tables(s)
    xc = x.reshape(t, d)
    for i in range(depth):
        wmla, wdil, wgate, wuq, wuk, wuv = _prep_mixer_weights(w_in[i], w_uq[i], w_ukv[i])
        q, k, v, zd0, zd1, zd2, gates = _inproj(
            xc, s, attn_norm[i][None], wmla, wdil, wgate, q_norm[i][None], kv_norm[i][None],
            wuq, wuk, wuv, cos_t, sina_t, sinb_t)
        oa = _mla_attention(q.reshape(b, s, -1), k.reshape(b, s, -1), v.reshape(b, s, -1))
        dil = [_dilated_attention(z.reshape(b, s, DIL_COLS), gi) for gi, z in enumerate((zd0, zd1, zd2))]
        x1, hp, topi_t, topw_t, wcol = _merge(
            xc, oa.reshape(t, -1), [o for o, _ in dil], [l for _, l in dil], gates,
            w_branch_a[i].astype(BF16), w_branch_b[i].astype(BF16), w_out[i].astype(BF16),
            ffn_norm[i][None], w_router[i].T.astype(BF16), b_router[i][:, None])
        dest_t, meta = _positions(topi_t)
        ends = meta[2, :N_EXPERTS]
        block_e = jnp.minimum(
            jnp.searchsorted(ends, jnp.arange(n_blocks, dtype=I32) * ROW_BLOCK, side="right"),
            N_EXPERTS - 1).astype(I32)
        n_used = (ends[N_EXPERTS - 1:] // ROW_BLOCK).astype(I32)
        dest_flat = dest_t[:TOP_K].reshape(n_assign)
        xs = _dispatch_rows(hp, dest_flat, n_rows)
        ys = _expert_ffn(xs, block_e, n_used,
                         w_gate[i].astype(BF16), b_gate[i][:, None, :],
                         w_up[i].astype(BF16), b_up[i][:, None, :],
                         w_down[i].astype(BF16), b_down[i][:, None, :])
        yg = _gather_rows(ys, dest_flat).reshape(TOP_K, t, HALF)
        final = i == depth - 1
        gout = final_norm[None] if final else attn_norm[i][None]
        xc = _combine(x1, yg, wcol, p[i].reshape(t, PLE_DIM), ple_norm[i][None],
                      w_ple_gate[i].astype(BF16), w_ple_proj[i].astype(BF16), gout, final)
    return xc.reshape(b, s, d)
```

```python
import functools
import math

import jax
import jax.numpy as jnp
import numpy as np
from jax import lax
from jax.experimental import pallas as pl
from jax.experimental.pallas import tpu as pltpu
from jax.experimental.pallas import tpu_sc as plsc

F32 = jnp.float32
BF16 = jnp.bfloat16
I32 = jnp.int32

D_MODEL = 1024
PLE_DIM = 256
NORM_EPS = 1e-6

MLA_HEADS = 8
MLA_Q_LORA = 384
MLA_KV_LORA = 256
MLA_NOPE = 64
MLA_ROPE = 32
MLA_V = 64
ROPE_THETA = 10000.0
HEAD_PAD = 128
HALF_ROPE = MLA_ROPE // 2

DIL_GROUPS = ((128, 1), (512, 4), (2048, 16))
DIL_HEADS = 4
DIL_HEAD_DIM = 64
DIL_STEPS = 128
DIL_COLS = 3 * DIL_HEADS * DIL_HEAD_DIM
DIL_OUT = DIL_HEADS * DIL_HEAD_DIM

N_EXPERTS = 32
TOP_K = 4
D_EXPERT = 1024
SWIGLU_LIMIT = 7.0
SWIGLU_ALPHA = 1.702
ROW_BLOCK = 256

TOKEN_TILE = 512
ATTN_TILE = 512
HALF = D_MODEL // 2
NEG = -1e30
HI_MASK = -65536

SC_CORES = 2
SC_SUBCORES = 16
SC_WORKERS = SC_CORES * SC_SUBCORES
SC_WINDOW = 64

VMEM_LIMIT = 56 * 1024 * 1024


def _cparams(sem):
    return pltpu.CompilerParams(dimension_semantics=sem, vmem_limit_bytes=VMEM_LIMIT)


def _rms(x, g):
    return x * lax.rsqrt(jnp.mean(x * x, axis=-1, keepdims=True) + NORM_EPS) * g


def _dot(a, b):
    return jnp.dot(a, b, preferred_element_type=F32)


def _dot_nt(a, b):
    return lax.dot_general(a, b, (((1,), (1,)), ((), ())), preferred_element_type=F32)


def _pack_halves(lo, hi):
    lo_i = lax.bitcast_convert_type(lo.astype(BF16).astype(F32), I32)
    hi_i = lax.bitcast_convert_type(hi.astype(BF16).astype(F32), I32)
    return (hi_i & HI_MASK) | lax.shift_right_logical(lo_i, 16)


def _unpack_halves(w):
    lo = lax.bitcast_convert_type(lax.shift_left(w, 16), F32)
    hi = lax.bitcast_convert_type(w & HI_MASK, F32)
    return lo, hi


def _inproj_kernel(x_ref, g_ref, wmla_ref, wdil_ref, wgate_ref, qn_ref, kvn_ref, wuq_ref, wuk_ref,
                   wuv_ref, cos_ref, sina_ref, sinb_ref,
                   q_ref, k_ref, vt_ref, zd0_ref, zd1_ref, zd2_ref, gate_ref):
    h = _rms(x_ref[...], g_ref[...]).astype(BF16)
    zm = _dot(h, wmla_ref[...])
    cq = _rms(zm[:, :MLA_Q_LORA], qn_ref[...]).astype(BF16)
    ckv = _rms(zm[:, MLA_Q_LORA:MLA_Q_LORA + MLA_KV_LORA], kvn_ref[...]).astype(BF16)
    kr = zm[:, MLA_Q_LORA + MLA_KV_LORA:]
    cos, sina, sinb = cos_ref[...], sina_ref[...], sinb_ref[...]

    def rope(t):
        return (t * cos + pltpu.roll(t, HALF_ROPE, 1) * sina
                + pltpu.roll(t, HEAD_PAD - HALF_ROPE, 1) * sinb)

    kr_rot = rope(kr)
    qraw = _dot(cq, wuq_ref[...])
    kraw = _dot(ckv, wuk_ref[...])
    vt_ref[...] = _dot(ckv, wuv_ref[...]).T.astype(BF16)
    scale = (MLA_NOPE + MLA_ROPE) ** -0.5 * math.log2(math.e)
    for hd in range(MLA_HEADS):
        sl = slice(hd * HEAD_PAD, (hd + 1) * HEAD_PAD)
        q_ref[:, sl] = (rope(qraw[:, sl]) * scale).astype(BF16)
        k_ref[:, sl] = (kraw[:, sl] + kr_rot).astype(BF16)
    for gi, zd_ref in enumerate((zd0_ref, zd1_ref, zd2_ref)):
        zd_ref[...] = _dot(h, wdil_ref[:, gi * DIL_COLS:(gi + 1) * DIL_COLS]).astype(BF16)
    for c in range(2):
        sl = slice(c * D_MODEL, (c + 1) * D_MODEL)
        gate_ref[:, sl] = jax.nn.sigmoid(_dot(h, wgate_ref[:, sl])).astype(BF16)


def _inproj(x2d, seq, g, wmla, wdil, wgate, qn, kvn, wuq, wuk, wuv, cos_t, sina_t, sinb_t):
    t = x2d.shape[0]
    tm = TOKEN_TILE
    n_seq_tiles = seq // tm

    def row(i):
        return (i, 0)

    def const(i):
        return (0, 0)

    def pos(i):
        return (i % n_seq_tiles, 0)

    def full(a):
        return pl.BlockSpec(a.shape, const)

    def vt_block(i):
        return (i // n_seq_tiles, 0, i % n_seq_tiles)

    out_shape = [
        jax.ShapeDtypeStruct((t, MLA_HEADS * HEAD_PAD), BF16),
        jax.ShapeDtypeStruct((t, MLA_HEADS * HEAD_PAD), BF16),
        jax.ShapeDtypeStruct((t // seq, MLA_HEADS * MLA_V, seq), BF16),
        jax.ShapeDtypeStruct((t, DIL_COLS), BF16),
        jax.ShapeDtypeStruct((t, DIL_COLS), BF16),
        jax.ShapeDtypeStruct((t, DIL_COLS), BF16),
        jax.ShapeDtypeStruct((t, 2 * D_MODEL), BF16),
    ]
    return pl.pallas_call(
        _inproj_kernel,
        out_shape=out_shape,
        grid=(t // tm,),
        in_specs=[pl.BlockSpec((tm, D_MODEL), row), full(g), full(wmla), full(wdil), full(wgate),
                  full(qn), full(kvn), full(wuq), full(wuk), full(wuv),
                  pl.BlockSpec((tm, HEAD_PAD), pos), pl.BlockSpec((tm, HEAD_PAD), pos),
                  pl.BlockSpec((tm, HEAD_PAD), pos)],
        out_specs=[pl.BlockSpec((None, MLA_HEADS * MLA_V, tm), vt_block) if len(s.shape) == 3
                   else pl.BlockSpec((tm, s.shape[1]), row) for s in out_shape],
        compiler_params=_cparams(("parallel",)),
        name="inproj",
    )(x2d, g, wmla, wdil, wgate, qn, kvn, wuq, wuk, wuv, cos_t, sina_t, sinb_t)


def _mla_kernel(qi_ref, kj_ref, q_ref, k_ref, vt_ref, o_ref, m_sc, l_sc, acc_sc):
    p = pl.program_id(2)
    i = qi_ref[p]
    j = kj_ref[p]
    tq = q_ref.shape[0]
    tk = k_ref.shape[0]

    @pl.when(j == 0)
    def _():
        m_sc[...] = jnp.full(m_sc.shape, NEG, F32)
        l_sc[...] = jnp.zeros(l_sc.shape, F32)
        acc_sc[...] = jnp.zeros(acc_sc.shape, F32)

    def step(diagonal):
        for hh in range(2):
            sl = slice(hh * HEAD_PAD, (hh + 1) * HEAD_PAD)
            st = _dot_nt(k_ref[:, sl], q_ref[:, sl])
            if diagonal:
                key = lax.broadcasted_iota(I32, (tk, tq), 0)
                qry = lax.broadcasted_iota(I32, (tk, tq), 1)
                st = jnp.where(qry >= key, st, NEG)
            m_prev = m_sc[hh]
            m_new = jnp.maximum(m_prev, jnp.max(st, axis=0, keepdims=True))
            alpha = jnp.exp2(m_prev - m_new)
            pt = jnp.exp2(st - m_new)
            l_sc[hh] = alpha * l_sc[hh] + jnp.sum(pt, axis=0, keepdims=True)
            pv = _dot(vt_ref[hh * MLA_V:(hh + 1) * MLA_V, :], pt.astype(BF16))
            acc_sc[hh] = alpha * acc_sc[hh] + pv
            m_sc[hh] = m_new

    @pl.when(j < i)
    def _():
        step(False)

    @pl.when(j == i)
    def _():
        step(True)
        ot = jnp.concatenate([acc_sc[0] / l_sc[0], acc_sc[1] / l_sc[1]], axis=0)
        o_ref[...] = ot.T.astype(BF16)


def _mla_attention(q, k, vt):
    b, s, _ = q.shape
    tq = ATTN_TILE
    nq = s // tq
    pairs = [(i, j) for i in range(nq) for j in range(i + 1)]
    qi = jnp.asarray([p[0] for p in pairs], I32)
    kj = jnp.asarray([p[1] for p in pairs], I32)
    grid_spec = pltpu.PrefetchScalarGridSpec(
        num_scalar_prefetch=2,
        grid=(b, MLA_HEADS // 2, len(pairs)),
        in_specs=[
            pl.BlockSpec((None, tq, 2 * HEAD_PAD), lambda bb, hp, p, qi, kj: (bb, qi[p], hp)),
            pl.BlockSpec((None, tq, 2 * HEAD_PAD), lambda bb, hp, p, qi, kj: (bb, kj[p], hp)),
            pl.BlockSpec((None, 2 * MLA_V, tq), lambda bb, hp, p, qi, kj: (bb, hp, kj[p])),
        ],
        out_specs=pl.BlockSpec((None, tq, 2 * MLA_V), lambda bb, hp, p, qi, kj: (bb, qi[p], hp)),
        scratch_shapes=[pltpu.VMEM((2, 1, tq), F32), pltpu.VMEM((2, 1, tq), F32),
                        pltpu.VMEM((2, MLA_V, tq), F32)],
    )
    return pl.pallas_call(
        _mla_kernel,
        out_shape=jax.ShapeDtypeStruct((b, s, MLA_HEADS * MLA_V), BF16),
        grid_spec=grid_spec,
        compiler_params=_cparams(("parallel", "parallel", "arbitrary")),
        name="mla_attention",
    )(qi, kj, q, k, vt)


def _alibi_slopes(n):
    def pow2(m):
        start = 2.0 ** (-8.0 / m)
        return [start ** (i + 1) for i in range(m)]
    if math.log2(n).is_integer():
        s = pow2(n)
    else:
        c = 2 ** int(math.floor(math.log2(n)))
        s = pow2(c) + pow2(2 * c)[0::2][: n - c]
    return np.array(sorted(s, reverse=True), dtype=np.float32)


def _dilated_kernel(cur_ref, prev_ref, o_ref, lse_ref, *, step_slopes):
    mi = pl.program_id(2)
    n = DIL_STEPS
    hw = DIL_HEADS * DIL_HEAD_DIM
    cur = cur_ref[...]
    prev = prev_ref[...]
    qi = lax.broadcasted_iota(I32, (n, 2 * n), 0)
    ki = lax.broadcasted_iota(I32, (n, 2 * n), 1)
    dist = qi + n - ki
    valid = (dist >= 0) & (dist <= n) & ((ki >= n) | (mi > 0))
    distf = dist.astype(F32)
    lane = lax.broadcasted_iota(I32, (n, 2 * DIL_HEAD_DIM), 1)
    low = lane < DIL_HEAD_DIM
    scale = DIL_HEAD_DIM ** -0.5
    for pair in range(DIL_HEADS // 2):
        sl = slice(pair * 2 * DIL_HEAD_DIM, (pair + 1) * 2 * DIL_HEAD_DIM)
        q = cur[:, sl]
        kk = jnp.concatenate([prev[:, hw:2 * hw][:, sl], cur[:, hw:2 * hw][:, sl]], axis=0)
        vv = jnp.concatenate([prev[:, 2 * hw:][:, sl], cur[:, 2 * hw:][:, sl]], axis=0)
        outs, lses = [], []
        for hh in range(2):
            keep = low if hh == 0 else jnp.logical_not(low)
            qm = jnp.where(keep, q, jnp.zeros_like(q))
            s = _dot_nt(qm, kk) * scale - step_slopes[pair * 2 + hh] * distf
            s = jnp.where(valid, s, NEG)
            m = jnp.max(s, axis=1, keepdims=True)
            pe = jnp.exp(s - m)
            l = jnp.sum(pe, axis=1, keepdims=True)
            outs.append(_dot(pe.astype(BF16), vv) / l)
            lses.append(m + jnp.log(l))
        o_ref[:, sl] = jnp.where(low, outs[0], outs[1])
        lse_ref[:, sl] = jnp.where(low, lses[0], lses[1])


def _dilated_attention(zd, gi):
    window, dil = DIL_GROUPS[gi]
    b, s, _ = zd.shape
    n_sub = s // dil
    nb = n_sub // DIL_STEPS
    slopes = _alibi_slopes(len(DIL_GROUPS) * DIL_HEADS).reshape(len(DIL_GROUPS), DIL_HEADS)[gi] * dil
    view = zd.reshape(b, n_sub, dil * DIL_COLS)
    o, lse = pl.pallas_call(
        functools.partial(_dilated_kernel, step_slopes=tuple(float(x) for x in slopes)),
        out_shape=[jax.ShapeDtypeStruct((b, n_sub, dil * DIL_OUT), F32)] * 2,
        grid=(b, dil, nb),
        in_specs=[pl.BlockSpec((None, DIL_STEPS, DIL_COLS), lambda bb, r, m: (bb, m, r)),
                  pl.BlockSpec((None, DIL_STEPS, DIL_COLS),
                               lambda bb, r, m: (bb, jnp.maximum(m - 1, 0), r))],
        out_specs=[pl.BlockSpec((None, DIL_STEPS, DIL_OUT), lambda bb, r, m: (bb, m, r))] * 2,
        compiler_params=_cparams(("parallel", "parallel", "arbitrary")),
        name=f"dilated_attention_{gi}",
    )(view, view)
    return o.reshape(b * s, DIL_OUT), lse.reshape(b * s, DIL_OUT)


def _merge_kernel(x_ref, oa_ref, o0_ref, o1_ref, o2_ref, l0_ref, l1_ref, l2_ref, gate_ref,
                  wa_ref, wb_ref, wo_ref, g_ref, wr_ref, br_ref,
                  x1_ref, hp_ref, topi_ref, topw_ref, wcol_ref):
    tm = x_ref.shape[0]
    l0, l1, l2 = l0_ref[...], l1_ref[...], l2_ref[...]
    lmax = jnp.maximum(jnp.maximum(l0, l1), l2)
    e0, e1, e2 = jnp.exp(l0 - lmax), jnp.exp(l1 - lmax), jnp.exp(l2 - lmax)
    ob = (e0 * o0_ref[...] + e1 * o1_ref[...] + e2 * o2_ref[...]) / (e0 + e1 + e2)
    ya = _dot(oa_ref[...], wa_ref[...])
    yb = _dot(ob.astype(BF16), wb_ref[...])
    mixed = gate_ref[:, :D_MODEL].astype(F32) * ya + gate_ref[:, D_MODEL:].astype(F32) * yb
    x1 = x_ref[...] + _dot(mixed.astype(BF16), wo_ref[...])
    x1_ref[...] = x1
    h2 = _rms(x1, g_ref[...])
    hp_ref[...] = _pack_halves(h2[:, :HALF], h2[:, HALF:])

    logits = _dot_nt(wr_ref[...], h2.astype(BF16)) + br_ref[...]
    eidx = lax.broadcasted_iota(I32, (N_EXPERTS, tm), 0)
    vals, idxs = [], []
    for _ in range(TOP_K):
        m = jnp.max(logits, axis=0, keepdims=True)
        idx = jnp.min(jnp.where(logits == m, eidx, N_EXPERTS), axis=0, keepdims=True)
        vals.append(m)
        idxs.append(idx)
        logits = jnp.where(eidx == idx, -jnp.inf, logits)
    exps = [jnp.exp(vk - vals[0]) for vk in vals]
    den = exps[0] + exps[1] + exps[2] + exps[3]
    row8 = lax.broadcasted_iota(I32, (8, tm), 0)
    row128 = lax.broadcasted_iota(I32, (HEAD_PAD, tm), 0)
    topi = jnp.zeros((8, tm), I32)
    topw = jnp.zeros((8, tm), F32)
    wide = jnp.zeros((HEAD_PAD, tm), F32)
    for kk in range(TOP_K):
        wk = exps[kk] / den
        topi = jnp.where(row8 == kk, idxs[kk], topi)
        topw = jnp.where(row8 == kk, wk, topw)
        wide = jnp.where(row128 == kk, wk, wide)
    topi_ref[...] = topi
    topw_ref[...] = topw
    wcol_ref[...] = wide.T


def _merge(x2d, oa, obs, lses, gates, wa, wb, wo, g, wr_t, br_col):
    t = x2d.shape[0]
    tm = TOKEN_TILE

    def row(i):
        return (i, 0)

    def col(i):
        return (0, i)

    def full(a):
        return pl.BlockSpec(a.shape, lambda i: (0, 0))

    def rows(width):
        return pl.BlockSpec((tm, width), row)

    out_shape = [
        jax.ShapeDtypeStruct((t, D_MODEL), F32),
        jax.ShapeDtypeStruct((t, HALF), I32),
        jax.ShapeDtypeStruct((8, t), I32),
        jax.ShapeDtypeStruct((8, t), F32),
        jax.ShapeDtypeStruct((t, HEAD_PAD), F32),
    ]
    return pl.pallas_call(
        _merge_kernel,
        out_shape=out_shape,
        grid=(t // tm,),
        in_specs=[rows(D_MODEL), rows(MLA_HEADS * MLA_V)] + [rows(DIL_OUT)] * 6 + [rows(2 * D_MODEL)]
        + [full(wa), full(wb), full(wo), full(g), full(wr_t), full(br_col)],
        out_specs=[rows(D_MODEL), rows(HALF), pl.BlockSpec((8, tm), col), pl.BlockSpec((8, tm), col),
                   rows(HEAD_PAD)],
        compiler_params=_cparams(("parallel",)),
        name="merge_router",
    )(x2d, oa, *obs, *lses, gates, wa, wb, wo, g, wr_t, br_col)


def _positions_kernel(topi_ref, dest_ref, meta_ref, cnt_sc, carry_sc, start_sc):
    ps = pl.program_id(0)
    i = pl.program_id(1)
    tm = topi_ref.shape[1]
    eidx = lax.broadcasted_iota(I32, (N_EXPERTS, tm), 0)
    topi = topi_ref[...]
    hits = [eidx == topi[kk:kk + 1, :] for kk in range(TOP_K)]
    member = (hits[0] | hits[1] | hits[2] | hits[3])
    tile_cnt = jnp.sum(member.astype(F32), axis=1, keepdims=True)

    @pl.when((ps == 0) & (i == 0))
    def _():
        cnt_sc[...] = jnp.zeros(cnt_sc.shape, F32)

    @pl.when(ps == 0)
    def _():
        cnt_sc[...] += tile_cnt

    @pl.when((ps == 1) & (i == 0))
    def _():
        cnt = cnt_sc[...].astype(I32)
        padded = lax.shift_left(lax.shift_right_logical(cnt + (ROW_BLOCK - 1), 8), 8)
        sub = lax.broadcasted_iota(I32, (N_EXPERTS, HEAD_PAD), 0)
        lane = lax.broadcasted_iota(I32, (N_EXPERTS, HEAD_PAD), 1)
        padded_row = jnp.sum(jnp.where(sub == lane, padded, 0), axis=0, keepdims=True)
        start = jnp.sum(jnp.where(lane < sub, padded_row, 0), axis=1, keepdims=True)
        start_sc[...] = start.astype(F32)
        carry_sc[...] = jnp.zeros(carry_sc.shape, F32)
        cnt_row = jnp.sum(jnp.where(sub == lane, cnt, 0), axis=0, keepdims=True)
        start_row = jnp.sum(jnp.where(sub == lane, start, 0), axis=0, keepdims=True)
        row8 = lax.broadcasted_iota(I32, (8, HEAD_PAD), 0)
        meta = jnp.where(row8 == 0, cnt_row, 0)
        meta = jnp.where(row8 == 1, start_row, meta)
        meta = jnp.where(row8 == 2, start_row + padded_row, meta)
        meta_ref[...] = meta

    @pl.when(ps == 1)
    def _():
        tr = lax.broadcasted_iota(I32, (tm, tm), 0)
        tc = lax.broadcasted_iota(I32, (tm, tm), 1)
        before = (tr < tc).astype(BF16)
        prefix = _dot(member.astype(BF16), before)
        base = prefix + carry_sc[...] + start_sc[...]
        row8 = lax.broadcasted_iota(I32, (8, tm), 0)
        dest = jnp.zeros((8, tm), I32)
        for kk in range(TOP_K):
            dk = jnp.sum(jnp.where(hits[kk], base, 0.0), axis=0, keepdims=True).astype(I32)
            dest = jnp.where(row8 == kk, dk, dest)
        dest_ref[...] = dest
        carry_sc[...] += tile_cnt


def _positions(topi_t):
    t = topi_t.shape[1]
    tm = TOKEN_TILE
    return pl.pallas_call(
        _positions_kernel,
        out_shape=[jax.ShapeDtypeStruct((8, t), I32), jax.ShapeDtypeStruct((8, HEAD_PAD), I32)],
        grid=(2, t // tm),
        in_specs=[pl.BlockSpec((8, tm), lambda ps, i: (0, i))],
        out_specs=[pl.BlockSpec((8, tm), lambda ps, i: (0, i * ps)),
                   pl.BlockSpec((8, HEAD_PAD), lambda ps, i: (0, 0))],
        scratch_shapes=[pltpu.VMEM((N_EXPERTS, 1), F32)] * 3,
        compiler_params=_cparams(("arbitrary", "arbitrary")),
        name="routing_positions",
    )(topi_t)


def _sc_mesh():
    return plsc.VectorSubcoreMesh(core_axis_name="c", subcore_axis_name="s")


def _dispatch_rows(table, dest_flat, n_rows):
    t, c = table.shape
    n_slots = dest_flat.shape[0] // t
    per_w = t // SC_WORKERS
    assert per_w * SC_WORKERS == t and per_w % SC_WINDOW == 0
    n_chunks = per_w // SC_WINDOW

    @functools.partial(
        pl.kernel, mesh=_sc_mesh(),
        out_type=jax.ShapeDtypeStruct((n_rows, c), table.dtype),
        scratch_types=[pltpu.VMEM((SC_WINDOW,), I32), pltpu.VMEM((SC_WINDOW, c), table.dtype),
                       pltpu.SemaphoreType.DMA],
        name="dispatch_rows",
    )
    def k(table_hbm, dest_hbm, out_hbm, idx_v, rows_v, sem):
        wid = lax.axis_index("s") * SC_CORES + lax.axis_index("c")
        base = wid * per_w

        @pl.loop(0, n_chunks)
        def _(i):
            off = pl.multiple_of(base + i * SC_WINDOW, SC_WINDOW)
            pltpu.sync_copy(table_hbm.at[pl.ds(off, SC_WINDOW)], rows_v)
            for kk in range(n_slots):
                src = pl.multiple_of(kk * t + off, SC_WINDOW)
                pltpu.sync_copy(dest_hbm.at[pl.ds(src, SC_WINDOW)], idx_v)
                pltpu.async_copy(rows_v, out_hbm.at[idx_v], sem).wait()

    return k(table, dest_flat)


def _gather_rows(table, idx):
    n = idx.shape[0]
    c = table.shape[1]
    per_w = n // SC_WORKERS
    assert per_w * SC_WORKERS == n and per_w % SC_WINDOW == 0
    n_chunks = per_w // SC_WINDOW

    @functools.partial(
        pl.kernel, mesh=_sc_mesh(),
        out_type=jax.ShapeDtypeStruct((n, c), table.dtype),
        scratch_types=[pltpu.VMEM((SC_WINDOW,), I32), pltpu.VMEM((SC_WINDOW, c), table.dtype),
                       pltpu.SemaphoreType.DMA],
        name="gather_rows",
    )
    def k(table_hbm, idx_hbm, out_hbm, idx_v, rows_v, sem):
        wid = lax.axis_index("s") * SC_CORES + lax.axis_index("c")
        base = wid * per_w

        @pl.loop(0, n_chunks)
        def _(i):
            off = pl.multiple_of(base + i * SC_WINDOW, SC_WINDOW)
            pltpu.sync_copy(idx_hbm.at[pl.ds(off, SC_WINDOW)], idx_v)
            pltpu.async_copy(table_hbm.at[idx_v], rows_v, sem).wait()
            pltpu.sync_copy(rows_v, out_hbm.at[pl.ds(off, SC_WINDOW)])

    return k(table, idx)


def _expert_kernel(be_ref, nused_ref, xs_ref, wg_ref, bg_ref, wu_ref, bu_ref, wd_ref, bd_ref, ys_ref):
    b = pl.program_id(0)

    @pl.when(b < nused_ref[0])
    def _():
        lo, hi = _unpack_halves(xs_ref[...])
        lo = lo.astype(BF16)
        hi = hi.astype(BF16)
        a = _dot(lo, wg_ref[:HALF, :]) + _dot(hi, wg_ref[HALF:, :]) + bg_ref[...]
        u = _dot(lo, wu_ref[:HALF, :]) + _dot(hi, wu_ref[HALF:, :]) + bu_ref[...]
        a = jnp.minimum(a, SWIGLU_LIMIT)
        u = jnp.clip(u, -SWIGLU_LIMIT, SWIGLU_LIMIT)
        y = (a * jax.nn.sigmoid(SWIGLU_ALPHA * a)) * (u + 1.0)
        out = _dot(y.astype(BF16), wd_ref[...]) + bd_ref[...]
        ys_ref[...] = _pack_halves(out[:, :HALF], out[:, HALF:])

    @pl.when(b >= nused_ref[0])
    def _():
        ys_ref[...] = jnp.zeros(ys_ref.shape, I32)


def _expert_ffn(xs, block_e, n_used, wg, bg, wu, bu, wd, bd):
    n_rows = xs.shape[0]
    n_blocks = n_rows // ROW_BLOCK

    def rows(b, be, nu):
        return (b, 0)

    def expert(b, be, nu):
        return (be[b], 0, 0)

    grid_spec = pltpu.PrefetchScalarGridSpec(
        num_scalar_prefetch=2,
        grid=(n_blocks,),
        in_specs=[pl.BlockSpec((ROW_BLOCK, HALF), rows),
                  pl.BlockSpec((None, D_MODEL, D_EXPERT), expert),
                  pl.BlockSpec((None, 1, D_EXPERT), expert),
                  pl.BlockSpec((None, D_MODEL, D_EXPERT), expert),
                  pl.BlockSpec((None, 1, D_EXPERT), expert),
                  pl.BlockSpec((None, D_EXPERT, D_MODEL), expert),
                  pl.BlockSpec((None, 1, D_MODEL), expert)],
        out_specs=pl.BlockSpec((ROW_BLOCK, HALF), rows),
    )
    return pl.pallas_call(
        _expert_kernel,
        out_shape=jax.ShapeDtypeStruct((n_rows, HALF), I32),
        grid_spec=grid_spec,
        compiler_params=_cparams(("arbitrary",)),
        name="expert_ffn",
    )(block_e, n_used, xs, wg, bg, wu, bu, wd, bd)


def _combine_kernel(x1_ref, yg_ref, wcol_ref, p_ref, gple_ref, wpg_ref, wpp_ref, gout_ref, o_ref,
                    *, final):
    x1 = x1_ref[...]
    acc_lo = x1[:, :HALF]
    acc_hi = x1[:, HALF:]
    wcol = wcol_ref[...]
    for kk in range(TOP_K):
        lo, hi = _unpack_halves(yg_ref[kk])
        wk = wcol[:, kk:kk + 1]
        acc_lo = acc_lo + wk * lo
        acc_hi = acc_hi + wk * hi
    x2 = jnp.concatenate([acc_lo, acc_hi], axis=1)
    gate = jax.nn.sigmoid(_dot(_rms(x2, gple_ref[...]).astype(BF16), wpg_ref[...]))
    x3 = x2 + gate * _dot(p_ref[...].astype(BF16), wpp_ref[...])
    o_ref[...] = _rms(x3, gout_ref[...]) if final else x3


def _combine(x1, yg, wcol, p2d, gple, wpg, wpp, gout, final):
    t = x1.shape[0]
    tm = TOKEN_TILE

    def row(i):
        return (i, 0)

    def full(a):
        return pl.BlockSpec(a.shape, lambda i: (0, 0))

    return pl.pallas_call(
        functools.partial(_combine_kernel, final=final),
        out_shape=jax.ShapeDtypeStruct((t, D_MODEL), F32),
        grid=(t // tm,),
        in_specs=[pl.BlockSpec((tm, D_MODEL), row),
                  pl.BlockSpec((TOP_K, tm, HALF), lambda i: (0, i, 0)),
                  pl.BlockSpec((tm, HEAD_PAD), row),
                  pl.BlockSpec((tm, PLE_DIM), row),
                  full(gple), full(wpg), full(wpp), full(gout)],
        out_specs=pl.BlockSpec((tm, D_MODEL), row),
        compiler_params=_cparams(("parallel",)),
        name="combine_ple",
    )(x1, yg, wcol, p2d, gple, wpg, wpp, gout)


def _rope_tables(seq):
    inv_freq = ROPE_THETA ** (-jnp.arange(HALF_ROPE, dtype=F32) * 2.0 / MLA_ROPE)
    ang = jnp.arange(seq, dtype=F32)[:, None] * inv_freq[None, :]
    cos, sin = jnp.cos(ang), jnp.sin(ang)
    ones = jnp.ones((seq, MLA_NOPE), F32)
    zeros16 = jnp.zeros((seq, HALF_ROPE), F32)
    zeros64 = jnp.zeros((seq, MLA_NOPE), F32)
    tail = jnp.ones((seq, HEAD_PAD - MLA_NOPE - MLA_ROPE), F32)
    ztail = jnp.zeros_like(tail)
    cos_t = jnp.concatenate([ones, cos, cos, tail], axis=1)
    sina_t = jnp.concatenate([zeros64, zeros16, sin, ztail], axis=1)
    sinb_t = jnp.concatenate([zeros64, -sin, zeros16, ztail], axis=1)
    return cos_t, sina_t, sinb_t


def _prep_mixer_weights(w_in, w_uq, w_ukv):
    c0 = MLA_Q_LORA + MLA_KV_LORA
    c1 = c0 + MLA_ROPE
    c2 = c1 + len(DIL_GROUPS) * DIL_COLS
    kr_pad = jnp.zeros((D_MODEL, HEAD_PAD), F32).at[:, MLA_NOPE:MLA_NOPE + MLA_ROPE].set(w_in[:, c0:c1])
    wmla = jnp.concatenate([w_in[:, :c0], kr_pad], axis=1).astype(BF16)
    wdil = w_in[:, c1:c2].astype(BF16)
    wgate = w_in[:, c2:].astype(BF16)
    pad = HEAD_PAD - MLA_NOPE - MLA_ROPE
    wuq_h = w_uq.reshape(MLA_Q_LORA, MLA_HEADS, MLA_NOPE + MLA_ROPE)
    wuq = jnp.pad(wuq_h, ((0, 0), (0, 0), (0, pad))).reshape(MLA_Q_LORA, MLA_HEADS * HEAD_PAD).astype(BF16)
    wukv_h = w_ukv.reshape(MLA_KV_LORA, MLA_HEADS, MLA_NOPE + MLA_V)
    wuk = jnp.pad(wukv_h[:, :, :MLA_NOPE], ((0, 0), (0, 0), (0, HEAD_PAD - MLA_NOPE)))
    wuk = wuk.reshape(MLA_KV_LORA, MLA_HEADS * HEAD_PAD).astype(BF16)
    wuv = wukv_h[:, :, MLA_NOPE:].reshape(MLA_KV_LORA, MLA_HEADS * MLA_V).astype(BF16)
    return wmla, wdil, wgate, wuq, wuk, wuv


def kernel(x, p, attn_norm, w_in, q_norm, w_uq, kv_norm, w_ukv, w_branch_a, w_branch_b, w_out, ffn_norm, w_router, b_router, w_gate, b_gate, w_up, b_up, w_down, b_down, ple_norm, w_ple_gate, w_ple_proj, final_norm):
    b, s, d = x.shape
    depth = w_in.shape[0]
    t = b * s
    assert d == D_MODEL and s % (DIL_GROUPS[-1][0]) == 0 and t % (SC_WORKERS * SC_WINDOW) == 0
    n_assign = t * TOP_K
    n_blocks = -(-(n_assign + N_EXPERTS * (ROW_BLOCK - 1)) // ROW_BLOCK)
    n_rows = n_blocks * ROW_BLOCK
    cos_t, sina_t, sinb_t = _rope_tables(s)
    xc = x.reshape(t, d)
    for i in range(depth):
        wmla, wdil, wgate, wuq, wuk, wuv = _prep_mixer_weights(w_in[i], w_uq[i], w_ukv[i])
        q, k, vt, zd0, zd1, zd2, gates = _inproj(
            xc, s, attn_norm[i][None], wmla, wdil, wgate, q_norm[i][None], kv_norm[i][None],
            wuq, wuk, wuv, cos_t, sina_t, sinb_t)
        oa = _mla_attention(q.reshape(b, s, -1), k.reshape(b, s, -1), vt)
        dil = [_dilated_attention(z.reshape(b, s, DIL_COLS), gi) for gi, z in enumerate((zd0, zd1, zd2))]
        x1, hp, topi_t, topw_t, wcol = _merge(
            xc, oa.reshape(t, -1), [o for o, _ in dil], [l for _, l in dil], gates,
            w_branch_a[i].astype(BF16), w_branch_b[i].astype(BF16), w_out[i].astype(BF16),
            ffn_norm[i][None], w_router[i].T.astype(BF16), b_router[i][:, None])
        dest_t, meta = _positions(topi_t)
        ends = meta[2, :N_EXPERTS]
        block_start = jnp.arange(n_blocks, dtype=I32) * ROW_BLOCK
        block_e = jnp.minimum(
            jnp.sum((ends[None, :] <= block_start[:, None]).astype(I32), axis=1), N_EXPERTS - 1)
        n_used = (ends[N_EXPERTS - 1:] // ROW_BLOCK).astype(I32)
        dest_flat = dest_t[:TOP_K].reshape(n_assign)
        xs = _dispatch_rows(hp, dest_flat, n_rows)
        ys = _expert_ffn(xs, block_e, n_used,
                         w_gate[i].astype(BF16), b_gate[i][:, None, :],
                         w_up[i].astype(BF16), b_up[i][:, None, :],
                         w_down[i].astype(BF16), b_down[i][:, None, :])
        yg = _gather_rows(ys, dest_flat).reshape(TOP_K, t, HALF)
        final = i == depth - 1
        gout = final_norm[None] if final else attn_norm[i][None]
        xc = _combine(x1, yg, wcol, p[i].reshape(t, PLE_DIM), ple_norm[i][None],
                      w_ple_gate[i].astype(BF16), w_ple_proj[i].astype(BF16), gout, final)
    return xc.reshape(b, s, d)
```

```python
import functools
import math

import jax
import jax.numpy as jnp
import numpy as np
from jax import lax
from jax.experimental import pallas as pl
from jax.experimental.pallas import tpu as pltpu
from jax.experimental.pallas import tpu_sc as plsc

F32 = jnp.float32
BF16 = jnp.bfloat16
I32 = jnp.int32

D_MODEL = 1024
PLE_DIM = 256
NORM_EPS = 1e-6

MLA_HEADS = 8
MLA_Q_LORA = 384
MLA_KV_LORA = 256
MLA_NOPE = 64
MLA_ROPE = 32
MLA_V = 64
ROPE_THETA = 10000.0
HEAD_PAD = 128
HALF_ROPE = MLA_ROPE // 2

DIL_GROUPS = ((128, 1), (512, 4), (2048, 16))
DIL_HEADS = 4
DIL_HEAD_DIM = 64
DIL_STEPS = 128
DIL_COLS = 3 * DIL_HEADS * DIL_HEAD_DIM
DIL_OUT = DIL_HEADS * DIL_HEAD_DIM

N_EXPERTS = 32
TOP_K = 4
D_EXPERT = 1024
SWIGLU_LIMIT = 7.0
SWIGLU_ALPHA = 1.702
ROW_BLOCK = 256

TOKEN_TILE = 512
ATTN_TQ = 1024
ATTN_TK = 512
ATTN_SUB = 1024
HALF = D_MODEL // 2
NEG = -1e30
SPECULATION_HEADROOM = 60.0
HI_MASK = -65536

SC_CORES = 2
SC_SUBCORES = 16
SC_WORKERS = SC_CORES * SC_SUBCORES
SC_WINDOW = 64

VMEM_LIMIT = 56 * 1024 * 1024


def _cparams(sem):
    return pltpu.CompilerParams(dimension_semantics=sem, vmem_limit_bytes=VMEM_LIMIT)


def _rms(x, g):
    return x * lax.rsqrt(jnp.mean(x * x, axis=-1, keepdims=True) + NORM_EPS) * g


def _dot(a, b):
    return jnp.dot(a, b, preferred_element_type=F32)


def _dot_nt(a, b):
    return lax.dot_general(a, b, (((1,), (1,)), ((), ())), preferred_element_type=F32)


def _pack_halves(lo, hi):
    lo_i = lax.bitcast_convert_type(lo.astype(BF16).astype(F32), I32)
    hi_i = lax.bitcast_convert_type(hi.astype(BF16).astype(F32), I32)
    return (hi_i & HI_MASK) | lax.shift_right_logical(lo_i, 16)


def _unpack_halves(w):
    lo = lax.bitcast_convert_type(lax.shift_left(w, 16), F32)
    hi = lax.bitcast_convert_type(w & HI_MASK, F32)
    return lo, hi


def _inproj_kernel(x_ref, g_ref, wmla_ref, wdil_ref, wgate_ref, qn_ref, kvn_ref, wuq_ref, wuk_ref,
                   wuv_ref, cos_ref, sina_ref, sinb_ref,
                   q_ref, k_ref, vt_ref, zd0_ref, zd1_ref, zd2_ref, gate_ref):
    h = _rms(x_ref[...], g_ref[...]).astype(BF16)
    zm = _dot(h, wmla_ref[...])
    cq = _rms(zm[:, :MLA_Q_LORA], qn_ref[...]).astype(BF16)
    ckv = _rms(zm[:, MLA_Q_LORA:MLA_Q_LORA + MLA_KV_LORA], kvn_ref[...]).astype(BF16)
    kr = zm[:, MLA_Q_LORA + MLA_KV_LORA:]
    cos, sina, sinb = cos_ref[...], sina_ref[...], sinb_ref[...]

    def rope(t):
        return (t * cos + pltpu.roll(t, HALF_ROPE, 1) * sina
                + pltpu.roll(t, HEAD_PAD - HALF_ROPE, 1) * sinb)

    kr_rot = rope(kr)
    qraw = _dot(cq, wuq_ref[...])
    kraw = _dot(ckv, wuk_ref[...])
    vt_ref[...] = _dot(ckv, wuv_ref[...]).T.astype(BF16)
    scale = (MLA_NOPE + MLA_ROPE) ** -0.5 * math.log2(math.e)
    for hd in range(MLA_HEADS):
        sl = slice(hd * HEAD_PAD, (hd + 1) * HEAD_PAD)
        q_ref[:, sl] = (rope(qraw[:, sl]) * scale).astype(BF16)
        k_ref[:, sl] = (kraw[:, sl] + kr_rot).astype(BF16)
    for gi, zd_ref in enumerate((zd0_ref, zd1_ref, zd2_ref)):
        zd_ref[...] = _dot(h, wdil_ref[:, gi * DIL_COLS:(gi + 1) * DIL_COLS]).astype(BF16)
    for c in range(2):
        sl = slice(c * D_MODEL, (c + 1) * D_MODEL)
        gate_ref[:, sl] = jax.nn.sigmoid(_dot(h, wgate_ref[:, sl])).astype(BF16)


def _inproj(x2d, seq, g, wmla, wdil, wgate, qn, kvn, wuq, wuk, wuv, cos_t, sina_t, sinb_t):
    t = x2d.shape[0]
    tm = TOKEN_TILE
    n_seq_tiles = seq // tm

    def row(i):
        return (i, 0)

    def const(i):
        return (0, 0)

    def pos(i):
        return (i % n_seq_tiles, 0)

    def full(a):
        return pl.BlockSpec(a.shape, const)

    def vt_block(i):
        return (i // n_seq_tiles, 0, i % n_seq_tiles)

    out_shape = [
        jax.ShapeDtypeStruct((t, MLA_HEADS * HEAD_PAD), BF16),
        jax.ShapeDtypeStruct((t, MLA_HEADS * HEAD_PAD), BF16),
        jax.ShapeDtypeStruct((t // seq, MLA_HEADS * MLA_V, seq), BF16),
        jax.ShapeDtypeStruct((t, DIL_COLS), BF16),
        jax.ShapeDtypeStruct((t, DIL_COLS), BF16),
        jax.ShapeDtypeStruct((t, DIL_COLS), BF16),
        jax.ShapeDtypeStruct((t, 2 * D_MODEL), BF16),
    ]
    return pl.pallas_call(
        _inproj_kernel,
        out_shape=out_shape,
        grid=(t // tm,),
        in_specs=[pl.BlockSpec((tm, D_MODEL), row), full(g), full(wmla), full(wdil), full(wgate),
                  full(qn), full(kvn), full(wuq), full(wuk), full(wuv),
                  pl.BlockSpec((tm, HEAD_PAD), pos), pl.BlockSpec((tm, HEAD_PAD), pos),
                  pl.BlockSpec((tm, HEAD_PAD), pos)],
        out_specs=[pl.BlockSpec((None, MLA_HEADS * MLA_V, tm), vt_block) if len(s.shape) == 3
                   else pl.BlockSpec((tm, s.shape[1]), row) for s in out_shape],
        compiler_params=_cparams(("parallel",)),
        name="inproj",
    )(x2d, g, wmla, wdil, wgate, qn, kvn, wuq, wuk, wuv, cos_t, sina_t, sinb_t)


def _mla_kernel(qi_ref, kj_ref, q_ref, k_ref, vt_ref, o_ref, m_sc, l_sc, acc_sc):
    p = pl.program_id(2)
    i = qi_ref[p]
    j = kj_ref[p]
    tq = q_ref.shape[0]
    tk = k_ref.shape[0]

    @pl.when(j == 0)
    def _():
        m_sc[...] = jnp.full(m_sc.shape, NEG, F32)
        l_sc[...] = jnp.zeros(l_sc.shape, F32)
        acc_sc[...] = jnp.zeros(acc_sc.shape, F32)

    ratio = tq // tk
    sub = ATTN_SUB

    def step(diagonal, speculative):
        chains = [(hh, c) for hh in range(2) for c in range(tq // sub)]
        ones_rows = (lax.broadcasted_iota(I32, (16, tk), 0) == 0).astype(BF16)
        state = {}
        for hh, c in chains:
            cs = slice(c * sub, (c + 1) * sub)
            state[hh, c] = (m_sc[hh, :, cs], l_sc[hh, :, cs], acc_sc[hh, :, cs])
        new_state = {}
        within = None
        for hh, c in chains:
            sl = slice(hh * HEAD_PAD, (hh + 1) * HEAD_PAD)
            st = _dot_nt(k_ref[:, sl], q_ref[c * sub:(c + 1) * sub, sl])
            if diagonal:
                key = lax.broadcasted_iota(I32, (tk, sub), 0) + j * tk
                qry = lax.broadcasted_iota(I32, (tk, sub), 1) + (i * tq + c * sub)
                st = jnp.where(qry >= key, st, NEG)
            m_prev, l_prev, acc_prev = state[hh, c]
            m_blk = jnp.max(st, axis=0, keepdims=True)
            m_new = jnp.maximum(m_prev, m_blk)
            alpha = jnp.exp2(m_prev - m_new)
            lhs = jnp.concatenate([vt_ref[hh * MLA_V:(hh + 1) * MLA_V, :], ones_rows], axis=0)
            if speculative:
                pv = _dot(lhs, jnp.exp2(st - m_prev).astype(BF16))
                l_new = alpha * (l_prev + pv[MLA_V:MLA_V + 1])
                acc_new = alpha * (acc_prev + pv[:MLA_V])
                ok = jnp.max(m_blk - m_prev) <= SPECULATION_HEADROOM
                within = ok if within is None else (within & ok)
            else:
                pv = _dot(lhs, jnp.exp2(st - m_new).astype(BF16))
                l_new = alpha * l_prev + pv[MLA_V:MLA_V + 1]
                acc_new = alpha * acc_prev + pv[:MLA_V]
            new_state[hh, c] = (m_new, l_new, acc_new)

        def commit():
            for hh, c in chains:
                cs = slice(c * sub, (c + 1) * sub)
                m_sc[hh, :, cs], l_sc[hh, :, cs], acc_sc[hh, :, cs] = new_state[hh, c]

        if speculative:
            pl.when(within)(commit)
            pl.when(jnp.logical_not(within))(lambda: step(True, False))
        else:
            commit()

    @pl.when(j == 0)
    def _():
        step(True, False)

    @pl.when((j > 0) & (j < ratio * i))
    def _():
        step(False, True)

    @pl.when((j > 0) & (j >= ratio * i))
    def _():
        step(True, True)

    @pl.when(j == ratio * i + (ratio - 1))
    def _():
        ot = jnp.concatenate([acc_sc[0] / l_sc[0], acc_sc[1] / l_sc[1]], axis=0)
        o_ref[...] = ot.T.astype(BF16)


def _mla_attention(q, k, vt):
    b, s, _ = q.shape
    tq, tk = ATTN_TQ, ATTN_TK
    ratio = tq // tk
    nq = s // tq
    pairs = [(i, j) for i in range(nq) for j in range(ratio * (i + 1))]
    qi = jnp.asarray([p[0] for p in pairs], I32)
    kj = jnp.asarray([p[1] for p in pairs], I32)
    grid_spec = pltpu.PrefetchScalarGridSpec(
        num_scalar_prefetch=2,
        grid=(b, MLA_HEADS // 2, len(pairs)),
        in_specs=[
            pl.BlockSpec((None, tq, 2 * HEAD_PAD), lambda bb, hp, p, qi, kj: (bb, qi[p], hp)),
            pl.BlockSpec((None, tk, 2 * HEAD_PAD), lambda bb, hp, p, qi, kj: (bb, kj[p], hp)),
            pl.BlockSpec((None, 2 * MLA_V, tk), lambda bb, hp, p, qi, kj: (bb, hp, kj[p])),
        ],
        out_specs=pl.BlockSpec((None, tq, 2 * MLA_V), lambda bb, hp, p, qi, kj: (bb, qi[p], hp)),
        scratch_shapes=[pltpu.VMEM((2, 1, tq), F32), pltpu.VMEM((2, 1, tq), F32),
                        pltpu.VMEM((2, MLA_V, tq), F32)],
    )
    return pl.pallas_call(
        _mla_kernel,
        out_shape=jax.ShapeDtypeStruct((b, s, MLA_HEADS * MLA_V), BF16),
        grid_spec=grid_spec,
        compiler_params=_cparams(("parallel", "parallel", "arbitrary")),
        name="mla_attention",
    )(qi, kj, q, k, vt)


def _alibi_slopes(n):
    def pow2(m):
        start = 2.0 ** (-8.0 / m)
        return [start ** (i + 1) for i in range(m)]
    if math.log2(n).is_integer():
        s = pow2(n)
    else:
        c = 2 ** int(math.floor(math.log2(n)))
        s = pow2(c) + pow2(2 * c)[0::2][: n - c]
    return np.array(sorted(s, reverse=True), dtype=np.float32)


def _dilated_kernel(cur_ref, prev_ref, o_ref, lse_ref, *, step_slopes):
    mi = pl.program_id(2)
    n = DIL_STEPS
    hw = DIL_HEADS * DIL_HEAD_DIM
    cur = cur_ref[...]
    prev = prev_ref[...]
    qi = lax.broadcasted_iota(I32, (n, 2 * n), 0)
    ki = lax.broadcasted_iota(I32, (n, 2 * n), 1)
    dist = qi + n - ki
    valid = (dist >= 0) & (dist <= n) & ((ki >= n) | (mi > 0))
    distf = dist.astype(F32)
    lane = lax.broadcasted_iota(I32, (n, 2 * DIL_HEAD_DIM), 1)
    low = lane < DIL_HEAD_DIM
    scale = DIL_HEAD_DIM ** -0.5
    for pair in range(DIL_HEADS // 2):
        sl = slice(pair * 2 * DIL_HEAD_DIM, (pair + 1) * 2 * DIL_HEAD_DIM)
        q = cur[:, sl]
        kk = jnp.concatenate([prev[:, hw:2 * hw][:, sl], cur[:, hw:2 * hw][:, sl]], axis=0)
        vv = jnp.concatenate([prev[:, 2 * hw:][:, sl], cur[:, 2 * hw:][:, sl]], axis=0)
        outs, lses = [], []
        for hh in range(2):
            keep = low if hh == 0 else jnp.logical_not(low)
            qm = jnp.where(keep, q, jnp.zeros_like(q))
            s = _dot_nt(qm, kk) * scale - step_slopes[pair * 2 + hh] * distf
            s = jnp.where(valid, s, NEG)
            m = jnp.max(s, axis=1, keepdims=True)
            pe = jnp.exp(s - m)
            l = jnp.sum(pe, axis=1, keepdims=True)
            outs.append(_dot(pe.astype(BF16), vv) / l)
            lses.append(m + jnp.log(l))
        o_ref[:, sl] = jnp.where(low, outs[0], outs[1])
        lse_ref[:, sl] = jnp.where(low, lses[0], lses[1])


def _dilated_attention(zd, gi):
    window, dil = DIL_GROUPS[gi]
    b, s, _ = zd.shape
    n_sub = s // dil
    nb = n_sub // DIL_STEPS
    slopes = _alibi_slopes(len(DIL_GROUPS) * DIL_HEADS).reshape(len(DIL_GROUPS), DIL_HEADS)[gi] * dil
    view = zd.reshape(b, n_sub, dil * DIL_COLS)
    o, lse = pl.pallas_call(
        functools.partial(_dilated_kernel, step_slopes=tuple(float(x) for x in slopes)),
        out_shape=[jax.ShapeDtypeStruct((b, n_sub, dil * DIL_OUT), F32)] * 2,
        grid=(b, dil, nb),
        in_specs=[pl.BlockSpec((None, DIL_STEPS, DIL_COLS), lambda bb, r, m: (bb, m, r)),
                  pl.BlockSpec((None, DIL_STEPS, DIL_COLS),
                               lambda bb, r, m: (bb, jnp.maximum(m - 1, 0), r))],
        out_specs=[pl.BlockSpec((None, DIL_STEPS, DIL_OUT), lambda bb, r, m: (bb, m, r))] * 2,
        compiler_params=_cparams(("parallel", "parallel", "arbitrary")),
        name=f"dilated_attention_{gi}",
    )(view, view)
    return o.reshape(b * s, DIL_OUT), lse.reshape(b * s, DIL_OUT)


def _merge_kernel(x_ref, oa_ref, o0_ref, o1_ref, o2_ref, l0_ref, l1_ref, l2_ref, gate_ref,
                  wa_ref, wb_ref, wo_ref, g_ref, wr_ref, br_ref,
                  x1_ref, hp_ref, topi_ref, topw_ref, wcol_ref):
    tm = x_ref.shape[0]
    l0, l1, l2 = l0_ref[...], l1_ref[...], l2_ref[...]
    lmax = jnp.maximum(jnp.maximum(l0, l1), l2)
    e0, e1, e2 = jnp.exp(l0 - lmax), jnp.exp(l1 - lmax), jnp.exp(l2 - lmax)
    ob = (e0 * o0_ref[...] + e1 * o1_ref[...] + e2 * o2_ref[...]) / (e0 + e1 + e2)
    ya = _dot(oa_ref[...], wa_ref[...])
    yb = _dot(ob.astype(BF16), wb_ref[...])
    mixed = gate_ref[:, :D_MODEL].astype(F32) * ya + gate_ref[:, D_MODEL:].astype(F32) * yb
    x1 = x_ref[...] + _dot(mixed.astype(BF16), wo_ref[...])
    x1_ref[...] = x1
    h2 = _rms(x1, g_ref[...])
    hp_ref[...] = _pack_halves(h2[:, :HALF], h2[:, HALF:])

    logits = _dot_nt(wr_ref[...], h2.astype(BF16)) + br_ref[...]
    eidx = lax.broadcasted_iota(I32, (N_EXPERTS, tm), 0)
    vals, idxs = [], []
    for _ in range(TOP_K):
        m = jnp.max(logits, axis=0, keepdims=True)
        idx = jnp.min(jnp.where(logits == m, eidx, N_EXPERTS), axis=0, keepdims=True)
        vals.append(m)
        idxs.append(idx)
        logits = jnp.where(eidx == idx, -jnp.inf, logits)
    exps = [jnp.exp(vk - vals[0]) for vk in vals]
    den = exps[0] + exps[1] + exps[2] + exps[3]
    row8 = lax.broadcasted_iota(I32, (8, tm), 0)
    row128 = lax.broadcasted_iota(I32, (HEAD_PAD, tm), 0)
    topi = jnp.zeros((8, tm), I32)
    topw = jnp.zeros((8, tm), F32)
    wide = jnp.zeros((HEAD_PAD, tm), F32)
    for kk in range(TOP_K):
        wk = exps[kk] / den
        topi = jnp.where(row8 == kk, idxs[kk], topi)
        topw = jnp.where(row8 == kk, wk, topw)
        wide = jnp.where(row128 == kk, wk, wide)
    topi_ref[...] = topi
    topw_ref[...] = topw
    wcol_ref[...] = wide.T


def _merge(x2d, oa, obs, lses, gates, wa, wb, wo, g, wr_t, br_col):
    t = x2d.shape[0]
    tm = TOKEN_TILE

    def row(i):
        return (i, 0)

    def col(i):
        return (0, i)

    def full(a):
        return pl.BlockSpec(a.shape, lambda i: (0, 0))

    def rows(width):
        return pl.BlockSpec((tm, width), row)

    out_shape = [
        jax.ShapeDtypeStruct((t, D_MODEL), F32),
        jax.ShapeDtypeStruct((t, HALF), I32),
        jax.ShapeDtypeStruct((8, t), I32),
        jax.ShapeDtypeStruct((8, t), F32),
        jax.ShapeDtypeStruct((t, HEAD_PAD), F32),
    ]
    return pl.pallas_call(
        _merge_kernel,
        out_shape=out_shape,
        grid=(t // tm,),
        in_specs=[rows(D_MODEL), rows(MLA_HEADS * MLA_V)] + [rows(DIL_OUT)] * 6 + [rows(2 * D_MODEL)]
        + [full(wa), full(wb), full(wo), full(g), full(wr_t), full(br_col)],
        out_specs=[rows(D_MODEL), rows(HALF), pl.BlockSpec((8, tm), col), pl.BlockSpec((8, tm), col),
                   rows(HEAD_PAD)],
        compiler_params=_cparams(("parallel",)),
        name="merge_router",
    )(x2d, oa, *obs, *lses, gates, wa, wb, wo, g, wr_t, br_col)


def _positions_kernel(topi_ref, dest_ref, meta_ref, cnt_sc, carry_sc, start_sc):
    ps = pl.program_id(0)
    i = pl.program_id(1)
    tm = topi_ref.shape[1]
    eidx = lax.broadcasted_iota(I32, (N_EXPERTS, tm), 0)
    topi = topi_ref[...]
    hits = [eidx == topi[kk:kk + 1, :] for kk in range(TOP_K)]
    member = (hits[0] | hits[1] | hits[2] | hits[3])
    tile_cnt = jnp.sum(member.astype(F32), axis=1, keepdims=True)

    @pl.when((ps == 0) & (i == 0))
    def _():
        cnt_sc[...] = jnp.zeros(cnt_sc.shape, F32)

    @pl.when(ps == 0)
    def _():
        cnt_sc[...] += tile_cnt

    @pl.when((ps == 1) & (i == 0))
    def _():
        cnt = cnt_sc[...].astype(I32)
        padded = lax.shift_left(lax.shift_right_logical(cnt + (ROW_BLOCK - 1), 8), 8)
        sub = lax.broadcasted_iota(I32, (N_EXPERTS, HEAD_PAD), 0)
        lane = lax.broadcasted_iota(I32, (N_EXPERTS, HEAD_PAD), 1)
        padded_row = jnp.sum(jnp.where(sub == lane, padded, 0), axis=0, keepdims=True)
        start = jnp.sum(jnp.where(lane < sub, padded_row, 0), axis=1, keepdims=True)
        start_sc[...] = start.astype(F32)
        carry_sc[...] = jnp.zeros(carry_sc.shape, F32)
        cnt_row = jnp.sum(jnp.where(sub == lane, cnt, 0), axis=0, keepdims=True)
        start_row = jnp.sum(jnp.where(sub == lane, start, 0), axis=0, keepdims=True)
        row8 = lax.broadcasted_iota(I32, (8, HEAD_PAD), 0)
        meta = jnp.where(row8 == 0, cnt_row, 0)
        meta = jnp.where(row8 == 1, start_row, meta)
        meta = jnp.where(row8 == 2, start_row + padded_row, meta)
        meta_ref[...] = meta

    @pl.when(ps == 1)
    def _():
        tr = lax.broadcasted_iota(I32, (tm, tm), 0)
        tc = lax.broadcasted_iota(I32, (tm, tm), 1)
        before = (tr < tc).astype(BF16)
        prefix = _dot(member.astype(BF16), before)
        base = prefix + carry_sc[...] + start_sc[...]
        row8 = lax.broadcasted_iota(I32, (8, tm), 0)
        dest = jnp.zeros((8, tm), I32)
        for kk in range(TOP_K):
            dk = jnp.sum(jnp.where(hits[kk], base, 0.0), axis=0, keepdims=True).astype(I32)
            dest = jnp.where(row8 == kk, dk, dest)
        dest_ref[...] = dest
        carry_sc[...] += tile_cnt


def _positions(topi_t):
    t = topi_t.shape[1]
    tm = TOKEN_TILE
    return pl.pallas_call(
        _positions_kernel,
        out_shape=[jax.ShapeDtypeStruct((8, t), I32), jax.ShapeDtypeStruct((8, HEAD_PAD), I32)],
        grid=(2, t // tm),
        in_specs=[pl.BlockSpec((8, tm), lambda ps, i: (0, i))],
        out_specs=[pl.BlockSpec((8, tm), lambda ps, i: (0, i * ps)),
                   pl.BlockSpec((8, HEAD_PAD), lambda ps, i: (0, 0))],
        scratch_shapes=[pltpu.VMEM((N_EXPERTS, 1), F32)] * 3,
        compiler_params=_cparams(("arbitrary", "arbitrary")),
        name="routing_positions",
    )(topi_t)


def _sc_mesh():
    return plsc.VectorSubcoreMesh(core_axis_name="c", subcore_axis_name="s")


def _dispatch_rows(table, dest_flat, n_rows):
    t, c = table.shape
    n_slots = dest_flat.shape[0] // t
    per_w = t // SC_WORKERS
    assert per_w * SC_WORKERS == t and per_w % SC_WINDOW == 0
    n_chunks = per_w // SC_WINDOW

    @functools.partial(
        pl.kernel, mesh=_sc_mesh(),
        out_type=jax.ShapeDtypeStruct((n_rows, c), table.dtype),
        scratch_types=[pltpu.VMEM((SC_WINDOW,), I32), pltpu.VMEM((SC_WINDOW, c), table.dtype),
                       pltpu.SemaphoreType.DMA],
        name="dispatch_rows",
    )
    def k(table_hbm, dest_hbm, out_hbm, idx_v, rows_v, sem):
        wid = lax.axis_index("s") * SC_CORES + lax.axis_index("c")
        base = wid * per_w

        @pl.loop(0, n_chunks)
        def _(i):
            off = pl.multiple_of(base + i * SC_WINDOW, SC_WINDOW)
            pltpu.sync_copy(table_hbm.at[pl.ds(off, SC_WINDOW)], rows_v)
            for kk in range(n_slots):
                src = pl.multiple_of(kk * t + off, SC_WINDOW)
                pltpu.sync_copy(dest_hbm.at[pl.ds(src, SC_WINDOW)], idx_v)
                pltpu.async_copy(rows_v, out_hbm.at[idx_v], sem).wait()

    return k(table, dest_flat)


def _gather_rows(table, idx):
    n = idx.shape[0]
    c = table.shape[1]
    per_w = n // SC_WORKERS
    assert per_w * SC_WORKERS == n and per_w % SC_WINDOW == 0
    n_chunks = per_w // SC_WINDOW

    @functools.partial(
        pl.kernel, mesh=_sc_mesh(),
        out_type=jax.ShapeDtypeStruct((n, c), table.dtype),
        scratch_types=[pltpu.VMEM((SC_WINDOW,), I32), pltpu.VMEM((SC_WINDOW, c), table.dtype),
                       pltpu.SemaphoreType.DMA],
        name="gather_rows",
    )
    def k(table_hbm, idx_hbm, out_hbm, idx_v, rows_v, sem):
        wid = lax.axis_index("s") * SC_CORES + lax.axis_index("c")
        base = wid * per_w

        @pl.loop(0, n_chunks)
        def _(i):
            off = pl.multiple_of(base + i * SC_WINDOW, SC_WINDOW)
            pltpu.sync_copy(idx_hbm.at[pl.ds(off, SC_WINDOW)], idx_v)
            pltpu.async_copy(table_hbm.at[idx_v], rows_v, sem).wait()
            pltpu.sync_copy(rows_v, out_hbm.at[pl.ds(off, SC_WINDOW)])

    return k(table, idx)


def _expert_kernel(be_ref, nused_ref, xs_ref, wg32_ref, bg_ref, wu32_ref, bu_ref, wd32_ref, bd_ref,
                   ys_ref, wg_ref, wu_ref, wd_ref):
    b = pl.program_id(0)
    new_expert = (b == 0) | (be_ref[b] != be_ref[jnp.maximum(b - 1, 0)])

    @pl.when(new_expert & (b < nused_ref[0]))
    def _():
        wg_ref[...] = wg32_ref[...].astype(BF16)
        wu_ref[...] = wu32_ref[...].astype(BF16)
        wd_ref[...] = wd32_ref[...].astype(BF16)

    @pl.when(b < nused_ref[0])
    def _():
        lo, hi = _unpack_halves(xs_ref[...])
        lo = lo.astype(BF16)
        hi = hi.astype(BF16)
        a = _dot(lo, wg_ref[:HALF, :]) + _dot(hi, wg_ref[HALF:, :]) + bg_ref[...]
        u = _dot(lo, wu_ref[:HALF, :]) + _dot(hi, wu_ref[HALF:, :]) + bu_ref[...]
        a = jnp.minimum(a, SWIGLU_LIMIT)
        u = jnp.clip(u, -SWIGLU_LIMIT, SWIGLU_LIMIT)
        y = (a * jax.nn.sigmoid(SWIGLU_ALPHA * a)) * (u + 1.0)
        out = _dot(y.astype(BF16), wd_ref[...]) + bd_ref[...]
        ys_ref[...] = _pack_halves(out[:, :HALF], out[:, HALF:])

    @pl.when(b >= nused_ref[0])
    def _():
        ys_ref[...] = jnp.zeros(ys_ref.shape, I32)


def _expert_ffn(xs, block_e, n_used, layer, wg, bg, wu, bu, wd, bd):
    n_rows = xs.shape[0]
    n_blocks = n_rows // ROW_BLOCK

    def rows(b, be, nu):
        return (b, 0)

    def expert(b, be, nu):
        return (layer, be[b], 0, 0)

    grid_spec = pltpu.PrefetchScalarGridSpec(
        num_scalar_prefetch=2,
        grid=(n_blocks,),
        in_specs=[pl.BlockSpec((ROW_BLOCK, HALF), rows),
                  pl.BlockSpec((None, None, D_MODEL, D_EXPERT), expert),
                  pl.BlockSpec((None, None, 1, D_EXPERT), expert),
                  pl.BlockSpec((None, None, D_MODEL, D_EXPERT), expert),
                  pl.BlockSpec((None, None, 1, D_EXPERT), expert),
                  pl.BlockSpec((None, None, D_EXPERT, D_MODEL), expert),
                  pl.BlockSpec((None, None, 1, D_MODEL), expert)],
        out_specs=pl.BlockSpec((ROW_BLOCK, HALF), rows),
        scratch_shapes=[pltpu.VMEM((D_MODEL, D_EXPERT), BF16), pltpu.VMEM((D_MODEL, D_EXPERT), BF16),
                        pltpu.VMEM((D_EXPERT, D_MODEL), BF16)],
    )
    return pl.pallas_call(
        _expert_kernel,
        out_shape=jax.ShapeDtypeStruct((n_rows, HALF), I32),
        grid_spec=grid_spec,
        compiler_params=_cparams(("arbitrary",)),
        name="expert_ffn",
    )(block_e, n_used, xs, wg, bg, wu, bu, wd, bd)


def _combine_kernel(x1_ref, yg_ref, wcol_ref, p_ref, gple_ref, wpg_ref, wpp_ref, gout_ref, o_ref,
                    *, final):
    x1 = x1_ref[...]
    acc_lo = x1[:, :HALF]
    acc_hi = x1[:, HALF:]
    wcol = wcol_ref[...]
    for kk in range(TOP_K):
        lo, hi = _unpack_halves(yg_ref[kk])
        wk = wcol[:, kk:kk + 1]
        acc_lo = acc_lo + wk * lo
        acc_hi = acc_hi + wk * hi
    x2 = jnp.concatenate([acc_lo, acc_hi], axis=1)
    gate = jax.nn.sigmoid(_dot(_rms(x2, gple_ref[...]).astype(BF16), wpg_ref[...]))
    x3 = x2 + gate * _dot(p_ref[...].astype(BF16), wpp_ref[...])
    o_ref[...] = _rms(x3, gout_ref[...]) if final else x3


def _combine(x1, yg, wcol, p3d, layer, gple, wpg, wpp, gout, final):
    t = x1.shape[0]
    tm = TOKEN_TILE

    def row(i):
        return (i, 0)

    def full(a):
        return pl.BlockSpec(a.shape, lambda i: (0, 0))

    return pl.pallas_call(
        functools.partial(_combine_kernel, final=final),
        out_shape=jax.ShapeDtypeStruct((t, D_MODEL), F32),
        grid=(t // tm,),
        in_specs=[pl.BlockSpec((tm, D_MODEL), row),
                  pl.BlockSpec((TOP_K, tm, HALF), lambda i: (0, i, 0)),
                  pl.BlockSpec((tm, HEAD_PAD), row),
                  pl.BlockSpec((None, tm, PLE_DIM), lambda i: (layer, i, 0)),
                  full(gple), full(wpg), full(wpp), full(gout)],
        out_specs=pl.BlockSpec((tm, D_MODEL), row),
        compiler_params=_cparams(("parallel",)),
        name="combine_ple",
    )(x1, yg, wcol, p3d, gple, wpg, wpp, gout)


def _rope_tables(seq):
    inv_freq = ROPE_THETA ** (-jnp.arange(HALF_ROPE, dtype=F32) * 2.0 / MLA_ROPE)
    ang = jnp.arange(seq, dtype=F32)[:, None] * inv_freq[None, :]
    cos, sin = jnp.cos(ang), jnp.sin(ang)
    ones = jnp.ones((seq, MLA_NOPE), F32)
    zeros16 = jnp.zeros((seq, HALF_ROPE), F32)
    zeros64 = jnp.zeros((seq, MLA_NOPE), F32)
    tail = jnp.ones((seq, HEAD_PAD - MLA_NOPE - MLA_ROPE), F32)
    ztail = jnp.zeros_like(tail)
    cos_t = jnp.concatenate([ones, cos, cos, tail], axis=1)
    sina_t = jnp.concatenate([zeros64, zeros16, sin, ztail], axis=1)
    sinb_t = jnp.concatenate([zeros64, -sin, zeros16, ztail], axis=1)
    return cos_t, sina_t, sinb_t


def _prep_mixer_weights(w_in, w_uq, w_ukv):
    c0 = MLA_Q_LORA + MLA_KV_LORA
    c1 = c0 + MLA_ROPE
    c2 = c1 + len(DIL_GROUPS) * DIL_COLS
    kr_pad = jnp.zeros((D_MODEL, HEAD_PAD), F32).at[:, MLA_NOPE:MLA_NOPE + MLA_ROPE].set(w_in[:, c0:c1])
    wmla = jnp.concatenate([w_in[:, :c0], kr_pad], axis=1).astype(BF16)
    wdil = w_in[:, c1:c2].astype(BF16)
    wgate = w_in[:, c2:].astype(BF16)
    pad = HEAD_PAD - MLA_NOPE - MLA_ROPE
    wuq_h = w_uq.reshape(MLA_Q_LORA, MLA_HEADS, MLA_NOPE + MLA_ROPE)
    wuq = jnp.pad(wuq_h, ((0, 0), (0, 0), (0, pad))).reshape(MLA_Q_LORA, MLA_HEADS * HEAD_PAD).astype(BF16)
    wukv_h = w_ukv.reshape(MLA_KV_LORA, MLA_HEADS, MLA_NOPE + MLA_V)
    wuk = jnp.pad(wukv_h[:, :, :MLA_NOPE], ((0, 0), (0, 0), (0, HEAD_PAD - MLA_NOPE)))
    wuk = wuk.reshape(MLA_KV_LORA, MLA_HEADS * HEAD_PAD).astype(BF16)
    wuv = wukv_h[:, :, MLA_NOPE:].reshape(MLA_KV_LORA, MLA_HEADS * MLA_V).astype(BF16)
    return wmla, wdil, wgate, wuq, wuk, wuv


def kernel(x, p, attn_norm, w_in, q_norm, w_uq, kv_norm, w_ukv, w_branch_a, w_branch_b, w_out, ffn_norm, w_router, b_router, w_gate, b_gate, w_up, b_up, w_down, b_down, ple_norm, w_ple_gate, w_ple_proj, final_norm):
    b, s, d = x.shape
    depth = w_in.shape[0]
    t = b * s
    assert d == D_MODEL and s % (DIL_GROUPS[-1][0]) == 0 and t % (SC_WORKERS * SC_WINDOW) == 0
    n_assign = t * TOP_K
    n_blocks = -(-(n_assign + N_EXPERTS * (ROW_BLOCK - 1)) // ROW_BLOCK)
    n_rows = n_blocks * ROW_BLOCK
    cos_t, sina_t, sinb_t = _rope_tables(s)
    xc = x.reshape(t, d)
    for i in range(depth):
        wmla, wdil, wgate, wuq, wuk, wuv = _prep_mixer_weights(w_in[i], w_uq[i], w_ukv[i])
        q, k, vt, zd0, zd1, zd2, gates = _inproj(
            xc, s, attn_norm[i][None], wmla, wdil, wgate, q_norm[i][None], kv_norm[i][None],
            wuq, wuk, wuv, cos_t, sina_t, sinb_t)
        oa = _mla_attention(q.reshape(b, s, -1), k.reshape(b, s, -1), vt)
        dil = [_dilated_attention(z.reshape(b, s, DIL_COLS), gi) for gi, z in enumerate((zd0, zd1, zd2))]
        x1, hp, topi_t, topw_t, wcol = _merge(
            xc, oa.reshape(t, -1), [o for o, _ in dil], [l for _, l in dil], gates,
            w_branch_a[i].astype(BF16), w_branch_b[i].astype(BF16), w_out[i].astype(BF16),
            ffn_norm[i][None], w_router[i].T.astype(BF16), b_router[i][:, None])
        dest_t, meta = _positions(topi_t)
        ends = meta[2, :N_EXPERTS]
        block_start = jnp.arange(n_blocks, dtype=I32) * ROW_BLOCK
        block_e = jnp.minimum(
            jnp.sum((ends[None, :] <= block_start[:, None]).astype(I32), axis=1), N_EXPERTS - 1)
        n_used = (ends[N_EXPERTS - 1:] // ROW_BLOCK).astype(I32)
        dest_flat = dest_t[:TOP_K].reshape(n_assign)
        xs = _dispatch_rows(hp, dest_flat, n_rows)
        ys = _expert_ffn(xs, block_e, n_used, i,
                         w_gate, b_gate[:, :, None, :], w_up, b_up[:, :, None, :],
                         w_down, b_down[:, :, None, :])
        yg = _gather_rows(ys, dest_flat).reshape(TOP_K, t, HALF)
        final = i == depth - 1
        gout = final_norm[None] if final else attn_norm[i][None]
        xc = _combine(x1, yg, wcol, p.reshape(depth, t, PLE_DIM), i, ple_norm[i][None],
                      w_ple_gate[i].astype(BF16), w_ple_proj[i].astype(BF16), gout, final)
    return xc.reshape(b, s, d)
```

```python
import functools
import math

import jax
import jax.numpy as jnp
import numpy as np
from jax import lax
from jax.experimental import pallas as pl
from jax.experimental.pallas import tpu as pltpu
from jax.experimental.pallas import tpu_sc as plsc

F32 = jnp.float32
BF16 = jnp.bfloat16
I32 = jnp.int32

D_MODEL = 1024
PLE_DIM = 256
NORM_EPS = 1e-6

MLA_HEADS = 8
MLA_Q_LORA = 384
MLA_KV_LORA = 256
MLA_NOPE = 64
MLA_ROPE = 32
MLA_V = 64
ROPE_THETA = 10000.0
HEAD_PAD = 128
HALF_ROPE = MLA_ROPE // 2

DIL_GROUPS = ((128, 1), (512, 4), (2048, 16))
DIL_HEADS = 4
DIL_HEAD_DIM = 64
DIL_STEPS = 128
DIL_COLS = 3 * DIL_HEADS * DIL_HEAD_DIM
DIL_OUT = DIL_HEADS * DIL_HEAD_DIM
DIL_UNROLL = 4

N_EXPERTS = 32
TOP_K = 4
D_EXPERT = 1024
SWIGLU_LIMIT = 7.0
SWIGLU_ALPHA = 1.702
ROW_BLOCK = 256

TOKEN_TILE = 512
ATTN_TQ = 1024
ATTN_TK = 512
ATTN_SUB = 1024
HALF = D_MODEL // 2
NEG = -1e30
SPECULATION_HEADROOM = 60.0
HI_MASK = -65536

SC_CORES = 2
SC_SUBCORES = 16
SC_WORKERS = SC_CORES * SC_SUBCORES
SC_WINDOW = 64

VMEM_LIMIT = 56 * 1024 * 1024


def _cparams(sem):
    return pltpu.CompilerParams(dimension_semantics=sem, vmem_limit_bytes=VMEM_LIMIT)


def _rms(x, g):
    return x * lax.rsqrt(jnp.mean(x * x, axis=-1, keepdims=True) + NORM_EPS) * g


def _dot(a, b):
    return jnp.dot(a, b, preferred_element_type=F32)


def _dot_nt(a, b):
    return lax.dot_general(a, b, (((1,), (1,)), ((), ())), preferred_element_type=F32)


def _pack_halves(lo, hi):
    lo_i = lax.bitcast_convert_type(lo.astype(BF16).astype(F32), I32)
    hi_i = lax.bitcast_convert_type(hi.astype(BF16).astype(F32), I32)
    return (hi_i & HI_MASK) | lax.shift_right_logical(lo_i, 16)


def _unpack_halves(w):
    lo = lax.bitcast_convert_type(lax.shift_left(w, 16), F32)
    hi = lax.bitcast_convert_type(w & HI_MASK, F32)
    return lo, hi


def _inproj_kernel(x_ref, g_ref, wmla_ref, wdil_ref, wgate_ref, qn_ref, kvn_ref, wuq_ref, wuk_ref,
                   wuv_ref, cos_ref, sina_ref, sinb_ref,
                   q_ref, k_ref, vt_ref, zd0_ref, zd1_ref, zd2_ref, gate_ref, zs_sc):
    h = _rms(x_ref[...], g_ref[...]).astype(BF16)
    zm = _dot(h, wmla_ref[...])
    cq = _rms(zm[:, :MLA_Q_LORA], qn_ref[...]).astype(BF16)
    ckv = _rms(zm[:, MLA_Q_LORA:MLA_Q_LORA + MLA_KV_LORA], kvn_ref[...]).astype(BF16)
    kr = zm[:, MLA_Q_LORA + MLA_KV_LORA:]
    cos, sina, sinb = cos_ref[...], sina_ref[...], sinb_ref[...]

    def rope(t):
        return (t * cos + pltpu.roll(t, HALF_ROPE, 1) * sina
                + pltpu.roll(t, HEAD_PAD - HALF_ROPE, 1) * sinb)

    kr_rot = rope(kr)
    qraw = _dot(cq, wuq_ref[...])
    kraw = _dot(ckv, wuk_ref[...])
    vt_ref[...] = _dot(ckv, wuv_ref[...]).T.astype(BF16)
    scale = (MLA_NOPE + MLA_ROPE) ** -0.5 * math.log2(math.e)
    for hd in range(MLA_HEADS):
        sl = slice(hd * HEAD_PAD, (hd + 1) * HEAD_PAD)
        q_ref[:, sl] = (rope(qraw[:, sl]) * scale).astype(BF16)
        k_ref[:, sl] = (kraw[:, sl] + kr_rot).astype(BF16)
    tm = x_ref.shape[0]
    for gi, zd_ref in enumerate((zd0_ref, zd1_ref, zd2_ref)):
        z = _dot(h, wdil_ref[:, gi * DIL_COLS:(gi + 1) * DIL_COLS])
        window, dil = DIL_GROUPS[gi]
        if dil == 1:
            zd_ref[...] = z.astype(BF16)
            continue
        n_col = DIL_COLS // HEAD_PAD
        for c in range(n_col):
            zs_sc[c] = z[:, c * HEAD_PAD:(c + 1) * HEAD_PAD]
        rows = tm // dil
        part = pl.program_id(0) % (window // tm)
        for r in range(dil):
            dst = pl.ds(pl.multiple_of(r * DIL_STEPS + part * rows, rows), rows)
            for c in range(n_col):
                chunk = zs_sc[c, pl.ds(r, rows, stride=dil), :]
                zd_ref[dst, c * HEAD_PAD:(c + 1) * HEAD_PAD] = chunk.astype(BF16)
    for c in range(2):
        sl = slice(c * D_MODEL, (c + 1) * D_MODEL)
        gate_ref[:, sl] = jax.nn.sigmoid(_dot(h, wgate_ref[:, sl])).astype(BF16)


def _inproj(x2d, seq, g, wmla, wdil, wgate, qn, kvn, wuq, wuk, wuv, cos_t, sina_t, sinb_t):
    t = x2d.shape[0]
    tm = TOKEN_TILE
    n_seq_tiles = seq // tm

    def row(i):
        return (i, 0)

    def const(i):
        return (0, 0)

    def pos(i):
        return (i % n_seq_tiles, 0)

    def full(a):
        return pl.BlockSpec(a.shape, const)

    def vt_block(i):
        return (i // n_seq_tiles, 0, i % n_seq_tiles)

    def rows(width):
        return pl.BlockSpec((tm, width), row)

    def unit_rows(window):
        return pl.BlockSpec((window, DIL_COLS), lambda i: (i // (window // tm), 0))

    out_shape = [
        jax.ShapeDtypeStruct((t, MLA_HEADS * HEAD_PAD), BF16),
        jax.ShapeDtypeStruct((t, MLA_HEADS * HEAD_PAD), BF16),
        jax.ShapeDtypeStruct((t // seq, MLA_HEADS * MLA_V, seq), BF16),
        jax.ShapeDtypeStruct((t, DIL_COLS), BF16),
        jax.ShapeDtypeStruct((t, DIL_COLS), BF16),
        jax.ShapeDtypeStruct((t, DIL_COLS), BF16),
        jax.ShapeDtypeStruct((t, 2 * D_MODEL), BF16),
    ]
    return pl.pallas_call(
        _inproj_kernel,
        out_shape=out_shape,
        grid=(t // tm,),
        in_specs=[pl.BlockSpec((tm, D_MODEL), row), full(g), full(wmla), full(wdil), full(wgate),
                  full(qn), full(kvn), full(wuq), full(wuk), full(wuv),
                  pl.BlockSpec((tm, HEAD_PAD), pos), pl.BlockSpec((tm, HEAD_PAD), pos),
                  pl.BlockSpec((tm, HEAD_PAD), pos)],
        out_specs=[rows(D_MODEL), rows(D_MODEL),
                   pl.BlockSpec((None, MLA_HEADS * MLA_V, tm), vt_block),
                   rows(DIL_COLS)] + [unit_rows(window) for window, _ in DIL_GROUPS[1:]]
        + [rows(2 * D_MODEL)],
        scratch_shapes=[pltpu.VMEM((DIL_COLS // HEAD_PAD, tm, HEAD_PAD), F32)],
        compiler_params=_cparams(("arbitrary",)),
        name="inproj",
    )(x2d, g, wmla, wdil, wgate, qn, kvn, wuq, wuk, wuv, cos_t, sina_t, sinb_t)


def _mla_kernel(qi_ref, kj_ref, q_ref, k_ref, vt_ref, o_ref, m_sc, l_sc, acc_sc):
    p = pl.program_id(2)
    i = qi_ref[p]
    j = kj_ref[p]
    tq = q_ref.shape[0]
    tk = k_ref.shape[0]

    @pl.when(j == 0)
    def _():
        m_sc[...] = jnp.full(m_sc.shape, NEG, F32)
        l_sc[...] = jnp.zeros(l_sc.shape, F32)
        acc_sc[...] = jnp.zeros(acc_sc.shape, F32)

    ratio = tq // tk
    sub = ATTN_SUB

    def step(diagonal, speculative):
        chains = [(hh, c) for hh in range(2) for c in range(tq // sub)]
        ones_rows = (lax.broadcasted_iota(I32, (16, tk), 0) == 0).astype(BF16)
        state = {}
        for hh, c in chains:
            cs = slice(c * sub, (c + 1) * sub)
            state[hh, c] = (m_sc[hh, :, cs], l_sc[hh, :, cs], acc_sc[hh, :, cs])
        new_state = {}
        within = None
        for hh, c in chains:
            sl = slice(hh * HEAD_PAD, (hh + 1) * HEAD_PAD)
            st = _dot_nt(k_ref[:, sl], q_ref[c * sub:(c + 1) * sub, sl])
            if diagonal:
                key = lax.broadcasted_iota(I32, (tk, sub), 0) + j * tk
                qry = lax.broadcasted_iota(I32, (tk, sub), 1) + (i * tq + c * sub)
                st = jnp.where(qry >= key, st, NEG)
            m_prev, l_prev, acc_prev = state[hh, c]
            m_blk = jnp.max(st, axis=0, keepdims=True)
            m_new = jnp.maximum(m_prev, m_blk)
            alpha = jnp.exp2(m_prev - m_new)
            lhs = jnp.concatenate([vt_ref[hh * MLA_V:(hh + 1) * MLA_V, :], ones_rows], axis=0)
            if speculative:
                pv = _dot(lhs, jnp.exp2(st - m_prev).astype(BF16))
                l_new = alpha * (l_prev + pv[MLA_V:MLA_V + 1])
                acc_new = alpha * (acc_prev + pv[:MLA_V])
                ok = jnp.max(m_blk - m_prev) <= SPECULATION_HEADROOM
                within = ok if within is None else (within & ok)
            else:
                pv = _dot(lhs, jnp.exp2(st - m_new).astype(BF16))
                l_new = alpha * l_prev + pv[MLA_V:MLA_V + 1]
                acc_new = alpha * acc_prev + pv[:MLA_V]
            new_state[hh, c] = (m_new, l_new, acc_new)

        def commit():
            for hh, c in chains:
                cs = slice(c * sub, (c + 1) * sub)
                m_sc[hh, :, cs], l_sc[hh, :, cs], acc_sc[hh, :, cs] = new_state[hh, c]

        if speculative:
            pl.when(within)(commit)
            pl.when(jnp.logical_not(within))(lambda: step(True, False))
        else:
            commit()

    @pl.when(j == 0)
    def _():
        step(True, False)

    @pl.when((j > 0) & (j < ratio * i))
    def _():
        step(False, True)

    @pl.when((j > 0) & (j >= ratio * i))
    def _():
        step(True, True)

    @pl.when(j == ratio * i + (ratio - 1))
    def _():
        ot = jnp.concatenate([acc_sc[0] / l_sc[0], acc_sc[1] / l_sc[1]], axis=0)
        o_ref[...] = ot.T.astype(BF16)


def _mla_attention(q, k, vt):
    b, s, _ = q.shape
    tq, tk = ATTN_TQ, ATTN_TK
    ratio = tq // tk
    nq = s // tq
    pairs = [(i, j) for i in range(nq) for j in range(ratio * (i + 1))]
    qi = jnp.asarray([p[0] for p in pairs], I32)
    kj = jnp.asarray([p[1] for p in pairs], I32)
    grid_spec = pltpu.PrefetchScalarGridSpec(
        num_scalar_prefetch=2,
        grid=(b, MLA_HEADS // 2, len(pairs)),
        in_specs=[
            pl.BlockSpec((None, tq, 2 * HEAD_PAD), lambda bb, hp, p, qi, kj: (bb, qi[p], hp)),
            pl.BlockSpec((None, tk, 2 * HEAD_PAD), lambda bb, hp, p, qi, kj: (bb, kj[p], hp)),
            pl.BlockSpec((None, 2 * MLA_V, tk), lambda bb, hp, p, qi, kj: (bb, hp, kj[p])),
        ],
        out_specs=pl.BlockSpec((None, tq, 2 * MLA_V), lambda bb, hp, p, qi, kj: (bb, qi[p], hp)),
        scratch_shapes=[pltpu.VMEM((2, 1, tq), F32), pltpu.VMEM((2, 1, tq), F32),
                        pltpu.VMEM((2, MLA_V, tq), F32)],
    )
    return pl.pallas_call(
        _mla_kernel,
        out_shape=jax.ShapeDtypeStruct((b, s, MLA_HEADS * MLA_V), BF16),
        grid_spec=grid_spec,
        compiler_params=_cparams(("parallel", "parallel", "arbitrary")),
        name="mla_attention",
    )(qi, kj, q, k, vt)


def _alibi_slopes(n):
    def pow2(m):
        start = 2.0 ** (-8.0 / m)
        return [start ** (i + 1) for i in range(m)]
    if math.log2(n).is_integer():
        s = pow2(n)
    else:
        c = 2 ** int(math.floor(math.log2(n)))
        s = pow2(c) + pow2(2 * c)[0::2][: n - c]
    return np.array(sorted(s, reverse=True), dtype=np.float32)


def _dilated_block(cur, prev, bias4, first):
    n = DIL_STEPS
    hw = DIL_OUT
    q = cur[:, :hw]
    kk = jnp.concatenate([prev[:, hw:2 * hw], cur[:, hw:2 * hw]], axis=0)
    vv = jnp.concatenate([prev[:, 2 * hw:], cur[:, 2 * hw:]], axis=0)
    head_of_lane = lax.broadcasted_iota(I32, (n, hw), 1) // DIL_HEAD_DIM
    zero = jnp.zeros_like(q)
    q4 = jnp.concatenate([jnp.where(head_of_lane == h, q, zero) for h in range(DIL_HEADS)], axis=0)
    s4 = _dot_nt(q4, kk) + bias4
    if first is not None:
        ki = lax.broadcasted_iota(I32, (DIL_HEADS * n, 2 * n), 1)
        s4 = jnp.where(first & (ki < n), NEG, s4)
    m4 = jnp.max(s4, axis=1, keepdims=True)
    p4 = jnp.exp(s4 - m4).astype(BF16)
    l4 = _dot(p4, jnp.ones((2 * n, HEAD_PAD), BF16))
    pv4 = _dot(p4, vv)
    m4 = jnp.broadcast_to(m4, (DIL_HEADS * n, HEAD_PAD))

    def rows(a, h):
        return a[h * n:(h + 1) * n]

    o_un = rows(pv4, DIL_HEADS - 1)
    for h in range(DIL_HEADS - 2, -1, -1):
        o_un = jnp.where(head_of_lane == h, rows(pv4, h), o_un)
    low = lax.broadcasted_iota(I32, (n, HEAD_PAD), 1) < DIL_HEAD_DIM

    def per_lane(a):
        return jnp.concatenate([jnp.where(low, rows(a, 0), rows(a, 1)),
                                jnp.where(low, rows(a, 2), rows(a, 3))], axis=1)

    l_sel = per_lane(l4)
    return o_un / l_sel, per_lane(m4) + jnp.log(l_sel)


def _dilated_kernel(c0_ref, h0_ref, c1_ref, h1_ref, c2_ref, h2_ref, ob_ref, o_sc, l_sc, bias_sc,
                    *, slopes):
    u = pl.program_id(1)
    n = DIL_STEPS
    unit = ob_ref.shape[0]
    n_sb = unit // n
    first = u == 0
    qi = lax.broadcasted_iota(I32, (n, 2 * n), 0)
    ki = lax.broadcasted_iota(I32, (n, 2 * n), 1)
    dist = qi + n - ki
    valid = (dist >= 0) & (dist <= n)
    distf = dist.astype(F32)
    for gi in range(len(DIL_GROUPS)):
        for h in range(DIL_HEADS):
            bias_sc[gi, h * n:(h + 1) * n, :] = jnp.where(valid, -slopes[gi][h] * distf, NEG)

    def rows_of(ref, sb):
        return ref[pl.ds(pl.multiple_of(sb * n, n), n), :]

    def emit(gi, start, stride, o, lse):
        if isinstance(start, int):
            idx = pl.ds(start, n)
        elif stride == 1:
            idx = pl.ds(pl.multiple_of(start, n), n)
        else:
            idx = pl.ds(start, n, stride=stride)
        for half in range(DIL_OUT // HEAD_PAD):
            ls = slice(half * HEAD_PAD, (half + 1) * HEAD_PAD)
            o_sc[gi, half, idx, :] = o[:, ls]
            l_sc[gi, half, idx, :] = lse[:, ls]

    emit(0, 0, 1, *_dilated_block(c0_ref[:n, :], h0_ref[...], bias_sc[0], first))

    def g0_body(sb, carry):
        emit(0, sb * n, 1, *_dilated_block(rows_of(c0_ref, sb), rows_of(c0_ref, sb - 1), bias_sc[0], None))
        return carry

    lax.fori_loop(1, n_sb, g0_body, 0, unroll=DIL_UNROLL)

    d1 = DIL_GROUPS[1][1]

    def g1_head(r, carry):
        emit(1, r, d1, *_dilated_block(rows_of(c1_ref, r), rows_of(h1_ref, r), bias_sc[1], first))
        return carry

    def g1_body(sb, carry):
        start = (sb // d1) * (d1 * n) + sb % d1
        emit(1, start, d1, *_dilated_block(rows_of(c1_ref, sb), rows_of(c1_ref, sb - d1), bias_sc[1], None))
        return carry

    lax.fori_loop(0, d1, g1_head, 0, unroll=DIL_UNROLL)
    lax.fori_loop(d1, n_sb, g1_body, 0, unroll=DIL_UNROLL)

    d2 = DIL_GROUPS[2][1]

    def g2_body(r, carry):
        emit(2, r, d2, *_dilated_block(rows_of(c2_ref, r), rows_of(h2_ref, r), bias_sc[2], first))
        return carry

    lax.fori_loop(0, n_sb, g2_body, 0, unroll=DIL_UNROLL)

    def merge_body(c, carry):
        idx = pl.ds(pl.multiple_of(c * n, n), n)
        for half in range(DIL_OUT // HEAD_PAD):
            l0, l1, l2 = l_sc[0, half, idx, :], l_sc[1, half, idx, :], l_sc[2, half, idx, :]
            lmax = jnp.maximum(jnp.maximum(l0, l1), l2)
            e0, e1, e2 = jnp.exp(l0 - lmax), jnp.exp(l1 - lmax), jnp.exp(l2 - lmax)
            ob = (e0 * o_sc[0, half, idx, :] + e1 * o_sc[1, half, idx, :]
                  + e2 * o_sc[2, half, idx, :]) / (e0 + e1 + e2)
            ob_ref[idx, half * HEAD_PAD:(half + 1) * HEAD_PAD] = ob.astype(BF16)
        return carry

    lax.fori_loop(0, n_sb, merge_body, 0)


def _dilated_attention(zd0, zd1, zd2, seq):
    t = zd0.shape[0]
    unit = DIL_GROUPS[-1][0]
    upb = seq // unit
    n = DIL_STEPS
    u1 = DIL_GROUPS[1][0]
    all_slopes = _alibi_slopes(len(DIL_GROUPS) * DIL_HEADS).reshape(len(DIL_GROUPS), DIL_HEADS)
    slopes = tuple(tuple(float(x) * dil for x in all_slopes[gi]) for gi, (_, dil) in enumerate(DIL_GROUPS))

    def cur(bb, u):
        return (bb * upb + u, 0)

    def halo(rows):
        per_unit = unit // rows
        return lambda bb, u: ((bb * upb) * per_unit + jnp.maximum(u * per_unit - 1, 0), 0)

    return pl.pallas_call(
        functools.partial(_dilated_kernel, slopes=slopes),
        out_shape=jax.ShapeDtypeStruct((t, DIL_OUT), BF16),
        grid=(t // seq, upb),
        in_specs=[pl.BlockSpec((unit, DIL_COLS), cur), pl.BlockSpec((n, DIL_COLS), halo(n)),
                  pl.BlockSpec((unit, DIL_COLS), cur), pl.BlockSpec((u1, DIL_COLS), halo(u1)),
                  pl.BlockSpec((unit, DIL_COLS), cur), pl.BlockSpec((unit, DIL_COLS), halo(unit))],
        out_specs=pl.BlockSpec((unit, DIL_OUT), cur),
        scratch_shapes=[pltpu.VMEM((len(DIL_GROUPS), DIL_OUT // HEAD_PAD, unit, HEAD_PAD), F32),
                        pltpu.VMEM((len(DIL_GROUPS), DIL_OUT // HEAD_PAD, unit, HEAD_PAD), F32),
                        pltpu.VMEM((len(DIL_GROUPS), DIL_HEADS * n, 2 * n), F32)],
        compiler_params=_cparams(("parallel", "arbitrary")),
        name="dilated_attention",
    )(zd0, zd0, zd1, zd1, zd2, zd2)


def _merge_kernel(x_ref, oa_ref, ob_ref, gate_ref,
                  wa_ref, wb_ref, wo_ref, g_ref, wr_ref, br_ref,
                  x1_ref, hp_ref, topi_ref, topw_ref, wcol_ref):
    tm = x_ref.shape[0]
    ya = _dot(oa_ref[...], wa_ref[...])
    yb = _dot(ob_ref[...], wb_ref[...])
    mixed = gate_ref[:, :D_MODEL].astype(F32) * ya + gate_ref[:, D_MODEL:].astype(F32) * yb
    x1 = x_ref[...] + _dot(mixed.astype(BF16), wo_ref[...])
    x1_ref[...] = x1
    h2 = _rms(x1, g_ref[...])
    hp_ref[...] = _pack_halves(h2[:, :HALF], h2[:, HALF:])

    logits = _dot_nt(wr_ref[...], h2.astype(BF16)) + br_ref[...]
    eidx = lax.broadcasted_iota(I32, (N_EXPERTS, tm), 0)
    vals, idxs = [], []
    for _ in range(TOP_K):
        m = jnp.max(logits, axis=0, keepdims=True)
        idx = jnp.min(jnp.where(logits == m, eidx, N_EXPERTS), axis=0, keepdims=True)
        vals.append(m)
        idxs.append(idx)
        logits = jnp.where(eidx == idx, -jnp.inf, logits)
    exps = [jnp.exp(vk - vals[0]) for vk in vals]
    den = exps[0] + exps[1] + exps[2] + exps[3]
    row8 = lax.broadcasted_iota(I32, (8, tm), 0)
    row128 = lax.broadcasted_iota(I32, (HEAD_PAD, tm), 0)
    topi = jnp.zeros((8, tm), I32)
    topw = jnp.zeros((8, tm), F32)
    wide = jnp.zeros((HEAD_PAD, tm), F32)
    for kk in range(TOP_K):
        wk = exps[kk] / den
        topi = jnp.where(row8 == kk, idxs[kk], topi)
        topw = jnp.where(row8 == kk, wk, topw)
        wide = jnp.where(row128 == kk, wk, wide)
    topi_ref[...] = topi
    topw_ref[...] = topw
    wcol_ref[...] = wide.T


def _merge(x2d, oa, ob, gates, wa, wb, wo, g, wr_t, br_col):
    t = x2d.shape[0]
    tm = TOKEN_TILE

    def row(i):
        return (i, 0)

    def col(i):
        return (0, i)

    def full(a):
        return pl.BlockSpec(a.shape, lambda i: (0, 0))

    def rows(width):
        return pl.BlockSpec((tm, width), row)

    out_shape = [
        jax.ShapeDtypeStruct((t, D_MODEL), F32),
        jax.ShapeDtypeStruct((t, HALF), I32),
        jax.ShapeDtypeStruct((8, t), I32),
        jax.ShapeDtypeStruct((8, t), F32),
        jax.ShapeDtypeStruct((t, HEAD_PAD), F32),
    ]
    return pl.pallas_call(
        _merge_kernel,
        out_shape=out_shape,
        grid=(t // tm,),
        in_specs=[rows(D_MODEL), rows(MLA_HEADS * MLA_V), rows(DIL_OUT), rows(2 * D_MODEL)]
        + [full(wa), full(wb), full(wo), full(g), full(wr_t), full(br_col)],
        out_specs=[rows(D_MODEL), rows(HALF), pl.BlockSpec((8, tm), col), pl.BlockSpec((8, tm), col),
                   rows(HEAD_PAD)],
        compiler_params=_cparams(("parallel",)),
        name="merge_router",
    )(x2d, oa, ob, gates, wa, wb, wo, g, wr_t, br_col)


def _positions_kernel(topi_ref, dest_ref, meta_ref, cnt_sc, carry_sc, start_sc):
    ps = pl.program_id(0)
    i = pl.program_id(1)
    tm = topi_ref.shape[1]
    eidx = lax.broadcasted_iota(I32, (N_EXPERTS, tm), 0)
    topi = topi_ref[...]
    hits = [eidx == topi[kk:kk + 1, :] for kk in range(TOP_K)]
    member = (hits[0] | hits[1] | hits[2] | hits[3])
    tile_cnt = jnp.sum(member.astype(F32), axis=1, keepdims=True)

    @pl.when((ps == 0) & (i == 0))
    def _():
        cnt_sc[...] = jnp.zeros(cnt_sc.shape, F32)

    @pl.when(ps == 0)
    def _():
        cnt_sc[...] += tile_cnt

    @pl.when((ps == 1) & (i == 0))
    def _():
        cnt = cnt_sc[...].astype(I32)
        padded = lax.shift_left(lax.shift_right_logical(cnt + (ROW_BLOCK - 1), 8), 8)
        sub = lax.broadcasted_iota(I32, (N_EXPERTS, HEAD_PAD), 0)
        lane = lax.broadcasted_iota(I32, (N_EXPERTS, HEAD_PAD), 1)
        padded_row = jnp.sum(jnp.where(sub == lane, padded, 0), axis=0, keepdims=True)
        start = jnp.sum(jnp.where(lane < sub, padded_row, 0), axis=1, keepdims=True)
        start_sc[...] = start.astype(F32)
        carry_sc[...] = jnp.zeros(carry_sc.shape, F32)
        cnt_row = jnp.sum(jnp.where(sub == lane, cnt, 0), axis=0, keepdims=True)
        start_row = jnp.sum(jnp.where(sub == lane, start, 0), axis=0, keepdims=True)
        row8 = lax.broadcasted_iota(I32, (8, HEAD_PAD), 0)
        meta = jnp.where(row8 == 0, cnt_row, 0)
        meta = jnp.where(row8 == 1, start_row, meta)
        meta = jnp.where(row8 == 2, start_row + padded_row, meta)
        meta_ref[...] = meta

    @pl.when(ps == 1)
    def _():
        tr = lax.broadcasted_iota(I32, (tm, tm), 0)
        tc = lax.broadcasted_iota(I32, (tm, tm), 1)
        before = (tr < tc).astype(BF16)
        prefix = _dot(member.astype(BF16), before)
        base = prefix + carry_sc[...] + start_sc[...]
        row8 = lax.broadcasted_iota(I32, (8, tm), 0)
        dest = jnp.zeros((8, tm), I32)
        for kk in range(TOP_K):
            dk = jnp.sum(jnp.where(hits[kk], base, 0.0), axis=0, keepdims=True).astype(I32)
            dest = jnp.where(row8 == kk, dk, dest)
        dest_ref[...] = dest
        carry_sc[...] += tile_cnt


def _positions(topi_t):
    t = topi_t.shape[1]
    tm = TOKEN_TILE
    return pl.pallas_call(
        _positions_kernel,
        out_shape=[jax.ShapeDtypeStruct((8, t), I32), jax.ShapeDtypeStruct((8, HEAD_PAD), I32)],
        grid=(2, t // tm),
        in_specs=[pl.BlockSpec((8, tm), lambda ps, i: (0, i))],
        out_specs=[pl.BlockSpec((8, tm), lambda ps, i: (0, i * ps)),
                   pl.BlockSpec((8, HEAD_PAD), lambda ps, i: (0, 0))],
        scratch_shapes=[pltpu.VMEM((N_EXPERTS, 1), F32)] * 3,
        compiler_params=_cparams(("arbitrary", "arbitrary")),
        name="routing_positions",
    )(topi_t)


def _sc_mesh():
    return plsc.VectorSubcoreMesh(core_axis_name="c", subcore_axis_name="s")


def _dispatch_rows(table, dest_flat, n_rows):
    t, c = table.shape
    n_slots = dest_flat.shape[0] // t
    per_w = t // SC_WORKERS
    assert per_w * SC_WORKERS == t and per_w % (2 * SC_WINDOW) == 0
    n_chunks = per_w // SC_WINDOW
    w = SC_WINDOW

    @functools.partial(
        pl.kernel, mesh=_sc_mesh(),
        out_type=jax.ShapeDtypeStruct((n_rows, c), table.dtype),
        scratch_types=[pltpu.VMEM((w,), I32)] * n_slots + [pltpu.VMEM((w, c), table.dtype)] * 2
        + [pltpu.SemaphoreType.DMA] * (n_slots + 2),
        name="dispatch_rows",
    )
    def k(table_hbm, dest_hbm, out_hbm, *scratch):
        idx = scratch[:n_slots]
        rows = scratch[n_slots:n_slots + 2]
        scatter_sems = scratch[n_slots + 2:2 * n_slots + 2]
        read_sems = scratch[2 * n_slots + 2:]
        wid = lax.axis_index("s") * SC_CORES + lax.axis_index("c")
        base = wid * per_w

        def off(chunk):
            return pl.multiple_of(base + chunk * w, w)

        def read(chunk, buf):
            return pltpu.make_async_copy(table_hbm.at[pl.ds(off(chunk), w)], rows[buf], read_sems[buf])

        def scatter(kk, buf):
            return pltpu.make_async_copy(rows[buf], out_hbm.at[idx[kk]], scatter_sems[kk])

        read(0, 0).start()

        @pl.loop(0, n_chunks // 2)
        def _(p):
            for buf in range(2):
                chunk = 2 * p + buf

                @pl.when(chunk + 1 < n_chunks)
                def _():
                    read(chunk + 1, 1 - buf).start()

                read(chunk, buf).wait()
                for kk in range(n_slots):
                    src = pl.multiple_of(kk * t + off(chunk), w)
                    pltpu.sync_copy(dest_hbm.at[pl.ds(src, w)], idx[kk])
                    scatter(kk, buf).start()
                for kk in range(n_slots):
                    scatter(kk, buf).wait()

    return k(table, dest_flat)


def _gather_rows(table, idx):
    n = idx.shape[0]
    c = table.shape[1]
    per_w = n // SC_WORKERS
    assert per_w * SC_WORKERS == n and per_w % (2 * SC_WINDOW) == 0
    n_chunks = per_w // SC_WINDOW
    w = SC_WINDOW

    @functools.partial(
        pl.kernel, mesh=_sc_mesh(),
        out_type=jax.ShapeDtypeStruct((n, c), table.dtype),
        scratch_types=[pltpu.VMEM((w,), I32)] * 2 + [pltpu.VMEM((w, c), table.dtype)] * 2
        + [pltpu.SemaphoreType.DMA] * 4,
        name="gather_rows",
    )
    def k(table_hbm, idx_hbm, out_hbm, idx_a, idx_b, rows_a, rows_b, g_a, g_b, w_a, w_b):
        idx, rows, gather_sems, write_sems = (idx_a, idx_b), (rows_a, rows_b), (g_a, g_b), (w_a, w_b)
        wid = lax.axis_index("s") * SC_CORES + lax.axis_index("c")
        base = wid * per_w

        def off(chunk):
            return pl.multiple_of(base + chunk * w, w)

        def gather(buf):
            return pltpu.make_async_copy(table_hbm.at[idx[buf]], rows[buf], gather_sems[buf])

        def write(chunk, buf):
            return pltpu.make_async_copy(rows[buf], out_hbm.at[pl.ds(off(chunk), w)], write_sems[buf])

        def start_gather(chunk, buf):
            pltpu.sync_copy(idx_hbm.at[pl.ds(off(chunk), w)], idx[buf])
            gather(buf).start()

        start_gather(0, 0)

        @pl.loop(0, n_chunks // 2)
        def _(p):
            for buf in range(2):
                chunk = 2 * p + buf

                @pl.when(chunk + 1 < n_chunks)
                def _():
                    @pl.when(chunk >= 1)
                    def _():
                        write(chunk - 1, 1 - buf).wait()
                    start_gather(chunk + 1, 1 - buf)

                gather(buf).wait()
                write(chunk, buf).start()

        write(n_chunks - 2, 0).wait()
        write(n_chunks - 1, 1).wait()

    return k(table, idx)


def _expert_kernel(be_ref, nused_ref, xs_ref, wg32_ref, bg_ref, wu32_ref, bu_ref, wd32_ref, bd_ref,
                   ys_ref, wg_ref, wu_ref, wd_ref):
    b = pl.program_id(0)
    new_expert = (b == 0) | (be_ref[b] != be_ref[jnp.maximum(b - 1, 0)])

    @pl.when(new_expert & (b < nused_ref[0]))
    def _():
        wg_ref[...] = wg32_ref[...].astype(BF16)
        wu_ref[...] = wu32_ref[...].astype(BF16)
        wd_ref[...] = wd32_ref[...].astype(BF16)

    @pl.when(b < nused_ref[0])
    def _():
        lo, hi = _unpack_halves(xs_ref[...])
        lo = lo.astype(BF16)
        hi = hi.astype(BF16)
        a = _dot(lo, wg_ref[:HALF, :]) + _dot(hi, wg_ref[HALF:, :]) + bg_ref[...]
        u = _dot(lo, wu_ref[:HALF, :]) + _dot(hi, wu_ref[HALF:, :]) + bu_ref[...]
        a = jnp.minimum(a, SWIGLU_LIMIT)
        u = jnp.clip(u, -SWIGLU_LIMIT, SWIGLU_LIMIT)
        y = (a * jax.nn.sigmoid(SWIGLU_ALPHA * a)) * (u + 1.0)
        out = _dot(y.astype(BF16), wd_ref[...]) + bd_ref[...]
        ys_ref[...] = _pack_halves(out[:, :HALF], out[:, HALF:])

    @pl.when(b >= nused_ref[0])
    def _():
        ys_ref[...] = jnp.zeros(ys_ref.shape, I32)


def _expert_ffn(xs, block_e, n_used, layer, wg, bg, wu, bu, wd, bd):
    n_rows = xs.shape[0]
    n_blocks = n_rows // ROW_BLOCK

    def rows(b, be, nu):
        return (b, 0)

    def expert(b, be, nu):
        return (layer, be[b], 0, 0)

    grid_spec = pltpu.PrefetchScalarGridSpec(
        num_scalar_prefetch=2,
        grid=(n_blocks,),
        in_specs=[pl.BlockSpec((ROW_BLOCK, HALF), rows),
                  pl.BlockSpec((None, None, D_MODEL, D_EXPERT), expert),
                  pl.BlockSpec((None, None, 1, D_EXPERT), expert),
                  pl.BlockSpec((None, None, D_MODEL, D_EXPERT), expert),
                  pl.BlockSpec((None, None, 1, D_EXPERT), expert),
                  pl.BlockSpec((None, None, D_EXPERT, D_MODEL), expert),
                  pl.BlockSpec((None, None, 1, D_MODEL), expert)],
        out_specs=pl.BlockSpec((ROW_BLOCK, HALF), rows),
        scratch_shapes=[pltpu.VMEM((D_MODEL, D_EXPERT), BF16), pltpu.VMEM((D_MODEL, D_EXPERT), BF16),
                        pltpu.VMEM((D_EXPERT, D_MODEL), BF16)],
    )
    return pl.pallas_call(
        _expert_kernel,
        out_shape=jax.ShapeDtypeStruct((n_rows, HALF), I32),
        grid_spec=grid_spec,
        compiler_params=_cparams(("arbitrary",)),
        name="expert_ffn",
    )(block_e, n_used, xs, wg, bg, wu, bu, wd, bd)


def _combine_kernel(x1_ref, yg_ref, wcol_ref, p_ref, gple_ref, wpg_ref, wpp_ref, gout_ref, o_ref,
                    *, final):
    x1 = x1_ref[...]
    acc_lo = x1[:, :HALF]
    acc_hi = x1[:, HALF:]
    wcol = wcol_ref[...]
    for kk in range(TOP_K):
        lo, hi = _unpack_halves(yg_ref[kk])
        wk = wcol[:, kk:kk + 1]
        acc_lo = acc_lo + wk * lo
        acc_hi = acc_hi + wk * hi
    x2 = jnp.concatenate([acc_lo, acc_hi], axis=1)
    gate = jax.nn.sigmoid(_dot(_rms(x2, gple_ref[...]).astype(BF16), wpg_ref[...]))
    x3 = x2 + gate * _dot(p_ref[...].astype(BF16), wpp_ref[...])
    o_ref[...] = _rms(x3, gout_ref[...]) if final else x3


def _combine(x1, yg, wcol, p3d, layer, gple, wpg, wpp, gout, final):
    t = x1.shape[0]
    tm = TOKEN_TILE

    def row(i):
        return (i, 0)

    def full(a):
        return pl.BlockSpec(a.shape, lambda i: (0, 0))

    return pl.pallas_call(
        functools.partial(_combine_kernel, final=final),
        out_shape=jax.ShapeDtypeStruct((t, D_MODEL), F32),
        grid=(t // tm,),
        in_specs=[pl.BlockSpec((tm, D_MODEL), row),
                  pl.BlockSpec((TOP_K, tm, HALF), lambda i: (0, i, 0)),
                  pl.BlockSpec((tm, HEAD_PAD), row),
                  pl.BlockSpec((None, tm, PLE_DIM), lambda i: (layer, i, 0)),
                  full(gple), full(wpg), full(wpp), full(gout)],
        out_specs=pl.BlockSpec((tm, D_MODEL), row),
        compiler_params=_cparams(("parallel",)),
        name="combine_ple",
    )(x1, yg, wcol, p3d, gple, wpg, wpp, gout)


def _rope_tables(seq):
    inv_freq = ROPE_THETA ** (-jnp.arange(HALF_ROPE, dtype=F32) * 2.0 / MLA_ROPE)
    ang = jnp.arange(seq, dtype=F32)[:, None] * inv_freq[None, :]
    cos, sin = jnp.cos(ang), jnp.sin(ang)
    ones = jnp.ones((seq, MLA_NOPE), F32)
    zeros16 = jnp.zeros((seq, HALF_ROPE), F32)
    zeros64 = jnp.zeros((seq, MLA_NOPE), F32)
    tail = jnp.ones((seq, HEAD_PAD - MLA_NOPE - MLA_ROPE), F32)
    ztail = jnp.zeros_like(tail)
    cos_t = jnp.concatenate([ones, cos, cos, tail], axis=1)
    sina_t = jnp.concatenate([zeros64, zeros16, sin, ztail], axis=1)
    sinb_t = jnp.concatenate([zeros64, -sin, zeros16, ztail], axis=1)
    return cos_t, sina_t, sinb_t


def _prep_mixer_weights(w_in, w_uq, w_ukv):
    c0 = MLA_Q_LORA + MLA_KV_LORA
    c1 = c0 + MLA_ROPE
    c2 = c1 + len(DIL_GROUPS) * DIL_COLS
    kr_pad = jnp.zeros((D_MODEL, HEAD_PAD), F32).at[:, MLA_NOPE:MLA_NOPE + MLA_ROPE].set(w_in[:, c0:c1])
    wmla = jnp.concatenate([w_in[:, :c0], kr_pad], axis=1).astype(BF16)
    wdil = w_in[:, c1:c2].reshape(D_MODEL, len(DIL_GROUPS), 3, DIL_OUT)
    wdil = wdil.at[:, :, 0, :].multiply(DIL_HEAD_DIM ** -0.5).reshape(D_MODEL, -1).astype(BF16)
    wgate = w_in[:, c2:].astype(BF16)
    pad = HEAD_PAD - MLA_NOPE - MLA_ROPE
    wuq_h = w_uq.reshape(MLA_Q_LORA, MLA_HEADS, MLA_NOPE + MLA_ROPE)
    wuq = jnp.pad(wuq_h, ((0, 0), (0, 0), (0, pad))).reshape(MLA_Q_LORA, MLA_HEADS * HEAD_PAD).astype(BF16)
    wukv_h = w_ukv.reshape(MLA_KV_LORA, MLA_HEADS, MLA_NOPE + MLA_V)
    wuk = jnp.pad(wukv_h[:, :, :MLA_NOPE], ((0, 0), (0, 0), (0, HEAD_PAD - MLA_NOPE)))
    wuk = wuk.reshape(MLA_KV_LORA, MLA_HEADS * HEAD_PAD).astype(BF16)
    wuv = wukv_h[:, :, MLA_NOPE:].reshape(MLA_KV_LORA, MLA_HEADS * MLA_V).astype(BF16)
    return wmla, wdil, wgate, wuq, wuk, wuv


def kernel(x, p, attn_norm, w_in, q_norm, w_uq, kv_norm, w_ukv, w_branch_a, w_branch_b, w_out, ffn_norm, w_router, b_router, w_gate, b_gate, w_up, b_up, w_down, b_down, ple_norm, w_ple_gate, w_ple_proj, final_norm):
    b, s, d = x.shape
    depth = w_in.shape[0]
    t = b * s
    assert d == D_MODEL and s % (DIL_GROUPS[-1][0]) == 0 and t % (SC_WORKERS * SC_WINDOW) == 0
    n_assign = t * TOP_K
    n_blocks = -(-(n_assign + N_EXPERTS * (ROW_BLOCK - 1)) // ROW_BLOCK)
    n_rows = n_blocks * ROW_BLOCK
    cos_t, sina_t, sinb_t = _rope_tables(s)
    xc = x.reshape(t, d)
    for i in range(depth):
        wmla, wdil, wgate, wuq, wuk, wuv = _prep_mixer_weights(w_in[i], w_uq[i], w_ukv[i])
        q, k, vt, zd0, zd1, zd2, gates = _inproj(
            xc, s, attn_norm[i][None], wmla, wdil, wgate, q_norm[i][None], kv_norm[i][None],
            wuq, wuk, wuv, cos_t, sina_t, sinb_t)
        oa = _mla_attention(q.reshape(b, s, -1), k.reshape(b, s, -1), vt)
        ob = _dilated_attention(zd0, zd1, zd2, s)
        x1, hp, topi_t, topw_t, wcol = _merge(
            xc, oa.reshape(t, -1), ob, gates,
            w_branch_a[i].astype(BF16), w_branch_b[i].astype(BF16), w_out[i].astype(BF16),
            ffn_norm[i][None], w_router[i].T.astype(BF16), b_router[i][:, None])
        dest_t, meta = _positions(topi_t)
        ends = meta[2, :N_EXPERTS]
        block_start = jnp.arange(n_blocks, dtype=I32) * ROW_BLOCK
        block_e = jnp.minimum(
            jnp.sum((ends[None, :] <= block_start[:, None]).astype(I32), axis=1), N_EXPERTS - 1)
        n_used = (ends[N_EXPERTS - 1:] // ROW_BLOCK).astype(I32)
        dest_flat = dest_t[:TOP_K].reshape(n_assign)
        xs = _dispatch_rows(hp, dest_flat, n_rows)
        ys = _expert_ffn(xs, block_e, n_used, i,
                         w_gate, b_gate[:, :, None, :], w_up, b_up[:, :, None, :],
                         w_down, b_down[:, :, None, :])
        yg = _gather_rows(ys, dest_flat).reshape(TOP_K, t, HALF)
        final = i == depth - 1
        gout = final_norm[None] if final else attn_norm[i][None]
        xc = _combine(x1, yg, wcol, p.reshape(depth, t, PLE_DIM), i, ple_norm[i][None],
                      w_ple_gate[i].astype(BF16), w_ple_proj[i].astype(BF16), gout, final)
    return xc.reshape(b, s, d)
```

```python
import functools
import math

import jax
import jax.numpy as jnp
import numpy as np
from jax import lax
from jax.experimental import pallas as pl
from jax.experimental.pallas import tpu as pltpu
from jax.experimental.pallas import tpu_sc as plsc

F32 = jnp.float32
BF16 = jnp.bfloat16
I32 = jnp.int32

D_MODEL = 1024
PLE_DIM = 256
NORM_EPS = 1e-6

MLA_HEADS = 8
MLA_Q_LORA = 384
MLA_KV_LORA = 256
MLA_NOPE = 64
MLA_ROPE = 32
MLA_V = 64
ROPE_THETA = 10000.0
HEAD_PAD = 128
HALF_ROPE = MLA_ROPE // 2

DIL_GROUPS = ((128, 1), (512, 4), (2048, 16))
DIL_HEADS = 4
DIL_HEAD_DIM = 64
DIL_STEPS = 128
DIL_COLS = 3 * DIL_HEADS * DIL_HEAD_DIM
DIL_OUT = DIL_HEADS * DIL_HEAD_DIM
DIL_UNROLL = 4

N_EXPERTS = 32
TOP_K = 4
D_EXPERT = 1024
SWIGLU_LIMIT = 7.0
SWIGLU_ALPHA = 1.702
ROW_BLOCK = 256

TOKEN_TILE = 512
ATTN_TQ = 1024
ATTN_TK = 1024
ATTN_SUB = 1024
HALF = D_MODEL // 2
NEG = -1e30
SPECULATION_HEADROOM = 60.0
HI_MASK = -65536

SC_CORES = 2
SC_SUBCORES = 16
SC_WORKERS = SC_CORES * SC_SUBCORES
SC_WINDOW = 64

VMEM_LIMIT = 56 * 1024 * 1024


def _cparams(sem):
    return pltpu.CompilerParams(dimension_semantics=sem, vmem_limit_bytes=VMEM_LIMIT)


def _rms(x, g):
    return x * lax.rsqrt(jnp.mean(x * x, axis=-1, keepdims=True) + NORM_EPS) * g


def _dot(a, b):
    return jnp.dot(a, b, preferred_element_type=F32)


def _dot_nt(a, b):
    return lax.dot_general(a, b, (((1,), (1,)), ((), ())), preferred_element_type=F32)


def _pack_halves(lo, hi):
    lo_i = lax.bitcast_convert_type(lo.astype(BF16).astype(F32), I32)
    hi_i = lax.bitcast_convert_type(hi.astype(BF16).astype(F32), I32)
    return (hi_i & HI_MASK) | lax.shift_right_logical(lo_i, 16)


def _unpack_halves(w):
    lo = lax.bitcast_convert_type(lax.shift_left(w, 16), F32)
    hi = lax.bitcast_convert_type(w & HI_MASK, F32)
    return lo, hi


def _inproj_kernel(x_ref, g_ref, wmla_ref, wdil_ref, wgate_ref, qn_ref, kvn_ref, wuq_ref, wuk_ref,
                   wuv_ref, cos_ref, sina_ref, sinb_ref,
                   q_ref, k_ref, vt_ref, zd0_ref, zd1_ref, zd2_ref, gate_ref, zs_sc):
    h = _rms(x_ref[...], g_ref[...]).astype(BF16)
    zm = _dot(h, wmla_ref[...])
    cq = _rms(zm[:, :MLA_Q_LORA], qn_ref[...]).astype(BF16)
    ckv = _rms(zm[:, MLA_Q_LORA:MLA_Q_LORA + MLA_KV_LORA], kvn_ref[...]).astype(BF16)
    kr = zm[:, MLA_Q_LORA + MLA_KV_LORA:]
    cos, sina, sinb = cos_ref[...], sina_ref[...], sinb_ref[...]

    def rope(t):
        return (t * cos + pltpu.roll(t, HALF_ROPE, 1) * sina
                + pltpu.roll(t, HEAD_PAD - HALF_ROPE, 1) * sinb)

    kr_rot = rope(kr)
    qraw = _dot(cq, wuq_ref[...])
    kraw = _dot(ckv, wuk_ref[...])
    vt_ref[...] = _dot(ckv, wuv_ref[...]).T.astype(BF16)
    scale = (MLA_NOPE + MLA_ROPE) ** -0.5 * math.log2(math.e)
    for hd in range(MLA_HEADS):
        sl = slice(hd * HEAD_PAD, (hd + 1) * HEAD_PAD)
        q_ref[:, sl] = (rope(qraw[:, sl]) * scale).astype(BF16)
        k_ref[:, sl] = (kraw[:, sl] + kr_rot).astype(BF16)
    tm = x_ref.shape[0]
    for gi, zd_ref in enumerate((zd0_ref, zd1_ref, zd2_ref)):
        z = _dot(h, wdil_ref[:, gi * DIL_COLS:(gi + 1) * DIL_COLS])
        window, dil = DIL_GROUPS[gi]
        if dil == 1:
            zd_ref[...] = z.astype(BF16)
            continue
        n_col = DIL_COLS // HEAD_PAD
        for c in range(n_col):
            zs_sc[c] = z[:, c * HEAD_PAD:(c + 1) * HEAD_PAD]
        rows = tm // dil
        part = pl.program_id(0) % (window // tm)
        for r in range(dil):
            dst = pl.ds(pl.multiple_of(r * DIL_STEPS + part * rows, rows), rows)
            for c in range(n_col):
                chunk = zs_sc[c, pl.ds(r, rows, stride=dil), :]
                zd_ref[dst, c * HEAD_PAD:(c + 1) * HEAD_PAD] = chunk.astype(BF16)
    for c in range(2):
        sl = slice(c * D_MODEL, (c + 1) * D_MODEL)
        gate_ref[:, sl] = jax.nn.sigmoid(_dot(h, wgate_ref[:, sl])).astype(BF16)


def _inproj(x2d, seq, g, wmla, wdil, wgate, qn, kvn, wuq, wuk, wuv, cos_t, sina_t, sinb_t):
    t = x2d.shape[0]
    tm = TOKEN_TILE
    n_seq_tiles = seq // tm

    def row(i):
        return (i, 0)

    def const(i):
        return (0, 0)

    def pos(i):
        return (i % n_seq_tiles, 0)

    def full(a):
        return pl.BlockSpec(a.shape, const)

    def vt_block(i):
        return (i // n_seq_tiles, 0, i % n_seq_tiles)

    def rows(width):
        return pl.BlockSpec((tm, width), row)

    def unit_rows(window):
        return pl.BlockSpec((window, DIL_COLS), lambda i: (i // (window // tm), 0))

    out_shape = [
        jax.ShapeDtypeStruct((t, MLA_HEADS * HEAD_PAD), BF16),
        jax.ShapeDtypeStruct((t, MLA_HEADS * HEAD_PAD), BF16),
        jax.ShapeDtypeStruct((t // seq, MLA_HEADS * MLA_V, seq), BF16),
        jax.ShapeDtypeStruct((t, DIL_COLS), BF16),
        jax.ShapeDtypeStruct((t, DIL_COLS), BF16),
        jax.ShapeDtypeStruct((t, DIL_COLS), BF16),
        jax.ShapeDtypeStruct((t, 2 * D_MODEL), BF16),
    ]
    return pl.pallas_call(
        _inproj_kernel,
        out_shape=out_shape,
        grid=(t // tm,),
        in_specs=[pl.BlockSpec((tm, D_MODEL), row), full(g), full(wmla), full(wdil), full(wgate),
                  full(qn), full(kvn), full(wuq), full(wuk), full(wuv),
                  pl.BlockSpec((tm, HEAD_PAD), pos), pl.BlockSpec((tm, HEAD_PAD), pos),
                  pl.BlockSpec((tm, HEAD_PAD), pos)],
        out_specs=[rows(D_MODEL), rows(D_MODEL),
                   pl.BlockSpec((None, MLA_HEADS * MLA_V, tm), vt_block),
                   rows(DIL_COLS)] + [unit_rows(window) for window, _ in DIL_GROUPS[1:]]
        + [rows(2 * D_MODEL)],
        scratch_shapes=[pltpu.VMEM((DIL_COLS // HEAD_PAD, tm, HEAD_PAD), F32)],
        compiler_params=_cparams(("arbitrary",)),
        name="inproj",
    )(x2d, g, wmla, wdil, wgate, qn, kvn, wuq, wuk, wuv, cos_t, sina_t, sinb_t)


def _mla_kernel(qi_ref, kj_ref, q_ref, k_ref, vt_ref, o_ref, m_sc, l_sc, acc_sc, redo_sc):
    p = pl.program_id(2)
    i = qi_ref[p]
    j = kj_ref[p]
    tq = q_ref.shape[0]
    tk = k_ref.shape[0]

    @pl.when(j == 0)
    def _():
        m_sc[...] = jnp.full(m_sc.shape, NEG, F32)
        l_sc[...] = jnp.zeros(l_sc.shape, F32)
        acc_sc[...] = jnp.zeros(acc_sc.shape, F32)

    ratio = tq // tk
    sub = ATTN_SUB

    def step(diagonal, speculative, first=False):
        chains = [(hh, c) for hh in range(2) for c in range(tq // sub)]
        ones_rows = (lax.broadcasted_iota(I32, (16, tk), 0) == 0).astype(BF16)
        state = {}
        for hh, c in chains:
            cs = slice(c * sub, (c + 1) * sub)
            if first and speculative:
                sl = slice(hh * HEAD_PAD, (hh + 1) * HEAD_PAD)
                s0 = _dot_nt(k_ref[:8, sl], q_ref[cs, sl])[:1]
                state[hh, c] = (s0, jnp.zeros((1, sub), F32), jnp.zeros((MLA_V, sub), F32))
            else:
                state[hh, c] = (m_sc[hh, :, cs], l_sc[hh, :, cs], acc_sc[hh, :, cs])
        new_state = {}
        within = None
        for hh, c in chains:
            sl = slice(hh * HEAD_PAD, (hh + 1) * HEAD_PAD)
            st = _dot_nt(k_ref[:, sl], q_ref[c * sub:(c + 1) * sub, sl])
            if diagonal:
                key = lax.broadcasted_iota(I32, (tk, sub), 0) + j * tk
                qry = lax.broadcasted_iota(I32, (tk, sub), 1) + (i * tq + c * sub)
                st = jnp.where(qry >= key, st, NEG)
            m_prev, l_prev, acc_prev = state[hh, c]
            m_blk = jnp.max(st, axis=0, keepdims=True)
            m_new = jnp.maximum(m_prev, m_blk)
            alpha = jnp.exp2(m_prev - m_new)
            lhs = jnp.concatenate([vt_ref[hh * MLA_V:(hh + 1) * MLA_V, :], ones_rows], axis=0)
            if speculative:
                pv = _dot(lhs, jnp.exp2(st - m_prev).astype(BF16))
                l_new = alpha * (l_prev + pv[MLA_V:MLA_V + 1])
                acc_new = alpha * (acc_prev + pv[:MLA_V])
                ok = jnp.max(m_blk - m_prev) <= SPECULATION_HEADROOM
                within = ok if within is None else (within & ok)
            else:
                pv = _dot(lhs, jnp.exp2(st - m_new).astype(BF16))
                l_new = alpha * l_prev + pv[MLA_V:MLA_V + 1]
                acc_new = alpha * acc_prev + pv[:MLA_V]
            new_state[hh, c] = (m_new, l_new, acc_new)

        def commit():
            for hh, c in chains:
                cs = slice(c * sub, (c + 1) * sub)
                m_sc[hh, :, cs], l_sc[hh, :, cs], acc_sc[hh, :, cs] = new_state[hh, c]

        if speculative:
            pl.when(within)(commit)
            redo_sc[0] = jnp.logical_not(within).astype(I32)
        else:
            commit()

    redo_sc[0] = 0
    on_diagonal = j >= ratio * i

    @pl.when((j == 0) & on_diagonal)
    def _():
        step(True, True, first=True)

    @pl.when((j == 0) & jnp.logical_not(on_diagonal))
    def _():
        step(False, True, first=True)

    @pl.when((j > 0) & jnp.logical_not(on_diagonal))
    def _():
        step(False, True)

    @pl.when((j > 0) & on_diagonal)
    def _():
        step(True, True)

    @pl.when(redo_sc[0] != 0)
    def _():
        step(True, False)

    @pl.when(j == ratio * i + (ratio - 1))
    def _():
        ot = jnp.concatenate([acc_sc[0] / l_sc[0], acc_sc[1] / l_sc[1]], axis=0)
        o_ref[...] = ot.T.astype(BF16)


def _mla_attention(q, k, vt):
    b, s, _ = q.shape
    tq, tk = ATTN_TQ, ATTN_TK
    ratio = tq // tk
    nq = s // tq
    pairs = [(i, j) for i in range(nq) for j in range(ratio * (i + 1))]
    qi = jnp.asarray([p[0] for p in pairs], I32)
    kj = jnp.asarray([p[1] for p in pairs], I32)
    grid_spec = pltpu.PrefetchScalarGridSpec(
        num_scalar_prefetch=2,
        grid=(b, MLA_HEADS // 2, len(pairs)),
        in_specs=[
            pl.BlockSpec((None, tq, 2 * HEAD_PAD), lambda bb, hp, p, qi, kj: (bb, qi[p], hp)),
            pl.BlockSpec((None, tk, 2 * HEAD_PAD), lambda bb, hp, p, qi, kj: (bb, kj[p], hp)),
            pl.BlockSpec((None, 2 * MLA_V, tk), lambda bb, hp, p, qi, kj: (bb, hp, kj[p])),
        ],
        out_specs=pl.BlockSpec((None, tq, 2 * MLA_V), lambda bb, hp, p, qi, kj: (bb, qi[p], hp)),
        scratch_shapes=[pltpu.VMEM((2, 1, tq), F32), pltpu.VMEM((2, 1, tq), F32),
                        pltpu.VMEM((2, MLA_V, tq), F32), pltpu.SMEM((1,), I32)],
    )
    return pl.pallas_call(
        _mla_kernel,
        out_shape=jax.ShapeDtypeStruct((b, s, MLA_HEADS * MLA_V), BF16),
        grid_spec=grid_spec,
        compiler_params=_cparams(("parallel", "parallel", "arbitrary")),
        name="mla_attention",
    )(qi, kj, q, k, vt)


def _alibi_slopes(n):
    def pow2(m):
        start = 2.0 ** (-8.0 / m)
        return [start ** (i + 1) for i in range(m)]
    if math.log2(n).is_integer():
        s = pow2(n)
    else:
        c = 2 ** int(math.floor(math.log2(n)))
        s = pow2(c) + pow2(2 * c)[0::2][: n - c]
    return np.array(sorted(s, reverse=True), dtype=np.float32)


def _dilated_block(cur, prev, bias4, first):
    n = DIL_STEPS
    hw = DIL_OUT
    q = cur[:, :hw]
    kk = jnp.concatenate([prev[:, hw:2 * hw], cur[:, hw:2 * hw]], axis=0)
    vv = jnp.concatenate([prev[:, 2 * hw:], cur[:, 2 * hw:]], axis=0)
    head_of_lane = lax.broadcasted_iota(I32, (n, hw), 1) // DIL_HEAD_DIM
    zero = jnp.zeros_like(q)
    q4 = jnp.concatenate([jnp.where(head_of_lane == h, q, zero) for h in range(DIL_HEADS)], axis=0)
    s4 = _dot_nt(q4, kk) + bias4
    if first is not None:
        ki = lax.broadcasted_iota(I32, (DIL_HEADS * n, 2 * n), 1)
        s4 = jnp.where(first & (ki < n), NEG, s4)
    m4 = jnp.max(s4, axis=1, keepdims=True)
    p4 = jnp.exp(s4 - m4).astype(BF16)
    l4 = _dot(p4, jnp.ones((2 * n, HEAD_PAD), BF16))
    pv4 = _dot(p4, vv)
    m4 = jnp.broadcast_to(m4, (DIL_HEADS * n, HEAD_PAD))

    def rows(a, h):
        return a[h * n:(h + 1) * n]

    o_un = rows(pv4, DIL_HEADS - 1)
    for h in range(DIL_HEADS - 2, -1, -1):
        o_un = jnp.where(head_of_lane == h, rows(pv4, h), o_un)
    low = lax.broadcasted_iota(I32, (n, HEAD_PAD), 1) < DIL_HEAD_DIM

    def per_lane(a):
        return jnp.concatenate([jnp.where(low, rows(a, 0), rows(a, 1)),
                                jnp.where(low, rows(a, 2), rows(a, 3))], axis=1)

    l_sel = per_lane(l4)
    return o_un / l_sel, per_lane(m4) + jnp.log(l_sel)


def _dilated_kernel(c0_ref, h0_ref, c1_ref, h1_ref, c2_ref, h2_ref, ob_ref, o_sc, l_sc, bias_sc,
                    *, slopes):
    u = pl.program_id(1)
    n = DIL_STEPS
    unit = ob_ref.shape[0]
    n_sb = unit // n
    first = u == 0
    qi = lax.broadcasted_iota(I32, (n, 2 * n), 0)
    ki = lax.broadcasted_iota(I32, (n, 2 * n), 1)
    dist = qi + n - ki
    valid = (dist >= 0) & (dist <= n)
    distf = dist.astype(F32)
    for gi in range(len(DIL_GROUPS)):
        for h in range(DIL_HEADS):
            bias_sc[gi, h * n:(h + 1) * n, :] = jnp.where(valid, -slopes[gi][h] * distf, NEG)

    def rows_of(ref, sb):
        return ref[pl.ds(pl.multiple_of(sb * n, n), n), :]

    def emit(gi, start, stride, o, lse):
        if isinstance(start, int):
            idx = pl.ds(start, n)
        elif stride == 1:
            idx = pl.ds(pl.multiple_of(start, n), n)
        else:
            idx = pl.ds(start, n, stride=stride)
        for half in range(DIL_OUT // HEAD_PAD):
            ls = slice(half * HEAD_PAD, (half + 1) * HEAD_PAD)
            o_sc[gi, half, idx, :] = o[:, ls]
            l_sc[gi, half, idx, :] = lse[:, ls]

    emit(0, 0, 1, *_dilated_block(c0_ref[:n, :], h0_ref[...], bias_sc[0], first))

    def g0_body(sb, carry):
        emit(0, sb * n, 1, *_dilated_block(rows_of(c0_ref, sb), rows_of(c0_ref, sb - 1), bias_sc[0], None))
        return carry

    lax.fori_loop(1, n_sb, g0_body, 0, unroll=DIL_UNROLL)

    d1 = DIL_GROUPS[1][1]

    def g1_head(r, carry):
        emit(1, r, d1, *_dilated_block(rows_of(c1_ref, r), rows_of(h1_ref, r), bias_sc[1], first))
        return carry

    def g1_body(sb, carry):
        start = (sb // d1) * (d1 * n) + sb % d1
        emit(1, start, d1, *_dilated_block(rows_of(c1_ref, sb), rows_of(c1_ref, sb - d1), bias_sc[1], None))
        return carry

    lax.fori_loop(0, d1, g1_head, 0, unroll=DIL_UNROLL)
    lax.fori_loop(d1, n_sb, g1_body, 0, unroll=DIL_UNROLL)

    d2 = DIL_GROUPS[2][1]

    def g2_body(r, carry):
        emit(2, r, d2, *_dilated_block(rows_of(c2_ref, r), rows_of(h2_ref, r), bias_sc[2], first))
        return carry

    lax.fori_loop(0, n_sb, g2_body, 0, unroll=DIL_UNROLL)

    def merge_body(c, carry):
        idx = pl.ds(pl.multiple_of(c * n, n), n)
        for half in range(DIL_OUT // HEAD_PAD):
            l0, l1, l2 = l_sc[0, half, idx, :], l_sc[1, half, idx, :], l_sc[2, half, idx, :]
            lmax = jnp.maximum(jnp.maximum(l0, l1), l2)
            e0, e1, e2 = jnp.exp(l0 - lmax), jnp.exp(l1 - lmax), jnp.exp(l2 - lmax)
            ob = (e0 * o_sc[0, half, idx, :] + e1 * o_sc[1, half, idx, :]
                  + e2 * o_sc[2, half, idx, :]) / (e0 + e1 + e2)
            ob_ref[idx, half * HEAD_PAD:(half + 1) * HEAD_PAD] = ob.astype(BF16)
        return carry

    lax.fori_loop(0, n_sb, merge_body, 0)


def _dilated_attention(zd0, zd1, zd2, seq):
    t = zd0.shape[0]
    unit = DIL_GROUPS[-1][0]
    upb = seq // unit
    n = DIL_STEPS
    u1 = DIL_GROUPS[1][0]
    all_slopes = _alibi_slopes(len(DIL_GROUPS) * DIL_HEADS).reshape(len(DIL_GROUPS), DIL_HEADS)
    slopes = tuple(tuple(float(x) * dil for x in all_slopes[gi]) for gi, (_, dil) in enumerate(DIL_GROUPS))

    def cur(bb, u):
        return (bb * upb + u, 0)

    def halo(rows):
        per_unit = unit // rows
        return lambda bb, u: ((bb * upb) * per_unit + jnp.maximum(u * per_unit - 1, 0), 0)

    return pl.pallas_call(
        functools.partial(_dilated_kernel, slopes=slopes),
        out_shape=jax.ShapeDtypeStruct((t, DIL_OUT), BF16),
        grid=(t // seq, upb),
        in_specs=[pl.BlockSpec((unit, DIL_COLS), cur), pl.BlockSpec((n, DIL_COLS), halo(n)),
                  pl.BlockSpec((unit, DIL_COLS), cur), pl.BlockSpec((u1, DIL_COLS), halo(u1)),
                  pl.BlockSpec((unit, DIL_COLS), cur), pl.BlockSpec((unit, DIL_COLS), halo(unit))],
        out_specs=pl.BlockSpec((unit, DIL_OUT), cur),
        scratch_shapes=[pltpu.VMEM((len(DIL_GROUPS), DIL_OUT // HEAD_PAD, unit, HEAD_PAD), F32),
                        pltpu.VMEM((len(DIL_GROUPS), DIL_OUT // HEAD_PAD, unit, HEAD_PAD), F32),
                        pltpu.VMEM((len(DIL_GROUPS), DIL_HEADS * n, 2 * n), F32)],
        compiler_params=_cparams(("parallel", "arbitrary")),
        name="dilated_attention",
    )(zd0, zd0, zd1, zd1, zd2, zd2)


def _merge_kernel(x_ref, oa_ref, ob_ref, gate_ref,
                  wa_ref, wb_ref, wo_ref, g_ref, wr_ref, br_ref,
                  x1_ref, hp_ref, topi_ref, topw_ref, wcol_ref):
    tm = x_ref.shape[0]
    ya = _dot(oa_ref[...], wa_ref[...])
    yb = _dot(ob_ref[...], wb_ref[...])
    mixed = gate_ref[:, :D_MODEL].astype(F32) * ya + gate_ref[:, D_MODEL:].astype(F32) * yb
    x1 = x_ref[...] + _dot(mixed.astype(BF16), wo_ref[...])
    x1_ref[...] = x1
    h2 = _rms(x1, g_ref[...])
    hp_ref[...] = _pack_halves(h2[:, :HALF], h2[:, HALF:])

    logits = _dot_nt(wr_ref[...], h2.astype(BF16)) + br_ref[...]
    eidx = lax.broadcasted_iota(I32, (N_EXPERTS, tm), 0)
    vals, idxs = [], []
    for _ in range(TOP_K):
        m = jnp.max(logits, axis=0, keepdims=True)
        idx = jnp.min(jnp.where(logits == m, eidx, N_EXPERTS), axis=0, keepdims=True)
        vals.append(m)
        idxs.append(idx)
        logits = jnp.where(eidx == idx, -jnp.inf, logits)
    exps = [jnp.exp(vk - vals[0]) for vk in vals]
    den = exps[0] + exps[1] + exps[2] + exps[3]
    row8 = lax.broadcasted_iota(I32, (8, tm), 0)
    row128 = lax.broadcasted_iota(I32, (HEAD_PAD, tm), 0)
    topi = jnp.zeros((8, tm), I32)
    topw = jnp.zeros((8, tm), F32)
    wide = jnp.zeros((HEAD_PAD, tm), F32)
    for kk in range(TOP_K):
        wk = exps[kk] / den
        topi = jnp.where(row8 == kk, idxs[kk], topi)
        topw = jnp.where(row8 == kk, wk, topw)
        wide = jnp.where(row128 == kk, wk, wide)
    topi_ref[...] = topi
    topw_ref[...] = topw
    wcol_ref[...] = wide.T


def _merge(x2d, oa, ob, gates, wa, wb, wo, g, wr_t, br_col):
    t = x2d.shape[0]
    tm = TOKEN_TILE

    def row(i):
        return (i, 0)

    def col(i):
        return (0, i)

    def full(a):
        return pl.BlockSpec(a.shape, lambda i: (0, 0))

    def rows(width):
        return pl.BlockSpec((tm, width), row)

    out_shape = [
        jax.ShapeDtypeStruct((t, D_MODEL), F32),
        jax.ShapeDtypeStruct((t, HALF), I32),
        jax.ShapeDtypeStruct((8, t), I32),
        jax.ShapeDtypeStruct((8, t), F32),
        jax.ShapeDtypeStruct((t, HEAD_PAD), F32),
    ]
    return pl.pallas_call(
        _merge_kernel,
        out_shape=out_shape,
        grid=(t // tm,),
        in_specs=[rows(D_MODEL), rows(MLA_HEADS * MLA_V), rows(DIL_OUT), rows(2 * D_MODEL)]
        + [full(wa), full(wb), full(wo), full(g), full(wr_t), full(br_col)],
        out_specs=[rows(D_MODEL), rows(HALF), pl.BlockSpec((8, tm), col), pl.BlockSpec((8, tm), col),
                   rows(HEAD_PAD)],
        compiler_params=_cparams(("parallel",)),
        name="merge_router",
    )(x2d, oa, ob, gates, wa, wb, wo, g, wr_t, br_col)


def _positions_kernel(topi_ref, dest_ref, meta_ref, cnt_sc, carry_sc, start_sc):
    ps = pl.program_id(0)
    i = pl.program_id(1)
    tm = topi_ref.shape[1]
    eidx = lax.broadcasted_iota(I32, (N_EXPERTS, tm), 0)
    topi = topi_ref[...]
    hits = [eidx == topi[kk:kk + 1, :] for kk in range(TOP_K)]
    member = (hits[0] | hits[1] | hits[2] | hits[3])
    tile_cnt = jnp.sum(member.astype(F32), axis=1, keepdims=True)

    @pl.when((ps == 0) & (i == 0))
    def _():
        cnt_sc[...] = jnp.zeros(cnt_sc.shape, F32)

    @pl.when(ps == 0)
    def _():
        cnt_sc[...] += tile_cnt

    @pl.when((ps == 1) & (i == 0))
    def _():
        cnt = cnt_sc[...].astype(I32)
        padded = lax.shift_left(lax.shift_right_logical(cnt + (ROW_BLOCK - 1), 8), 8)
        sub = lax.broadcasted_iota(I32, (N_EXPERTS, HEAD_PAD), 0)
        lane = lax.broadcasted_iota(I32, (N_EXPERTS, HEAD_PAD), 1)
        padded_row = jnp.sum(jnp.where(sub == lane, padded, 0), axis=0, keepdims=True)
        start = jnp.sum(jnp.where(lane < sub, padded_row, 0), axis=1, keepdims=True)
        start_sc[...] = start.astype(F32)
        carry_sc[...] = jnp.zeros(carry_sc.shape, F32)
        cnt_row = jnp.sum(jnp.where(sub == lane, cnt, 0), axis=0, keepdims=True)
        start_row = jnp.sum(jnp.where(sub == lane, start, 0), axis=0, keepdims=True)
        row8 = lax.broadcasted_iota(I32, (8, HEAD_PAD), 0)
        meta = jnp.where(row8 == 0, cnt_row, 0)
        meta = jnp.where(row8 == 1, start_row, meta)
        meta = jnp.where(row8 == 2, start_row + padded_row, meta)
        meta_ref[...] = meta

    @pl.when(ps == 1)
    def _():
        tr = lax.broadcasted_iota(I32, (tm, tm), 0)
        tc = lax.broadcasted_iota(I32, (tm, tm), 1)
        before = (tr < tc).astype(BF16)
        prefix = _dot(member.astype(BF16), before)
        base = prefix + carry_sc[...] + start_sc[...]
        row8 = lax.broadcasted_iota(I32, (8, tm), 0)
        dest = jnp.zeros((8, tm), I32)
        for kk in range(TOP_K):
            dk = jnp.sum(jnp.where(hits[kk], base, 0.0), axis=0, keepdims=True).astype(I32)
            dest = jnp.where(row8 == kk, dk, dest)
        dest_ref[...] = dest
        carry_sc[...] += tile_cnt


def _positions(topi_t):
    t = topi_t.shape[1]
    tm = TOKEN_TILE
    return pl.pallas_call(
        _positions_kernel,
        out_shape=[jax.ShapeDtypeStruct((8, t), I32), jax.ShapeDtypeStruct((8, HEAD_PAD), I32)],
        grid=(2, t // tm),
        in_specs=[pl.BlockSpec((8, tm), lambda ps, i: (0, i))],
        out_specs=[pl.BlockSpec((8, tm), lambda ps, i: (0, i * ps)),
                   pl.BlockSpec((8, HEAD_PAD), lambda ps, i: (0, 0))],
        scratch_shapes=[pltpu.VMEM((N_EXPERTS, 1), F32)] * 3,
        compiler_params=_cparams(("arbitrary", "arbitrary")),
        name="routing_positions",
    )(topi_t)


def _sc_mesh():
    return plsc.VectorSubcoreMesh(core_axis_name="c", subcore_axis_name="s")


def _dispatch_rows(table, dest_flat, n_rows):
    t, c = table.shape
    n_slots = dest_flat.shape[0] // t
    per_w = t // SC_WORKERS
    assert per_w * SC_WORKERS == t and per_w % (2 * SC_WINDOW) == 0
    n_chunks = per_w // SC_WINDOW
    w = SC_WINDOW

    @functools.partial(
        pl.kernel, mesh=_sc_mesh(),
        out_type=jax.ShapeDtypeStruct((n_rows, c), table.dtype),
        scratch_types=[pltpu.VMEM((w,), I32)] * n_slots + [pltpu.VMEM((w, c), table.dtype)] * 2
        + [pltpu.SemaphoreType.DMA] * (n_slots + 2),
        name="dispatch_rows",
    )
    def k(table_hbm, dest_hbm, out_hbm, *scratch):
        idx = scratch[:n_slots]
        rows = scratch[n_slots:n_slots + 2]
        scatter_sems = scratch[n_slots + 2:2 * n_slots + 2]
        read_sems = scratch[2 * n_slots + 2:]
        wid = lax.axis_index("s") * SC_CORES + lax.axis_index("c")
        base = wid * per_w

        def off(chunk):
            return pl.multiple_of(base + chunk * w, w)

        def read(chunk, buf):
            return pltpu.make_async_copy(table_hbm.at[pl.ds(off(chunk), w)], rows[buf], read_sems[buf])

        def scatter(kk, buf):
            return pltpu.make_async_copy(rows[buf], out_hbm.at[idx[kk]], scatter_sems[kk])

        read(0, 0).start()

        @pl.loop(0, n_chunks // 2)
        def _(p):
            for buf in range(2):
                chunk = 2 * p + buf

                @pl.when(chunk + 1 < n_chunks)
                def _():
                    read(chunk + 1, 1 - buf).start()

                read(chunk, buf).wait()
                for kk in range(n_slots):
                    src = pl.multiple_of(kk * t + off(chunk), w)
                    pltpu.sync_copy(dest_hbm.at[pl.ds(src, w)], idx[kk])
                    scatter(kk, buf).start()
                for kk in range(n_slots):
                    scatter(kk, buf).wait()

    return k(table, dest_flat)


def _gather_rows(table, idx):
    n = idx.shape[0]
    c = table.shape[1]
    per_w = n // SC_WORKERS
    assert per_w * SC_WORKERS == n and per_w % (2 * SC_WINDOW) == 0
    n_chunks = per_w // SC_WINDOW
    w = SC_WINDOW

    @functools.partial(
        pl.kernel, mesh=_sc_mesh(),
        out_type=jax.ShapeDtypeStruct((n, c), table.dtype),
        scratch_types=[pltpu.VMEM((w,), I32)] * 2 + [pltpu.VMEM((w, c), table.dtype)] * 2
        + [pltpu.SemaphoreType.DMA] * 4,
        name="gather_rows",
    )
    def k(table_hbm, idx_hbm, out_hbm, idx_a, idx_b, rows_a, rows_b, g_a, g_b, w_a, w_b):
        idx, rows, gather_sems, write_sems = (idx_a, idx_b), (rows_a, rows_b), (g_a, g_b), (w_a, w_b)
        wid = lax.axis_index("s") * SC_CORES + lax.axis_index("c")
        base = wid * per_w

        def off(chunk):
            return pl.multiple_of(base + chunk * w, w)

        def gather(buf):
            return pltpu.make_async_copy(table_hbm.at[idx[buf]], rows[buf], gather_sems[buf])

        def write(chunk, buf):
            return pltpu.make_async_copy(rows[buf], out_hbm.at[pl.ds(off(chunk), w)], write_sems[buf])

        def start_gather(chunk, buf):
            pltpu.sync_copy(idx_hbm.at[pl.ds(off(chunk), w)], idx[buf])
            gather(buf).start()

        start_gather(0, 0)

        @pl.loop(0, n_chunks // 2)
        def _(p):
            for buf in range(2):
                chunk = 2 * p + buf

                @pl.when(chunk + 1 < n_chunks)
                def _():
                    @pl.when(chunk >= 1)
                    def _():
                        write(chunk - 1, 1 - buf).wait()
                    start_gather(chunk + 1, 1 - buf)

                gather(buf).wait()
                write(chunk, buf).start()

        write(n_chunks - 2, 0).wait()
        write(n_chunks - 1, 1).wait()

    return k(table, idx)


def _expert_kernel(be_ref, nused_ref, xs_ref, wg32_ref, bg_ref, wu32_ref, bu_ref, wd32_ref, bd_ref,
                   ys_ref, wg_ref, wu_ref, wd_ref):
    b = pl.program_id(0)
    new_expert = (b == 0) | (be_ref[b] != be_ref[jnp.maximum(b - 1, 0)])

    @pl.when(new_expert & (b < nused_ref[0]))
    def _():
        wg_ref[...] = wg32_ref[...].astype(BF16)
        wu_ref[...] = wu32_ref[...].astype(BF16)
        wd_ref[...] = wd32_ref[...].astype(BF16)

    @pl.when(b < nused_ref[0])
    def _():
        lo, hi = _unpack_halves(xs_ref[...])
        lo = lo.astype(BF16)
        hi = hi.astype(BF16)
        a = _dot(lo, wg_ref[:HALF, :]) + _dot(hi, wg_ref[HALF:, :]) + bg_ref[...]
        u = _dot(lo, wu_ref[:HALF, :]) + _dot(hi, wu_ref[HALF:, :]) + bu_ref[...]
        a = jnp.minimum(a, SWIGLU_LIMIT)
        u = jnp.clip(u, -SWIGLU_LIMIT, SWIGLU_LIMIT)
        y = (a * jax.nn.sigmoid(SWIGLU_ALPHA * a)) * (u + 1.0)
        out = _dot(y.astype(BF16), wd_ref[...]) + bd_ref[...]
        ys_ref[...] = _pack_halves(out[:, :HALF], out[:, HALF:])

    @pl.when(b >= nused_ref[0])
    def _():
        ys_ref[...] = jnp.zeros(ys_ref.shape, I32)


def _expert_ffn(xs, block_e, n_used, layer, wg, bg, wu, bu, wd, bd):
    n_rows = xs.shape[0]
    n_blocks = n_rows // ROW_BLOCK

    def rows(b, be, nu):
        return (b, 0)

    def expert(b, be, nu):
        return (layer, be[b], 0, 0)

    grid_spec = pltpu.PrefetchScalarGridSpec(
        num_scalar_prefetch=2,
        grid=(n_blocks,),
        in_specs=[pl.BlockSpec((ROW_BLOCK, HALF), rows),
                  pl.BlockSpec((None, None, D_MODEL, D_EXPERT), expert),
                  pl.BlockSpec((None, None, 1, D_EXPERT), expert),
                  pl.BlockSpec((None, None, D_MODEL, D_EXPERT), expert),
                  pl.BlockSpec((None, None, 1, D_EXPERT), expert),
                  pl.BlockSpec((None, None, D_EXPERT, D_MODEL), expert),
                  pl.BlockSpec((None, None, 1, D_MODEL), expert)],
        out_specs=pl.BlockSpec((ROW_BLOCK, HALF), rows),
        scratch_shapes=[pltpu.VMEM((D_MODEL, D_EXPERT), BF16), pltpu.VMEM((D_MODEL, D_EXPERT), BF16),
                        pltpu.VMEM((D_EXPERT, D_MODEL), BF16)],
    )
    return pl.pallas_call(
        _expert_kernel,
        out_shape=jax.ShapeDtypeStruct((n_rows, HALF), I32),
        grid_spec=grid_spec,
        compiler_params=_cparams(("arbitrary",)),
        name="expert_ffn",
    )(block_e, n_used, xs, wg, bg, wu, bu, wd, bd)


def _combine_kernel(x1_ref, yg_ref, wcol_ref, p_ref, gple_ref, wpg_ref, wpp_ref, gout_ref, o_ref,
                    *, final):
    x1 = x1_ref[...]
    acc_lo = x1[:, :HALF]
    acc_hi = x1[:, HALF:]
    wcol = wcol_ref[...]
    for kk in range(TOP_K):
        lo, hi = _unpack_halves(yg_ref[kk])
        wk = wcol[:, kk:kk + 1]
        acc_lo = acc_lo + wk * lo
        acc_hi = acc_hi + wk * hi
    x2 = jnp.concatenate([acc_lo, acc_hi], axis=1)
    gate = jax.nn.sigmoid(_dot(_rms(x2, gple_ref[...]).astype(BF16), wpg_ref[...]))
    x3 = x2 + gate * _dot(p_ref[...].astype(BF16), wpp_ref[...])
    o_ref[...] = _rms(x3, gout_ref[...]) if final else x3


def _combine(x1, yg, wcol, p3d, layer, gple, wpg, wpp, gout, final):
    t = x1.shape[0]
    tm = TOKEN_TILE

    def row(i):
        return (i, 0)

    def full(a):
        return pl.BlockSpec(a.shape, lambda i: (0, 0))

    return pl.pallas_call(
        functools.partial(_combine_kernel, final=final),
        out_shape=jax.ShapeDtypeStruct((t, D_MODEL), F32),
        grid=(t // tm,),
        in_specs=[pl.BlockSpec((tm, D_MODEL), row),
                  pl.BlockSpec((TOP_K, tm, HALF), lambda i: (0, i, 0)),
                  pl.BlockSpec((tm, HEAD_PAD), row),
                  pl.BlockSpec((None, tm, PLE_DIM), lambda i: (layer, i, 0)),
                  full(gple), full(wpg), full(wpp), full(gout)],
        out_specs=pl.BlockSpec((tm, D_MODEL), row),
        compiler_params=_cparams(("parallel",)),
        name="combine_ple",
    )(x1, yg, wcol, p3d, gple, wpg, wpp, gout)


def _rope_tables(seq):
    inv_freq = ROPE_THETA ** (-jnp.arange(HALF_ROPE, dtype=F32) * 2.0 / MLA_ROPE)
    ang = jnp.arange(seq, dtype=F32)[:, None] * inv_freq[None, :]
    cos, sin = jnp.cos(ang), jnp.sin(ang)
    ones = jnp.ones((seq, MLA_NOPE), F32)
    zeros16 = jnp.zeros((seq, HALF_ROPE), F32)
    zeros64 = jnp.zeros((seq, MLA_NOPE), F32)
    tail = jnp.ones((seq, HEAD_PAD - MLA_NOPE - MLA_ROPE), F32)
    ztail = jnp.zeros_like(tail)
    cos_t = jnp.concatenate([ones, cos, cos, tail], axis=1)
    sina_t = jnp.concatenate([zeros64, zeros16, sin, ztail], axis=1)
    sinb_t = jnp.concatenate([zeros64, -sin, zeros16, ztail], axis=1)
    return cos_t, sina_t, sinb_t


def _prep_mixer_weights(w_in, w_uq, w_ukv):
    c0 = MLA_Q_LORA + MLA_KV_LORA
    c1 = c0 + MLA_ROPE
    c2 = c1 + len(DIL_GROUPS) * DIL_COLS
    kr_pad = jnp.zeros((D_MODEL, HEAD_PAD), F32).at[:, MLA_NOPE:MLA_NOPE + MLA_ROPE].set(w_in[:, c0:c1])
    wmla = jnp.concatenate([w_in[:, :c0], kr_pad], axis=1).astype(BF16)
    wdil = w_in[:, c1:c2].reshape(D_MODEL, len(DIL_GROUPS), 3, DIL_OUT)
    wdil = wdil.at[:, :, 0, :].multiply(DIL_HEAD_DIM ** -0.5).reshape(D_MODEL, -1).astype(BF16)
    wgate = w_in[:, c2:].astype(BF16)
    pad = HEAD_PAD - MLA_NOPE - MLA_ROPE
    wuq_h = w_uq.reshape(MLA_Q_LORA, MLA_HEADS, MLA_NOPE + MLA_ROPE)
    wuq = jnp.pad(wuq_h, ((0, 0), (0, 0), (0, pad))).reshape(MLA_Q_LORA, MLA_HEADS * HEAD_PAD).astype(BF16)
    wukv_h = w_ukv.reshape(MLA_KV_LORA, MLA_HEADS, MLA_NOPE + MLA_V)
    wuk = jnp.pad(wukv_h[:, :, :MLA_NOPE], ((0, 0), (0, 0), (0, HEAD_PAD - MLA_NOPE)))
    wuk = wuk.reshape(MLA_KV_LORA, MLA_HEADS * HEAD_PAD).astype(BF16)
    wuv = wukv_h[:, :, MLA_NOPE:].reshape(MLA_KV_LORA, MLA_HEADS * MLA_V).astype(BF16)
    return wmla, wdil, wgate, wuq, wuk, wuv


def kernel(x, p, attn_norm, w_in, q_norm, w_uq, kv_norm, w_ukv, w_branch_a, w_branch_b, w_out, ffn_norm, w_router, b_router, w_gate, b_gate, w_up, b_up, w_down, b_down, ple_norm, w_ple_gate, w_ple_proj, final_norm):
    b, s, d = x.shape
    depth = w_in.shape[0]
    t = b * s
    assert d == D_MODEL and s % (DIL_GROUPS[-1][0]) == 0 and t % (SC_WORKERS * SC_WINDOW) == 0
    n_assign = t * TOP_K
    n_blocks = -(-(n_assign + N_EXPERTS * (ROW_BLOCK - 1)) // ROW_BLOCK)
    n_rows = n_blocks * ROW_BLOCK
    cos_t, sina_t, sinb_t = _rope_tables(s)
    xc = x.reshape(t, d)
    for i in range(depth):
        wmla, wdil, wgate, wuq, wuk, wuv = _prep_mixer_weights(w_in[i], w_uq[i], w_ukv[i])
        q, k, vt, zd0, zd1, zd2, gates = _inproj(
            xc, s, attn_norm[i][None], wmla, wdil, wgate, q_norm[i][None], kv_norm[i][None],
            wuq, wuk, wuv, cos_t, sina_t, sinb_t)
        oa = _mla_attention(q.reshape(b, s, -1), k.reshape(b, s, -1), vt)
        ob = _dilated_attention(zd0, zd1, zd2, s)
        x1, hp, topi_t, topw_t, wcol = _merge(
            xc, oa.reshape(t, -1), ob, gates,
            w_branch_a[i].astype(BF16), w_branch_b[i].astype(BF16), w_out[i].astype(BF16),
            ffn_norm[i][None], w_router[i].T.astype(BF16), b_router[i][:, None])
        dest_t, meta = _positions(topi_t)
        ends = meta[2, :N_EXPERTS]
        block_start = jnp.arange(n_blocks, dtype=I32) * ROW_BLOCK
        block_e = jnp.minimum(
            jnp.sum((ends[None, :] <= block_start[:, None]).astype(I32), axis=1), N_EXPERTS - 1)
        n_used = (ends[N_EXPERTS - 1:] // ROW_BLOCK).astype(I32)
        dest_flat = dest_t[:TOP_K].reshape(n_assign)
        xs = _dispatch_rows(hp, dest_flat, n_rows)
        ys = _expert_ffn(xs, block_e, n_used, i,
                         w_gate, b_gate[:, :, None, :], w_up, b_up[:, :, None, :],
                         w_down, b_down[:, :, None, :])
        yg = _gather_rows(ys, dest_flat).reshape(TOP_K, t, HALF)
        final = i == depth - 1
        gout = final_norm[None] if final else attn_norm[i][None]
        xc = _combine(x1, yg, wcol, p.reshape(depth, t, PLE_DIM), i, ple_norm[i][None],
                      w_ple_gate[i].astype(BF16), w_ple_proj[i].astype(BF16), gout, final)
    return xc.reshape(b, s, d)
```

```python
import functools
import math

import jax
import jax.numpy as jnp
import numpy as np
from jax import lax
from jax.experimental import pallas as pl
from jax.experimental.pallas import tpu as pltpu
from jax.experimental.pallas import tpu_sc as plsc

F32 = jnp.float32
BF16 = jnp.bfloat16
I32 = jnp.int32

D_MODEL = 1024
PLE_DIM = 256
NORM_EPS = 1e-6

MLA_HEADS = 8
MLA_Q_LORA = 384
MLA_KV_LORA = 256
MLA_NOPE = 64
MLA_ROPE = 32
MLA_V = 64
ROPE_THETA = 10000.0
HEAD_PAD = 128
HALF_ROPE = MLA_ROPE // 2

DIL_GROUPS = ((128, 1), (512, 4), (2048, 16))
DIL_HEADS = 4
DIL_HEAD_DIM = 64
DIL_STEPS = 128
DIL_COLS = 3 * DIL_HEADS * DIL_HEAD_DIM
DIL_OUT = DIL_HEADS * DIL_HEAD_DIM
DIL_UNROLL = 4

N_EXPERTS = 32
TOP_K = 4
D_EXPERT = 1024
SWIGLU_LIMIT = 7.0
SWIGLU_ALPHA = 1.702
ROW_BLOCK = 256

TOKEN_TILE = 512
ATTN_TQ = 1024
ATTN_TK = 1024
ATTN_SUB = 1024
HALF = D_MODEL // 2
NEG = -1e30
SPECULATION_HEADROOM = 60.0
HI_MASK = -65536

SC_CORES = 2
SC_SUBCORES = 16
SC_WORKERS = SC_CORES * SC_SUBCORES
SC_WINDOW = 64

VMEM_LIMIT = 56 * 1024 * 1024


def _cparams(sem):
    return pltpu.CompilerParams(dimension_semantics=sem, vmem_limit_bytes=VMEM_LIMIT)


def _rms(x, g):
    return x * lax.rsqrt(jnp.mean(x * x, axis=-1, keepdims=True) + NORM_EPS) * g


def _dot(a, b):
    return jnp.dot(a, b, preferred_element_type=F32)


def _dot_nt(a, b):
    return lax.dot_general(a, b, (((1,), (1,)), ((), ())), preferred_element_type=F32)


def _pack_halves(lo, hi):
    lo_i = lax.bitcast_convert_type(lo.astype(BF16).astype(F32), I32)
    hi_i = lax.bitcast_convert_type(hi.astype(BF16).astype(F32), I32)
    return (hi_i & HI_MASK) | lax.shift_right_logical(lo_i, 16)


def _unpack_halves(w):
    lo = lax.bitcast_convert_type(lax.shift_left(w, 16), F32)
    hi = lax.bitcast_convert_type(w & HI_MASK, F32)
    return lo, hi


def _inproj_kernel(x_ref, g_ref, wmla_ref, wdil_ref, wgate_ref, qn_ref, kvn_ref, wuq_ref, wuk_ref,
                   wuv_ref, cos_ref, sina_ref, sinb_ref,
                   q_ref, k_ref, vt_ref, zd0_ref, zd1_ref, zd2_ref, gate_ref, zs_sc):
    h = _rms(x_ref[...], g_ref[...]).astype(BF16)
    zm = _dot(h, wmla_ref[...])
    cq = _rms(zm[:, :MLA_Q_LORA], qn_ref[...]).astype(BF16)
    ckv = _rms(zm[:, MLA_Q_LORA:MLA_Q_LORA + MLA_KV_LORA], kvn_ref[...]).astype(BF16)
    kr = zm[:, MLA_Q_LORA + MLA_KV_LORA:]
    cos, sina, sinb = cos_ref[...], sina_ref[...], sinb_ref[...]

    def rope(t):
        return (t * cos + pltpu.roll(t, HALF_ROPE, 1) * sina
                + pltpu.roll(t, HEAD_PAD - HALF_ROPE, 1) * sinb)

    kr_rot = rope(kr)
    qraw = _dot(cq, wuq_ref[...])
    kraw = _dot(ckv, wuk_ref[...])
    vt_ref[...] = _dot(ckv, wuv_ref[...]).T.astype(BF16)
    scale = (MLA_NOPE + MLA_ROPE) ** -0.5 * math.log2(math.e)
    for hd in range(MLA_HEADS):
        sl = slice(hd * HEAD_PAD, (hd + 1) * HEAD_PAD)
        q_ref[:, sl] = (rope(qraw[:, sl]) * scale).astype(BF16)
        k_ref[:, sl] = (kraw[:, sl] + kr_rot).astype(BF16)
    tm = x_ref.shape[0]
    for gi, zd_ref in enumerate((zd0_ref, zd1_ref, zd2_ref)):
        z = _dot(h, wdil_ref[:, gi * DIL_COLS:(gi + 1) * DIL_COLS])
        window, dil = DIL_GROUPS[gi]
        if dil == 1:
            zd_ref[...] = z.astype(BF16)
            continue
        n_col = DIL_COLS // HEAD_PAD
        for c in range(n_col):
            zs_sc[c] = z[:, c * HEAD_PAD:(c + 1) * HEAD_PAD]
        rows = tm // dil
        part = pl.program_id(0) % (window // tm)
        for r in range(dil):
            dst = pl.ds(pl.multiple_of(r * DIL_STEPS + part * rows, rows), rows)
            for c in range(n_col):
                chunk = zs_sc[c, pl.ds(r, rows, stride=dil), :]
                zd_ref[dst, c * HEAD_PAD:(c + 1) * HEAD_PAD] = chunk.astype(BF16)
    for c in range(2):
        sl = slice(c * D_MODEL, (c + 1) * D_MODEL)
        gate_ref[:, sl] = jax.nn.sigmoid(_dot(h, wgate_ref[:, sl])).astype(BF16)


def _inproj(x2d, seq, g, wmla, wdil, wgate, qn, kvn, wuq, wuk, wuv, cos_t, sina_t, sinb_t):
    t = x2d.shape[0]
    tm = TOKEN_TILE
    n_seq_tiles = seq // tm

    def row(i):
        return (i, 0)

    def const(i):
        return (0, 0)

    def pos(i):
        return (i % n_seq_tiles, 0)

    def full(a):
        return pl.BlockSpec(a.shape, const)

    def vt_block(i):
        return (i // n_seq_tiles, 0, i % n_seq_tiles)

    def rows(width):
        return pl.BlockSpec((tm, width), row)

    def unit_rows(window):
        return pl.BlockSpec((window, DIL_COLS), lambda i: (i // (window // tm), 0))

    out_shape = [
        jax.ShapeDtypeStruct((t, MLA_HEADS * HEAD_PAD), BF16),
        jax.ShapeDtypeStruct((t, MLA_HEADS * HEAD_PAD), BF16),
        jax.ShapeDtypeStruct((t // seq, MLA_HEADS * MLA_V, seq), BF16),
        jax.ShapeDtypeStruct((t, DIL_COLS), BF16),
        jax.ShapeDtypeStruct((t, DIL_COLS), BF16),
        jax.ShapeDtypeStruct((t, DIL_COLS), BF16),
        jax.ShapeDtypeStruct((t, 2 * D_MODEL), BF16),
    ]
    return pl.pallas_call(
        _inproj_kernel,
        out_shape=out_shape,
        grid=(t // tm,),
        in_specs=[pl.BlockSpec((tm, D_MODEL), row), full(g), full(wmla), full(wdil), full(wgate),
                  full(qn), full(kvn), full(wuq), full(wuk), full(wuv),
                  pl.BlockSpec((tm, HEAD_PAD), pos), pl.BlockSpec((tm, HEAD_PAD), pos),
                  pl.BlockSpec((tm, HEAD_PAD), pos)],
        out_specs=[rows(D_MODEL), rows(D_MODEL),
                   pl.BlockSpec((None, MLA_HEADS * MLA_V, tm), vt_block),
                   rows(DIL_COLS)] + [unit_rows(window) for window, _ in DIL_GROUPS[1:]]
        + [rows(2 * D_MODEL)],
        scratch_shapes=[pltpu.VMEM((DIL_COLS // HEAD_PAD, tm, HEAD_PAD), F32)],
        compiler_params=_cparams(("arbitrary",)),
        name="inproj",
    )(x2d, g, wmla, wdil, wgate, qn, kvn, wuq, wuk, wuv, cos_t, sina_t, sinb_t)


def _mla_kernel(qi_ref, kj_ref, q_ref, k_ref, vt_ref, o_ref, m_sc, l_sc, acc_sc, redo_sc):
    p = pl.program_id(2)
    i = qi_ref[p]
    j = kj_ref[p]
    tq = q_ref.shape[0]
    tk = k_ref.shape[0]

    @pl.when(j == 0)
    def _():
        m_sc[...] = jnp.full(m_sc.shape, NEG, F32)
        l_sc[...] = jnp.zeros(l_sc.shape, F32)
        acc_sc[...] = jnp.zeros(acc_sc.shape, F32)

    ratio = tq // tk
    sub = ATTN_SUB

    def step(diagonal, speculative, first=False):
        chains = [(hh, c) for hh in range(2) for c in range(tq // sub)]
        ones_rows = (lax.broadcasted_iota(I32, (16, tk), 0) == 0).astype(BF16)
        state = {}
        for hh, c in chains:
            cs = slice(c * sub, (c + 1) * sub)
            if first and speculative:
                sl = slice(hh * HEAD_PAD, (hh + 1) * HEAD_PAD)
                s0 = _dot_nt(k_ref[:8, sl], q_ref[cs, sl])[:1]
                state[hh, c] = (s0, jnp.zeros((1, sub), F32), jnp.zeros((MLA_V, sub), F32))
            else:
                state[hh, c] = (m_sc[hh, :, cs], l_sc[hh, :, cs], acc_sc[hh, :, cs])
        new_state = {}
        within = None
        for hh, c in chains:
            sl = slice(hh * HEAD_PAD, (hh + 1) * HEAD_PAD)
            st = _dot_nt(k_ref[:, sl], q_ref[c * sub:(c + 1) * sub, sl])
            if diagonal:
                key = lax.broadcasted_iota(I32, (tk, sub), 0) + j * tk
                qry = lax.broadcasted_iota(I32, (tk, sub), 1) + (i * tq + c * sub)
                st = jnp.where(qry >= key, st, NEG)
            m_prev, l_prev, acc_prev = state[hh, c]
            m_blk = jnp.max(st, axis=0, keepdims=True)
            m_new = jnp.maximum(m_prev, m_blk)
            alpha = jnp.exp2(m_prev - m_new)
            lhs = jnp.concatenate([vt_ref[hh * MLA_V:(hh + 1) * MLA_V, :], ones_rows], axis=0)
            if speculative:
                pv = _dot(lhs, jnp.exp2(st - m_prev).astype(BF16))
                l_new = alpha * (l_prev + pv[MLA_V:MLA_V + 1])
                acc_new = alpha * (acc_prev + pv[:MLA_V])
                ok = jnp.max(m_blk - m_prev) <= SPECULATION_HEADROOM
                within = ok if within is None else (within & ok)
            else:
                pv = _dot(lhs, jnp.exp2(st - m_new).astype(BF16))
                l_new = alpha * l_prev + pv[MLA_V:MLA_V + 1]
                acc_new = alpha * acc_prev + pv[:MLA_V]
            new_state[hh, c] = (m_new, l_new, acc_new)

        def commit():
            for hh, c in chains:
                cs = slice(c * sub, (c + 1) * sub)
                m_sc[hh, :, cs], l_sc[hh, :, cs], acc_sc[hh, :, cs] = new_state[hh, c]

        if speculative:
            pl.when(within)(commit)
            redo_sc[0] = jnp.logical_not(within).astype(I32)
        else:
            commit()

    redo_sc[0] = 0
    on_diagonal = j >= ratio * i

    @pl.when((j == 0) & on_diagonal)
    def _():
        step(True, True, first=True)

    @pl.when((j == 0) & jnp.logical_not(on_diagonal))
    def _():
        step(False, True, first=True)

    @pl.when((j > 0) & jnp.logical_not(on_diagonal))
    def _():
        step(False, True)

    @pl.when((j > 0) & on_diagonal)
    def _():
        step(True, True)

    @pl.when(redo_sc[0] != 0)
    def _():
        step(True, False)

    @pl.when(j == ratio * i + (ratio - 1))
    def _():
        ot = jnp.concatenate([acc_sc[0] / l_sc[0], acc_sc[1] / l_sc[1]], axis=0)
        o_ref[...] = ot.T.astype(BF16)


def _mla_attention(q, k, vt):
    b, s, _ = q.shape
    tq, tk = ATTN_TQ, ATTN_TK
    ratio = tq // tk
    nq = s // tq
    pairs = [(i, j) for i in range(nq) for j in range(ratio * (i + 1))]
    qi = jnp.asarray([p[0] for p in pairs], I32)
    kj = jnp.asarray([p[1] for p in pairs], I32)
    grid_spec = pltpu.PrefetchScalarGridSpec(
        num_scalar_prefetch=2,
        grid=(b, MLA_HEADS // 2, len(pairs)),
        in_specs=[
            pl.BlockSpec((None, tq, 2 * HEAD_PAD), lambda bb, hp, p, qi, kj: (bb, qi[p], hp)),
            pl.BlockSpec((None, tk, 2 * HEAD_PAD), lambda bb, hp, p, qi, kj: (bb, kj[p], hp)),
            pl.BlockSpec((None, 2 * MLA_V, tk), lambda bb, hp, p, qi, kj: (bb, hp, kj[p])),
        ],
        out_specs=pl.BlockSpec((None, tq, 2 * MLA_V), lambda bb, hp, p, qi, kj: (bb, qi[p], hp)),
        scratch_shapes=[pltpu.VMEM((2, 1, tq), F32), pltpu.VMEM((2, 1, tq), F32),
                        pltpu.VMEM((2, MLA_V, tq), F32), pltpu.SMEM((1,), I32)],
    )
    return pl.pallas_call(
        _mla_kernel,
        out_shape=jax.ShapeDtypeStruct((b, s, MLA_HEADS * MLA_V), BF16),
        grid_spec=grid_spec,
        compiler_params=_cparams(("parallel", "parallel", "arbitrary")),
        name="mla_attention",
    )(qi, kj, q, k, vt)


def _alibi_slopes(n):
    def pow2(m):
        start = 2.0 ** (-8.0 / m)
        return [start ** (i + 1) for i in range(m)]
    if math.log2(n).is_integer():
        s = pow2(n)
    else:
        c = 2 ** int(math.floor(math.log2(n)))
        s = pow2(c) + pow2(2 * c)[0::2][: n - c]
    return np.array(sorted(s, reverse=True), dtype=np.float32)


def _dilated_block(cur, prev, bias4, first):
    n = DIL_STEPS
    hw = DIL_OUT
    q = cur[:, :hw]
    kk = jnp.concatenate([prev[:, hw:2 * hw], cur[:, hw:2 * hw]], axis=0)
    vv = jnp.concatenate([prev[:, 2 * hw:], cur[:, 2 * hw:]], axis=0)
    head_of_lane = lax.broadcasted_iota(I32, (n, hw), 1) // DIL_HEAD_DIM
    zero = jnp.zeros_like(q)
    q4 = jnp.concatenate([jnp.where(head_of_lane == h, q, zero) for h in range(DIL_HEADS)], axis=0)
    s4 = _dot_nt(q4, kk) + bias4
    if first is not None:
        ki = lax.broadcasted_iota(I32, (DIL_HEADS * n, 2 * n), 1)
        s4 = jnp.where(first & (ki < n), NEG, s4)
    m4 = jnp.max(s4, axis=1, keepdims=True)
    p4 = jnp.exp(s4 - m4).astype(BF16)
    l4 = _dot(p4, jnp.ones((2 * n, HEAD_PAD), BF16))
    pv4 = _dot(p4, vv)
    m4 = jnp.broadcast_to(m4, (DIL_HEADS * n, HEAD_PAD))

    def rows(a, h):
        return a[h * n:(h + 1) * n]

    o_un = rows(pv4, DIL_HEADS - 1)
    for h in range(DIL_HEADS - 2, -1, -1):
        o_un = jnp.where(head_of_lane == h, rows(pv4, h), o_un)
    low = lax.broadcasted_iota(I32, (n, HEAD_PAD), 1) < DIL_HEAD_DIM

    def per_lane(a):
        return jnp.concatenate([jnp.where(low, rows(a, 0), rows(a, 1)),
                                jnp.where(low, rows(a, 2), rows(a, 3))], axis=1)

    l_sel = per_lane(l4)
    return o_un / l_sel, per_lane(m4) + jnp.log(l_sel)


def _dilated_kernel(c0_ref, h0_ref, c1_ref, h1_ref, c2_ref, h2_ref, ob_ref, o_sc, l_sc, bias_sc,
                    *, slopes):
    u = pl.program_id(1)
    n = DIL_STEPS
    unit = ob_ref.shape[0]
    n_sb = unit // n
    first = u == 0
    qi = lax.broadcasted_iota(I32, (n, 2 * n), 0)
    ki = lax.broadcasted_iota(I32, (n, 2 * n), 1)
    dist = qi + n - ki
    valid = (dist >= 0) & (dist <= n)
    distf = dist.astype(F32)
    for gi in range(len(DIL_GROUPS)):
        for h in range(DIL_HEADS):
            bias_sc[gi, h * n:(h + 1) * n, :] = jnp.where(valid, -slopes[gi][h] * distf, NEG)

    def rows_of(ref, sb):
        return ref[pl.ds(pl.multiple_of(sb * n, n), n), :]

    def emit(gi, start, stride, o, lse):
        if isinstance(start, int):
            idx = pl.ds(start, n)
        elif stride == 1:
            idx = pl.ds(pl.multiple_of(start, n), n)
        else:
            idx = pl.ds(start, n, stride=stride)
        for half in range(DIL_OUT // HEAD_PAD):
            ls = slice(half * HEAD_PAD, (half + 1) * HEAD_PAD)
            o_sc[gi, half, idx, :] = o[:, ls]
            l_sc[gi, half, idx, :] = lse[:, ls]

    emit(0, 0, 1, *_dilated_block(c0_ref[:n, :], h0_ref[...], bias_sc[0], first))

    def g0_body(sb, carry):
        emit(0, sb * n, 1, *_dilated_block(rows_of(c0_ref, sb), rows_of(c0_ref, sb - 1), bias_sc[0], None))
        return carry

    lax.fori_loop(1, n_sb, g0_body, 0, unroll=DIL_UNROLL)

    d1 = DIL_GROUPS[1][1]

    def g1_head(r, carry):
        emit(1, r, d1, *_dilated_block(rows_of(c1_ref, r), rows_of(h1_ref, r), bias_sc[1], first))
        return carry

    def g1_body(sb, carry):
        start = (sb // d1) * (d1 * n) + sb % d1
        emit(1, start, d1, *_dilated_block(rows_of(c1_ref, sb), rows_of(c1_ref, sb - d1), bias_sc[1], None))
        return carry

    lax.fori_loop(0, d1, g1_head, 0, unroll=DIL_UNROLL)
    lax.fori_loop(d1, n_sb, g1_body, 0, unroll=DIL_UNROLL)

    d2 = DIL_GROUPS[2][1]

    def g2_body(r, carry):
        emit(2, r, d2, *_dilated_block(rows_of(c2_ref, r), rows_of(h2_ref, r), bias_sc[2], first))
        return carry

    lax.fori_loop(0, n_sb, g2_body, 0, unroll=DIL_UNROLL)

    def merge_body(c, carry):
        idx = pl.ds(pl.multiple_of(c * n, n), n)
        for half in range(DIL_OUT // HEAD_PAD):
            l0, l1, l2 = l_sc[0, half, idx, :], l_sc[1, half, idx, :], l_sc[2, half, idx, :]
            lmax = jnp.maximum(jnp.maximum(l0, l1), l2)
            e0, e1, e2 = jnp.exp(l0 - lmax), jnp.exp(l1 - lmax), jnp.exp(l2 - lmax)
            ob = (e0 * o_sc[0, half, idx, :] + e1 * o_sc[1, half, idx, :]
                  + e2 * o_sc[2, half, idx, :]) / (e0 + e1 + e2)
            ob_ref[idx, half * HEAD_PAD:(half + 1) * HEAD_PAD] = ob.astype(BF16)
        return carry

    lax.fori_loop(0, n_sb, merge_body, 0)


def _dilated_attention(zd0, zd1, zd2, seq):
    t = zd0.shape[0]
    unit = DIL_GROUPS[-1][0]
    upb = seq // unit
    n = DIL_STEPS
    u1 = DIL_GROUPS[1][0]
    all_slopes = _alibi_slopes(len(DIL_GROUPS) * DIL_HEADS).reshape(len(DIL_GROUPS), DIL_HEADS)
    slopes = tuple(tuple(float(x) * dil for x in all_slopes[gi]) for gi, (_, dil) in enumerate(DIL_GROUPS))

    def cur(bb, u):
        return (bb * upb + u, 0)

    def halo(rows):
        per_unit = unit // rows
        return lambda bb, u: ((bb * upb) * per_unit + jnp.maximum(u * per_unit - 1, 0), 0)

    return pl.pallas_call(
        functools.partial(_dilated_kernel, slopes=slopes),
        out_shape=jax.ShapeDtypeStruct((t, DIL_OUT), BF16),
        grid=(t // seq, upb),
        in_specs=[pl.BlockSpec((unit, DIL_COLS), cur), pl.BlockSpec((n, DIL_COLS), halo(n)),
                  pl.BlockSpec((unit, DIL_COLS), cur), pl.BlockSpec((u1, DIL_COLS), halo(u1)),
                  pl.BlockSpec((unit, DIL_COLS), cur), pl.BlockSpec((unit, DIL_COLS), halo(unit))],
        out_specs=pl.BlockSpec((unit, DIL_OUT), cur),
        scratch_shapes=[pltpu.VMEM((len(DIL_GROUPS), DIL_OUT // HEAD_PAD, unit, HEAD_PAD), F32),
                        pltpu.VMEM((len(DIL_GROUPS), DIL_OUT // HEAD_PAD, unit, HEAD_PAD), F32),
                        pltpu.VMEM((len(DIL_GROUPS), DIL_HEADS * n, 2 * n), F32)],
        compiler_params=_cparams(("parallel", "arbitrary")),
        name="dilated_attention",
    )(zd0, zd0, zd1, zd1, zd2, zd2)


def _merge_kernel(x_ref, oa_ref, ob_ref, gate_ref,
                  wa_ref, wb_ref, wo_ref, g_ref, wr_ref, br_ref,
                  x1_ref, hp_ref, topi_ref, topw_ref, wcol_ref):
    tm = x_ref.shape[0]
    ya = _dot(oa_ref[...], wa_ref[...])
    yb = _dot(ob_ref[...], wb_ref[...])
    mixed = gate_ref[:, :D_MODEL].astype(F32) * ya + gate_ref[:, D_MODEL:].astype(F32) * yb
    x1 = x_ref[...] + _dot(mixed.astype(BF16), wo_ref[...])
    x1_ref[...] = x1
    h2 = _rms(x1, g_ref[...])
    hp_ref[...] = _pack_halves(h2[:, :HALF], h2[:, HALF:])

    logits = _dot_nt(wr_ref[...], h2.astype(BF16)) + br_ref[...]
    eidx = lax.broadcasted_iota(I32, (N_EXPERTS, tm), 0)
    vals, idxs = [], []
    for _ in range(TOP_K):
        m = jnp.max(logits, axis=0, keepdims=True)
        idx = jnp.min(jnp.where(logits == m, eidx, N_EXPERTS), axis=0, keepdims=True)
        vals.append(m)
        idxs.append(idx)
        logits = jnp.where(eidx == idx, -jnp.inf, logits)
    exps = [jnp.exp(vk - vals[0]) for vk in vals]
    den = exps[0] + exps[1] + exps[2] + exps[3]
    row8 = lax.broadcasted_iota(I32, (8, tm), 0)
    row128 = lax.broadcasted_iota(I32, (HEAD_PAD, tm), 0)
    topi = jnp.zeros((8, tm), I32)
    topw = jnp.zeros((8, tm), F32)
    wide = jnp.zeros((HEAD_PAD, tm), F32)
    for kk in range(TOP_K):
        wk = exps[kk] / den
        topi = jnp.where(row8 == kk, idxs[kk], topi)
        topw = jnp.where(row8 == kk, wk, topw)
        wide = jnp.where(row128 == kk, wk, wide)
    topi_ref[...] = topi
    topw_ref[...] = topw
    wcol_ref[...] = wide.T


def _merge(x2d, oa, ob, gates, wa, wb, wo, g, wr_t, br_col):
    t = x2d.shape[0]
    tm = TOKEN_TILE

    def row(i):
        return (i, 0)

    def col(i):
        return (0, i)

    def full(a):
        return pl.BlockSpec(a.shape, lambda i: (0, 0))

    def rows(width):
        return pl.BlockSpec((tm, width), row)

    out_shape = [
        jax.ShapeDtypeStruct((t, D_MODEL), F32),
        jax.ShapeDtypeStruct((t, HALF), I32),
        jax.ShapeDtypeStruct((8, t), I32),
        jax.ShapeDtypeStruct((8, t), F32),
        jax.ShapeDtypeStruct((t, HEAD_PAD), F32),
    ]
    return pl.pallas_call(
        _merge_kernel,
        out_shape=out_shape,
        grid=(t // tm,),
        in_specs=[rows(D_MODEL), rows(MLA_HEADS * MLA_V), rows(DIL_OUT), rows(2 * D_MODEL)]
        + [full(wa), full(wb), full(wo), full(g), full(wr_t), full(br_col)],
        out_specs=[rows(D_MODEL), rows(HALF), pl.BlockSpec((8, tm), col), pl.BlockSpec((8, tm), col),
                   rows(HEAD_PAD)],
        compiler_params=_cparams(("parallel",)),
        name="merge_router",
    )(x2d, oa, ob, gates, wa, wb, wo, g, wr_t, br_col)


def _positions_kernel(topi_ref, dest_ref, meta_ref, cnt_sc, carry_sc, start_sc):
    ps = pl.program_id(0)
    i = pl.program_id(1)
    tm = topi_ref.shape[1]
    eidx = lax.broadcasted_iota(I32, (N_EXPERTS, tm), 0)
    topi = topi_ref[...]
    hits = [eidx == topi[kk:kk + 1, :] for kk in range(TOP_K)]
    member = (hits[0] | hits[1] | hits[2] | hits[3])
    tile_cnt = jnp.sum(member.astype(F32), axis=1, keepdims=True)

    @pl.when((ps == 0) & (i == 0))
    def _():
        cnt_sc[...] = jnp.zeros(cnt_sc.shape, F32)

    @pl.when(ps == 0)
    def _():
        cnt_sc[...] += tile_cnt

    @pl.when((ps == 1) & (i == 0))
    def _():
        cnt = cnt_sc[...].astype(I32)
        padded = lax.shift_left(lax.shift_right_logical(cnt + (ROW_BLOCK - 1), 8), 8)
        sub = lax.broadcasted_iota(I32, (N_EXPERTS, HEAD_PAD), 0)
        lane = lax.broadcasted_iota(I32, (N_EXPERTS, HEAD_PAD), 1)
        padded_row = jnp.sum(jnp.where(sub == lane, padded, 0), axis=0, keepdims=True)
        start = jnp.sum(jnp.where(lane < sub, padded_row, 0), axis=1, keepdims=True)
        start_sc[...] = start.astype(F32)
        carry_sc[...] = jnp.zeros(carry_sc.shape, F32)
        cnt_row = jnp.sum(jnp.where(sub == lane, cnt, 0), axis=0, keepdims=True)
        start_row = jnp.sum(jnp.where(sub == lane, start, 0), axis=0, keepdims=True)
        row8 = lax.broadcasted_iota(I32, (8, HEAD_PAD), 0)
        meta = jnp.where(row8 == 0, cnt_row, 0)
        meta = jnp.where(row8 == 1, start_row, meta)
        meta = jnp.where(row8 == 2, start_row + padded_row, meta)
        meta_ref[...] = meta

    @pl.when(ps == 1)
    def _():
        tr = lax.broadcasted_iota(I32, (tm, tm), 0)
        tc = lax.broadcasted_iota(I32, (tm, tm), 1)
        before = (tr < tc).astype(BF16)
        prefix = _dot(member.astype(BF16), before)
        base = prefix + carry_sc[...] + start_sc[...]
        row8 = lax.broadcasted_iota(I32, (8, tm), 0)
        dest = jnp.zeros((8, tm), I32)
        for kk in range(TOP_K):
            dk = jnp.sum(jnp.where(hits[kk], base, 0.0), axis=0, keepdims=True).astype(I32)
            dest = jnp.where(row8 == kk, dk, dest)
        dest_ref[...] = dest
        carry_sc[...] += tile_cnt


def _positions(topi_t):
    t = topi_t.shape[1]
    tm = TOKEN_TILE
    return pl.pallas_call(
        _positions_kernel,
        out_shape=[jax.ShapeDtypeStruct((8, t), I32), jax.ShapeDtypeStruct((8, HEAD_PAD), I32)],
        grid=(2, t // tm),
        in_specs=[pl.BlockSpec((8, tm), lambda ps, i: (0, i))],
        out_specs=[pl.BlockSpec((8, tm), lambda ps, i: (0, i * ps)),
                   pl.BlockSpec((8, HEAD_PAD), lambda ps, i: (0, 0))],
        scratch_shapes=[pltpu.VMEM((N_EXPERTS, 1), F32)] * 3,
        compiler_params=_cparams(("arbitrary", "arbitrary")),
        name="routing_positions",
    )(topi_t)


def _sc_mesh():
    return plsc.VectorSubcoreMesh(core_axis_name="c", subcore_axis_name="s")


def _dispatch_rows(table, dest_flat, n_rows):
    t, c = table.shape
    n_slots = dest_flat.shape[0] // t
    per_w = t // SC_WORKERS
    assert per_w * SC_WORKERS == t and per_w % (2 * SC_WINDOW) == 0
    n_chunks = per_w // SC_WINDOW
    w = SC_WINDOW

    @functools.partial(
        pl.kernel, mesh=_sc_mesh(),
        out_type=jax.ShapeDtypeStruct((n_rows, c), table.dtype),
        scratch_types=[pltpu.VMEM((w,), I32)] * n_slots + [pltpu.VMEM((w, c), table.dtype)] * 2
        + [pltpu.SemaphoreType.DMA] * (n_slots + 2),
        name="dispatch_rows",
    )
    def k(table_hbm, dest_hbm, out_hbm, *scratch):
        idx = scratch[:n_slots]
        rows = scratch[n_slots:n_slots + 2]
        scatter_sems = scratch[n_slots + 2:2 * n_slots + 2]
        read_sems = scratch[2 * n_slots + 2:]
        wid = lax.axis_index("s") * SC_CORES + lax.axis_index("c")
        base = wid * per_w

        def off(chunk):
            return pl.multiple_of(base + chunk * w, w)

        def read(chunk, buf):
            return pltpu.make_async_copy(table_hbm.at[pl.ds(off(chunk), w)], rows[buf], read_sems[buf])

        def scatter(kk, buf):
            return pltpu.make_async_copy(rows[buf], out_hbm.at[idx[kk]], scatter_sems[kk])

        read(0, 0).start()

        @pl.loop(0, n_chunks // 2)
        def _(p):
            for buf in range(2):
                chunk = 2 * p + buf

                @pl.when(chunk + 1 < n_chunks)
                def _():
                    read(chunk + 1, 1 - buf).start()

                read(chunk, buf).wait()
                for kk in range(n_slots):
                    src = pl.multiple_of(kk * t + off(chunk), w)
                    pltpu.sync_copy(dest_hbm.at[pl.ds(src, w)], idx[kk])
                    scatter(kk, buf).start()
                for kk in range(n_slots):
                    scatter(kk, buf).wait()

    return k(table, dest_flat)


def _gather_rows(table, idx):
    n = idx.shape[0]
    c = table.shape[1]
    per_w = n // SC_WORKERS
    assert per_w * SC_WORKERS == n and per_w % (2 * SC_WINDOW) == 0
    n_chunks = per_w // SC_WINDOW
    w = SC_WINDOW

    @functools.partial(
        pl.kernel, mesh=_sc_mesh(),
        out_type=jax.ShapeDtypeStruct((n, c), table.dtype),
        scratch_types=[pltpu.VMEM((w,), I32)] * 2 + [pltpu.VMEM((w, c), table.dtype)] * 2
        + [pltpu.SemaphoreType.DMA] * 4,
        name="gather_rows",
    )
    def k(table_hbm, idx_hbm, out_hbm, idx_a, idx_b, rows_a, rows_b, g_a, g_b, w_a, w_b):
        idx, rows, gather_sems, write_sems = (idx_a, idx_b), (rows_a, rows_b), (g_a, g_b), (w_a, w_b)
        wid = lax.axis_index("s") * SC_CORES + lax.axis_index("c")
        base = wid * per_w

        def off(chunk):
            return pl.multiple_of(base + chunk * w, w)

        def gather(buf):
            return pltpu.make_async_copy(table_hbm.at[idx[buf]], rows[buf], gather_sems[buf])

        def write(chunk, buf):
            return pltpu.make_async_copy(rows[buf], out_hbm.at[pl.ds(off(chunk), w)], write_sems[buf])

        def start_gather(chunk, buf):
            pltpu.sync_copy(idx_hbm.at[pl.ds(off(chunk), w)], idx[buf])
            gather(buf).start()

        start_gather(0, 0)

        @pl.loop(0, n_chunks // 2)
        def _(p):
            for buf in range(2):
                chunk = 2 * p + buf

                @pl.when(chunk + 1 < n_chunks)
                def _():
                    @pl.when(chunk >= 1)
                    def _():
                        write(chunk - 1, 1 - buf).wait()
                    start_gather(chunk + 1, 1 - buf)

                gather(buf).wait()
                write(chunk, buf).start()

        write(n_chunks - 2, 0).wait()
        write(n_chunks - 1, 1).wait()

    return k(table, idx)


def _expert_kernel(be_ref, nused_ref, first_ref, slot_ref, next_ref,
                   xs_ref, wg_hbm, bg_ref, wu_hbm, bu_ref, wd_hbm, bd_ref,
                   ys_ref, wg_ref, wu_ref, wd_ref, stage_g, stage_u, stage_d, sems, *, layer):
    b = pl.program_id(0)
    used = b < nused_ref[0]
    weights = ((wg_hbm, stage_g), (wu_hbm, stage_u), (wd_hbm, stage_d))

    def fetch(expert, slot):
        return [pltpu.make_async_copy(hbm.at[layer, expert], stage.at[slot], sems.at[slot, n])
                for n, (hbm, stage) in enumerate(weights)]

    @pl.when(used & (first_ref[b] == 1))
    def _():
        slot = slot_ref[b]

        @pl.when(b == 0)
        def _():
            for cp in fetch(be_ref[0], slot):
                cp.start()

        for cp in fetch(be_ref[b], slot):
            cp.wait()
        wg_ref[...] = stage_g[slot].astype(BF16)
        wu_ref[...] = stage_u[slot].astype(BF16)
        wd_ref[...] = stage_d[slot].astype(BF16)

        @pl.when(next_ref[b] >= 0)
        def _():
            for cp in fetch(next_ref[b], 1 - slot):
                cp.start()

    @pl.when(used)
    def _():
        lo, hi = _unpack_halves(xs_ref[...])
        lo = lo.astype(BF16)
        hi = hi.astype(BF16)
        a = _dot(lo, wg_ref[:HALF, :]) + _dot(hi, wg_ref[HALF:, :]) + bg_ref[...]
        u = _dot(lo, wu_ref[:HALF, :]) + _dot(hi, wu_ref[HALF:, :]) + bu_ref[...]
        a = jnp.minimum(a, SWIGLU_LIMIT)
        u = jnp.clip(u, -SWIGLU_LIMIT, SWIGLU_LIMIT)
        y = (a * jax.nn.sigmoid(SWIGLU_ALPHA * a)) * (u + 1.0)
        out = _dot(y.astype(BF16), wd_ref[...]) + bd_ref[...]
        ys_ref[...] = _pack_halves(out[:, :HALF], out[:, HALF:])

    @pl.when(b >= nused_ref[0])
    def _():
        ys_ref[...] = jnp.zeros(ys_ref.shape, I32)


def _expert_ffn(xs, block_e, n_used, counts, layer, wg, bg, wu, bu, wd, bd):
    n_rows = xs.shape[0]
    n_blocks = n_rows // ROW_BLOCK

    blk = jnp.arange(n_blocks, dtype=I32)
    used = blk < n_used[0]
    first = used & ((blk == 0) | (block_e != jnp.roll(block_e, 1)))
    slot = (jnp.cumsum(first.astype(I32)) - 1) % 2
    eid = jnp.arange(N_EXPERTS, dtype=I32)
    later = (eid[None, :] > eid[:, None]) & (counts[None, :] > 0)
    next_of_expert = jnp.min(jnp.where(later, eid[None, :], N_EXPERTS), axis=1)
    next_of_expert = jnp.where(next_of_expert == N_EXPERTS, -1, next_of_expert)
    next_e = next_of_expert[block_e]

    def rows(b, *_):
        return (b, 0)

    def expert(b, be, *_):
        return (layer, be[b], 0, 0)

    hbm = pl.BlockSpec(memory_space=pl.ANY)
    grid_spec = pltpu.PrefetchScalarGridSpec(
        num_scalar_prefetch=5,
        grid=(n_blocks,),
        in_specs=[pl.BlockSpec((ROW_BLOCK, HALF), rows),
                  hbm, pl.BlockSpec((None, None, 1, D_EXPERT), expert),
                  hbm, pl.BlockSpec((None, None, 1, D_EXPERT), expert),
                  hbm, pl.BlockSpec((None, None, 1, D_MODEL), expert)],
        out_specs=pl.BlockSpec((ROW_BLOCK, HALF), rows),
        scratch_shapes=[pltpu.VMEM((D_MODEL, D_EXPERT), BF16), pltpu.VMEM((D_MODEL, D_EXPERT), BF16),
                        pltpu.VMEM((D_EXPERT, D_MODEL), BF16),
                        pltpu.VMEM((2, D_MODEL, D_EXPERT), F32), pltpu.VMEM((2, D_MODEL, D_EXPERT), F32),
                        pltpu.VMEM((2, D_EXPERT, D_MODEL), F32),
                        pltpu.SemaphoreType.DMA((2, 3))],
    )
    return pl.pallas_call(
        functools.partial(_expert_kernel, layer=layer),
        out_shape=jax.ShapeDtypeStruct((n_rows, HALF), I32),
        grid_spec=grid_spec,
        compiler_params=_cparams(("arbitrary",)),
        name="expert_ffn",
    )(block_e, n_used, first.astype(I32), slot.astype(I32), next_e.astype(I32),
      xs, wg, bg, wu, bu, wd, bd)


def _combine_kernel(x1_ref, yg_ref, wcol_ref, p_ref, gple_ref, wpg_ref, wpp_ref, gout_ref, o_ref,
                    *, final):
    x1 = x1_ref[...]
    acc_lo = x1[:, :HALF]
    acc_hi = x1[:, HALF:]
    wcol = wcol_ref[...]
    for kk in range(TOP_K):
        lo, hi = _unpack_halves(yg_ref[kk])
        wk = wcol[:, kk:kk + 1]
        acc_lo = acc_lo + wk * lo
        acc_hi = acc_hi + wk * hi
    x2 = jnp.concatenate([acc_lo, acc_hi], axis=1)
    gate = jax.nn.sigmoid(_dot(_rms(x2, gple_ref[...]).astype(BF16), wpg_ref[...]))
    x3 = x2 + gate * _dot(p_ref[...].astype(BF16), wpp_ref[...])
    o_ref[...] = _rms(x3, gout_ref[...]) if final else x3


def _combine(x1, yg, wcol, p3d, layer, gple, wpg, wpp, gout, final):
    t = x1.shape[0]
    tm = TOKEN_TILE

    def row(i):
        return (i, 0)

    def full(a):
        return pl.BlockSpec(a.shape, lambda i: (0, 0))

    return pl.pallas_call(
        functools.partial(_combine_kernel, final=final),
        out_shape=jax.ShapeDtypeStruct((t, D_MODEL), F32),
        grid=(t // tm,),
        in_specs=[pl.BlockSpec((tm, D_MODEL), row),
                  pl.BlockSpec((TOP_K, tm, HALF), lambda i: (0, i, 0)),
                  pl.BlockSpec((tm, HEAD_PAD), row),
                  pl.BlockSpec((None, tm, PLE_DIM), lambda i: (layer, i, 0)),
                  full(gple), full(wpg), full(wpp), full(gout)],
        out_specs=pl.BlockSpec((tm, D_MODEL), row),
        compiler_params=_cparams(("parallel",)),
        name="combine_ple",
    )(x1, yg, wcol, p3d, gple, wpg, wpp, gout)


def _rope_tables(seq):
    inv_freq = ROPE_THETA ** (-jnp.arange(HALF_ROPE, dtype=F32) * 2.0 / MLA_ROPE)
    ang = jnp.arange(seq, dtype=F32)[:, None] * inv_freq[None, :]
    cos, sin = jnp.cos(ang), jnp.sin(ang)
    ones = jnp.ones((seq, MLA_NOPE), F32)
    zeros16 = jnp.zeros((seq, HALF_ROPE), F32)
    zeros64 = jnp.zeros((seq, MLA_NOPE), F32)
    tail = jnp.ones((seq, HEAD_PAD - MLA_NOPE - MLA_ROPE), F32)
    ztail = jnp.zeros_like(tail)
    cos_t = jnp.concatenate([ones, cos, cos, tail], axis=1)
    sina_t = jnp.concatenate([zeros64, zeros16, sin, ztail], axis=1)
    sinb_t = jnp.concatenate([zeros64, -sin, zeros16, ztail], axis=1)
    return cos_t, sina_t, sinb_t


def _prep_mixer_weights(w_in, w_uq, w_ukv):
    c0 = MLA_Q_LORA + MLA_KV_LORA
    c1 = c0 + MLA_ROPE
    c2 = c1 + len(DIL_GROUPS) * DIL_COLS
    kr_pad = jnp.zeros((D_MODEL, HEAD_PAD), F32).at[:, MLA_NOPE:MLA_NOPE + MLA_ROPE].set(w_in[:, c0:c1])
    wmla = jnp.concatenate([w_in[:, :c0], kr_pad], axis=1).astype(BF16)
    wdil = w_in[:, c1:c2].reshape(D_MODEL, len(DIL_GROUPS), 3, DIL_OUT)
    wdil = wdil.at[:, :, 0, :].multiply(DIL_HEAD_DIM ** -0.5).reshape(D_MODEL, -1).astype(BF16)
    wgate = w_in[:, c2:].astype(BF16)
    pad = HEAD_PAD - MLA_NOPE - MLA_ROPE
    wuq_h = w_uq.reshape(MLA_Q_LORA, MLA_HEADS, MLA_NOPE + MLA_ROPE)
    wuq = jnp.pad(wuq_h, ((0, 0), (0, 0), (0, pad))).reshape(MLA_Q_LORA, MLA_HEADS * HEAD_PAD).astype(BF16)
    wukv_h = w_ukv.reshape(MLA_KV_LORA, MLA_HEADS, MLA_NOPE + MLA_V)
    wuk = jnp.pad(wukv_h[:, :, :MLA_NOPE], ((0, 0), (0, 0), (0, HEAD_PAD - MLA_NOPE)))
    wuk = wuk.reshape(MLA_KV_LORA, MLA_HEADS * HEAD_PAD).astype(BF16)
    wuv = wukv_h[:, :, MLA_NOPE:].reshape(MLA_KV_LORA, MLA_HEADS * MLA_V).astype(BF16)
    return wmla, wdil, wgate, wuq, wuk, wuv


def kernel(x, p, attn_norm, w_in, q_norm, w_uq, kv_norm, w_ukv, w_branch_a, w_branch_b, w_out, ffn_norm, w_router, b_router, w_gate, b_gate, w_up, b_up, w_down, b_down, ple_norm, w_ple_gate, w_ple_proj, final_norm):
    b, s, d = x.shape
    depth = w_in.shape[0]
    t = b * s
    assert d == D_MODEL and s % (DIL_GROUPS[-1][0]) == 0 and t % (SC_WORKERS * SC_WINDOW) == 0
    n_assign = t * TOP_K
    n_blocks = -(-(n_assign + N_EXPERTS * (ROW_BLOCK - 1)) // ROW_BLOCK)
    n_rows = n_blocks * ROW_BLOCK
    cos_t, sina_t, sinb_t = _rope_tables(s)
    xc = x.reshape(t, d)
    for i in range(depth):
        wmla, wdil, wgate, wuq, wuk, wuv = _prep_mixer_weights(w_in[i], w_uq[i], w_ukv[i])
        q, k, vt, zd0, zd1, zd2, gates = _inproj(
            xc, s, attn_norm[i][None], wmla, wdil, wgate, q_norm[i][None], kv_norm[i][None],
            wuq, wuk, wuv, cos_t, sina_t, sinb_t)
        oa = _mla_attention(q.reshape(b, s, -1), k.reshape(b, s, -1), vt)
        ob = _dilated_attention(zd0, zd1, zd2, s)
        x1, hp, topi_t, topw_t, wcol = _merge(
            xc, oa.reshape(t, -1), ob, gates,
            w_branch_a[i].astype(BF16), w_branch_b[i].astype(BF16), w_out[i].astype(BF16),
            ffn_norm[i][None], w_router[i].T.astype(BF16), b_router[i][:, None])
        dest_t, meta = _positions(topi_t)
        ends = meta[2, :N_EXPERTS]
        block_start = jnp.arange(n_blocks, dtype=I32) * ROW_BLOCK
        block_e = jnp.minimum(
            jnp.sum((ends[None, :] <= block_start[:, None]).astype(I32), axis=1), N_EXPERTS - 1)
        n_used = (ends[N_EXPERTS - 1:] // ROW_BLOCK).astype(I32)
        dest_flat = dest_t[:TOP_K].reshape(n_assign)
        xs = _dispatch_rows(hp, dest_flat, n_rows)
        ys = _expert_ffn(xs, block_e, n_used, meta[0, :N_EXPERTS], i,
                         w_gate, b_gate[:, :, None, :], w_up, b_up[:, :, None, :],
                         w_down, b_down[:, :, None, :])
        yg = _gather_rows(ys, dest_flat).reshape(TOP_K, t, HALF)
        final = i == depth - 1
        gout = final_norm[None] if final else attn_norm[i][None]
        xc = _combine(x1, yg, wcol, p.reshape(depth, t, PLE_DIM), i, ple_norm[i][None],
                      w_ple_gate[i].astype(BF16), w_ple_proj[i].astype(BF16), gout, final)
    return xc.reshape(b, s, d)
```

```python
import functools
import math

import jax
import jax.numpy as jnp
import numpy as np
from jax import lax
from jax.experimental import pallas as pl
from jax.experimental.pallas import tpu as pltpu
from jax.experimental.pallas import tpu_sc as plsc

F32 = jnp.float32
BF16 = jnp.bfloat16
I32 = jnp.int32

D_MODEL = 1024
PLE_DIM = 256
NORM_EPS = 1e-6

MLA_HEADS = 8
MLA_Q_LORA = 384
MLA_KV_LORA = 256
MLA_NOPE = 64
MLA_ROPE = 32
MLA_V = 64
ROPE_THETA = 10000.0
HEAD_PAD = 128
HALF_ROPE = MLA_ROPE // 2

DIL_GROUPS = ((128, 1), (512, 4), (2048, 16))
DIL_HEADS = 4
DIL_HEAD_DIM = 64
DIL_STEPS = 128
DIL_COLS = 3 * DIL_HEADS * DIL_HEAD_DIM
DIL_OUT = DIL_HEADS * DIL_HEAD_DIM
DIL_UNROLL = 4

N_EXPERTS = 32
TOP_K = 4
D_EXPERT = 1024
SWIGLU_LIMIT = 7.0
SWIGLU_ALPHA = 1.702
ROW_BLOCK = 256

TOKEN_TILE = 512
POSITION_TILE = 4096
POSITION_SUBTILE = 512
COMBINE_PARTS = 2
ATTN_TQ = 1024
ATTN_TK = 1024
ATTN_SUB = 1024
HALF = D_MODEL // 2
NEG = -1e30
SPECULATION_HEADROOM = 60.0
HI_MASK = -65536

SC_CORES = 2
SC_SUBCORES = 16
SC_WORKERS = SC_CORES * SC_SUBCORES
SC_WINDOW = 64

VMEM_LIMIT = 56 * 1024 * 1024


def _cparams(sem):
    return pltpu.CompilerParams(dimension_semantics=sem, vmem_limit_bytes=VMEM_LIMIT)


def _rms(x, g):
    return x * lax.rsqrt(jnp.mean(x * x, axis=-1, keepdims=True) + NORM_EPS) * g


def _dot(a, b):
    return jnp.dot(a, b, preferred_element_type=F32)


def _dot_nt(a, b):
    return lax.dot_general(a, b, (((1,), (1,)), ((), ())), preferred_element_type=F32)


def _pack_halves(lo, hi):
    lo_i = lax.bitcast_convert_type(lo.astype(BF16).astype(F32), I32)
    hi_i = lax.bitcast_convert_type(hi.astype(BF16).astype(F32), I32)
    return (hi_i & HI_MASK) | lax.shift_right_logical(lo_i, 16)


def _unpack_halves(w):
    lo = lax.bitcast_convert_type(lax.shift_left(w, 16), F32)
    hi = lax.bitcast_convert_type(w & HI_MASK, F32)
    return lo, hi


def _inproj_kernel(x_ref, g_ref, wmla_ref, wdil_ref, qn_ref, kvn_ref, wuq_ref, wuk_ref,
                   wuv_ref, cos_ref, sina_ref, sinb_ref,
                   q_ref, k_ref, vt_ref, zd0_ref, zd1_ref, zd2_ref, zs_sc):
    h = _rms(x_ref[...], g_ref[...]).astype(BF16)
    zm = _dot(h, wmla_ref[...])
    cq = _rms(zm[:, :MLA_Q_LORA], qn_ref[...]).astype(BF16)
    ckv = _rms(zm[:, MLA_Q_LORA:MLA_Q_LORA + MLA_KV_LORA], kvn_ref[...]).astype(BF16)
    kr = zm[:, MLA_Q_LORA + MLA_KV_LORA:]
    cos, sina, sinb = cos_ref[...], sina_ref[...], sinb_ref[...]

    def rope(t):
        return (t * cos + pltpu.roll(t, HALF_ROPE, 1) * sina
                + pltpu.roll(t, HEAD_PAD - HALF_ROPE, 1) * sinb)

    kr_rot = rope(kr)
    qraw = _dot(cq, wuq_ref[...])
    kraw = _dot(ckv, wuk_ref[...])
    vt_ref[...] = _dot(ckv, wuv_ref[...]).T.astype(BF16)
    scale = (MLA_NOPE + MLA_ROPE) ** -0.5 * math.log2(math.e)
    for hd in range(MLA_HEADS):
        sl = slice(hd * HEAD_PAD, (hd + 1) * HEAD_PAD)
        q_ref[:, sl] = (rope(qraw[:, sl]) * scale).astype(BF16)
        k_ref[:, sl] = (kraw[:, sl] + kr_rot).astype(BF16)
    tm = x_ref.shape[0]
    for gi, zd_ref in enumerate((zd0_ref, zd1_ref, zd2_ref)):
        z = _dot(h, wdil_ref[:, gi * DIL_COLS:(gi + 1) * DIL_COLS])
        window, dil = DIL_GROUPS[gi]
        if dil == 1:
            zd_ref[...] = z.astype(BF16)
            continue
        n_col = DIL_COLS // HEAD_PAD
        for c in range(n_col):
            zs_sc[c] = z[:, c * HEAD_PAD:(c + 1) * HEAD_PAD]
        rows = tm // dil
        part = pl.program_id(0) % (window // tm)
        for r in range(dil):
            dst = pl.ds(pl.multiple_of(r * DIL_STEPS + part * rows, rows), rows)
            for c in range(n_col):
                chunk = zs_sc[c, pl.ds(r, rows, stride=dil), :]
                zd_ref[dst, c * HEAD_PAD:(c + 1) * HEAD_PAD] = chunk.astype(BF16)


def _inproj(x2d, seq, g, wmla, wdil, qn, kvn, wuq, wuk, wuv, cos_t, sina_t, sinb_t):
    t = x2d.shape[0]
    tm = TOKEN_TILE
    n_seq_tiles = seq // tm

    def row(i):
        return (i, 0)

    def const(i):
        return (0, 0)

    def pos(i):
        return (i % n_seq_tiles, 0)

    def full(a):
        return pl.BlockSpec(a.shape, const)

    def vt_block(i):
        return (i // n_seq_tiles, 0, i % n_seq_tiles)

    def rows(width):
        return pl.BlockSpec((tm, width), row)

    def unit_rows(window):
        return pl.BlockSpec((window, DIL_COLS), lambda i: (i // (window // tm), 0))

    out_shape = [
        jax.ShapeDtypeStruct((t, MLA_HEADS * HEAD_PAD), BF16),
        jax.ShapeDtypeStruct((t, MLA_HEADS * HEAD_PAD), BF16),
        jax.ShapeDtypeStruct((t // seq, MLA_HEADS * MLA_V, seq), BF16),
        jax.ShapeDtypeStruct((t, DIL_COLS), BF16),
        jax.ShapeDtypeStruct((t, DIL_COLS), BF16),
        jax.ShapeDtypeStruct((t, DIL_COLS), BF16),
    ]
    return pl.pallas_call(
        _inproj_kernel,
        out_shape=out_shape,
        grid=(t // tm,),
        in_specs=[pl.BlockSpec((tm, D_MODEL), row), full(g), full(wmla), full(wdil),
                  full(qn), full(kvn), full(wuq), full(wuk), full(wuv),
                  pl.BlockSpec((tm, HEAD_PAD), pos), pl.BlockSpec((tm, HEAD_PAD), pos),
                  pl.BlockSpec((tm, HEAD_PAD), pos)],
        out_specs=[rows(D_MODEL), rows(D_MODEL),
                   pl.BlockSpec((None, MLA_HEADS * MLA_V, tm), vt_block),
                   rows(DIL_COLS)] + [unit_rows(window) for window, _ in DIL_GROUPS[1:]],
        scratch_shapes=[pltpu.VMEM((DIL_COLS // HEAD_PAD, tm, HEAD_PAD), F32)],
        compiler_params=_cparams(("arbitrary",)),
        name="inproj",
    )(x2d, g, wmla, wdil, qn, kvn, wuq, wuk, wuv, cos_t, sina_t, sinb_t)


def _mla_kernel(qi_ref, kj_ref, q_ref, k_ref, vt_ref, o_ref, m_sc, l_sc, acc_sc, redo_sc):
    p = pl.program_id(2)
    i = qi_ref[p]
    j = kj_ref[p]
    tq = q_ref.shape[0]
    tk = k_ref.shape[0]

    @pl.when(j == 0)
    def _():
        m_sc[...] = jnp.full(m_sc.shape, NEG, F32)
        l_sc[...] = jnp.zeros(l_sc.shape, F32)
        acc_sc[...] = jnp.zeros(acc_sc.shape, F32)

    ratio = tq // tk
    sub = ATTN_SUB

    def step(diagonal, speculative, first=False):
        chains = [(hh, c) for hh in range(2) for c in range(tq // sub)]
        ones_rows = (lax.broadcasted_iota(I32, (16, tk), 0) == 0).astype(BF16)
        state = {}
        for hh, c in chains:
            cs = slice(c * sub, (c + 1) * sub)
            if first and speculative:
                sl = slice(hh * HEAD_PAD, (hh + 1) * HEAD_PAD)
                s0 = _dot_nt(k_ref[:8, sl], q_ref[cs, sl])[:1]
                state[hh, c] = (s0, jnp.zeros((1, sub), F32), jnp.zeros((MLA_V, sub), F32))
            else:
                state[hh, c] = (m_sc[hh, :, cs], l_sc[hh, :, cs], acc_sc[hh, :, cs])
        new_state = {}
        within = None
        for hh, c in chains:
            sl = slice(hh * HEAD_PAD, (hh + 1) * HEAD_PAD)
            st = _dot_nt(k_ref[:, sl], q_ref[c * sub:(c + 1) * sub, sl])
            if diagonal:
                key = lax.broadcasted_iota(I32, (tk, sub), 0) + j * tk
                qry = lax.broadcasted_iota(I32, (tk, sub), 1) + (i * tq + c * sub)
                st = jnp.where(qry >= key, st, NEG)
            m_prev, l_prev, acc_prev = state[hh, c]
            m_blk = jnp.max(st, axis=0, keepdims=True)
            m_new = jnp.maximum(m_prev, m_blk)
            alpha = jnp.exp2(m_prev - m_new)
            lhs = jnp.concatenate([vt_ref[hh * MLA_V:(hh + 1) * MLA_V, :], ones_rows], axis=0)
            if speculative:
                pv = _dot(lhs, jnp.exp2(st - m_prev).astype(BF16))
                l_new = alpha * (l_prev + pv[MLA_V:MLA_V + 1])
                acc_new = alpha * (acc_prev + pv[:MLA_V])
                ok = jnp.max(m_blk - m_prev) <= SPECULATION_HEADROOM
                within = ok if within is None else (within & ok)
            else:
                pv = _dot(lhs, jnp.exp2(st - m_new).astype(BF16))
                l_new = alpha * l_prev + pv[MLA_V:MLA_V + 1]
                acc_new = alpha * acc_prev + pv[:MLA_V]
            new_state[hh, c] = (m_new, l_new, acc_new)

        def commit():
            for hh, c in chains:
                cs = slice(c * sub, (c + 1) * sub)
                m_sc[hh, :, cs], l_sc[hh, :, cs], acc_sc[hh, :, cs] = new_state[hh, c]

        if speculative:
            pl.when(within)(commit)
            redo_sc[0] = jnp.logical_not(within).astype(I32)
        else:
            commit()

    redo_sc[0] = 0
    on_diagonal = j >= ratio * i

    @pl.when((j == 0) & on_diagonal)
    def _():
        step(True, True, first=True)

    @pl.when((j == 0) & jnp.logical_not(on_diagonal))
    def _():
        step(False, True, first=True)

    @pl.when((j > 0) & jnp.logical_not(on_diagonal))
    def _():
        step(False, True)

    @pl.when((j > 0) & on_diagonal)
    def _():
        step(True, True)

    @pl.when(redo_sc[0] != 0)
    def _():
        step(True, False)

    @pl.when(j == ratio * i + (ratio - 1))
    def _():
        ot = jnp.concatenate([acc_sc[0] / l_sc[0], acc_sc[1] / l_sc[1]], axis=0)
        o_ref[...] = ot.T.astype(BF16)


def _mla_attention(q, k, vt):
    b, s, _ = q.shape
    tq, tk = ATTN_TQ, ATTN_TK
    ratio = tq // tk
    nq = s // tq
    pairs = [(i, j) for i in range(nq) for j in range(ratio * (i + 1))]
    qi = jnp.asarray([p[0] for p in pairs], I32)
    kj = jnp.asarray([p[1] for p in pairs], I32)
    grid_spec = pltpu.PrefetchScalarGridSpec(
        num_scalar_prefetch=2,
        grid=(b, MLA_HEADS // 2, len(pairs)),
        in_specs=[
            pl.BlockSpec((None, tq, 2 * HEAD_PAD), lambda bb, hp, p, qi, kj: (bb, qi[p], hp)),
            pl.BlockSpec((None, tk, 2 * HEAD_PAD), lambda bb, hp, p, qi, kj: (bb, kj[p], hp)),
            pl.BlockSpec((None, 2 * MLA_V, tk), lambda bb, hp, p, qi, kj: (bb, hp, kj[p])),
        ],
        out_specs=pl.BlockSpec((None, tq, 2 * MLA_V), lambda bb, hp, p, qi, kj: (bb, qi[p], hp)),
        scratch_shapes=[pltpu.VMEM((2, 1, tq), F32), pltpu.VMEM((2, 1, tq), F32),
                        pltpu.VMEM((2, MLA_V, tq), F32), pltpu.SMEM((1,), I32)],
    )
    return pl.pallas_call(
        _mla_kernel,
        out_shape=jax.ShapeDtypeStruct((b, s, MLA_HEADS * MLA_V), BF16),
        grid_spec=grid_spec,
        compiler_params=_cparams(("parallel", "parallel", "arbitrary")),
        name="mla_attention",
    )(qi, kj, q, k, vt)


def _alibi_slopes(n):
    def pow2(m):
        start = 2.0 ** (-8.0 / m)
        return [start ** (i + 1) for i in range(m)]
    if math.log2(n).is_integer():
        s = pow2(n)
    else:
        c = 2 ** int(math.floor(math.log2(n)))
        s = pow2(c) + pow2(2 * c)[0::2][: n - c]
    return np.array(sorted(s, reverse=True), dtype=np.float32)


def _dilated_block(cur, prev, bias4, first):
    n = DIL_STEPS
    hw = DIL_OUT
    q = cur[:, :hw]
    kk = jnp.concatenate([prev[:, hw:2 * hw], cur[:, hw:2 * hw]], axis=0)
    vv = jnp.concatenate([prev[:, 2 * hw:], cur[:, 2 * hw:]], axis=0)
    head_of_lane = lax.broadcasted_iota(I32, (n, hw), 1) // DIL_HEAD_DIM
    zero = jnp.zeros_like(q)
    q4 = jnp.concatenate([jnp.where(head_of_lane == h, q, zero) for h in range(DIL_HEADS)], axis=0)
    s4 = _dot_nt(q4, kk) + bias4
    if first is not None:
        ki = lax.broadcasted_iota(I32, (DIL_HEADS * n, 2 * n), 1)
        s4 = jnp.where(first & (ki < n), NEG, s4)
    m4 = jnp.max(s4, axis=1, keepdims=True)
    p4 = jnp.exp(s4 - m4).astype(BF16)
    l4 = _dot(p4, jnp.ones((2 * n, HEAD_PAD), BF16))
    pv4 = _dot(p4, vv)
    m4 = jnp.broadcast_to(m4, (DIL_HEADS * n, HEAD_PAD))

    def rows(a, h):
        return a[h * n:(h + 1) * n]

    o_un = rows(pv4, DIL_HEADS - 1)
    for h in range(DIL_HEADS - 2, -1, -1):
        o_un = jnp.where(head_of_lane == h, rows(pv4, h), o_un)
    low = lax.broadcasted_iota(I32, (n, HEAD_PAD), 1) < DIL_HEAD_DIM

    def per_lane(a):
        return jnp.concatenate([jnp.where(low, rows(a, 0), rows(a, 1)),
                                jnp.where(low, rows(a, 2), rows(a, 3))], axis=1)

    l_sel = per_lane(l4)
    return o_un / l_sel, per_lane(m4) + jnp.log(l_sel)


def _dilated_kernel(c0_ref, h0_ref, c1_ref, h1_ref, c2_ref, h2_ref, ob_ref, o_sc, l_sc, bias_sc,
                    *, slopes):
    u = pl.program_id(1)
    n = DIL_STEPS
    unit = ob_ref.shape[0]
    n_sb = unit // n
    first = u == 0
    qi = lax.broadcasted_iota(I32, (n, 2 * n), 0)
    ki = lax.broadcasted_iota(I32, (n, 2 * n), 1)
    dist = qi + n - ki
    valid = (dist >= 0) & (dist <= n)
    distf = dist.astype(F32)
    for gi in range(len(DIL_GROUPS)):
        for h in range(DIL_HEADS):
            bias_sc[gi, h * n:(h + 1) * n, :] = jnp.where(valid, -slopes[gi][h] * distf, NEG)

    def rows_of(ref, sb):
        return ref[pl.ds(pl.multiple_of(sb * n, n), n), :]

    def emit(gi, start, stride, o, lse):
        if isinstance(start, int):
            idx = pl.ds(start, n)
        elif stride == 1:
            idx = pl.ds(pl.multiple_of(start, n), n)
        else:
            idx = pl.ds(start, n, stride=stride)
        for half in range(DIL_OUT // HEAD_PAD):
            ls = slice(half * HEAD_PAD, (half + 1) * HEAD_PAD)
            o_sc[gi, half, idx, :] = o[:, ls]
            l_sc[gi, half, idx, :] = lse[:, ls]

    emit(0, 0, 1, *_dilated_block(c0_ref[:n, :], h0_ref[...], bias_sc[0], first))

    def g0_body(sb, carry):
        emit(0, sb * n, 1, *_dilated_block(rows_of(c0_ref, sb), rows_of(c0_ref, sb - 1), bias_sc[0], None))
        return carry

    lax.fori_loop(1, n_sb, g0_body, 0, unroll=DIL_UNROLL)

    d1 = DIL_GROUPS[1][1]

    def g1_head(r, carry):
        emit(1, r, d1, *_dilated_block(rows_of(c1_ref, r), rows_of(h1_ref, r), bias_sc[1], first))
        return carry

    def g1_body(sb, carry):
        start = (sb // d1) * (d1 * n) + sb % d1
        emit(1, start, d1, *_dilated_block(rows_of(c1_ref, sb), rows_of(c1_ref, sb - d1), bias_sc[1], None))
        return carry

    lax.fori_loop(0, d1, g1_head, 0, unroll=DIL_UNROLL)
    lax.fori_loop(d1, n_sb, g1_body, 0, unroll=DIL_UNROLL)

    d2 = DIL_GROUPS[2][1]

    def g2_body(r, carry):
        emit(2, r, d2, *_dilated_block(rows_of(c2_ref, r), rows_of(h2_ref, r), bias_sc[2], first))
        return carry

    lax.fori_loop(0, n_sb, g2_body, 0, unroll=DIL_UNROLL)

    def merge_body(c, carry):
        idx = pl.ds(pl.multiple_of(c * n, n), n)
        for half in range(DIL_OUT // HEAD_PAD):
            l0, l1, l2 = l_sc[0, half, idx, :], l_sc[1, half, idx, :], l_sc[2, half, idx, :]
            lmax = jnp.maximum(jnp.maximum(l0, l1), l2)
            e0, e1, e2 = jnp.exp(l0 - lmax), jnp.exp(l1 - lmax), jnp.exp(l2 - lmax)
            ob = (e0 * o_sc[0, half, idx, :] + e1 * o_sc[1, half, idx, :]
                  + e2 * o_sc[2, half, idx, :]) / (e0 + e1 + e2)
            ob_ref[idx, half * HEAD_PAD:(half + 1) * HEAD_PAD] = ob.astype(BF16)
        return carry

    lax.fori_loop(0, n_sb, merge_body, 0)


def _dilated_attention(zd0, zd1, zd2, seq):
    t = zd0.shape[0]
    unit = DIL_GROUPS[-1][0]
    upb = seq // unit
    n = DIL_STEPS
    u1 = DIL_GROUPS[1][0]
    all_slopes = _alibi_slopes(len(DIL_GROUPS) * DIL_HEADS).reshape(len(DIL_GROUPS), DIL_HEADS)
    slopes = tuple(tuple(float(x) * dil for x in all_slopes[gi]) for gi, (_, dil) in enumerate(DIL_GROUPS))

    def cur(bb, u):
        return (bb * upb + u, 0)

    def halo(rows):
        per_unit = unit // rows
        return lambda bb, u: ((bb * upb) * per_unit + jnp.maximum(u * per_unit - 1, 0), 0)

    return pl.pallas_call(
        functools.partial(_dilated_kernel, slopes=slopes),
        out_shape=jax.ShapeDtypeStruct((t, DIL_OUT), BF16),
        grid=(t // seq, upb),
        in_specs=[pl.BlockSpec((unit, DIL_COLS), cur), pl.BlockSpec((n, DIL_COLS), halo(n)),
                  pl.BlockSpec((unit, DIL_COLS), cur), pl.BlockSpec((u1, DIL_COLS), halo(u1)),
                  pl.BlockSpec((unit, DIL_COLS), cur), pl.BlockSpec((unit, DIL_COLS), halo(unit))],
        out_specs=pl.BlockSpec((unit, DIL_OUT), cur),
        scratch_shapes=[pltpu.VMEM((len(DIL_GROUPS), DIL_OUT // HEAD_PAD, unit, HEAD_PAD), F32),
                        pltpu.VMEM((len(DIL_GROUPS), DIL_OUT // HEAD_PAD, unit, HEAD_PAD), F32),
                        pltpu.VMEM((len(DIL_GROUPS), DIL_HEADS * n, 2 * n), F32)],
        compiler_params=_cparams(("parallel", "arbitrary")),
        name="dilated_attention",
    )(zd0, zd0, zd1, zd1, zd2, zd2)


def _merge_kernel(x_ref, oa_ref, ob_ref, ga_ref, wgate_ref,
                  wa_ref, wb_ref, wo_ref, g_ref, wr_ref, br_ref,
                  x1_ref, hp_ref, topi_ref, topw_ref, wcol_ref):
    tm = x_ref.shape[0]
    x = x_ref[...]
    h = _rms(x, ga_ref[...]).astype(BF16)
    mixed = (jax.nn.sigmoid(_dot(h, wgate_ref[:, :D_MODEL])) * _dot(oa_ref[...], wa_ref[...])
             + jax.nn.sigmoid(_dot(h, wgate_ref[:, D_MODEL:])) * _dot(ob_ref[...], wb_ref[...]))
    x1 = x + _dot(mixed.astype(BF16), wo_ref[...])
    x1_ref[...] = x1
    h2 = _rms(x1, g_ref[...])
    hp_ref[...] = _pack_halves(h2[:, :HALF], h2[:, HALF:])

    logits = _dot_nt(wr_ref[...], h2.astype(BF16)) + br_ref[...]
    eidx = lax.broadcasted_iota(I32, (N_EXPERTS, tm), 0)
    vals, idxs = [], []
    for _ in range(TOP_K):
        m = jnp.max(logits, axis=0, keepdims=True)
        idx = jnp.min(jnp.where(logits == m, eidx, N_EXPERTS), axis=0, keepdims=True)
        vals.append(m)
        idxs.append(idx)
        logits = jnp.where(eidx == idx, -jnp.inf, logits)
    exps = [jnp.exp(vk - vals[0]) for vk in vals]
    den = exps[0] + exps[1] + exps[2] + exps[3]
    row8 = lax.broadcasted_iota(I32, (8, tm), 0)
    row128 = lax.broadcasted_iota(I32, (HEAD_PAD, tm), 0)
    topi = jnp.zeros((8, tm), I32)
    topw = jnp.zeros((8, tm), F32)
    wide = jnp.zeros((HEAD_PAD, tm), F32)
    for kk in range(TOP_K):
        wk = exps[kk] / den
        topi = jnp.where(row8 == kk, idxs[kk], topi)
        topw = jnp.where(row8 == kk, wk, topw)
        wide = jnp.where(row128 == kk, wk, wide)
    topi_ref[...] = topi
    topw_ref[...] = topw
    wcol_ref[...] = wide.T


def _merge(x2d, oa, ob, g_attn, wgate, wa, wb, wo, g, wr_t, br_col):
    t = x2d.shape[0]
    tm = TOKEN_TILE

    def row(i):
        return (i, 0)

    def col(i):
        return (0, i)

    def full(a):
        return pl.BlockSpec(a.shape, lambda i: (0, 0))

    def rows(width):
        return pl.BlockSpec((tm, width), row)

    out_shape = [
        jax.ShapeDtypeStruct((t, D_MODEL), F32),
        jax.ShapeDtypeStruct((t, HALF), I32),
        jax.ShapeDtypeStruct((8, t), I32),
        jax.ShapeDtypeStruct((8, t), F32),
        jax.ShapeDtypeStruct((t, HEAD_PAD), F32),
    ]
    return pl.pallas_call(
        _merge_kernel,
        out_shape=out_shape,
        grid=(t // tm,),
        in_specs=[rows(D_MODEL), rows(MLA_HEADS * MLA_V), rows(DIL_OUT), full(g_attn), full(wgate)]
        + [full(wa), full(wb), full(wo), full(g), full(wr_t), full(br_col)],
        out_specs=[rows(D_MODEL), rows(HALF), pl.BlockSpec((8, tm), col), pl.BlockSpec((8, tm), col),
                   rows(HEAD_PAD)],
        compiler_params=_cparams(("parallel",)),
        name="merge_router",
    )(x2d, oa, ob, g_attn, wgate, wa, wb, wo, g, wr_t, br_col)


def _positions_kernel(topi_ref, dest_ref, meta_ref, cnt_sc, carry_sc, start_sc):
    ps = pl.program_id(0)
    i = pl.program_id(1)
    tm = POSITION_SUBTILE
    n_sub = topi_ref.shape[1] // tm
    eidx = lax.broadcasted_iota(I32, (N_EXPERTS, tm), 0)

    def hits_of(sb):
        topi = topi_ref[:, sb * tm:(sb + 1) * tm]
        return [eidx == topi[kk:kk + 1, :] for kk in range(TOP_K)]

    def members(hits):
        return hits[0] | hits[1] | hits[2] | hits[3]

    @pl.when((ps == 0) & (i == 0))
    def _():
        cnt_sc[...] = jnp.zeros(cnt_sc.shape, F32)

    @pl.when(ps == 0)
    def _():
        total = cnt_sc[...]
        for sb in range(n_sub):
            total = total + jnp.sum(members(hits_of(sb)).astype(F32), axis=1, keepdims=True)
        cnt_sc[...] = total

    @pl.when((ps == 1) & (i == 0))
    def _():
        cnt = cnt_sc[...].astype(I32)
        padded = lax.shift_left(lax.shift_right_logical(cnt + (ROW_BLOCK - 1), 8), 8)
        sub = lax.broadcasted_iota(I32, (N_EXPERTS, HEAD_PAD), 0)
        lane = lax.broadcasted_iota(I32, (N_EXPERTS, HEAD_PAD), 1)
        padded_row = jnp.sum(jnp.where(sub == lane, padded, 0), axis=0, keepdims=True)
        start = jnp.sum(jnp.where(lane < sub, padded_row, 0), axis=1, keepdims=True)
        start_sc[...] = start.astype(F32)
        carry_sc[...] = jnp.zeros(carry_sc.shape, F32)
        cnt_row = jnp.sum(jnp.where(sub == lane, cnt, 0), axis=0, keepdims=True)
        start_row = jnp.sum(jnp.where(sub == lane, start, 0), axis=0, keepdims=True)
        row8 = lax.broadcasted_iota(I32, (8, HEAD_PAD), 0)
        meta = jnp.where(row8 == 0, cnt_row, 0)
        meta = jnp.where(row8 == 1, start_row, meta)
        meta = jnp.where(row8 == 2, start_row + padded_row, meta)
        meta_ref[...] = meta

    @pl.when(ps == 1)
    def _():
        tr = lax.broadcasted_iota(I32, (tm, tm), 0)
        tc = lax.broadcasted_iota(I32, (tm, tm), 1)
        before = (tr < tc).astype(BF16)
        row8 = lax.broadcasted_iota(I32, (8, tm), 0)
        offset = carry_sc[...] + start_sc[...]
        for sb in range(n_sub):
            hits = hits_of(sb)
            member = members(hits)
            base = _dot(member.astype(BF16), before) + offset
            dest = jnp.zeros((8, tm), I32)
            for kk in range(TOP_K):
                dk = jnp.sum(jnp.where(hits[kk], base, 0.0), axis=0, keepdims=True).astype(I32)
                dest = jnp.where(row8 == kk, dk, dest)
            dest_ref[:, sb * tm:(sb + 1) * tm] = dest
            offset = offset + jnp.sum(member.astype(F32), axis=1, keepdims=True)
        carry_sc[...] = offset - start_sc[...]


def _positions(topi_t):
    t = topi_t.shape[1]
    tm = min(POSITION_TILE, t)
    return pl.pallas_call(
        _positions_kernel,
        out_shape=[jax.ShapeDtypeStruct((8, t), I32), jax.ShapeDtypeStruct((8, HEAD_PAD), I32)],
        grid=(2, t // tm),
        in_specs=[pl.BlockSpec((8, tm), lambda ps, i: (0, i))],
        out_specs=[pl.BlockSpec((8, tm), lambda ps, i: (0, i * ps)),
                   pl.BlockSpec((8, HEAD_PAD), lambda ps, i: (0, 0))],
        scratch_shapes=[pltpu.VMEM((N_EXPERTS, 1), F32)] * 3,
        compiler_params=_cparams(("arbitrary", "arbitrary")),
        name="routing_positions",
    )(topi_t)


def _sc_mesh():
    return plsc.VectorSubcoreMesh(core_axis_name="c", subcore_axis_name="s")


def _dispatch_rows(table, dest_flat, n_rows):
    t, c = table.shape
    n_slots = dest_flat.shape[0] // t
    per_w = t // SC_WORKERS
    assert per_w * SC_WORKERS == t and per_w % (2 * SC_WINDOW) == 0
    n_chunks = per_w // SC_WINDOW
    w = SC_WINDOW

    @functools.partial(
        pl.kernel, mesh=_sc_mesh(),
        out_type=jax.ShapeDtypeStruct((n_rows, c), table.dtype),
        scratch_types=[pltpu.VMEM((w,), I32)] * n_slots + [pltpu.VMEM((w, c), table.dtype)] * 2
        + [pltpu.SemaphoreType.DMA] * (n_slots + 2),
        name="dispatch_rows",
    )
    def k(table_hbm, dest_hbm, out_hbm, *scratch):
        idx = scratch[:n_slots]
        rows = scratch[n_slots:n_slots + 2]
        scatter_sems = scratch[n_slots + 2:2 * n_slots + 2]
        read_sems = scratch[2 * n_slots + 2:]
        wid = lax.axis_index("s") * SC_CORES + lax.axis_index("c")
        base = wid * per_w

        def off(chunk):
            return pl.multiple_of(base + chunk * w, w)

        def read(chunk, buf):
            return pltpu.make_async_copy(table_hbm.at[pl.ds(off(chunk), w)], rows[buf], read_sems[buf])

        def scatter(kk, buf):
            return pltpu.make_async_copy(rows[buf], out_hbm.at[idx[kk]], scatter_sems[kk])

        read(0, 0).start()

        @pl.loop(0, n_chunks // 2)
        def _(p):
            for buf in range(2):
                chunk = 2 * p + buf

                @pl.when(chunk + 1 < n_chunks)
                def _():
                    read(chunk + 1, 1 - buf).start()

                read(chunk, buf).wait()
                for kk in range(n_slots):
                    src = pl.multiple_of(kk * t + off(chunk), w)
                    pltpu.sync_copy(dest_hbm.at[pl.ds(src, w)], idx[kk])
                    scatter(kk, buf).start()
                for kk in range(n_slots):
                    scatter(kk, buf).wait()

    return k(table, dest_flat)


def _gather_rows(table, idx):
    n = idx.shape[0]
    c = table.shape[1]
    per_w = n // SC_WORKERS
    assert per_w * SC_WORKERS == n and per_w % (2 * SC_WINDOW) == 0
    n_chunks = per_w // SC_WINDOW
    w = SC_WINDOW

    @functools.partial(
        pl.kernel, mesh=_sc_mesh(),
        out_type=jax.ShapeDtypeStruct((n, c), table.dtype),
        scratch_types=[pltpu.VMEM((w,), I32)] * 2 + [pltpu.VMEM((w, c), table.dtype)] * 2
        + [pltpu.SemaphoreType.DMA] * 4,
        name="gather_rows",
    )
    def k(table_hbm, idx_hbm, out_hbm, idx_a, idx_b, rows_a, rows_b, g_a, g_b, w_a, w_b):
        idx, rows, gather_sems, write_sems = (idx_a, idx_b), (rows_a, rows_b), (g_a, g_b), (w_a, w_b)
        wid = lax.axis_index("s") * SC_CORES + lax.axis_index("c")
        base = wid * per_w

        def off(chunk):
            return pl.multiple_of(base + chunk * w, w)

        def gather(buf):
            return pltpu.make_async_copy(table_hbm.at[idx[buf]], rows[buf], gather_sems[buf])

        def write(chunk, buf):
            return pltpu.make_async_copy(rows[buf], out_hbm.at[pl.ds(off(chunk), w)], write_sems[buf])

        def start_gather(chunk, buf):
            pltpu.sync_copy(idx_hbm.at[pl.ds(off(chunk), w)], idx[buf])
            gather(buf).start()

        start_gather(0, 0)

        @pl.loop(0, n_chunks // 2)
        def _(p):
            for buf in range(2):
                chunk = 2 * p + buf

                @pl.when(chunk + 1 < n_chunks)
                def _():
                    @pl.when(chunk >= 1)
                    def _():
                        write(chunk - 1, 1 - buf).wait()
                    start_gather(chunk + 1, 1 - buf)

                gather(buf).wait()
                write(chunk, buf).start()

        write(n_chunks - 2, 0).wait()
        write(n_chunks - 1, 1).wait()

    return k(table, idx)


def _expert_kernel(be_ref, nused_ref, first_ref, slot_ref, next_ref,
                   xs_ref, wg_hbm, bg_ref, wu_hbm, bu_ref, wd_hbm, bd_ref,
                   ys_ref, wg_ref, wu_ref, wd_ref, stage_g, stage_u, stage_d, sems, *, layer):
    b = pl.program_id(0)
    used = b < nused_ref[0]
    weights = ((wg_hbm, stage_g), (wu_hbm, stage_u), (wd_hbm, stage_d))

    def fetch(expert, slot):
        return [pltpu.make_async_copy(hbm.at[layer, expert], stage.at[slot], sems.at[slot, n])
                for n, (hbm, stage) in enumerate(weights)]

    @pl.when(used & (first_ref[b] == 1))
    def _():
        slot = slot_ref[b]

        @pl.when(b == 0)
        def _():
            for cp in fetch(be_ref[0], slot):
                cp.start()

        for cp in fetch(be_ref[b], slot):
            cp.wait()
        wg_ref[...] = stage_g[slot].astype(BF16)
        wu_ref[...] = stage_u[slot].astype(BF16)
        wd_ref[...] = stage_d[slot].astype(BF16)

        @pl.when(next_ref[b] >= 0)
        def _():
            for cp in fetch(next_ref[b], 1 - slot):
                cp.start()

    @pl.when(used)
    def _():
        lo, hi = _unpack_halves(xs_ref[...])
        lo = lo.astype(BF16)
        hi = hi.astype(BF16)
        a = _dot(lo, wg_ref[:HALF, :]) + _dot(hi, wg_ref[HALF:, :]) + bg_ref[...]
        u = _dot(lo, wu_ref[:HALF, :]) + _dot(hi, wu_ref[HALF:, :]) + bu_ref[...]
        a = jnp.minimum(a, SWIGLU_LIMIT)
        u = jnp.clip(u, -SWIGLU_LIMIT, SWIGLU_LIMIT)
        y = (a * jax.nn.sigmoid(SWIGLU_ALPHA * a)) * (u + 1.0)
        out = _dot(y.astype(BF16), wd_ref[...]) + bd_ref[...]
        ys_ref[...] = _pack_halves(out[:, :HALF], out[:, HALF:])

    @pl.when(b >= nused_ref[0])
    def _():
        ys_ref[...] = jnp.zeros(ys_ref.shape, I32)


def _expert_ffn(xs, block_e, n_used, counts, layer, wg, bg, wu, bu, wd, bd):
    n_rows = xs.shape[0]
    n_blocks = n_rows // ROW_BLOCK

    blk = jnp.arange(n_blocks, dtype=I32)
    used = blk < n_used[0]
    first = used & ((blk == 0) | (block_e != jnp.roll(block_e, 1)))
    slot = (jnp.cumsum(first.astype(I32)) - 1) % 2
    eid = jnp.arange(N_EXPERTS, dtype=I32)
    later = (eid[None, :] > eid[:, None]) & (counts[None, :] > 0)
    next_of_expert = jnp.min(jnp.where(later, eid[None, :], N_EXPERTS), axis=1)
    next_of_expert = jnp.where(next_of_expert == N_EXPERTS, -1, next_of_expert)
    next_e = next_of_expert[block_e]

    def rows(b, *_):
        return (b, 0)

    def expert(b, be, *_):
        return (layer, be[b], 0, 0)

    hbm = pl.BlockSpec(memory_space=pl.ANY)
    grid_spec = pltpu.PrefetchScalarGridSpec(
        num_scalar_prefetch=5,
        grid=(n_blocks,),
        in_specs=[pl.BlockSpec((ROW_BLOCK, HALF), rows),
                  hbm, pl.BlockSpec((None, None, 1, D_EXPERT), expert),
                  hbm, pl.BlockSpec((None, None, 1, D_EXPERT), expert),
                  hbm, pl.BlockSpec((None, None, 1, D_MODEL), expert)],
        out_specs=pl.BlockSpec((ROW_BLOCK, HALF), rows),
        scratch_shapes=[pltpu.VMEM((D_MODEL, D_EXPERT), BF16), pltpu.VMEM((D_MODEL, D_EXPERT), BF16),
                        pltpu.VMEM((D_EXPERT, D_MODEL), BF16),
                        pltpu.VMEM((2, D_MODEL, D_EXPERT), F32), pltpu.VMEM((2, D_MODEL, D_EXPERT), F32),
                        pltpu.VMEM((2, D_EXPERT, D_MODEL), F32),
                        pltpu.SemaphoreType.DMA((2, 3))],
    )
    return pl.pallas_call(
        functools.partial(_expert_kernel, layer=layer),
        out_shape=jax.ShapeDtypeStruct((n_rows, HALF), I32),
        grid_spec=grid_spec,
        compiler_params=_cparams(("arbitrary",)),
        name="expert_ffn",
    )(block_e, n_used, first.astype(I32), slot.astype(I32), next_e.astype(I32),
      xs, wg, bg, wu, bu, wd, bd)


def _combine_kernel(x1_ref, yg_ref, wcol_ref, p_ref, gple_ref, wpg_ref, wpp_ref, gout_ref, o_ref,
                    *, final):
    x1 = x1_ref[...]
    acc_lo = x1[:, :HALF]
    acc_hi = x1[:, HALF:]
    wcol = wcol_ref[...]
    for kk in range(TOP_K):
        lo, hi = _unpack_halves(yg_ref[kk])
        wk = wcol[:, kk:kk + 1]
        acc_lo = acc_lo + wk * lo
        acc_hi = acc_hi + wk * hi
    x2 = jnp.concatenate([acc_lo, acc_hi], axis=1)
    gate = jax.nn.sigmoid(_dot(_rms(x2, gple_ref[...]).astype(BF16), wpg_ref[...]))
    x3 = x2 + gate * _dot(p_ref[...].astype(BF16), wpp_ref[...])
    o_ref[...] = _rms(x3, gout_ref[...]) if final else x3


def _combine_kernel_inplace(x1_ref, yg_ref, wcol_ref, p_ref, gple_ref, wpg_ref, wpp_ref, gout_ref,
                            prev_ref, o_ref, *, final):
    del prev_ref
    _combine_kernel(x1_ref, yg_ref, wcol_ref, p_ref, gple_ref, wpg_ref, wpp_ref, gout_ref, o_ref,
                    final=final)


def _combine(x1, yg, wcol, p3d, layer, gple, wpg, wpp, gout, final, part, prev):
    t = x1.shape[0]
    tm = TOKEN_TILE
    n_tiles = yg.shape[1] // tm
    first_tile = part * n_tiles

    def row(i, *_):
        return (first_tile + i, 0)

    def full(a):
        return pl.BlockSpec(a.shape, lambda i: (0, 0))

    in_specs = [pl.BlockSpec((tm, D_MODEL), row),
                pl.BlockSpec((TOP_K, tm, HALF), lambda i: (0, i, 0)),
                pl.BlockSpec((tm, HEAD_PAD), row),
                pl.BlockSpec((None, tm, PLE_DIM), lambda i: (layer, first_tile + i, 0)),
                full(gple), full(wpg), full(wpp), full(gout)]
    args = [x1, yg, wcol, p3d, gple, wpg, wpp, gout]
    kern = functools.partial(_combine_kernel, final=final)
    aliases = {}
    if prev is not None:
        in_specs.append(pl.BlockSpec(memory_space=pl.ANY))
        args.append(prev)
        aliases = {len(args) - 1: 0}
        kern = functools.partial(_combine_kernel_inplace, final=final)
    return pl.pallas_call(
        kern,
        out_shape=jax.ShapeDtypeStruct((t, D_MODEL), F32),
        grid=(n_tiles,),
        in_specs=in_specs,
        out_specs=pl.BlockSpec((tm, D_MODEL), row),
        input_output_aliases=aliases,
        compiler_params=_cparams(("parallel",)),
        name="combine_ple",
    )(*args)


def _rope_tables(seq):
    inv_freq = ROPE_THETA ** (-jnp.arange(HALF_ROPE, dtype=F32) * 2.0 / MLA_ROPE)
    ang = jnp.arange(seq, dtype=F32)[:, None] * inv_freq[None, :]
    cos, sin = jnp.cos(ang), jnp.sin(ang)
    ones = jnp.ones((seq, MLA_NOPE), F32)
    zeros16 = jnp.zeros((seq, HALF_ROPE), F32)
    zeros64 = jnp.zeros((seq, MLA_NOPE), F32)
    tail = jnp.ones((seq, HEAD_PAD - MLA_NOPE - MLA_ROPE), F32)
    ztail = jnp.zeros_like(tail)
    cos_t = jnp.concatenate([ones, cos, cos, tail], axis=1)
    sina_t = jnp.concatenate([zeros64, zeros16, sin, ztail], axis=1)
    sinb_t = jnp.concatenate([zeros64, -sin, zeros16, ztail], axis=1)
    return cos_t, sina_t, sinb_t


def _prep_mixer_weights(w_in, w_uq, w_ukv):
    c0 = MLA_Q_LORA + MLA_KV_LORA
    c1 = c0 + MLA_ROPE
    c2 = c1 + len(DIL_GROUPS) * DIL_COLS
    kr_pad = jnp.zeros((D_MODEL, HEAD_PAD), F32).at[:, MLA_NOPE:MLA_NOPE + MLA_ROPE].set(w_in[:, c0:c1])
    wmla = jnp.concatenate([w_in[:, :c0], kr_pad], axis=1).astype(BF16)
    wdil = w_in[:, c1:c2].reshape(D_MODEL, len(DIL_GROUPS), 3, DIL_OUT)
    wdil = wdil.at[:, :, 0, :].multiply(DIL_HEAD_DIM ** -0.5).reshape(D_MODEL, -1).astype(BF16)
    wgate = w_in[:, c2:].astype(BF16)
    pad = HEAD_PAD - MLA_NOPE - MLA_ROPE
    wuq_h = w_uq.reshape(MLA_Q_LORA, MLA_HEADS, MLA_NOPE + MLA_ROPE)
    wuq = jnp.pad(wuq_h, ((0, 0), (0, 0), (0, pad))).reshape(MLA_Q_LORA, MLA_HEADS * HEAD_PAD).astype(BF16)
    wukv_h = w_ukv.reshape(MLA_KV_LORA, MLA_HEADS, MLA_NOPE + MLA_V)
    wuk = jnp.pad(wukv_h[:, :, :MLA_NOPE], ((0, 0), (0, 0), (0, HEAD_PAD - MLA_NOPE)))
    wuk = wuk.reshape(MLA_KV_LORA, MLA_HEADS * HEAD_PAD).astype(BF16)
    wuv = wukv_h[:, :, MLA_NOPE:].reshape(MLA_KV_LORA, MLA_HEADS * MLA_V).astype(BF16)
    return wmla, wdil, wgate, wuq, wuk, wuv


def kernel(x, p, attn_norm, w_in, q_norm, w_uq, kv_norm, w_ukv, w_branch_a, w_branch_b, w_out, ffn_norm, w_router, b_router, w_gate, b_gate, w_up, b_up, w_down, b_down, ple_norm, w_ple_gate, w_ple_proj, final_norm):
    b, s, d = x.shape
    depth = w_in.shape[0]
    t = b * s
    assert d == D_MODEL and s % (DIL_GROUPS[-1][0]) == 0 and t % (SC_WORKERS * SC_WINDOW) == 0
    n_assign = t * TOP_K
    n_blocks = -(-(n_assign + N_EXPERTS * (ROW_BLOCK - 1)) // ROW_BLOCK)
    n_rows = n_blocks * ROW_BLOCK
    cos_t, sina_t, sinb_t = _rope_tables(s)
    xc = x.reshape(t, d)
    for i in range(depth):
        wmla, wdil, wgate, wuq, wuk, wuv = _prep_mixer_weights(w_in[i], w_uq[i], w_ukv[i])
        q, k, vt, zd0, zd1, zd2 = _inproj(
            xc, s, attn_norm[i][None], wmla, wdil, q_norm[i][None], kv_norm[i][None],
            wuq, wuk, wuv, cos_t, sina_t, sinb_t)
        oa = _mla_attention(q.reshape(b, s, -1), k.reshape(b, s, -1), vt)
        ob = _dilated_attention(zd0, zd1, zd2, s)
        x1, hp, topi_t, topw_t, wcol = _merge(
            xc, oa.reshape(t, -1), ob, attn_norm[i][None], wgate,
            w_branch_a[i].astype(BF16), w_branch_b[i].astype(BF16), w_out[i].astype(BF16),
            ffn_norm[i][None], w_router[i].T.astype(BF16), b_router[i][:, None])
        dest_t, meta = _positions(topi_t)
        ends = meta[2, :N_EXPERTS]
        block_start = jnp.arange(n_blocks, dtype=I32) * ROW_BLOCK
        block_e = jnp.minimum(
            jnp.sum((ends[None, :] <= block_start[:, None]).astype(I32), axis=1), N_EXPERTS - 1)
        n_used = (ends[N_EXPERTS - 1:] // ROW_BLOCK).astype(I32)
        dest_flat = dest_t[:TOP_K].reshape(n_assign)
        xs = _dispatch_rows(hp, dest_flat, n_rows)
        ys = _expert_ffn(xs, block_e, n_used, meta[0, :N_EXPERTS], i,
                         w_gate, b_gate[:, :, None, :], w_up, b_up[:, :, None, :],
                         w_down, b_down[:, :, None, :])
        final = i == depth - 1
        gout = final_norm[None] if final else attn_norm[i][None]
        wpg, wpp = w_ple_gate[i].astype(BF16), w_ple_proj[i].astype(BF16)
        tp = t // COMBINE_PARTS
        xc = None
        for part in range(COMBINE_PARTS):
            dest_part = dest_t[:TOP_K, part * tp:(part + 1) * tp].reshape(TOP_K * tp)
            yg = _gather_rows(ys, dest_part).reshape(TOP_K, tp, HALF)
            xc = _combine(x1, yg, wcol, p.reshape(depth, t, PLE_DIM), i, ple_norm[i][None],
                          wpg, wpp, gout, final, part, xc)
    return xc.reshape(b, s, d)
```

```python
import functools
import math

import jax
import jax.numpy as jnp
import numpy as np
from jax import lax
from jax.experimental import pallas as pl
from jax.experimental.pallas import tpu as pltpu
from jax.experimental.pallas import tpu_sc as plsc

F32 = jnp.float32
BF16 = jnp.bfloat16
I32 = jnp.int32

D_MODEL = 1024
PLE_DIM = 256
NORM_EPS = 1e-6

MLA_HEADS = 8
MLA_Q_LORA = 384
MLA_KV_LORA = 256
MLA_NOPE = 64
MLA_ROPE = 32
MLA_V = 64
ROPE_THETA = 10000.0
HEAD_PAD = 128
HALF_ROPE = MLA_ROPE // 2

DIL_GROUPS = ((128, 1), (512, 4), (2048, 16))
DIL_HEADS = 4
DIL_HEAD_DIM = 64
DIL_STEPS = 128
DIL_COLS = 3 * DIL_HEADS * DIL_HEAD_DIM
DIL_OUT = DIL_HEADS * DIL_HEAD_DIM
DIL_UNROLL = 4

N_EXPERTS = 32
TOP_K = 4
D_EXPERT = 1024
SWIGLU_LIMIT = 7.0
SWIGLU_ALPHA = 1.702
ROW_BLOCK = 512

TOKEN_TILE = 512
POSITION_TILE = 4096
POSITION_SUBTILE = 512
COMBINE_PARTS = 2
ATTN_TQ = 1024
ATTN_TK = 1024
ATTN_SUB = 1024
ATTN_KEY_CHUNKS = 2
HALF = D_MODEL // 2
NEG = -1e30
SPECULATION_HEADROOM = 60.0
HI_MASK = -65536

SC_CORES = 2
SC_SUBCORES = 16
SC_WORKERS = SC_CORES * SC_SUBCORES
SC_WINDOW = 64

VMEM_LIMIT = 56 * 1024 * 1024


def _cparams(sem):
    return pltpu.CompilerParams(dimension_semantics=sem, vmem_limit_bytes=VMEM_LIMIT)


def _rms(x, g):
    return x * lax.rsqrt(jnp.mean(x * x, axis=-1, keepdims=True) + NORM_EPS) * g


def _dot(a, b):
    return jnp.dot(a, b, preferred_element_type=F32)


def _dot_nt(a, b):
    return lax.dot_general(a, b, (((1,), (1,)), ((), ())), preferred_element_type=F32)


def _pack_halves(lo, hi):
    lo_i = lax.bitcast_convert_type(lo.astype(BF16).astype(F32), I32)
    hi_i = lax.bitcast_convert_type(hi.astype(BF16).astype(F32), I32)
    return (hi_i & HI_MASK) | lax.shift_right_logical(lo_i, 16)


def _unpack_halves(w):
    lo = lax.bitcast_convert_type(lax.shift_left(w, 16), F32)
    hi = lax.bitcast_convert_type(w & HI_MASK, F32)
    return lo, hi


def _inproj_kernel(x_ref, g_ref, wmla_ref, wdil_ref, qn_ref, kvn_ref, wuq_ref, wuk_ref,
                   wuv_ref, cos_ref, sina_ref, sinb_ref,
                   q_ref, k_ref, vt_ref, zd0_ref, zd1_ref, zd2_ref, zs_sc):
    h = _rms(x_ref[...], g_ref[...]).astype(BF16)
    zm = _dot(h, wmla_ref[...])
    cq = _rms(zm[:, :MLA_Q_LORA], qn_ref[...]).astype(BF16)
    ckv = _rms(zm[:, MLA_Q_LORA:MLA_Q_LORA + MLA_KV_LORA], kvn_ref[...]).astype(BF16)
    kr = zm[:, MLA_Q_LORA + MLA_KV_LORA:]
    cos, sina, sinb = cos_ref[...], sina_ref[...], sinb_ref[...]

    def rope(t):
        return (t * cos + pltpu.roll(t, HALF_ROPE, 1) * sina
                + pltpu.roll(t, HEAD_PAD - HALF_ROPE, 1) * sinb)

    kr_rot = rope(kr)
    qraw = _dot(cq, wuq_ref[...])
    kraw = _dot(ckv, wuk_ref[...])
    vt_ref[...] = _dot(ckv, wuv_ref[...]).T.astype(BF16)
    scale = (MLA_NOPE + MLA_ROPE) ** -0.5 * math.log2(math.e)
    for hd in range(MLA_HEADS):
        sl = slice(hd * HEAD_PAD, (hd + 1) * HEAD_PAD)
        q_ref[:, sl] = (rope(qraw[:, sl]) * scale).astype(BF16)
        k_ref[:, sl] = (kraw[:, sl] + kr_rot).astype(BF16)
    tm = x_ref.shape[0]
    for gi, zd_ref in enumerate((zd0_ref, zd1_ref, zd2_ref)):
        z = _dot(h, wdil_ref[:, gi * DIL_COLS:(gi + 1) * DIL_COLS])
        window, dil = DIL_GROUPS[gi]
        if dil == 1:
            zd_ref[...] = z.astype(BF16)
            continue
        n_col = DIL_COLS // HEAD_PAD
        for c in range(n_col):
            zs_sc[c] = z[:, c * HEAD_PAD:(c + 1) * HEAD_PAD]
        rows = tm // dil
        part = pl.program_id(0) % (window // tm)
        for r in range(dil):
            dst = pl.ds(pl.multiple_of(r * DIL_STEPS + part * rows, rows), rows)
            for c in range(n_col):
                chunk = zs_sc[c, pl.ds(r, rows, stride=dil), :]
                zd_ref[dst, c * HEAD_PAD:(c + 1) * HEAD_PAD] = chunk.astype(BF16)


def _inproj(x2d, seq, g, wmla, wdil, qn, kvn, wuq, wuk, wuv, cos_t, sina_t, sinb_t):
    t = x2d.shape[0]
    tm = TOKEN_TILE
    n_seq_tiles = seq // tm

    def row(i):
        return (i, 0)

    def const(i):
        return (0, 0)

    def pos(i):
        return (i % n_seq_tiles, 0)

    def full(a):
        return pl.BlockSpec(a.shape, const)

    def vt_block(i):
        return (i // n_seq_tiles, 0, i % n_seq_tiles)

    def rows(width):
        return pl.BlockSpec((tm, width), row)

    def unit_rows(window):
        return pl.BlockSpec((window, DIL_COLS), lambda i: (i // (window // tm), 0))

    out_shape = [
        jax.ShapeDtypeStruct((t, MLA_HEADS * HEAD_PAD), BF16),
        jax.ShapeDtypeStruct((t, MLA_HEADS * HEAD_PAD), BF16),
        jax.ShapeDtypeStruct((t // seq, MLA_HEADS * MLA_V, seq), BF16),
        jax.ShapeDtypeStruct((t, DIL_COLS), BF16),
        jax.ShapeDtypeStruct((t, DIL_COLS), BF16),
        jax.ShapeDtypeStruct((t, DIL_COLS), BF16),
    ]
    return pl.pallas_call(
        _inproj_kernel,
        out_shape=out_shape,
        grid=(t // tm,),
        in_specs=[pl.BlockSpec((tm, D_MODEL), row), full(g), full(wmla), full(wdil),
                  full(qn), full(kvn), full(wuq), full(wuk), full(wuv),
                  pl.BlockSpec((tm, HEAD_PAD), pos), pl.BlockSpec((tm, HEAD_PAD), pos),
                  pl.BlockSpec((tm, HEAD_PAD), pos)],
        out_specs=[rows(D_MODEL), rows(D_MODEL),
                   pl.BlockSpec((None, MLA_HEADS * MLA_V, tm), vt_block),
                   rows(DIL_COLS)] + [unit_rows(window) for window, _ in DIL_GROUPS[1:]],
        scratch_shapes=[pltpu.VMEM((DIL_COLS // HEAD_PAD, tm, HEAD_PAD), F32)],
        compiler_params=_cparams(("arbitrary",)),
        name="inproj",
    )(x2d, g, wmla, wdil, qn, kvn, wuq, wuk, wuv, cos_t, sina_t, sinb_t)


def _mla_kernel(qi_ref, kj_ref, q_ref, k_ref, vt_ref, o_ref, m_sc, l_sc, acc_sc, redo_sc):
    p = pl.program_id(2)
    i = qi_ref[p]
    j = kj_ref[p]
    tq = q_ref.shape[0]
    tk = k_ref.shape[0]

    @pl.when(j == 0)
    def _():
        m_sc[...] = jnp.full(m_sc.shape, NEG, F32)
        l_sc[...] = jnp.zeros(l_sc.shape, F32)
        acc_sc[...] = jnp.zeros(acc_sc.shape, F32)

    ratio = tq // tk
    sub = ATTN_SUB

    def step(diagonal, speculative, first=False):
        chains = [(hh, c) for hh in range(2) for c in range(tq // sub)]
        ones_rows = (lax.broadcasted_iota(I32, (16, tk), 0) == 0).astype(BF16)
        state = {}
        for hh, c in chains:
            cs = slice(c * sub, (c + 1) * sub)
            if first and speculative:
                sl = slice(hh * HEAD_PAD, (hh + 1) * HEAD_PAD)
                s0 = _dot_nt(k_ref[:8, sl], q_ref[cs, sl])[:1]
                state[hh, c] = (s0, jnp.zeros((1, sub), F32), jnp.zeros((MLA_V, sub), F32))
            else:
                state[hh, c] = (m_sc[hh, :, cs], l_sc[hh, :, cs], acc_sc[hh, :, cs])
        new_state = {}
        within = None
        def scores(hh, c, rows):
            sl = slice(hh * HEAD_PAD, (hh + 1) * HEAD_PAD)
            st = _dot_nt(k_ref[rows, sl], q_ref[c * sub:(c + 1) * sub, sl])
            if diagonal:
                n_keys = rows.stop - rows.start
                key = lax.broadcasted_iota(I32, (n_keys, sub), 0) + (j * tk + rows.start)
                qry = lax.broadcasted_iota(I32, (n_keys, sub), 1) + (i * tq + c * sub)
                st = jnp.where(qry >= key, st, NEG)
            return st

        def values(hh, rows):
            ones_row = (lax.broadcasted_iota(I32, (16, rows.stop - rows.start), 0) == 0).astype(BF16)
            return jnp.concatenate([vt_ref[hh * MLA_V:(hh + 1) * MLA_V, rows], ones_row], axis=0)

        for hh, c in chains:
            m_prev, l_prev, acc_prev = state[hh, c]
            if speculative:
                kc = tk // ATTN_KEY_CHUNKS
                m_blk, pv = None, None
                for n in range(ATTN_KEY_CHUNKS):
                    rows = slice(n * kc, (n + 1) * kc)
                    st = scores(hh, c, rows)
                    m_part = jnp.max(st, axis=0, keepdims=True)
                    m_blk = m_part if m_blk is None else jnp.maximum(m_blk, m_part)
                    part = _dot(values(hh, rows), jnp.exp2(st - m_prev).astype(BF16))
                    pv = part if pv is None else pv + part
                m_new = jnp.maximum(m_prev, m_blk)
                alpha = jnp.exp2(m_prev - m_new)
                l_new = alpha * (l_prev + pv[MLA_V:MLA_V + 1])
                acc_new = alpha * (acc_prev + pv[:MLA_V])
                ok = jnp.max(m_blk - m_prev) <= SPECULATION_HEADROOM
                within = ok if within is None else (within & ok)
            else:
                rows = slice(0, tk)
                st = scores(hh, c, rows)
                m_new = jnp.maximum(m_prev, jnp.max(st, axis=0, keepdims=True))
                alpha = jnp.exp2(m_prev - m_new)
                pv = _dot(values(hh, rows), jnp.exp2(st - m_new).astype(BF16))
                l_new = alpha * l_prev + pv[MLA_V:MLA_V + 1]
                acc_new = alpha * acc_prev + pv[:MLA_V]
            new_state[hh, c] = (m_new, l_new, acc_new)

        def commit():
            for hh, c in chains:
                cs = slice(c * sub, (c + 1) * sub)
                m_sc[hh, :, cs], l_sc[hh, :, cs], acc_sc[hh, :, cs] = new_state[hh, c]

        if speculative:
            pl.when(within)(commit)
            redo_sc[0] = jnp.logical_not(within).astype(I32)
        else:
            commit()

    redo_sc[0] = 0
    on_diagonal = j >= ratio * i

    @pl.when((j == 0) & on_diagonal)
    def _():
        step(True, True, first=True)

    @pl.when((j == 0) & jnp.logical_not(on_diagonal))
    def _():
        step(False, True, first=True)

    @pl.when((j > 0) & jnp.logical_not(on_diagonal))
    def _():
        step(False, True)

    @pl.when((j > 0) & on_diagonal)
    def _():
        step(True, True)

    @pl.when(redo_sc[0] != 0)
    def _():
        step(True, False)

    @pl.when(j == ratio * i + (ratio - 1))
    def _():
        ot = jnp.concatenate([acc_sc[0] / l_sc[0], acc_sc[1] / l_sc[1]], axis=0)
        o_ref[...] = ot.T.astype(BF16)


def _mla_attention(q, k, vt):
    b, s, _ = q.shape
    tq, tk = ATTN_TQ, ATTN_TK
    ratio = tq // tk
    nq = s // tq
    pairs = [(i, j) for i in range(nq) for j in range(ratio * (i + 1))]
    qi = jnp.asarray([p[0] for p in pairs], I32)
    kj = jnp.asarray([p[1] for p in pairs], I32)
    grid_spec = pltpu.PrefetchScalarGridSpec(
        num_scalar_prefetch=2,
        grid=(b, MLA_HEADS // 2, len(pairs)),
        in_specs=[
            pl.BlockSpec((None, tq, 2 * HEAD_PAD), lambda bb, hp, p, qi, kj: (bb, qi[p], hp)),
            pl.BlockSpec((None, tk, 2 * HEAD_PAD), lambda bb, hp, p, qi, kj: (bb, kj[p], hp)),
            pl.BlockSpec((None, 2 * MLA_V, tk), lambda bb, hp, p, qi, kj: (bb, hp, kj[p])),
        ],
        out_specs=pl.BlockSpec((None, tq, 2 * MLA_V), lambda bb, hp, p, qi, kj: (bb, qi[p], hp)),
        scratch_shapes=[pltpu.VMEM((2, 1, tq), F32), pltpu.VMEM((2, 1, tq), F32),
                        pltpu.VMEM((2, MLA_V, tq), F32), pltpu.SMEM((1,), I32)],
    )
    return pl.pallas_call(
        _mla_kernel,
        out_shape=jax.ShapeDtypeStruct((b, s, MLA_HEADS * MLA_V), BF16),
        grid_spec=grid_spec,
        compiler_params=_cparams(("parallel", "parallel", "arbitrary")),
        name="mla_attention",
    )(qi, kj, q, k, vt)


def _alibi_slopes(n):
    def pow2(m):
        start = 2.0 ** (-8.0 / m)
        return [start ** (i + 1) for i in range(m)]
    if math.log2(n).is_integer():
        s = pow2(n)
    else:
        c = 2 ** int(math.floor(math.log2(n)))
        s = pow2(c) + pow2(2 * c)[0::2][: n - c]
    return np.array(sorted(s, reverse=True), dtype=np.float32)


def _dilated_block(cur, prev, bias4, first):
    n = DIL_STEPS
    hw = DIL_OUT
    q = cur[:, :hw]
    kk = jnp.concatenate([prev[:, hw:2 * hw], cur[:, hw:2 * hw]], axis=0)
    vv = jnp.concatenate([prev[:, 2 * hw:], cur[:, 2 * hw:]], axis=0)
    head_of_lane = lax.broadcasted_iota(I32, (n, hw), 1) // DIL_HEAD_DIM
    zero = jnp.zeros_like(q)
    q4 = jnp.concatenate([jnp.where(head_of_lane == h, q, zero) for h in range(DIL_HEADS)], axis=0)
    s4 = _dot_nt(q4, kk) + bias4
    if first is not None:
        ki = lax.broadcasted_iota(I32, (DIL_HEADS * n, 2 * n), 1)
        s4 = jnp.where(first & (ki < n), NEG, s4)
    m4 = jnp.max(s4, axis=1, keepdims=True)
    p4 = jnp.exp(s4 - m4).astype(BF16)
    l4 = _dot(p4, jnp.ones((2 * n, HEAD_PAD), BF16))
    pv4 = _dot(p4, vv)
    m4 = jnp.broadcast_to(m4, (DIL_HEADS * n, HEAD_PAD))

    def rows(a, h):
        return a[h * n:(h + 1) * n]

    o_un = rows(pv4, DIL_HEADS - 1)
    for h in range(DIL_HEADS - 2, -1, -1):
        o_un = jnp.where(head_of_lane == h, rows(pv4, h), o_un)
    low = lax.broadcasted_iota(I32, (n, HEAD_PAD), 1) < DIL_HEAD_DIM

    def per_lane(a):
        return jnp.concatenate([jnp.where(low, rows(a, 0), rows(a, 1)),
                                jnp.where(low, rows(a, 2), rows(a, 3))], axis=1)

    l_sel = per_lane(l4)
    return o_un / l_sel, per_lane(m4) + jnp.log(l_sel)


def _dilated_kernel(c0_ref, h0_ref, c1_ref, h1_ref, c2_ref, h2_ref, ob_ref, o_sc, l_sc, bias_sc,
                    *, slopes):
    u = pl.program_id(1)
    n = DIL_STEPS
    unit = ob_ref.shape[0]
    n_sb = unit // n
    first = u == 0
    qi = lax.broadcasted_iota(I32, (n, 2 * n), 0)
    ki = lax.broadcasted_iota(I32, (n, 2 * n), 1)
    dist = qi + n - ki
    valid = (dist >= 0) & (dist <= n)
    distf = dist.astype(F32)
    for gi in range(len(DIL_GROUPS)):
        for h in range(DIL_HEADS):
            bias_sc[gi, h * n:(h + 1) * n, :] = jnp.where(valid, -slopes[gi][h] * distf, NEG)

    def rows_of(ref, sb):
        return ref[pl.ds(pl.multiple_of(sb * n, n), n), :]

    def emit(gi, start, stride, o, lse):
        if isinstance(start, int):
            idx = pl.ds(start, n)
        elif stride == 1:
            idx = pl.ds(pl.multiple_of(start, n), n)
        else:
            idx = pl.ds(start, n, stride=stride)
        for half in range(DIL_OUT // HEAD_PAD):
            ls = slice(half * HEAD_PAD, (half + 1) * HEAD_PAD)
            o_sc[gi, half, idx, :] = o[:, ls]
            l_sc[gi, half, idx, :] = lse[:, ls]

    emit(0, 0, 1, *_dilated_block(c0_ref[:n, :], h0_ref[...], bias_sc[0], first))

    def g0_body(sb, carry):
        emit(0, sb * n, 1, *_dilated_block(rows_of(c0_ref, sb), rows_of(c0_ref, sb - 1), bias_sc[0], None))
        return carry

    lax.fori_loop(1, n_sb, g0_body, 0, unroll=DIL_UNROLL)

    d1 = DIL_GROUPS[1][1]

    def g1_head(r, carry):
        emit(1, r, d1, *_dilated_block(rows_of(c1_ref, r), rows_of(h1_ref, r), bias_sc[1], first))
        return carry

    def g1_body(sb, carry):
        start = (sb // d1) * (d1 * n) + sb % d1
        emit(1, start, d1, *_dilated_block(rows_of(c1_ref, sb), rows_of(c1_ref, sb - d1), bias_sc[1], None))
        return carry

    lax.fori_loop(0, d1, g1_head, 0, unroll=DIL_UNROLL)
    lax.fori_loop(d1, n_sb, g1_body, 0, unroll=DIL_UNROLL)

    d2 = DIL_GROUPS[2][1]

    def g2_body(r, carry):
        emit(2, r, d2, *_dilated_block(rows_of(c2_ref, r), rows_of(h2_ref, r), bias_sc[2], first))
        return carry

    lax.fori_loop(0, n_sb, g2_body, 0, unroll=DIL_UNROLL)

    def merge_body(c, carry):
        idx = pl.ds(pl.multiple_of(c * n, n), n)
        for half in range(DIL_OUT // HEAD_PAD):
            l0, l1, l2 = l_sc[0, half, idx, :], l_sc[1, half, idx, :], l_sc[2, half, idx, :]
            lmax = jnp.maximum(jnp.maximum(l0, l1), l2)
            e0, e1, e2 = jnp.exp(l0 - lmax), jnp.exp(l1 - lmax), jnp.exp(l2 - lmax)
            ob = (e0 * o_sc[0, half, idx, :] + e1 * o_sc[1, half, idx, :]
                  + e2 * o_sc[2, half, idx, :]) / (e0 + e1 + e2)
            ob_ref[idx, half * HEAD_PAD:(half + 1) * HEAD_PAD] = ob.astype(BF16)
        return carry

    lax.fori_loop(0, n_sb, merge_body, 0)


def _dilated_attention(zd0, zd1, zd2, seq):
    t = zd0.shape[0]
    unit = DIL_GROUPS[-1][0]
    upb = seq // unit
    n = DIL_STEPS
    u1 = DIL_GROUPS[1][0]
    all_slopes = _alibi_slopes(len(DIL_GROUPS) * DIL_HEADS).reshape(len(DIL_GROUPS), DIL_HEADS)
    slopes = tuple(tuple(float(x) * dil for x in all_slopes[gi]) for gi, (_, dil) in enumerate(DIL_GROUPS))

    def cur(bb, u):
        return (bb * upb + u, 0)

    def halo(rows):
        per_unit = unit // rows
        return lambda bb, u: ((bb * upb) * per_unit + jnp.maximum(u * per_unit - 1, 0), 0)

    return pl.pallas_call(
        functools.partial(_dilated_kernel, slopes=slopes),
        out_shape=jax.ShapeDtypeStruct((t, DIL_OUT), BF16),
        grid=(t // seq, upb),
        in_specs=[pl.BlockSpec((unit, DIL_COLS), cur), pl.BlockSpec((n, DIL_COLS), halo(n)),
                  pl.BlockSpec((unit, DIL_COLS), cur), pl.BlockSpec((u1, DIL_COLS), halo(u1)),
                  pl.BlockSpec((unit, DIL_COLS), cur), pl.BlockSpec((unit, DIL_COLS), halo(unit))],
        out_specs=pl.BlockSpec((unit, DIL_OUT), cur),
        scratch_shapes=[pltpu.VMEM((len(DIL_GROUPS), DIL_OUT // HEAD_PAD, unit, HEAD_PAD), F32),
                        pltpu.VMEM((len(DIL_GROUPS), DIL_OUT // HEAD_PAD, unit, HEAD_PAD), F32),
                        pltpu.VMEM((len(DIL_GROUPS), DIL_HEADS * n, 2 * n), F32)],
        compiler_params=_cparams(("parallel", "arbitrary")),
        name="dilated_attention",
    )(zd0, zd0, zd1, zd1, zd2, zd2)


def _merge_kernel(x_ref, oa_ref, ob_ref, ga_ref, wgate_ref,
                  wa_ref, wb_ref, wo_ref, g_ref, wr_ref, br_ref,
                  x1_ref, hp_ref, topi_ref, topw_ref, wcol_ref):
    tm = x_ref.shape[0]
    x = x_ref[...]
    h = _rms(x, ga_ref[...]).astype(BF16)
    mixed = (jax.nn.sigmoid(_dot(h, wgate_ref[:, :D_MODEL])) * _dot(oa_ref[...], wa_ref[...])
             + jax.nn.sigmoid(_dot(h, wgate_ref[:, D_MODEL:])) * _dot(ob_ref[...], wb_ref[...]))
    x1 = x + _dot(mixed.astype(BF16), wo_ref[...])
    x1_ref[...] = x1
    h2 = _rms(x1, g_ref[...])
    hp_ref[...] = _pack_halves(h2[:, :HALF], h2[:, HALF:])

    logits = _dot_nt(wr_ref[...], h2.astype(BF16)) + br_ref[...]
    eidx = lax.broadcasted_iota(I32, (N_EXPERTS, tm), 0)
    vals, idxs = [], []
    for _ in range(TOP_K):
        m = jnp.max(logits, axis=0, keepdims=True)
        idx = jnp.min(jnp.where(logits == m, eidx, N_EXPERTS), axis=0, keepdims=True)
        vals.append(m)
        idxs.append(idx)
        logits = jnp.where(eidx == idx, -jnp.inf, logits)
    exps = [jnp.exp(vk - vals[0]) for vk in vals]
    den = exps[0] + exps[1] + exps[2] + exps[3]
    row8 = lax.broadcasted_iota(I32, (8, tm), 0)
    row128 = lax.broadcasted_iota(I32, (HEAD_PAD, tm), 0)
    topi = jnp.zeros((8, tm), I32)
    topw = jnp.zeros((8, tm), F32)
    wide = jnp.zeros((HEAD_PAD, tm), F32)
    for kk in range(TOP_K):
        wk = exps[kk] / den
        topi = jnp.where(row8 == kk, idxs[kk], topi)
        topw = jnp.where(row8 == kk, wk, topw)
        wide = jnp.where(row128 == kk, wk, wide)
    topi_ref[...] = topi
    topw_ref[...] = topw
    wcol_ref[...] = wide.T


def _merge(x2d, oa, ob, g_attn, wgate, wa, wb, wo, g, wr_t, br_col):
    t = x2d.shape[0]
    tm = TOKEN_TILE

    def row(i):
        return (i, 0)

    def col(i):
        return (0, i)

    def full(a):
        return pl.BlockSpec(a.shape, lambda i: (0, 0))

    def rows(width):
        return pl.BlockSpec((tm, width), row)

    out_shape = [
        jax.ShapeDtypeStruct((t, D_MODEL), F32),
        jax.ShapeDtypeStruct((t, HALF), I32),
        jax.ShapeDtypeStruct((8, t), I32),
        jax.ShapeDtypeStruct((8, t), F32),
        jax.ShapeDtypeStruct((t, HEAD_PAD), F32),
    ]
    return pl.pallas_call(
        _merge_kernel,
        out_shape=out_shape,
        grid=(t // tm,),
        in_specs=[rows(D_MODEL), rows(MLA_HEADS * MLA_V), rows(DIL_OUT), full(g_attn), full(wgate)]
        + [full(wa), full(wb), full(wo), full(g), full(wr_t), full(br_col)],
        out_specs=[rows(D_MODEL), rows(HALF), pl.BlockSpec((8, tm), col), pl.BlockSpec((8, tm), col),
                   rows(HEAD_PAD)],
        compiler_params=_cparams(("parallel",)),
        name="merge_router",
    )(x2d, oa, ob, g_attn, wgate, wa, wb, wo, g, wr_t, br_col)


def _positions_kernel(topi_ref, dest_ref, meta_ref, cnt_sc, carry_sc, start_sc):
    ps = pl.program_id(0)
    i = pl.program_id(1)
    tm = POSITION_SUBTILE
    n_sub = topi_ref.shape[1] // tm
    eidx = lax.broadcasted_iota(I32, (N_EXPERTS, tm), 0)

    def hits_of(sb):
        topi = topi_ref[:, sb * tm:(sb + 1) * tm]
        return [eidx == topi[kk:kk + 1, :] for kk in range(TOP_K)]

    def members(hits):
        return hits[0] | hits[1] | hits[2] | hits[3]

    @pl.when((ps == 0) & (i == 0))
    def _():
        cnt_sc[...] = jnp.zeros(cnt_sc.shape, F32)

    @pl.when(ps == 0)
    def _():
        total = cnt_sc[...]
        for sb in range(n_sub):
            total = total + jnp.sum(members(hits_of(sb)).astype(F32), axis=1, keepdims=True)
        cnt_sc[...] = total

    @pl.when((ps == 1) & (i == 0))
    def _():
        cnt = cnt_sc[...].astype(I32)
        shift = ROW_BLOCK.bit_length() - 1
        padded = lax.shift_left(lax.shift_right_logical(cnt + (ROW_BLOCK - 1), shift), shift)
        sub = lax.broadcasted_iota(I32, (N_EXPERTS, HEAD_PAD), 0)
        lane = lax.broadcasted_iota(I32, (N_EXPERTS, HEAD_PAD), 1)
        padded_row = jnp.sum(jnp.where(sub == lane, padded, 0), axis=0, keepdims=True)
        start = jnp.sum(jnp.where(lane < sub, padded_row, 0), axis=1, keepdims=True)
        start_sc[...] = start.astype(F32)
        carry_sc[...] = jnp.zeros(carry_sc.shape, F32)
        cnt_row = jnp.sum(jnp.where(sub == lane, cnt, 0), axis=0, keepdims=True)
        start_row = jnp.sum(jnp.where(sub == lane, start, 0), axis=0, keepdims=True)
        row8 = lax.broadcasted_iota(I32, (8, HEAD_PAD), 0)
        meta = jnp.where(row8 == 0, cnt_row, 0)
        meta = jnp.where(row8 == 1, start_row, meta)
        meta = jnp.where(row8 == 2, start_row + padded_row, meta)
        meta_ref[...] = meta

    @pl.when(ps == 1)
    def _():
        tr = lax.broadcasted_iota(I32, (tm, tm), 0)
        tc = lax.broadcasted_iota(I32, (tm, tm), 1)
        before = (tr < tc).astype(BF16)
        row8 = lax.broadcasted_iota(I32, (8, tm), 0)
        offset = carry_sc[...] + start_sc[...]
        for sb in range(n_sub):
            hits = hits_of(sb)
            member = members(hits)
            base = _dot(member.astype(BF16), before) + offset
            dest = jnp.zeros((8, tm), I32)
            for kk in range(TOP_K):
                dk = jnp.sum(jnp.where(hits[kk], base, 0.0), axis=0, keepdims=True).astype(I32)
                dest = jnp.where(row8 == kk, dk, dest)
            dest_ref[:, sb * tm:(sb + 1) * tm] = dest
            offset = offset + jnp.sum(member.astype(F32), axis=1, keepdims=True)
        carry_sc[...] = offset - start_sc[...]


def _positions(topi_t):
    t = topi_t.shape[1]
    tm = min(POSITION_TILE, t)
    return pl.pallas_call(
        _positions_kernel,
        out_shape=[jax.ShapeDtypeStruct((8, t), I32), jax.ShapeDtypeStruct((8, HEAD_PAD), I32)],
        grid=(2, t // tm),
        in_specs=[pl.BlockSpec((8, tm), lambda ps, i: (0, i))],
        out_specs=[pl.BlockSpec((8, tm), lambda ps, i: (0, i * ps)),
                   pl.BlockSpec((8, HEAD_PAD), lambda ps, i: (0, 0))],
        scratch_shapes=[pltpu.VMEM((N_EXPERTS, 1), F32)] * 3,
        compiler_params=_cparams(("arbitrary", "arbitrary")),
        name="routing_positions",
    )(topi_t)


def _sc_mesh():
    return plsc.VectorSubcoreMesh(core_axis_name="c", subcore_axis_name="s")


def _dispatch_rows(table, dest_flat, n_rows):
    t, c = table.shape
    n_slots = dest_flat.shape[0] // t
    per_w = t // SC_WORKERS
    assert per_w * SC_WORKERS == t and per_w % (2 * SC_WINDOW) == 0
    n_chunks = per_w // SC_WINDOW
    w = SC_WINDOW

    @functools.partial(
        pl.kernel, mesh=_sc_mesh(),
        out_type=jax.ShapeDtypeStruct((n_rows, c), table.dtype),
        scratch_types=[pltpu.VMEM((w,), I32)] * n_slots + [pltpu.VMEM((w, c), table.dtype)] * 2
        + [pltpu.SemaphoreType.DMA] * (n_slots + 2),
        name="dispatch_rows",
    )
    def k(table_hbm, dest_hbm, out_hbm, *scratch):
        idx = scratch[:n_slots]
        rows = scratch[n_slots:n_slots + 2]
        scatter_sems = scratch[n_slots + 2:2 * n_slots + 2]
        read_sems = scratch[2 * n_slots + 2:]
        wid = lax.axis_index("s") * SC_CORES + lax.axis_index("c")
        base = wid * per_w

        def off(chunk):
            return pl.multiple_of(base + chunk * w, w)

        def read(chunk, buf):
            return pltpu.make_async_copy(table_hbm.at[pl.ds(off(chunk), w)], rows[buf], read_sems[buf])

        def scatter(kk, buf):
            return pltpu.make_async_copy(rows[buf], out_hbm.at[idx[kk]], scatter_sems[kk])

        read(0, 0).start()

        @pl.loop(0, n_chunks // 2)
        def _(p):
            for buf in range(2):
                chunk = 2 * p + buf

                @pl.when(chunk + 1 < n_chunks)
                def _():
                    read(chunk + 1, 1 - buf).start()

                read(chunk, buf).wait()
                for kk in range(n_slots):
                    src = pl.multiple_of(kk * t + off(chunk), w)
                    pltpu.sync_copy(dest_hbm.at[pl.ds(src, w)], idx[kk])
                    scatter(kk, buf).start()
                for kk in range(n_slots):
                    scatter(kk, buf).wait()

    return k(table, dest_flat)


def _gather_rows(table, idx):
    n = idx.shape[0]
    c = table.shape[1]
    per_w = n // SC_WORKERS
    assert per_w * SC_WORKERS == n and per_w % (2 * SC_WINDOW) == 0
    n_chunks = per_w // SC_WINDOW
    w = SC_WINDOW

    @functools.partial(
        pl.kernel, mesh=_sc_mesh(),
        out_type=jax.ShapeDtypeStruct((n, c), table.dtype),
        scratch_types=[pltpu.VMEM((w,), I32)] * 2 + [pltpu.VMEM((w, c), table.dtype)] * 2
        + [pltpu.SemaphoreType.DMA] * 4,
        name="gather_rows",
    )
    def k(table_hbm, idx_hbm, out_hbm, idx_a, idx_b, rows_a, rows_b, g_a, g_b, w_a, w_b):
        idx, rows, gather_sems, write_sems = (idx_a, idx_b), (rows_a, rows_b), (g_a, g_b), (w_a, w_b)
        wid = lax.axis_index("s") * SC_CORES + lax.axis_index("c")
        base = wid * per_w

        def off(chunk):
            return pl.multiple_of(base + chunk * w, w)

        def gather(buf):
            return pltpu.make_async_copy(table_hbm.at[idx[buf]], rows[buf], gather_sems[buf])

        def write(chunk, buf):
            return pltpu.make_async_copy(rows[buf], out_hbm.at[pl.ds(off(chunk), w)], write_sems[buf])

        def start_gather(chunk, buf):
            pltpu.sync_copy(idx_hbm.at[pl.ds(off(chunk), w)], idx[buf])
            gather(buf).start()

        start_gather(0, 0)

        @pl.loop(0, n_chunks // 2)
        def _(p):
            for buf in range(2):
                chunk = 2 * p + buf

                @pl.when(chunk + 1 < n_chunks)
                def _():
                    @pl.when(chunk >= 1)
                    def _():
                        write(chunk - 1, 1 - buf).wait()
                    start_gather(chunk + 1, 1 - buf)

                gather(buf).wait()
                write(chunk, buf).start()

        write(n_chunks - 2, 0).wait()
        write(n_chunks - 1, 1).wait()

    return k(table, idx)


def _expert_kernel(be_ref, nused_ref, first_ref, slot_ref, next_ref,
                   xs_ref, wg_hbm, bg_ref, wu_hbm, bu_ref, wd_hbm, bd_ref,
                   ys_ref, wg_ref, wu_ref, wd_ref, stage_g, stage_u, stage_d, sems, *, layer):
    b = pl.program_id(0)
    used = b < nused_ref[0]
    weights = ((wg_hbm, stage_g), (wu_hbm, stage_u), (wd_hbm, stage_d))

    def fetch(expert, slot):
        return [pltpu.make_async_copy(hbm.at[layer, expert], stage.at[slot], sems.at[slot, n])
                for n, (hbm, stage) in enumerate(weights)]

    @pl.when(used & (first_ref[b] == 1))
    def _():
        slot = slot_ref[b]

        @pl.when(b == 0)
        def _():
            for cp in fetch(be_ref[0], slot):
                cp.start()

        for cp in fetch(be_ref[b], slot):
            cp.wait()
        wg_ref[...] = stage_g[slot].astype(BF16)
        wu_ref[...] = stage_u[slot].astype(BF16)
        wd_ref[...] = stage_d[slot].astype(BF16)

        @pl.when(next_ref[b] >= 0)
        def _():
            for cp in fetch(next_ref[b], 1 - slot):
                cp.start()

    @pl.when(used)
    def _():
        lo, hi = _unpack_halves(xs_ref[...])
        lo = lo.astype(BF16)
        hi = hi.astype(BF16)
        a = _dot(lo, wg_ref[:HALF, :]) + _dot(hi, wg_ref[HALF:, :]) + bg_ref[...]
        u = _dot(lo, wu_ref[:HALF, :]) + _dot(hi, wu_ref[HALF:, :]) + bu_ref[...]
        a = jnp.minimum(a, SWIGLU_LIMIT)
        u = jnp.clip(u, -SWIGLU_LIMIT, SWIGLU_LIMIT)
        y = (a * jax.nn.sigmoid(SWIGLU_ALPHA * a)) * (u + 1.0)
        out = _dot(y.astype(BF16), wd_ref[...]) + bd_ref[...]
        ys_ref[...] = _pack_halves(out[:, :HALF], out[:, HALF:])

    @pl.when(b >= nused_ref[0])
    def _():
        ys_ref[...] = jnp.zeros(ys_ref.shape, I32)


def _expert_ffn(xs, block_e, n_used, counts, layer, wg, bg, wu, bu, wd, bd):
    n_rows = xs.shape[0]
    n_blocks = n_rows // ROW_BLOCK

    blk = jnp.arange(n_blocks, dtype=I32)
    used = blk < n_used[0]
    first = used & ((blk == 0) | (block_e != jnp.roll(block_e, 1)))
    slot = (jnp.cumsum(first.astype(I32)) - 1) % 2
    eid = jnp.arange(N_EXPERTS, dtype=I32)
    later = (eid[None, :] > eid[:, None]) & (counts[None, :] > 0)
    next_of_expert = jnp.min(jnp.where(later, eid[None, :], N_EXPERTS), axis=1)
    next_of_expert = jnp.where(next_of_expert == N_EXPERTS, -1, next_of_expert)
    next_e = next_of_expert[block_e]

    def rows(b, *_):
        return (b, 0)

    def expert(b, be, *_):
        return (layer, be[b], 0, 0)

    hbm = pl.BlockSpec(memory_space=pl.ANY)
    grid_spec = pltpu.PrefetchScalarGridSpec(
        num_scalar_prefetch=5,
        grid=(n_blocks,),
        in_specs=[pl.BlockSpec((ROW_BLOCK, HALF), rows),
                  hbm, pl.BlockSpec((None, None, 1, D_EXPERT), expert),
                  hbm, pl.BlockSpec((None, None, 1, D_EXPERT), expert),
                  hbm, pl.BlockSpec((None, None, 1, D_MODEL), expert)],
        out_specs=pl.BlockSpec((ROW_BLOCK, HALF), rows),
        scratch_shapes=[pltpu.VMEM((D_MODEL, D_EXPERT), BF16), pltpu.VMEM((D_MODEL, D_EXPERT), BF16),
                        pltpu.VMEM((D_EXPERT, D_MODEL), BF16),
                        pltpu.VMEM((2, D_MODEL, D_EXPERT), F32), pltpu.VMEM((2, D_MODEL, D_EXPERT), F32),
                        pltpu.VMEM((2, D_EXPERT, D_MODEL), F32),
                        pltpu.SemaphoreType.DMA((2, 3))],
    )
    return pl.pallas_call(
        functools.partial(_expert_kernel, layer=layer),
        out_shape=jax.ShapeDtypeStruct((n_rows, HALF), I32),
        grid_spec=grid_spec,
        compiler_params=_cparams(("arbitrary",)),
        name="expert_ffn",
    )(block_e, n_used, first.astype(I32), slot.astype(I32), next_e.astype(I32),
      xs, wg, bg, wu, bu, wd, bd)


def _combine_kernel(x1_ref, yg_ref, wcol_ref, p_ref, gple_ref, wpg_ref, wpp_ref, gout_ref, o_ref,
                    *, final):
    x1 = x1_ref[...]
    acc_lo = x1[:, :HALF]
    acc_hi = x1[:, HALF:]
    wcol = wcol_ref[...]
    for kk in range(TOP_K):
        lo, hi = _unpack_halves(yg_ref[kk])
        wk = wcol[:, kk:kk + 1]
        acc_lo = acc_lo + wk * lo
        acc_hi = acc_hi + wk * hi
    x2 = jnp.concatenate([acc_lo, acc_hi], axis=1)
    gate = jax.nn.sigmoid(_dot(_rms(x2, gple_ref[...]).astype(BF16), wpg_ref[...]))
    x3 = x2 + gate * _dot(p_ref[...].astype(BF16), wpp_ref[...])
    o_ref[...] = _rms(x3, gout_ref[...]) if final else x3


def _combine_kernel_inplace(x1_ref, yg_ref, wcol_ref, p_ref, gple_ref, wpg_ref, wpp_ref, gout_ref,
                            prev_ref, o_ref, *, final):
    del prev_ref
    _combine_kernel(x1_ref, yg_ref, wcol_ref, p_ref, gple_ref, wpg_ref, wpp_ref, gout_ref, o_ref,
                    final=final)


def _combine(x1, yg, wcol, p3d, layer, gple, wpg, wpp, gout, final, part, prev):
    t = x1.shape[0]
    tm = TOKEN_TILE
    n_tiles = yg.shape[1] // tm
    first_tile = part * n_tiles

    def row(i, *_):
        return (first_tile + i, 0)

    def full(a):
        return pl.BlockSpec(a.shape, lambda i: (0, 0))

    in_specs = [pl.BlockSpec((tm, D_MODEL), row),
                pl.BlockSpec((TOP_K, tm, HALF), lambda i: (0, i, 0)),
                pl.BlockSpec((tm, HEAD_PAD), row),
                pl.BlockSpec((None, tm, PLE_DIM), lambda i: (layer, first_tile + i, 0)),
                full(gple), full(wpg), full(wpp), full(gout)]
    args = [x1, yg, wcol, p3d, gple, wpg, wpp, gout]
    kern = functools.partial(_combine_kernel, final=final)
    aliases = {}
    if prev is not None:
        in_specs.append(pl.BlockSpec(memory_space=pl.ANY))
        args.append(prev)
        aliases = {len(args) - 1: 0}
        kern = functools.partial(_combine_kernel_inplace, final=final)
    return pl.pallas_call(
        kern,
        out_shape=jax.ShapeDtypeStruct((t, D_MODEL), F32),
        grid=(n_tiles,),
        in_specs=in_specs,
        out_specs=pl.BlockSpec((tm, D_MODEL), row),
        input_output_aliases=aliases,
        compiler_params=_cparams(("parallel",)),
        name="combine_ple",
    )(*args)


def _rope_tables(seq):
    inv_freq = ROPE_THETA ** (-jnp.arange(HALF_ROPE, dtype=F32) * 2.0 / MLA_ROPE)
    ang = jnp.arange(seq, dtype=F32)[:, None] * inv_freq[None, :]
    cos, sin = jnp.cos(ang), jnp.sin(ang)
    ones = jnp.ones((seq, MLA_NOPE), F32)
    zeros16 = jnp.zeros((seq, HALF_ROPE), F32)
    zeros64 = jnp.zeros((seq, MLA_NOPE), F32)
    tail = jnp.ones((seq, HEAD_PAD - MLA_NOPE - MLA_ROPE), F32)
    ztail = jnp.zeros_like(tail)
    cos_t = jnp.concatenate([ones, cos, cos, tail], axis=1)
    sina_t = jnp.concatenate([zeros64, zeros16, sin, ztail], axis=1)
    sinb_t = jnp.concatenate([zeros64, -sin, zeros16, ztail], axis=1)
    return cos_t, sina_t, sinb_t


def _prep_mixer_weights(w_in, w_uq, w_ukv):
    c0 = MLA_Q_LORA + MLA_KV_LORA
    c1 = c0 + MLA_ROPE
    c2 = c1 + len(DIL_GROUPS) * DIL_COLS
    kr_pad = jnp.pad(w_in[:, c0:c1], ((0, 0), (MLA_NOPE, HEAD_PAD - MLA_NOPE - MLA_ROPE)))
    wmla = jnp.concatenate([w_in[:, :c0], kr_pad], axis=1).astype(BF16)
    col = np.arange(len(DIL_GROUPS) * DIL_COLS)
    q_scale = np.where(col % DIL_COLS < DIL_OUT, DIL_HEAD_DIM ** -0.5, 1.0).astype(np.float32)
    wdil = (w_in[:, c1:c2] * q_scale[None, :]).astype(BF16)
    wgate = w_in[:, c2:].astype(BF16)
    pad = HEAD_PAD - MLA_NOPE - MLA_ROPE
    wuq_h = w_uq.reshape(MLA_Q_LORA, MLA_HEADS, MLA_NOPE + MLA_ROPE)
    wuq = jnp.pad(wuq_h, ((0, 0), (0, 0), (0, pad))).reshape(MLA_Q_LORA, MLA_HEADS * HEAD_PAD).astype(BF16)
    wukv_h = w_ukv.reshape(MLA_KV_LORA, MLA_HEADS, MLA_NOPE + MLA_V)
    wuk = jnp.pad(wukv_h[:, :, :MLA_NOPE], ((0, 0), (0, 0), (0, HEAD_PAD - MLA_NOPE)))
    wuk = wuk.reshape(MLA_KV_LORA, MLA_HEADS * HEAD_PAD).astype(BF16)
    wuv = wukv_h[:, :, MLA_NOPE:].reshape(MLA_KV_LORA, MLA_HEADS * MLA_V).astype(BF16)
    return wmla, wdil, wgate, wuq, wuk, wuv


def kernel(x, p, attn_norm, w_in, q_norm, w_uq, kv_norm, w_ukv, w_branch_a, w_branch_b, w_out, ffn_norm, w_router, b_router, w_gate, b_gate, w_up, b_up, w_down, b_down, ple_norm, w_ple_gate, w_ple_proj, final_norm):
    b, s, d = x.shape
    depth = w_in.shape[0]
    t = b * s
    assert d == D_MODEL and s % (DIL_GROUPS[-1][0]) == 0 and t % (SC_WORKERS * SC_WINDOW) == 0
    n_assign = t * TOP_K
    n_blocks = -(-(n_assign + N_EXPERTS * (ROW_BLOCK - 1)) // ROW_BLOCK)
    n_rows = n_blocks * ROW_BLOCK
    cos_t, sina_t, sinb_t = _rope_tables(s)
    xc = x.reshape(t, d)
    for i in range(depth):
        wmla, wdil, wgate, wuq, wuk, wuv = _prep_mixer_weights(w_in[i], w_uq[i], w_ukv[i])
        q, k, vt, zd0, zd1, zd2 = _inproj(
            xc, s, attn_norm[i][None], wmla, wdil, q_norm[i][None], kv_norm[i][None],
            wuq, wuk, wuv, cos_t, sina_t, sinb_t)
        oa = _mla_attention(q.reshape(b, s, -1), k.reshape(b, s, -1), vt)
        ob = _dilated_attention(zd0, zd1, zd2, s)
        x1, hp, topi_t, topw_t, wcol = _merge(
            xc, oa.reshape(t, -1), ob, attn_norm[i][None], wgate,
            w_branch_a[i].astype(BF16), w_branch_b[i].astype(BF16), w_out[i].astype(BF16),
            ffn_norm[i][None], w_router[i].T.astype(BF16), b_router[i][:, None])
        dest_t, meta = _positions(topi_t)
        ends = meta[2, :N_EXPERTS]
        block_start = jnp.arange(n_blocks, dtype=I32) * ROW_BLOCK
        block_e = jnp.minimum(
            jnp.sum((ends[None, :] <= block_start[:, None]).astype(I32), axis=1), N_EXPERTS - 1)
        n_used = (ends[N_EXPERTS - 1:] // ROW_BLOCK).astype(I32)
        dest_flat = dest_t[:TOP_K].reshape(n_assign)
        xs = _dispatch_rows(hp, dest_flat, n_rows)
        ys = _expert_ffn(xs, block_e, n_used, meta[0, :N_EXPERTS], i,
                         w_gate, b_gate[:, :, None, :], w_up, b_up[:, :, None, :],
                         w_down, b_down[:, :, None, :])
        final = i == depth - 1
        gout = final_norm[None] if final else attn_norm[i][None]
        wpg, wpp = w_ple_gate[i].astype(BF16), w_ple_proj[i].astype(BF16)
        tp = t // COMBINE_PARTS
        xc = None
        for part in range(COMBINE_PARTS):
            dest_part = dest_t[:TOP_K, part * tp:(part + 1) * tp].reshape(TOP_K * tp)
            yg = _gather_rows(ys, dest_part).reshape(TOP_K, tp, HALF)
            xc = _combine(x1, yg, wcol, p.reshape(depth, t, PLE_DIM), i, ple_norm[i][None],
                          wpg, wpp, gout, final, part, xc)
    return xc.reshape(b, s, d)
```

```python
import functools
import math

import jax
import jax.numpy as jnp
import numpy as np
from jax import lax
from jax.experimental import pallas as pl
from jax.experimental.pallas import tpu as pltpu
from jax.experimental.pallas import tpu_sc as plsc

F32 = jnp.float32
BF16 = jnp.bfloat16
I32 = jnp.int32

D_MODEL = 1024
PLE_DIM = 256
NORM_EPS = 1e-6

MLA_HEADS = 8
MLA_Q_LORA = 384
MLA_KV_LORA = 256
MLA_NOPE = 64
MLA_ROPE = 32
MLA_V = 64
ROPE_THETA = 10000.0
HEAD_PAD = 128
HALF_ROPE = MLA_ROPE // 2

DIL_GROUPS = ((128, 1), (512, 4), (2048, 16))
DIL_HEADS = 4
DIL_HEAD_DIM = 64
DIL_STEPS = 128
DIL_COLS = 3 * DIL_HEADS * DIL_HEAD_DIM
DIL_OUT = DIL_HEADS * DIL_HEAD_DIM
DIL_UNROLL = 4

N_EXPERTS = 32
TOP_K = 4
D_EXPERT = 1024
SWIGLU_LIMIT = 7.0
SWIGLU_ALPHA = 1.702
ROW_BLOCK = 512

TOKEN_TILE = 512
POSITION_TILE = 4096
POSITION_SUBTILE = 512
COMBINE_PARTS = 4
ATTN_TQ = 1024
ATTN_TK = 1024
ATTN_SUB = 1024
ATTN_KEY_CHUNKS = 2
HALF = D_MODEL // 2
NEG = -1e30
SPECULATION_HEADROOM = 60.0
HI_MASK = -65536

SC_CORES = 2
SC_SUBCORES = 16
SC_WORKERS = SC_CORES * SC_SUBCORES
SC_WINDOW = 64

VMEM_LIMIT = 56 * 1024 * 1024


def _cparams(sem):
    return pltpu.CompilerParams(dimension_semantics=sem, vmem_limit_bytes=VMEM_LIMIT)


def _rms(x, g):
    return x * lax.rsqrt(jnp.mean(x * x, axis=-1, keepdims=True) + NORM_EPS) * g


def _dot(a, b):
    return jnp.dot(a, b, preferred_element_type=F32)


def _dot_nt(a, b):
    return lax.dot_general(a, b, (((1,), (1,)), ((), ())), preferred_element_type=F32)


def _pack_halves(lo, hi):
    lo_i = lax.bitcast_convert_type(lo.astype(BF16).astype(F32), I32)
    hi_i = lax.bitcast_convert_type(hi.astype(BF16).astype(F32), I32)
    return (hi_i & HI_MASK) | lax.shift_right_logical(lo_i, 16)


def _unpack_halves(w):
    lo = lax.bitcast_convert_type(lax.shift_left(w, 16), F32)
    hi = lax.bitcast_convert_type(w & HI_MASK, F32)
    return lo, hi


def _inproj_kernel(x_ref, g_ref, wmla_ref, wdil_ref, qn_ref, kvn_ref, wuq_ref, wuk_ref,
                   wuv_ref, cos_ref, sina_ref, sinb_ref,
                   q_ref, k_ref, vt_ref, zd0_ref, zd1_ref, zd2_ref, zs_sc):
    h = _rms(x_ref[...], g_ref[...]).astype(BF16)
    zm = _dot(h, wmla_ref[...])
    cq = _rms(zm[:, :MLA_Q_LORA], qn_ref[...]).astype(BF16)
    ckv = _rms(zm[:, MLA_Q_LORA:MLA_Q_LORA + MLA_KV_LORA], kvn_ref[...]).astype(BF16)
    kr = zm[:, MLA_Q_LORA + MLA_KV_LORA:]
    cos, sina, sinb = cos_ref[...], sina_ref[...], sinb_ref[...]

    def rope(t):
        return (t * cos + pltpu.roll(t, HALF_ROPE, 1) * sina
                + pltpu.roll(t, HEAD_PAD - HALF_ROPE, 1) * sinb)

    kr_rot = rope(kr)
    qraw = _dot(cq, wuq_ref[...])
    kraw = _dot(ckv, wuk_ref[...])
    vt_ref[...] = _dot(ckv, wuv_ref[...]).T.astype(BF16)
    scale = (MLA_NOPE + MLA_ROPE) ** -0.5 * math.log2(math.e)
    for hd in range(MLA_HEADS):
        sl = slice(hd * HEAD_PAD, (hd + 1) * HEAD_PAD)
        q_ref[:, sl] = (rope(qraw[:, sl]) * scale).astype(BF16)
        k_ref[:, sl] = (kraw[:, sl] + kr_rot).astype(BF16)
    tm = x_ref.shape[0]
    for gi, zd_ref in enumerate((zd0_ref, zd1_ref, zd2_ref)):
        z = _dot(h, wdil_ref[:, gi * DIL_COLS:(gi + 1) * DIL_COLS])
        window, dil = DIL_GROUPS[gi]
        if dil == 1:
            zd_ref[...] = z.astype(BF16)
            continue
        n_col = DIL_COLS // HEAD_PAD
        for c in range(n_col):
            zs_sc[c] = z[:, c * HEAD_PAD:(c + 1) * HEAD_PAD]
        rows = tm // dil
        part = pl.program_id(0) % (window // tm)
        for r in range(dil):
            dst = pl.ds(pl.multiple_of(r * DIL_STEPS + part * rows, rows), rows)
            for c in range(n_col):
                chunk = zs_sc[c, pl.ds(r, rows, stride=dil), :]
                zd_ref[dst, c * HEAD_PAD:(c + 1) * HEAD_PAD] = chunk.astype(BF16)


def _inproj(x2d, seq, g, wmla, wdil, qn, kvn, wuq, wuk, wuv, cos_t, sina_t, sinb_t):
    t = x2d.shape[0]
    tm = TOKEN_TILE
    n_seq_tiles = seq // tm

    def row(i):
        return (i, 0)

    def const(i):
        return (0, 0)

    def pos(i):
        return (i % n_seq_tiles, 0)

    def full(a):
        return pl.BlockSpec(a.shape, const)

    def vt_block(i):
        return (i // n_seq_tiles, 0, i % n_seq_tiles)

    def rows(width):
        return pl.BlockSpec((tm, width), row)

    def unit_rows(window):
        return pl.BlockSpec((window, DIL_COLS), lambda i: (i // (window // tm), 0))

    out_shape = [
        jax.ShapeDtypeStruct((t, MLA_HEADS * HEAD_PAD), BF16),
        jax.ShapeDtypeStruct((t, MLA_HEADS * HEAD_PAD), BF16),
        jax.ShapeDtypeStruct((t // seq, MLA_HEADS * MLA_V, seq), BF16),
        jax.ShapeDtypeStruct((t, DIL_COLS), BF16),
        jax.ShapeDtypeStruct((t, DIL_COLS), BF16),
        jax.ShapeDtypeStruct((t, DIL_COLS), BF16),
    ]
    return pl.pallas_call(
        _inproj_kernel,
        out_shape=out_shape,
        grid=(t // tm,),
        in_specs=[pl.BlockSpec((tm, D_MODEL), row), full(g), full(wmla), full(wdil),
                  full(qn), full(kvn), full(wuq), full(wuk), full(wuv),
                  pl.BlockSpec((tm, HEAD_PAD), pos), pl.BlockSpec((tm, HEAD_PAD), pos),
                  pl.BlockSpec((tm, HEAD_PAD), pos)],
        out_specs=[rows(D_MODEL), rows(D_MODEL),
                   pl.BlockSpec((None, MLA_HEADS * MLA_V, tm), vt_block),
                   rows(DIL_COLS)] + [unit_rows(window) for window, _ in DIL_GROUPS[1:]],
        scratch_shapes=[pltpu.VMEM((DIL_COLS // HEAD_PAD, tm, HEAD_PAD), F32)],
        compiler_params=_cparams(("arbitrary",)),
        name="inproj",
    )(x2d, g, wmla, wdil, qn, kvn, wuq, wuk, wuv, cos_t, sina_t, sinb_t)


def _mla_kernel(qi_ref, kj_ref, q_ref, k_ref, vt_ref, o_ref, m_sc, l_sc, acc_sc, redo_sc):
    p = pl.program_id(2)
    i = qi_ref[p]
    j = kj_ref[p]
    tq = q_ref.shape[0]
    tk = k_ref.shape[0]

    @pl.when(j == 0)
    def _():
        m_sc[...] = jnp.full(m_sc.shape, NEG, F32)
        l_sc[...] = jnp.zeros(l_sc.shape, F32)
        acc_sc[...] = jnp.zeros(acc_sc.shape, F32)

    ratio = tq // tk
    sub = ATTN_SUB

    def step(diagonal, speculative, first=False):
        chains = [(hh, c) for hh in range(2) for c in range(tq // sub)]
        ones_rows = (lax.broadcasted_iota(I32, (16, tk), 0) == 0).astype(BF16)
        state = {}
        for hh, c in chains:
            cs = slice(c * sub, (c + 1) * sub)
            if first and speculative:
                sl = slice(hh * HEAD_PAD, (hh + 1) * HEAD_PAD)
                s0 = _dot_nt(k_ref[:8, sl], q_ref[cs, sl])[:1]
                state[hh, c] = (s0, jnp.zeros((1, sub), F32), jnp.zeros((MLA_V, sub), F32))
            else:
                state[hh, c] = (m_sc[hh, :, cs], l_sc[hh, :, cs], acc_sc[hh, :, cs])
        new_state = {}
        within = None
        def scores(hh, c, rows, q0=0):
            sl = slice(hh * HEAD_PAD, (hh + 1) * HEAD_PAD)
            st = _dot_nt(k_ref[rows, sl], q_ref[c * sub + q0:(c + 1) * sub, sl])
            if diagonal:
                shape = (rows.stop - rows.start, sub - q0)
                key = lax.broadcasted_iota(I32, shape, 0) + (j * tk + rows.start)
                qry = lax.broadcasted_iota(I32, shape, 1) + (i * tq + c * sub + q0)
                st = jnp.where(qry >= key, st, NEG)
            return st

        def values(hh, rows):
            ones_row = (lax.broadcasted_iota(I32, (16, rows.stop - rows.start), 0) == 0).astype(BF16)
            return jnp.concatenate([vt_ref[hh * MLA_V:(hh + 1) * MLA_V, rows], ones_row], axis=0)

        for hh, c in chains:
            m_prev, l_prev, acc_prev = state[hh, c]
            if speculative:
                kc = tk // ATTN_KEY_CHUNKS
                m_blk, pv = None, None
                for n in range(ATTN_KEY_CHUNKS):
                    rows = slice(n * kc, (n + 1) * kc)
                    q0 = n * kc if (diagonal and tq == tk and sub == tq) else 0
                    st = scores(hh, c, rows, q0)
                    m_part = jnp.max(st, axis=0, keepdims=True)
                    part = _dot(values(hh, rows), jnp.exp2(st - m_prev[:, q0:]).astype(BF16))
                    if q0:
                        m_part = jnp.concatenate([jnp.full((1, q0), NEG, F32), m_part], axis=1)
                        part = jnp.concatenate([jnp.zeros((part.shape[0], q0), F32), part], axis=1)
                    m_blk = m_part if m_blk is None else jnp.maximum(m_blk, m_part)
                    pv = part if pv is None else pv + part
                m_new = jnp.maximum(m_prev, m_blk)
                alpha = jnp.exp2(m_prev - m_new)
                l_new = alpha * (l_prev + pv[MLA_V:MLA_V + 1])
                acc_new = alpha * (acc_prev + pv[:MLA_V])
                ok = jnp.max(m_blk - m_prev) <= SPECULATION_HEADROOM
                within = ok if within is None else (within & ok)
            else:
                rows = slice(0, tk)
                st = scores(hh, c, rows)
                m_new = jnp.maximum(m_prev, jnp.max(st, axis=0, keepdims=True))
                alpha = jnp.exp2(m_prev - m_new)
                pv = _dot(values(hh, rows), jnp.exp2(st - m_new).astype(BF16))
                l_new = alpha * l_prev + pv[MLA_V:MLA_V + 1]
                acc_new = alpha * acc_prev + pv[:MLA_V]
            new_state[hh, c] = (m_new, l_new, acc_new)

        def commit():
            for hh, c in chains:
                cs = slice(c * sub, (c + 1) * sub)
                m_sc[hh, :, cs], l_sc[hh, :, cs], acc_sc[hh, :, cs] = new_state[hh, c]

        if speculative:
            pl.when(within)(commit)
            redo_sc[0] = jnp.logical_not(within).astype(I32)
        else:
            commit()

    redo_sc[0] = 0
    on_diagonal = j >= ratio * i

    @pl.when((j == 0) & on_diagonal)
    def _():
        step(True, True, first=True)

    @pl.when((j == 0) & jnp.logical_not(on_diagonal))
    def _():
        step(False, True, first=True)

    @pl.when((j > 0) & jnp.logical_not(on_diagonal))
    def _():
        step(False, True)

    @pl.when((j > 0) & on_diagonal)
    def _():
        step(True, True)

    @pl.when(redo_sc[0] != 0)
    def _():
        step(True, False)

    @pl.when(j == ratio * i + (ratio - 1))
    def _():
        ot = jnp.concatenate([acc_sc[0] / l_sc[0], acc_sc[1] / l_sc[1]], axis=0)
        o_ref[...] = ot.T.astype(BF16)


def _mla_attention(q, k, vt):
    b, s, _ = q.shape
    tq, tk = ATTN_TQ, ATTN_TK
    ratio = tq // tk
    nq = s // tq
    pairs = [(i, j) for i in range(nq) for j in range(ratio * (i + 1))]
    qi = jnp.asarray([p[0] for p in pairs], I32)
    kj = jnp.asarray([p[1] for p in pairs], I32)
    grid_spec = pltpu.PrefetchScalarGridSpec(
        num_scalar_prefetch=2,
        grid=(b, MLA_HEADS // 2, len(pairs)),
        in_specs=[
            pl.BlockSpec((None, tq, 2 * HEAD_PAD), lambda bb, hp, p, qi, kj: (bb, qi[p], hp)),
            pl.BlockSpec((None, tk, 2 * HEAD_PAD), lambda bb, hp, p, qi, kj: (bb, kj[p], hp)),
            pl.BlockSpec((None, 2 * MLA_V, tk), lambda bb, hp, p, qi, kj: (bb, hp, kj[p])),
        ],
        out_specs=pl.BlockSpec((None, tq, 2 * MLA_V), lambda bb, hp, p, qi, kj: (bb, qi[p], hp)),
        scratch_shapes=[pltpu.VMEM((2, 1, tq), F32), pltpu.VMEM((2, 1, tq), F32),
                        pltpu.VMEM((2, MLA_V, tq), F32), pltpu.SMEM((1,), I32)],
    )
    return pl.pallas_call(
        _mla_kernel,
        out_shape=jax.ShapeDtypeStruct((b, s, MLA_HEADS * MLA_V), BF16),
        grid_spec=grid_spec,
        compiler_params=_cparams(("parallel", "parallel", "arbitrary")),
        name="mla_attention",
    )(qi, kj, q, k, vt)


def _alibi_slopes(n):
    def pow2(m):
        start = 2.0 ** (-8.0 / m)
        return [start ** (i + 1) for i in range(m)]
    if math.log2(n).is_integer():
        s = pow2(n)
    else:
        c = 2 ** int(math.floor(math.log2(n)))
        s = pow2(c) + pow2(2 * c)[0::2][: n - c]
    return np.array(sorted(s, reverse=True), dtype=np.float32)


def _dilated_block(cur, prev, bias4, first):
    n = DIL_STEPS
    hw = DIL_OUT
    q = cur[:, :hw]
    kk = jnp.concatenate([prev[:, hw:2 * hw], cur[:, hw:2 * hw]], axis=0)
    vv = jnp.concatenate([prev[:, 2 * hw:], cur[:, 2 * hw:]], axis=0)
    head_of_lane = lax.broadcasted_iota(I32, (n, hw), 1) // DIL_HEAD_DIM
    zero = jnp.zeros_like(q)
    q4 = jnp.concatenate([jnp.where(head_of_lane == h, q, zero) for h in range(DIL_HEADS)], axis=0)
    s4 = _dot_nt(q4, kk) + bias4
    if first is not None:
        ki = lax.broadcasted_iota(I32, (DIL_HEADS * n, 2 * n), 1)
        s4 = jnp.where(first & (ki < n), NEG, s4)
    m4 = jnp.max(s4, axis=1, keepdims=True)
    p4 = jnp.exp(s4 - m4).astype(BF16)
    l4 = _dot(p4, jnp.ones((2 * n, HEAD_PAD), BF16))
    pv4 = _dot(p4, vv)
    m4 = jnp.broadcast_to(m4, (DIL_HEADS * n, HEAD_PAD))

    def rows(a, h):
        return a[h * n:(h + 1) * n]

    o_un = rows(pv4, DIL_HEADS - 1)
    for h in range(DIL_HEADS - 2, -1, -1):
        o_un = jnp.where(head_of_lane == h, rows(pv4, h), o_un)
    low = lax.broadcasted_iota(I32, (n, HEAD_PAD), 1) < DIL_HEAD_DIM

    def per_lane(a):
        return jnp.concatenate([jnp.where(low, rows(a, 0), rows(a, 1)),
                                jnp.where(low, rows(a, 2), rows(a, 3))], axis=1)

    l_sel = per_lane(l4)
    return o_un / l_sel, per_lane(m4) + jnp.log(l_sel)


def _dilated_kernel(c0_ref, h0_ref, c1_ref, h1_ref, c2_ref, h2_ref, ob_ref, o_sc, l_sc, bias_sc,
                    *, slopes):
    u = pl.program_id(1)
    n = DIL_STEPS
    unit = ob_ref.shape[0]
    n_sb = unit // n
    first = u == 0
    qi = lax.broadcasted_iota(I32, (n, 2 * n), 0)
    ki = lax.broadcasted_iota(I32, (n, 2 * n), 1)
    dist = qi + n - ki
    valid = (dist >= 0) & (dist <= n)
    distf = dist.astype(F32)
    for gi in range(len(DIL_GROUPS)):
        for h in range(DIL_HEADS):
            bias_sc[gi, h * n:(h + 1) * n, :] = jnp.where(valid, -slopes[gi][h] * distf, NEG)

    def rows_of(ref, sb):
        return ref[pl.ds(pl.multiple_of(sb * n, n), n), :]

    def emit(gi, start, stride, o, lse):
        if isinstance(start, int):
            idx = pl.ds(start, n)
        elif stride == 1:
            idx = pl.ds(pl.multiple_of(start, n), n)
        else:
            idx = pl.ds(start, n, stride=stride)
        for half in range(DIL_OUT // HEAD_PAD):
            ls = slice(half * HEAD_PAD, (half + 1) * HEAD_PAD)
            o_sc[gi, half, idx, :] = o[:, ls]
            l_sc[gi, half, idx, :] = lse[:, ls]

    emit(0, 0, 1, *_dilated_block(c0_ref[:n, :], h0_ref[...], bias_sc[0], first))

    def g0_body(sb, carry):
        emit(0, sb * n, 1, *_dilated_block(rows_of(c0_ref, sb), rows_of(c0_ref, sb - 1), bias_sc[0], None))
        return carry

    lax.fori_loop(1, n_sb, g0_body, 0, unroll=DIL_UNROLL)

    d1 = DIL_GROUPS[1][1]

    def g1_head(r, carry):
        emit(1, r, d1, *_dilated_block(rows_of(c1_ref, r), rows_of(h1_ref, r), bias_sc[1], first))
        return carry

    def g1_body(sb, carry):
        start = (sb // d1) * (d1 * n) + sb % d1
        emit(1, start, d1, *_dilated_block(rows_of(c1_ref, sb), rows_of(c1_ref, sb - d1), bias_sc[1], None))
        return carry

    lax.fori_loop(0, d1, g1_head, 0, unroll=DIL_UNROLL)
    lax.fori_loop(d1, n_sb, g1_body, 0, unroll=DIL_UNROLL)

    d2 = DIL_GROUPS[2][1]

    def g2_body(r, carry):
        emit(2, r, d2, *_dilated_block(rows_of(c2_ref, r), rows_of(h2_ref, r), bias_sc[2], first))
        return carry

    lax.fori_loop(0, n_sb, g2_body, 0, unroll=DIL_UNROLL)

    def merge_body(c, carry):
        idx = pl.ds(pl.multiple_of(c * n, n), n)
        for half in range(DIL_OUT // HEAD_PAD):
            l0, l1, l2 = l_sc[0, half, idx, :], l_sc[1, half, idx, :], l_sc[2, half, idx, :]
            lmax = jnp.maximum(jnp.maximum(l0, l1), l2)
            e0, e1, e2 = jnp.exp(l0 - lmax), jnp.exp(l1 - lmax), jnp.exp(l2 - lmax)
            ob = (e0 * o_sc[0, half, idx, :] + e1 * o_sc[1, half, idx, :]
                  + e2 * o_sc[2, half, idx, :]) / (e0 + e1 + e2)
            ob_ref[idx, half * HEAD_PAD:(half + 1) * HEAD_PAD] = ob.astype(BF16)
        return carry

    lax.fori_loop(0, n_sb, merge_body, 0)


def _dilated_attention(zd0, zd1, zd2, seq):
    t = zd0.shape[0]
    unit = DIL_GROUPS[-1][0]
    upb = seq // unit
    n = DIL_STEPS
    u1 = DIL_GROUPS[1][0]
    all_slopes = _alibi_slopes(len(DIL_GROUPS) * DIL_HEADS).reshape(len(DIL_GROUPS), DIL_HEADS)
    slopes = tuple(tuple(float(x) * dil for x in all_slopes[gi]) for gi, (_, dil) in enumerate(DIL_GROUPS))

    def cur(bb, u):
        return (bb * upb + u, 0)

    def halo(rows):
        per_unit = unit // rows
        return lambda bb, u: ((bb * upb) * per_unit + jnp.maximum(u * per_unit - 1, 0), 0)

    return pl.pallas_call(
        functools.partial(_dilated_kernel, slopes=slopes),
        out_shape=jax.ShapeDtypeStruct((t, DIL_OUT), BF16),
        grid=(t // seq, upb),
        in_specs=[pl.BlockSpec((unit, DIL_COLS), cur), pl.BlockSpec((n, DIL_COLS), halo(n)),
                  pl.BlockSpec((unit, DIL_COLS), cur), pl.BlockSpec((u1, DIL_COLS), halo(u1)),
                  pl.BlockSpec((unit, DIL_COLS), cur), pl.BlockSpec((unit, DIL_COLS), halo(unit))],
        out_specs=pl.BlockSpec((unit, DIL_OUT), cur),
        scratch_shapes=[pltpu.VMEM((len(DIL_GROUPS), DIL_OUT // HEAD_PAD, unit, HEAD_PAD), F32),
                        pltpu.VMEM((len(DIL_GROUPS), DIL_OUT // HEAD_PAD, unit, HEAD_PAD), F32),
                        pltpu.VMEM((len(DIL_GROUPS), DIL_HEADS * n, 2 * n), F32)],
        compiler_params=_cparams(("parallel", "arbitrary")),
        name="dilated_attention",
    )(zd0, zd0, zd1, zd1, zd2, zd2)


def _merge_kernel(x_ref, oa_ref, ob_ref, ga_ref, wgate_ref,
                  wa_ref, wb_ref, wo_ref, g_ref, wr_ref, br_ref,
                  x1_ref, hp_ref, topi_ref, topw_ref, wcol_ref):
    tm = x_ref.shape[0]
    x = x_ref[...]
    h = _rms(x, ga_ref[...]).astype(BF16)
    mixed = (jax.nn.sigmoid(_dot(h, wgate_ref[:, :D_MODEL])) * _dot(oa_ref[...], wa_ref[...])
             + jax.nn.sigmoid(_dot(h, wgate_ref[:, D_MODEL:])) * _dot(ob_ref[...], wb_ref[...]))
    x1 = x + _dot(mixed.astype(BF16), wo_ref[...])
    x1_ref[...] = x1
    h2 = _rms(x1, g_ref[...])
    hp_ref[...] = _pack_halves(h2[:, :HALF], h2[:, HALF:])

    logits = _dot_nt(wr_ref[...], h2.astype(BF16)) + br_ref[...]
    eidx = lax.broadcasted_iota(I32, (N_EXPERTS, tm), 0)
    vals, idxs = [], []
    for _ in range(TOP_K):
        m = jnp.max(logits, axis=0, keepdims=True)
        idx = jnp.min(jnp.where(logits == m, eidx, N_EXPERTS), axis=0, keepdims=True)
        vals.append(m)
        idxs.append(idx)
        logits = jnp.where(eidx == idx, -jnp.inf, logits)
    exps = [jnp.exp(vk - vals[0]) for vk in vals]
    den = exps[0] + exps[1] + exps[2] + exps[3]
    row8 = lax.broadcasted_iota(I32, (8, tm), 0)
    row128 = lax.broadcasted_iota(I32, (HEAD_PAD, tm), 0)
    topi = jnp.zeros((8, tm), I32)
    topw = jnp.zeros((8, tm), F32)
    wide = jnp.zeros((HEAD_PAD, tm), F32)
    for kk in range(TOP_K):
        wk = exps[kk] / den
        topi = jnp.where(row8 == kk, idxs[kk], topi)
        topw = jnp.where(row8 == kk, wk, topw)
        wide = jnp.where(row128 == kk, wk, wide)
    topi_ref[...] = topi
    topw_ref[...] = topw
    wcol_ref[...] = wide.T


def _merge(x2d, oa, ob, g_attn, wgate, wa, wb, wo, g, wr_t, br_col):
    t = x2d.shape[0]
    tm = TOKEN_TILE

    def row(i):
        return (i, 0)

    def col(i):
        return (0, i)

    def full(a):
        return pl.BlockSpec(a.shape, lambda i: (0, 0))

    def rows(width):
        return pl.BlockSpec((tm, width), row)

    out_shape = [
        jax.ShapeDtypeStruct((t, D_MODEL), F32),
        jax.ShapeDtypeStruct((t, HALF), I32),
        jax.ShapeDtypeStruct((8, t), I32),
        jax.ShapeDtypeStruct((8, t), F32),
        jax.ShapeDtypeStruct((t, HEAD_PAD), F32),
    ]
    return pl.pallas_call(
        _merge_kernel,
        out_shape=out_shape,
        grid=(t // tm,),
        in_specs=[rows(D_MODEL), rows(MLA_HEADS * MLA_V), rows(DIL_OUT), full(g_attn), full(wgate)]
        + [full(wa), full(wb), full(wo), full(g), full(wr_t), full(br_col)],
        out_specs=[rows(D_MODEL), rows(HALF), pl.BlockSpec((8, tm), col), pl.BlockSpec((8, tm), col),
                   rows(HEAD_PAD)],
        compiler_params=_cparams(("parallel",)),
        name="merge_router",
    )(x2d, oa, ob, g_attn, wgate, wa, wb, wo, g, wr_t, br_col)


def _positions_kernel(topi_ref, dest_ref, meta_ref, cnt_sc, carry_sc, start_sc):
    ps = pl.program_id(0)
    i = pl.program_id(1)
    tm = POSITION_SUBTILE
    n_sub = topi_ref.shape[1] // tm
    eidx = lax.broadcasted_iota(I32, (N_EXPERTS, tm), 0)

    def hits_of(sb):
        topi = topi_ref[:, sb * tm:(sb + 1) * tm]
        return [eidx == topi[kk:kk + 1, :] for kk in range(TOP_K)]

    def members(hits):
        return hits[0] | hits[1] | hits[2] | hits[3]

    @pl.when((ps == 0) & (i == 0))
    def _():
        cnt_sc[...] = jnp.zeros(cnt_sc.shape, F32)

    @pl.when(ps == 0)
    def _():
        total = cnt_sc[...]
        for sb in range(n_sub):
            total = total + jnp.sum(members(hits_of(sb)).astype(F32), axis=1, keepdims=True)
        cnt_sc[...] = total

    @pl.when((ps == 1) & (i == 0))
    def _():
        cnt = cnt_sc[...].astype(I32)
        shift = ROW_BLOCK.bit_length() - 1
        padded = lax.shift_left(lax.shift_right_logical(cnt + (ROW_BLOCK - 1), shift), shift)
        sub = lax.broadcasted_iota(I32, (N_EXPERTS, HEAD_PAD), 0)
        lane = lax.broadcasted_iota(I32, (N_EXPERTS, HEAD_PAD), 1)
        padded_row = jnp.sum(jnp.where(sub == lane, padded, 0), axis=0, keepdims=True)
        start = jnp.sum(jnp.where(lane < sub, padded_row, 0), axis=1, keepdims=True)
        start_sc[...] = start.astype(F32)
        carry_sc[...] = jnp.zeros(carry_sc.shape, F32)
        cnt_row = jnp.sum(jnp.where(sub == lane, cnt, 0), axis=0, keepdims=True)
        start_row = jnp.sum(jnp.where(sub == lane, start, 0), axis=0, keepdims=True)
        row8 = lax.broadcasted_iota(I32, (8, HEAD_PAD), 0)
        meta = jnp.where(row8 == 0, cnt_row, 0)
        meta = jnp.where(row8 == 1, start_row, meta)
        meta = jnp.where(row8 == 2, start_row + padded_row, meta)
        meta_ref[...] = meta

    @pl.when(ps == 1)
    def _():
        tr = lax.broadcasted_iota(I32, (tm, tm), 0)
        tc = lax.broadcasted_iota(I32, (tm, tm), 1)
        before = (tr < tc).astype(BF16)
        row8 = lax.broadcasted_iota(I32, (8, tm), 0)
        offset = carry_sc[...] + start_sc[...]
        for sb in range(n_sub):
            hits = hits_of(sb)
            member = members(hits)
            base = _dot(member.astype(BF16), before) + offset
            dest = jnp.zeros((8, tm), I32)
            for kk in range(TOP_K):
                dk = jnp.sum(jnp.where(hits[kk], base, 0.0), axis=0, keepdims=True).astype(I32)
                dest = jnp.where(row8 == kk, dk, dest)
            dest_ref[:, sb * tm:(sb + 1) * tm] = dest
            offset = offset + jnp.sum(member.astype(F32), axis=1, keepdims=True)
        carry_sc[...] = offset - start_sc[...]


def _positions(topi_t):
    t = topi_t.shape[1]
    tm = min(POSITION_TILE, t)
    return pl.pallas_call(
        _positions_kernel,
        out_shape=[jax.ShapeDtypeStruct((8, t), I32), jax.ShapeDtypeStruct((8, HEAD_PAD), I32)],
        grid=(2, t // tm),
        in_specs=[pl.BlockSpec((8, tm), lambda ps, i: (0, i))],
        out_specs=[pl.BlockSpec((8, tm), lambda ps, i: (0, i * ps)),
                   pl.BlockSpec((8, HEAD_PAD), lambda ps, i: (0, 0))],
        scratch_shapes=[pltpu.VMEM((N_EXPERTS, 1), F32)] * 3,
        compiler_params=_cparams(("arbitrary", "arbitrary")),
        name="routing_positions",
    )(topi_t)


def _sc_mesh():
    return plsc.VectorSubcoreMesh(core_axis_name="c", subcore_axis_name="s")


def _dispatch_rows(table, dest_flat, n_rows):
    t, c = table.shape
    n_slots = dest_flat.shape[0] // t
    per_w = t // SC_WORKERS
    assert per_w * SC_WORKERS == t and per_w % (2 * SC_WINDOW) == 0
    n_chunks = per_w // SC_WINDOW
    w = SC_WINDOW

    @functools.partial(
        pl.kernel, mesh=_sc_mesh(),
        out_type=jax.ShapeDtypeStruct((n_rows, c), table.dtype),
        scratch_types=[pltpu.VMEM((w,), I32)] * n_slots + [pltpu.VMEM((w, c), table.dtype)] * 2
        + [pltpu.SemaphoreType.DMA] * (n_slots + 2),
        name="dispatch_rows",
    )
    def k(table_hbm, dest_hbm, out_hbm, *scratch):
        idx = scratch[:n_slots]
        rows = scratch[n_slots:n_slots + 2]
        scatter_sems = scratch[n_slots + 2:2 * n_slots + 2]
        read_sems = scratch[2 * n_slots + 2:]
        wid = lax.axis_index("s") * SC_CORES + lax.axis_index("c")
        base = wid * per_w

        def off(chunk):
            return pl.multiple_of(base + chunk * w, w)

        def read(chunk, buf):
            return pltpu.make_async_copy(table_hbm.at[pl.ds(off(chunk), w)], rows[buf], read_sems[buf])

        def scatter(kk, buf):
            return pltpu.make_async_copy(rows[buf], out_hbm.at[idx[kk]], scatter_sems[kk])

        read(0, 0).start()

        @pl.loop(0, n_chunks // 2)
        def _(p):
            for buf in range(2):
                chunk = 2 * p + buf

                @pl.when(chunk + 1 < n_chunks)
                def _():
                    read(chunk + 1, 1 - buf).start()

                read(chunk, buf).wait()
                for kk in range(n_slots):
                    src = pl.multiple_of(kk * t + off(chunk), w)
                    pltpu.sync_copy(dest_hbm.at[pl.ds(src, w)], idx[kk])
                    scatter(kk, buf).start()
                for kk in range(n_slots):
                    scatter(kk, buf).wait()

    return k(table, dest_flat)


def _gather_rows(table, idx):
    n = idx.shape[0]
    c = table.shape[1]
    per_w = n // SC_WORKERS
    assert per_w * SC_WORKERS == n and per_w % (2 * SC_WINDOW) == 0
    n_chunks = per_w // SC_WINDOW
    w = SC_WINDOW

    @functools.partial(
        pl.kernel, mesh=_sc_mesh(),
        out_type=jax.ShapeDtypeStruct((n, c), table.dtype),
        scratch_types=[pltpu.VMEM((w,), I32)] * 2 + [pltpu.VMEM((w, c), table.dtype)] * 2
        + [pltpu.SemaphoreType.DMA] * 4,
        name="gather_rows",
    )
    def k(table_hbm, idx_hbm, out_hbm, idx_a, idx_b, rows_a, rows_b, g_a, g_b, w_a, w_b):
        idx, rows, gather_sems, write_sems = (idx_a, idx_b), (rows_a, rows_b), (g_a, g_b), (w_a, w_b)
        wid = lax.axis_index("s") * SC_CORES + lax.axis_index("c")
        base = wid * per_w

        def off(chunk):
            return pl.multiple_of(base + chunk * w, w)

        def gather(buf):
            return pltpu.make_async_copy(table_hbm.at[idx[buf]], rows[buf], gather_sems[buf])

        def write(chunk, buf):
            return pltpu.make_async_copy(rows[buf], out_hbm.at[pl.ds(off(chunk), w)], write_sems[buf])

        def start_gather(chunk, buf):
            pltpu.sync_copy(idx_hbm.at[pl.ds(off(chunk), w)], idx[buf])
            gather(buf).start()

        start_gather(0, 0)

        @pl.loop(0, n_chunks // 2)
        def _(p):
            for buf in range(2):
                chunk = 2 * p + buf

                @pl.when(chunk + 1 < n_chunks)
                def _():
                    @pl.when(chunk >= 1)
                    def _():
                        write(chunk - 1, 1 - buf).wait()
                    start_gather(chunk + 1, 1 - buf)

                gather(buf).wait()
                write(chunk, buf).start()

        write(n_chunks - 2, 0).wait()
        write(n_chunks - 1, 1).wait()

    return k(table, idx)


def _expert_kernel(be_ref, nused_ref, first_ref, slot_ref, next_ref,
                   xs_ref, wg_hbm, bg_ref, wu_hbm, bu_ref, wd_hbm, bd_ref,
                   ys_ref, wg_ref, wu_ref, wd_ref, stage_g, stage_u, stage_d, sems, *, layer):
    b = pl.program_id(0)
    used = b < nused_ref[0]
    weights = ((wg_hbm, stage_g), (wu_hbm, stage_u), (wd_hbm, stage_d))

    def fetch(expert, slot):
        return [pltpu.make_async_copy(hbm.at[layer, expert], stage.at[slot], sems.at[slot, n])
                for n, (hbm, stage) in enumerate(weights)]

    @pl.when(used & (first_ref[b] == 1))
    def _():
        slot = slot_ref[b]

        @pl.when(b == 0)
        def _():
            for cp in fetch(be_ref[0], slot):
                cp.start()

        for cp in fetch(be_ref[b], slot):
            cp.wait()
        wg_ref[...] = stage_g[slot].astype(BF16)
        wu_ref[...] = stage_u[slot].astype(BF16)
        wd_ref[...] = stage_d[slot].astype(BF16)

        @pl.when(next_ref[b] >= 0)
        def _():
            for cp in fetch(next_ref[b], 1 - slot):
                cp.start()

    @pl.when(used)
    def _():
        lo, hi = _unpack_halves(xs_ref[...])
        lo = lo.astype(BF16)
        hi = hi.astype(BF16)
        a = _dot(lo, wg_ref[:HALF, :]) + _dot(hi, wg_ref[HALF:, :]) + bg_ref[...]
        u = _dot(lo, wu_ref[:HALF, :]) + _dot(hi, wu_ref[HALF:, :]) + bu_ref[...]
        a = jnp.minimum(a, SWIGLU_LIMIT)
        u = jnp.clip(u, -SWIGLU_LIMIT, SWIGLU_LIMIT)
        y = (a * jax.nn.sigmoid(SWIGLU_ALPHA * a)) * (u + 1.0)
        out = _dot(y.astype(BF16), wd_ref[...]) + bd_ref[...]
        ys_ref[...] = _pack_halves(out[:, :HALF], out[:, HALF:])

    @pl.when(b >= nused_ref[0])
    def _():
        ys_ref[...] = jnp.zeros(ys_ref.shape, I32)


def _expert_ffn(xs, block_e, n_used, counts, layer, wg, bg, wu, bu, wd, bd):
    n_rows = xs.shape[0]
    n_blocks = n_rows // ROW_BLOCK

    blk = jnp.arange(n_blocks, dtype=I32)
    used = blk < n_used[0]
    first = used & ((blk == 0) | (block_e != jnp.roll(block_e, 1)))
    slot = (jnp.cumsum(first.astype(I32)) - 1) % 2
    eid = jnp.arange(N_EXPERTS, dtype=I32)
    later = (eid[None, :] > eid[:, None]) & (counts[None, :] > 0)
    next_of_expert = jnp.min(jnp.where(later, eid[None, :], N_EXPERTS), axis=1)
    next_of_expert = jnp.where(next_of_expert == N_EXPERTS, -1, next_of_expert)
    next_e = next_of_expert[block_e]

    def rows(b, *_):
        return (b, 0)

    def expert(b, be, *_):
        return (layer, be[b], 0, 0)

    hbm = pl.BlockSpec(memory_space=pl.ANY)
    grid_spec = pltpu.PrefetchScalarGridSpec(
        num_scalar_prefetch=5,
        grid=(n_blocks,),
        in_specs=[pl.BlockSpec((ROW_BLOCK, HALF), rows),
                  hbm, pl.BlockSpec((None, None, 1, D_EXPERT), expert),
                  hbm, pl.BlockSpec((None, None, 1, D_EXPERT), expert),
                  hbm, pl.BlockSpec((None, None, 1, D_MODEL), expert)],
        out_specs=pl.BlockSpec((ROW_BLOCK, HALF), rows),
        scratch_shapes=[pltpu.VMEM((D_MODEL, D_EXPERT), BF16), pltpu.VMEM((D_MODEL, D_EXPERT), BF16),
                        pltpu.VMEM((D_EXPERT, D_MODEL), BF16),
                        pltpu.VMEM((2, D_MODEL, D_EXPERT), F32), pltpu.VMEM((2, D_MODEL, D_EXPERT), F32),
                        pltpu.VMEM((2, D_EXPERT, D_MODEL), F32),
                        pltpu.SemaphoreType.DMA((2, 3))],
    )
    return pl.pallas_call(
        functools.partial(_expert_kernel, layer=layer),
        out_shape=jax.ShapeDtypeStruct((n_rows, HALF), I32),
        grid_spec=grid_spec,
        compiler_params=_cparams(("arbitrary",)),
        name="expert_ffn",
    )(block_e, n_used, first.astype(I32), slot.astype(I32), next_e.astype(I32),
      xs, wg, bg, wu, bu, wd, bd)


def _combine_kernel(x1_ref, yg_ref, wcol_ref, p_ref, gple_ref, wpg_ref, wpp_ref, gout_ref, o_ref,
                    *, final):
    x1 = x1_ref[...]
    acc_lo = x1[:, :HALF]
    acc_hi = x1[:, HALF:]
    wcol = wcol_ref[...]
    for kk in range(TOP_K):
        lo, hi = _unpack_halves(yg_ref[kk])
        wk = wcol[:, kk:kk + 1]
        acc_lo = acc_lo + wk * lo
        acc_hi = acc_hi + wk * hi
    x2 = jnp.concatenate([acc_lo, acc_hi], axis=1)
    gate = jax.nn.sigmoid(_dot(_rms(x2, gple_ref[...]).astype(BF16), wpg_ref[...]))
    x3 = x2 + gate * _dot(p_ref[...].astype(BF16), wpp_ref[...])
    o_ref[...] = _rms(x3, gout_ref[...]) if final else x3


def _combine_kernel_inplace(x1_ref, yg_ref, wcol_ref, p_ref, gple_ref, wpg_ref, wpp_ref, gout_ref,
                            prev_ref, o_ref, *, final):
    del prev_ref
    _combine_kernel(x1_ref, yg_ref, wcol_ref, p_ref, gple_ref, wpg_ref, wpp_ref, gout_ref, o_ref,
                    final=final)


def _combine(x1, yg, wcol, p3d, layer, gple, wpg, wpp, gout, final, part, prev):
    t = x1.shape[0]
    tm = TOKEN_TILE
    n_tiles = yg.shape[1] // tm
    first_tile = part * n_tiles

    def row(i, *_):
        return (first_tile + i, 0)

    def full(a):
        return pl.BlockSpec(a.shape, lambda i: (0, 0))

    in_specs = [pl.BlockSpec((tm, D_MODEL), row),
                pl.BlockSpec((TOP_K, tm, HALF), lambda i: (0, i, 0)),
                pl.BlockSpec((tm, HEAD_PAD), row),
                pl.BlockSpec((None, tm, PLE_DIM), lambda i: (layer, first_tile + i, 0)),
                full(gple), full(wpg), full(wpp), full(gout)]
    args = [x1, yg, wcol, p3d, gple, wpg, wpp, gout]
    kern = functools.partial(_combine_kernel, final=final)
    aliases = {}
    if prev is not None:
        in_specs.append(pl.BlockSpec(memory_space=pl.ANY))
        args.append(prev)
        aliases = {len(args) - 1: 0}
        kern = functools.partial(_combine_kernel_inplace, final=final)
    return pl.pallas_call(
        kern,
        out_shape=jax.ShapeDtypeStruct((t, D_MODEL), F32),
        grid=(n_tiles,),
        in_specs=in_specs,
        out_specs=pl.BlockSpec((tm, D_MODEL), row),
        input_output_aliases=aliases,
        compiler_params=_cparams(("parallel",)),
        name="combine_ple",
    )(*args)


def _rope_tables(seq):
    inv_freq = ROPE_THETA ** (-jnp.arange(HALF_ROPE, dtype=F32) * 2.0 / MLA_ROPE)
    ang = jnp.arange(seq, dtype=F32)[:, None] * inv_freq[None, :]
    cos, sin = jnp.cos(ang), jnp.sin(ang)
    ones = jnp.ones((seq, MLA_NOPE), F32)
    zeros16 = jnp.zeros((seq, HALF_ROPE), F32)
    zeros64 = jnp.zeros((seq, MLA_NOPE), F32)
    tail = jnp.ones((seq, HEAD_PAD - MLA_NOPE - MLA_ROPE), F32)
    ztail = jnp.zeros_like(tail)
    cos_t = jnp.concatenate([ones, cos, cos, tail], axis=1)
    sina_t = jnp.concatenate([zeros64, zeros16, sin, ztail], axis=1)
    sinb_t = jnp.concatenate([zeros64, -sin, zeros16, ztail], axis=1)
    return cos_t, sina_t, sinb_t


def _prep_mixer_weights(w_in, w_uq, w_ukv):
    c0 = MLA_Q_LORA + MLA_KV_LORA
    c1 = c0 + MLA_ROPE
    c2 = c1 + len(DIL_GROUPS) * DIL_COLS
    kr_pad = jnp.pad(w_in[:, c0:c1], ((0, 0), (MLA_NOPE, HEAD_PAD - MLA_NOPE - MLA_ROPE)))
    wmla = jnp.concatenate([w_in[:, :c0], kr_pad], axis=1).astype(BF16)
    col = np.arange(len(DIL_GROUPS) * DIL_COLS)
    q_scale = np.where(col % DIL_COLS < DIL_OUT, DIL_HEAD_DIM ** -0.5, 1.0).astype(np.float32)
    wdil = (w_in[:, c1:c2] * q_scale[None, :]).astype(BF16)
    wgate = w_in[:, c2:].astype(BF16)
    pad = HEAD_PAD - MLA_NOPE - MLA_ROPE
    wuq_h = w_uq.reshape(MLA_Q_LORA, MLA_HEADS, MLA_NOPE + MLA_ROPE)
    wuq = jnp.pad(wuq_h, ((0, 0), (0, 0), (0, pad))).reshape(MLA_Q_LORA, MLA_HEADS * HEAD_PAD).astype(BF16)
    wukv_h = w_ukv.reshape(MLA_KV_LORA, MLA_HEADS, MLA_NOPE + MLA_V)
    wuk = jnp.pad(wukv_h[:, :, :MLA_NOPE], ((0, 0), (0, 0), (0, HEAD_PAD - MLA_NOPE)))
    wuk = wuk.reshape(MLA_KV_LORA, MLA_HEADS * HEAD_PAD).astype(BF16)
    wuv = wukv_h[:, :, MLA_NOPE:].reshape(MLA_KV_LORA, MLA_HEADS * MLA_V).astype(BF16)
    return wmla, wdil, wgate, wuq, wuk, wuv


def kernel(x, p, attn_norm, w_in, q_norm, w_uq, kv_norm, w_ukv, w_branch_a, w_branch_b, w_out, ffn_norm, w_router, b_router, w_gate, b_gate, w_up, b_up, w_down, b_down, ple_norm, w_ple_gate, w_ple_proj, final_norm):
    b, s, d = x.shape
    depth = w_in.shape[0]
    t = b * s
    assert d == D_MODEL and s % (DIL_GROUPS[-1][0]) == 0 and t % (SC_WORKERS * SC_WINDOW) == 0
    n_assign = t * TOP_K
    n_blocks = -(-(n_assign + N_EXPERTS * (ROW_BLOCK - 1)) // ROW_BLOCK)
    n_rows = n_blocks * ROW_BLOCK
    cos_t, sina_t, sinb_t = _rope_tables(s)
    xc = x.reshape(t, d)
    for i in range(depth):
        wmla, wdil, wgate, wuq, wuk, wuv = _prep_mixer_weights(w_in[i], w_uq[i], w_ukv[i])
        q, k, vt, zd0, zd1, zd2 = _inproj(
            xc, s, attn_norm[i][None], wmla, wdil, q_norm[i][None], kv_norm[i][None],
            wuq, wuk, wuv, cos_t, sina_t, sinb_t)
        oa = _mla_attention(q.reshape(b, s, -1), k.reshape(b, s, -1), vt)
        ob = _dilated_attention(zd0, zd1, zd2, s)
        x1, hp, topi_t, topw_t, wcol = _merge(
            xc, oa.reshape(t, -1), ob, attn_norm[i][None], wgate,
            w_branch_a[i].astype(BF16), w_branch_b[i].astype(BF16), w_out[i].astype(BF16),
            ffn_norm[i][None], w_router[i].T.astype(BF16), b_router[i][:, None])
        dest_t, meta = _positions(topi_t)
        ends = meta[2, :N_EXPERTS]
        block_start = jnp.arange(n_blocks, dtype=I32) * ROW_BLOCK
        block_e = jnp.minimum(
            jnp.sum((ends[None, :] <= block_start[:, None]).astype(I32), axis=1), N_EXPERTS - 1)
        n_used = (ends[N_EXPERTS - 1:] // ROW_BLOCK).astype(I32)
        dest_flat = dest_t[:TOP_K].reshape(n_assign)
        xs = _dispatch_rows(hp, dest_flat, n_rows)
        ys = _expert_ffn(xs, block_e, n_used, meta[0, :N_EXPERTS], i,
                         w_gate, b_gate[:, :, None, :], w_up, b_up[:, :, None, :],
                         w_down, b_down[:, :, None, :])
        final = i == depth - 1
        gout = final_norm[None] if final else attn_norm[i][None]
        wpg, wpp = w_ple_gate[i].astype(BF16), w_ple_proj[i].astype(BF16)
        tp = t // COMBINE_PARTS
        xc = None
        for part in range(COMBINE_PARTS):
            dest_part = dest_t[:TOP_K, part * tp:(part + 1) * tp].reshape(TOP_K * tp)
            yg = _gather_rows(ys, dest_part).reshape(TOP_K, tp, HALF)
            xc = _combine(x1, yg, wcol, p.reshape(depth, t, PLE_DIM), i, ple_norm[i][None],
                          wpg, wpp, gout, final, part, xc)
    return xc.reshape(b, s, d)
```

```python
import functools
import math

import jax
import jax.numpy as jnp
import numpy as np
from jax import lax
from jax.experimental import pallas as pl
from jax.experimental.pallas import tpu as pltpu
from jax.experimental.pallas import tpu_sc as plsc

F32 = jnp.float32
BF16 = jnp.bfloat16
I32 = jnp.int32

D_MODEL = 1024
PLE_DIM = 256
NORM_EPS = 1e-6

MLA_HEADS = 8
MLA_Q_LORA = 384
MLA_KV_LORA = 256
MLA_NOPE = 64
MLA_ROPE = 32
MLA_V = 64
ROPE_THETA = 10000.0
HEAD_PAD = 128
HALF_ROPE = MLA_ROPE // 2

DIL_GROUPS = ((128, 1), (512, 4), (2048, 16))
DIL_HEADS = 4
DIL_HEAD_DIM = 64
DIL_STEPS = 128
DIL_COLS = 3 * DIL_HEADS * DIL_HEAD_DIM
DIL_OUT = DIL_HEADS * DIL_HEAD_DIM
DIL_UNROLL = 4

N_EXPERTS = 32
TOP_K = 4
D_EXPERT = 1024
SWIGLU_LIMIT = 7.0
SWIGLU_ALPHA = 1.702
ROW_BLOCK = 512

TOKEN_TILE = 512
POSITION_TILE = 4096
POSITION_SUBTILE = 512
COMBINE_PARTS = 2
ATTN_TQ = 1024
ATTN_TK = 1024
ATTN_HEADS = 4
ATTN_SUB = 1024
ATTN_KEY_CHUNKS = 2
HALF = D_MODEL // 2
NEG = -1e30
SPECULATION_HEADROOM = 60.0
HI_MASK = -65536

SC_CORES = 2
SC_SUBCORES = 16
SC_WORKERS = SC_CORES * SC_SUBCORES
SC_WINDOW = 64

VMEM_LIMIT = 56 * 1024 * 1024


def _cparams(sem):
    return pltpu.CompilerParams(dimension_semantics=sem, vmem_limit_bytes=VMEM_LIMIT)


def _rms(x, g):
    return x * lax.rsqrt(jnp.mean(x * x, axis=-1, keepdims=True) + NORM_EPS) * g


def _dot(a, b):
    return jnp.dot(a, b, preferred_element_type=F32)


def _dot_nt(a, b):
    return lax.dot_general(a, b, (((1,), (1,)), ((), ())), preferred_element_type=F32)


def _pack_halves(lo, hi):
    lo_i = lax.bitcast_convert_type(lo.astype(BF16).astype(F32), I32)
    hi_i = lax.bitcast_convert_type(hi.astype(BF16).astype(F32), I32)
    return (hi_i & HI_MASK) | lax.shift_right_logical(lo_i, 16)


def _unpack_halves(w):
    lo = lax.bitcast_convert_type(lax.shift_left(w, 16), F32)
    hi = lax.bitcast_convert_type(w & HI_MASK, F32)
    return lo, hi


def _inproj_kernel(x_ref, g_ref, wmla_ref, wdil_ref, qn_ref, kvn_ref, wuq_ref, wuk_ref,
                   wuv_ref, cos_ref, sina_ref, sinb_ref,
                   q_ref, k_ref, vt_ref, zd0_ref, zd1_ref, zd2_ref, zs_sc):
    h = _rms(x_ref[...], g_ref[...]).astype(BF16)
    zm = _dot(h, wmla_ref[...])
    cq = _rms(zm[:, :MLA_Q_LORA], qn_ref[...]).astype(BF16)
    ckv = _rms(zm[:, MLA_Q_LORA:MLA_Q_LORA + MLA_KV_LORA], kvn_ref[...]).astype(BF16)
    kr = zm[:, MLA_Q_LORA + MLA_KV_LORA:]
    cos, sina, sinb = cos_ref[...], sina_ref[...], sinb_ref[...]

    def rope(t):
        return (t * cos + pltpu.roll(t, HALF_ROPE, 1) * sina
                + pltpu.roll(t, HEAD_PAD - HALF_ROPE, 1) * sinb)

    kr_rot = rope(kr)
    qraw = _dot(cq, wuq_ref[...])
    kraw = _dot(ckv, wuk_ref[...])
    vt_ref[...] = _dot(ckv, wuv_ref[...]).T.astype(BF16)
    scale = (MLA_NOPE + MLA_ROPE) ** -0.5 * math.log2(math.e)
    for hd in range(MLA_HEADS):
        sl = slice(hd * HEAD_PAD, (hd + 1) * HEAD_PAD)
        q_ref[:, sl] = (rope(qraw[:, sl]) * scale).astype(BF16)
        k_ref[:, sl] = (kraw[:, sl] + kr_rot).astype(BF16)
    tm = x_ref.shape[0]
    for gi, zd_ref in enumerate((zd0_ref, zd1_ref, zd2_ref)):
        z = _dot(h, wdil_ref[:, gi * DIL_COLS:(gi + 1) * DIL_COLS])
        window, dil = DIL_GROUPS[gi]
        if dil == 1:
            zd_ref[...] = z.astype(BF16)
            continue
        n_col = DIL_COLS // HEAD_PAD
        for c in range(n_col):
            zs_sc[c] = z[:, c * HEAD_PAD:(c + 1) * HEAD_PAD]
        rows = tm // dil
        part = pl.program_id(0) % (window // tm)
        for r in range(dil):
            dst = pl.ds(pl.multiple_of(r * DIL_STEPS + part * rows, rows), rows)
            for c in range(n_col):
                chunk = zs_sc[c, pl.ds(r, rows, stride=dil), :]
                zd_ref[dst, c * HEAD_PAD:(c + 1) * HEAD_PAD] = chunk.astype(BF16)


def _inproj(x2d, seq, g, wmla, wdil, qn, kvn, wuq, wuk, wuv, cos_t, sina_t, sinb_t):
    t = x2d.shape[0]
    tm = TOKEN_TILE
    n_seq_tiles = seq // tm

    def row(i):
        return (i, 0)

    def const(i):
        return (0, 0)

    def pos(i):
        return (i % n_seq_tiles, 0)

    def full(a):
        return pl.BlockSpec(a.shape, const)

    def vt_block(i):
        return (i // n_seq_tiles, 0, i % n_seq_tiles)

    def rows(width):
        return pl.BlockSpec((tm, width), row)

    def unit_rows(window):
        return pl.BlockSpec((window, DIL_COLS), lambda i: (i // (window // tm), 0))

    out_shape = [
        jax.ShapeDtypeStruct((t, MLA_HEADS * HEAD_PAD), BF16),
        jax.ShapeDtypeStruct((t, MLA_HEADS * HEAD_PAD), BF16),
        jax.ShapeDtypeStruct((t // seq, MLA_HEADS * MLA_V, seq), BF16),
        jax.ShapeDtypeStruct((t, DIL_COLS), BF16),
        jax.ShapeDtypeStruct((t, DIL_COLS), BF16),
        jax.ShapeDtypeStruct((t, DIL_COLS), BF16),
    ]
    return pl.pallas_call(
        _inproj_kernel,
        out_shape=out_shape,
        grid=(t // tm,),
        in_specs=[pl.BlockSpec((tm, D_MODEL), row), full(g), full(wmla), full(wdil),
                  full(qn), full(kvn), full(wuq), full(wuk), full(wuv),
                  pl.BlockSpec((tm, HEAD_PAD), pos), pl.BlockSpec((tm, HEAD_PAD), pos),
                  pl.BlockSpec((tm, HEAD_PAD), pos)],
        out_specs=[rows(D_MODEL), rows(D_MODEL),
                   pl.BlockSpec((None, MLA_HEADS * MLA_V, tm), vt_block),
                   rows(DIL_COLS)] + [unit_rows(window) for window, _ in DIL_GROUPS[1:]],
        scratch_shapes=[pltpu.VMEM((DIL_COLS // HEAD_PAD, tm, HEAD_PAD), F32)],
        compiler_params=_cparams(("arbitrary",)),
        name="inproj",
    )(x2d, g, wmla, wdil, qn, kvn, wuq, wuk, wuv, cos_t, sina_t, sinb_t)


def _mla_kernel(qi_ref, kj_ref, q_ref, k_ref, vt_ref, o_ref, m_sc, l_sc, acc_sc, redo_sc):
    p = pl.program_id(2)
    i = qi_ref[p]
    j = kj_ref[p]
    tq = q_ref.shape[0]
    tk = k_ref.shape[0]

    @pl.when(j == 0)
    def _():
        m_sc[...] = jnp.full(m_sc.shape, NEG, F32)
        l_sc[...] = jnp.zeros(l_sc.shape, F32)
        acc_sc[...] = jnp.zeros(acc_sc.shape, F32)

    ratio = tq // tk
    sub = ATTN_SUB

    def step(diagonal, speculative, first=False):
        chains = [(hh, c) for hh in range(ATTN_HEADS) for c in range(tq // sub)]
        ones_rows = (lax.broadcasted_iota(I32, (16, tk), 0) == 0).astype(BF16)
        state = {}
        for hh, c in chains:
            cs = slice(c * sub, (c + 1) * sub)
            if first and speculative:
                sl = slice(hh * HEAD_PAD, (hh + 1) * HEAD_PAD)
                s0 = _dot_nt(k_ref[:8, sl], q_ref[cs, sl])[:1]
                state[hh, c] = (s0, jnp.zeros((1, sub), F32), jnp.zeros((MLA_V, sub), F32))
            else:
                state[hh, c] = (m_sc[hh, :, cs], l_sc[hh, :, cs], acc_sc[hh, :, cs])
        new_state = {}
        within = None
        def scores(hh, c, rows, q0=0):
            sl = slice(hh * HEAD_PAD, (hh + 1) * HEAD_PAD)
            st = _dot_nt(k_ref[rows, sl], q_ref[c * sub + q0:(c + 1) * sub, sl])
            if diagonal:
                shape = (rows.stop - rows.start, sub - q0)
                key = lax.broadcasted_iota(I32, shape, 0) + (j * tk + rows.start)
                qry = lax.broadcasted_iota(I32, shape, 1) + (i * tq + c * sub + q0)
                st = jnp.where(qry >= key, st, NEG)
            return st

        def values(hh, rows):
            ones_row = (lax.broadcasted_iota(I32, (16, rows.stop - rows.start), 0) == 0).astype(BF16)
            return jnp.concatenate([vt_ref[hh * MLA_V:(hh + 1) * MLA_V, rows], ones_row], axis=0)

        for hh, c in chains:
            m_prev, l_prev, acc_prev = state[hh, c]
            if speculative:
                kc = tk // ATTN_KEY_CHUNKS
                m_blk, pv = None, None
                for n in range(ATTN_KEY_CHUNKS):
                    rows = slice(n * kc, (n + 1) * kc)
                    q0 = n * kc if (diagonal and tq == tk and sub == tq) else 0
                    st = scores(hh, c, rows, q0)
                    m_part = jnp.max(st, axis=0, keepdims=True)
                    part = _dot(values(hh, rows), jnp.exp2(st - m_prev[:, q0:]).astype(BF16))
                    if q0:
                        m_part = jnp.concatenate([jnp.full((1, q0), NEG, F32), m_part], axis=1)
                        part = jnp.concatenate([jnp.zeros((part.shape[0], q0), F32), part], axis=1)
                    m_blk = m_part if m_blk is None else jnp.maximum(m_blk, m_part)
                    pv = part if pv is None else pv + part
                m_new = jnp.maximum(m_prev, m_blk)
                alpha = jnp.exp2(m_prev - m_new)
                l_new = alpha * (l_prev + pv[MLA_V:MLA_V + 1])
                acc_new = alpha * (acc_prev + pv[:MLA_V])
                ok = jnp.max(m_blk - m_prev) <= SPECULATION_HEADROOM
                within = ok if within is None else (within & ok)
            else:
                rows = slice(0, tk)
                st = scores(hh, c, rows)
                m_new = jnp.maximum(m_prev, jnp.max(st, axis=0, keepdims=True))
                alpha = jnp.exp2(m_prev - m_new)
                pv = _dot(values(hh, rows), jnp.exp2(st - m_new).astype(BF16))
                l_new = alpha * l_prev + pv[MLA_V:MLA_V + 1]
                acc_new = alpha * acc_prev + pv[:MLA_V]
            new_state[hh, c] = (m_new, l_new, acc_new)

        def commit():
            for hh, c in chains:
                cs = slice(c * sub, (c + 1) * sub)
                m_sc[hh, :, cs], l_sc[hh, :, cs], acc_sc[hh, :, cs] = new_state[hh, c]

        if speculative:
            pl.when(within)(commit)
            redo_sc[0] = jnp.logical_not(within).astype(I32)
        else:
            commit()

    redo_sc[0] = 0
    on_diagonal = j >= ratio * i

    @pl.when((j == 0) & on_diagonal)
    def _():
        step(True, True, first=True)

    @pl.when((j == 0) & jnp.logical_not(on_diagonal))
    def _():
        step(False, True, first=True)

    @pl.when((j > 0) & jnp.logical_not(on_diagonal))
    def _():
        step(False, True)

    @pl.when((j > 0) & on_diagonal)
    def _():
        step(True, True)

    @pl.when(redo_sc[0] != 0)
    def _():
        step(True, False)

    @pl.when(j == ratio * i + (ratio - 1))
    def _():
        ot = jnp.concatenate([acc_sc[hh] / l_sc[hh] for hh in range(ATTN_HEADS)], axis=0)
        o_ref[...] = ot.T.astype(BF16)


def _mla_attention(q, k, vt):
    b, s, _ = q.shape
    tq, tk, nh = ATTN_TQ, ATTN_TK, ATTN_HEADS
    ratio = tq // tk
    nq = s // tq
    pairs = [(i, j) for i in range(nq) for j in range(ratio * (i + 1))]
    qi = jnp.asarray([p[0] for p in pairs], I32)
    kj = jnp.asarray([p[1] for p in pairs], I32)
    grid_spec = pltpu.PrefetchScalarGridSpec(
        num_scalar_prefetch=2,
        grid=(b, MLA_HEADS // nh, len(pairs)),
        in_specs=[
            pl.BlockSpec((None, tq, nh * HEAD_PAD), lambda bb, hp, p, qi, kj: (bb, qi[p], hp)),
            pl.BlockSpec((None, tk, nh * HEAD_PAD), lambda bb, hp, p, qi, kj: (bb, kj[p], hp)),
            pl.BlockSpec((None, nh * MLA_V, tk), lambda bb, hp, p, qi, kj: (bb, hp, kj[p])),
        ],
        out_specs=pl.BlockSpec((None, tq, nh * MLA_V), lambda bb, hp, p, qi, kj: (bb, qi[p], hp)),
        scratch_shapes=[pltpu.VMEM((nh, 1, tq), F32), pltpu.VMEM((nh, 1, tq), F32),
                        pltpu.VMEM((nh, MLA_V, tq), F32), pltpu.SMEM((1,), I32)],
    )
    return pl.pallas_call(
        _mla_kernel,
        out_shape=jax.ShapeDtypeStruct((b, s, MLA_HEADS * MLA_V), BF16),
        grid_spec=grid_spec,
        compiler_params=_cparams(("parallel", "parallel", "arbitrary")),
        name="mla_attention",
    )(qi, kj, q, k, vt)


def _alibi_slopes(n):
    def pow2(m):
        start = 2.0 ** (-8.0 / m)
        return [start ** (i + 1) for i in range(m)]
    if math.log2(n).is_integer():
        s = pow2(n)
    else:
        c = 2 ** int(math.floor(math.log2(n)))
        s = pow2(c) + pow2(2 * c)[0::2][: n - c]
    return np.array(sorted(s, reverse=True), dtype=np.float32)


def _dilated_block(cur, prev, bias4, first):
    n = DIL_STEPS
    hw = DIL_OUT
    q = cur[:, :hw]
    kk = jnp.concatenate([prev[:, hw:2 * hw], cur[:, hw:2 * hw]], axis=0)
    vv = jnp.concatenate([prev[:, 2 * hw:], cur[:, 2 * hw:]], axis=0)
    head_of_lane = lax.broadcasted_iota(I32, (n, hw), 1) // DIL_HEAD_DIM
    zero = jnp.zeros_like(q)
    q4 = jnp.concatenate([jnp.where(head_of_lane == h, q, zero) for h in range(DIL_HEADS)], axis=0)
    s4 = _dot_nt(q4, kk) + bias4
    if first is not None:
        ki = lax.broadcasted_iota(I32, (DIL_HEADS * n, 2 * n), 1)
        s4 = jnp.where(first & (ki < n), NEG, s4)
    m4 = jnp.max(s4, axis=1, keepdims=True)
    p4 = jnp.exp(s4 - m4).astype(BF16)
    l4 = _dot(p4, jnp.ones((2 * n, HEAD_PAD), BF16))
    pv4 = _dot(p4, vv)
    m4 = jnp.broadcast_to(m4, (DIL_HEADS * n, HEAD_PAD))

    def rows(a, h):
        return a[h * n:(h + 1) * n]

    o_un = rows(pv4, DIL_HEADS - 1)
    for h in range(DIL_HEADS - 2, -1, -1):
        o_un = jnp.where(head_of_lane == h, rows(pv4, h), o_un)
    low = lax.broadcasted_iota(I32, (n, HEAD_PAD), 1) < DIL_HEAD_DIM

    def per_lane(a):
        return jnp.concatenate([jnp.where(low, rows(a, 0), rows(a, 1)),
                                jnp.where(low, rows(a, 2), rows(a, 3))], axis=1)

    l_sel = per_lane(l4)
    return o_un / l_sel, per_lane(m4) + jnp.log(l_sel)


def _dilated_kernel(c0_ref, h0_ref, c1_ref, h1_ref, c2_ref, h2_ref, ob_ref, o_sc, l_sc, bias_sc,
                    *, slopes):
    u = pl.program_id(1)
    n = DIL_STEPS
    unit = ob_ref.shape[0]
    n_sb = unit // n
    first = u == 0
    qi = lax.broadcasted_iota(I32, (n, 2 * n), 0)
    ki = lax.broadcasted_iota(I32, (n, 2 * n), 1)
    dist = qi + n - ki
    valid = (dist >= 0) & (dist <= n)
    distf = dist.astype(F32)
    for gi in range(len(DIL_GROUPS)):
        for h in range(DIL_HEADS):
            bias_sc[gi, h * n:(h + 1) * n, :] = jnp.where(valid, -slopes[gi][h] * distf, NEG)

    def rows_of(ref, sb):
        return ref[pl.ds(pl.multiple_of(sb * n, n), n), :]

    def emit(gi, start, stride, o, lse):
        if isinstance(start, int):
            idx = pl.ds(start, n)
        elif stride == 1:
            idx = pl.ds(pl.multiple_of(start, n), n)
        else:
            idx = pl.ds(start, n, stride=stride)
        for half in range(DIL_OUT // HEAD_PAD):
            ls = slice(half * HEAD_PAD, (half + 1) * HEAD_PAD)
            o_sc[gi, half, idx, :] = o[:, ls]
            l_sc[gi, half, idx, :] = lse[:, ls]

    emit(0, 0, 1, *_dilated_block(c0_ref[:n, :], h0_ref[...], bias_sc[0], first))

    def g0_body(sb, carry):
        emit(0, sb * n, 1, *_dilated_block(rows_of(c0_ref, sb), rows_of(c0_ref, sb - 1), bias_sc[0], None))
        return carry

    lax.fori_loop(1, n_sb, g0_body, 0, unroll=DIL_UNROLL)

    d1 = DIL_GROUPS[1][1]

    def g1_head(r, carry):
        emit(1, r, d1, *_dilated_block(rows_of(c1_ref, r), rows_of(h1_ref, r), bias_sc[1], first))
        return carry

    def g1_body(sb, carry):
        start = (sb // d1) * (d1 * n) + sb % d1
        emit(1, start, d1, *_dilated_block(rows_of(c1_ref, sb), rows_of(c1_ref, sb - d1), bias_sc[1], None))
        return carry

    lax.fori_loop(0, d1, g1_head, 0, unroll=DIL_UNROLL)
    lax.fori_loop(d1, n_sb, g1_body, 0, unroll=DIL_UNROLL)

    d2 = DIL_GROUPS[2][1]

    def g2_body(r, carry):
        emit(2, r, d2, *_dilated_block(rows_of(c2_ref, r), rows_of(h2_ref, r), bias_sc[2], first))
        return carry

    lax.fori_loop(0, n_sb, g2_body, 0, unroll=DIL_UNROLL)

    def merge_body(c, carry):
        idx = pl.ds(pl.multiple_of(c * n, n), n)
        for half in range(DIL_OUT // HEAD_PAD):
            l0, l1, l2 = l_sc[0, half, idx, :], l_sc[1, half, idx, :], l_sc[2, half, idx, :]
            lmax = jnp.maximum(jnp.maximum(l0, l1), l2)
            e0, e1, e2 = jnp.exp(l0 - lmax), jnp.exp(l1 - lmax), jnp.exp(l2 - lmax)
            ob = (e0 * o_sc[0, half, idx, :] + e1 * o_sc[1, half, idx, :]
                  + e2 * o_sc[2, half, idx, :]) / (e0 + e1 + e2)
            ob_ref[idx, half * HEAD_PAD:(half + 1) * HEAD_PAD] = ob.astype(BF16)
        return carry

    lax.fori_loop(0, n_sb, merge_body, 0)


def _dilated_attention(zd0, zd1, zd2, seq):
    t = zd0.shape[0]
    unit = DIL_GROUPS[-1][0]
    upb = seq // unit
    n = DIL_STEPS
    u1 = DIL_GROUPS[1][0]
    all_slopes = _alibi_slopes(len(DIL_GROUPS) * DIL_HEADS).reshape(len(DIL_GROUPS), DIL_HEADS)
    slopes = tuple(tuple(float(x) * dil for x in all_slopes[gi]) for gi, (_, dil) in enumerate(DIL_GROUPS))

    def cur(bb, u):
        return (bb * upb + u, 0)

    def halo(rows):
        per_unit = unit // rows
        return lambda bb, u: ((bb * upb) * per_unit + jnp.maximum(u * per_unit - 1, 0), 0)

    return pl.pallas_call(
        functools.partial(_dilated_kernel, slopes=slopes),
        out_shape=jax.ShapeDtypeStruct((t, DIL_OUT), BF16),
        grid=(t // seq, upb),
        in_specs=[pl.BlockSpec((unit, DIL_COLS), cur), pl.BlockSpec((n, DIL_COLS), halo(n)),
                  pl.BlockSpec((unit, DIL_COLS), cur), pl.BlockSpec((u1, DIL_COLS), halo(u1)),
                  pl.BlockSpec((unit, DIL_COLS), cur), pl.BlockSpec((unit, DIL_COLS), halo(unit))],
        out_specs=pl.BlockSpec((unit, DIL_OUT), cur),
        scratch_shapes=[pltpu.VMEM((len(DIL_GROUPS), DIL_OUT // HEAD_PAD, unit, HEAD_PAD), F32),
                        pltpu.VMEM((len(DIL_GROUPS), DIL_OUT // HEAD_PAD, unit, HEAD_PAD), F32),
                        pltpu.VMEM((len(DIL_GROUPS), DIL_HEADS * n, 2 * n), F32)],
        compiler_params=_cparams(("parallel", "arbitrary")),
        name="dilated_attention",
    )(zd0, zd0, zd1, zd1, zd2, zd2)


def _merge_kernel(x_ref, oa_ref, ob_ref, ga_ref, wgate_ref,
                  wa_ref, wb_ref, wo_ref, g_ref, wr_ref, br_ref,
                  x1_ref, hp_ref, topi_ref, topw_ref, wcol_ref):
    tm = x_ref.shape[0]
    x = x_ref[...]
    h = _rms(x, ga_ref[...]).astype(BF16)
    mixed = (jax.nn.sigmoid(_dot(h, wgate_ref[:, :D_MODEL])) * _dot(oa_ref[...], wa_ref[...])
             + jax.nn.sigmoid(_dot(h, wgate_ref[:, D_MODEL:])) * _dot(ob_ref[...], wb_ref[...]))
    x1 = x + _dot(mixed.astype(BF16), wo_ref[...])
    x1_ref[...] = x1
    h2 = _rms(x1, g_ref[...])
    hp_ref[...] = _pack_halves(h2[:, :HALF], h2[:, HALF:])

    logits = _dot_nt(wr_ref[...], h2.astype(BF16)) + br_ref[...]
    eidx = lax.broadcasted_iota(I32, (N_EXPERTS, tm), 0)
    vals, idxs = [], []
    for _ in range(TOP_K):
        m = jnp.max(logits, axis=0, keepdims=True)
        idx = jnp.min(jnp.where(logits == m, eidx, N_EXPERTS), axis=0, keepdims=True)
        vals.append(m)
        idxs.append(idx)
        logits = jnp.where(eidx == idx, -jnp.inf, logits)
    exps = [jnp.exp(vk - vals[0]) for vk in vals]
    den = exps[0] + exps[1] + exps[2] + exps[3]
    row8 = lax.broadcasted_iota(I32, (8, tm), 0)
    row128 = lax.broadcasted_iota(I32, (HEAD_PAD, tm), 0)
    topi = jnp.zeros((8, tm), I32)
    topw = jnp.zeros((8, tm), F32)
    wide = jnp.zeros((HEAD_PAD, tm), F32)
    for kk in range(TOP_K):
        wk = exps[kk] / den
        topi = jnp.where(row8 == kk, idxs[kk], topi)
        topw = jnp.where(row8 == kk, wk, topw)
        wide = jnp.where(row128 == kk, wk, wide)
    topi_ref[...] = topi
    topw_ref[...] = topw
    wcol_ref[...] = wide.T


def _merge(x2d, oa, ob, g_attn, wgate, wa, wb, wo, g, wr_t, br_col):
    t = x2d.shape[0]
    tm = TOKEN_TILE

    def row(i):
        return (i, 0)

    def col(i):
        return (0, i)

    def full(a):
        return pl.BlockSpec(a.shape, lambda i: (0, 0))

    def rows(width):
        return pl.BlockSpec((tm, width), row)

    out_shape = [
        jax.ShapeDtypeStruct((t, D_MODEL), F32),
        jax.ShapeDtypeStruct((t, HALF), I32),
        jax.ShapeDtypeStruct((8, t), I32),
        jax.ShapeDtypeStruct((8, t), F32),
        jax.ShapeDtypeStruct((t, HEAD_PAD), F32),
    ]
    return pl.pallas_call(
        _merge_kernel,
        out_shape=out_shape,
        grid=(t // tm,),
        in_specs=[rows(D_MODEL), rows(MLA_HEADS * MLA_V), rows(DIL_OUT), full(g_attn), full(wgate)]
        + [full(wa), full(wb), full(wo), full(g), full(wr_t), full(br_col)],
        out_specs=[rows(D_MODEL), rows(HALF), pl.BlockSpec((8, tm), col), pl.BlockSpec((8, tm), col),
                   rows(HEAD_PAD)],
        compiler_params=_cparams(("parallel",)),
        name="merge_router",
    )(x2d, oa, ob, g_attn, wgate, wa, wb, wo, g, wr_t, br_col)


def _positions_kernel(topi_ref, dest_ref, meta_ref, cnt_sc, carry_sc, start_sc):
    ps = pl.program_id(0)
    i = pl.program_id(1)
    tm = POSITION_SUBTILE
    n_sub = topi_ref.shape[1] // tm
    eidx = lax.broadcasted_iota(I32, (N_EXPERTS, tm), 0)

    def hits_of(sb):
        topi = topi_ref[:, sb * tm:(sb + 1) * tm]
        return [eidx == topi[kk:kk + 1, :] for kk in range(TOP_K)]

    def members(hits):
        return hits[0] | hits[1] | hits[2] | hits[3]

    @pl.when((ps == 0) & (i == 0))
    def _():
        cnt_sc[...] = jnp.zeros(cnt_sc.shape, F32)

    @pl.when(ps == 0)
    def _():
        total = cnt_sc[...]
        for sb in range(n_sub):
            total = total + jnp.sum(members(hits_of(sb)).astype(F32), axis=1, keepdims=True)
        cnt_sc[...] = total

    @pl.when((ps == 1) & (i == 0))
    def _():
        cnt = cnt_sc[...].astype(I32)
        shift = ROW_BLOCK.bit_length() - 1
        padded = lax.shift_left(lax.shift_right_logical(cnt + (ROW_BLOCK - 1), shift), shift)
        sub = lax.broadcasted_iota(I32, (N_EXPERTS, HEAD_PAD), 0)
        lane = lax.broadcasted_iota(I32, (N_EXPERTS, HEAD_PAD), 1)
        padded_row = jnp.sum(jnp.where(sub == lane, padded, 0), axis=0, keepdims=True)
        start = jnp.sum(jnp.where(lane < sub, padded_row, 0), axis=1, keepdims=True)
        start_sc[...] = start.astype(F32)
        carry_sc[...] = jnp.zeros(carry_sc.shape, F32)
        cnt_row = jnp.sum(jnp.where(sub == lane, cnt, 0), axis=0, keepdims=True)
        start_row = jnp.sum(jnp.where(sub == lane, start, 0), axis=0, keepdims=True)
        row8 = lax.broadcasted_iota(I32, (8, HEAD_PAD), 0)
        meta = jnp.where(row8 == 0, cnt_row, 0)
        meta = jnp.where(row8 == 1, start_row, meta)
        meta = jnp.where(row8 == 2, start_row + padded_row, meta)
        meta_ref[...] = meta

    @pl.when(ps == 1)
    def _():
        tr = lax.broadcasted_iota(I32, (tm, tm), 0)
        tc = lax.broadcasted_iota(I32, (tm, tm), 1)
        before = (tr < tc).astype(BF16)
        row8 = lax.broadcasted_iota(I32, (8, tm), 0)
        offset = carry_sc[...] + start_sc[...]
        for sb in range(n_sub):
            hits = hits_of(sb)
            member = members(hits)
            base = _dot(member.astype(BF16), before) + offset
            dest = jnp.zeros((8, tm), I32)
            for kk in range(TOP_K):
                dk = jnp.sum(jnp.where(hits[kk], base, 0.0), axis=0, keepdims=True).astype(I32)
                dest = jnp.where(row8 == kk, dk, dest)
            dest_ref[:, sb * tm:(sb + 1) * tm] = dest
            offset = offset + jnp.sum(member.astype(F32), axis=1, keepdims=True)
        carry_sc[...] = offset - start_sc[...]


def _positions(topi_t):
    t = topi_t.shape[1]
    tm = min(POSITION_TILE, t)
    return pl.pallas_call(
        _positions_kernel,
        out_shape=[jax.ShapeDtypeStruct((8, t), I32), jax.ShapeDtypeStruct((8, HEAD_PAD), I32)],
        grid=(2, t // tm),
        in_specs=[pl.BlockSpec((8, tm), lambda ps, i: (0, i))],
        out_specs=[pl.BlockSpec((8, tm), lambda ps, i: (0, i * ps)),
                   pl.BlockSpec((8, HEAD_PAD), lambda ps, i: (0, 0))],
        scratch_shapes=[pltpu.VMEM((N_EXPERTS, 1), F32)] * 3,
        compiler_params=_cparams(("arbitrary", "arbitrary")),
        name="routing_positions",
    )(topi_t)


def _sc_mesh():
    return plsc.VectorSubcoreMesh(core_axis_name="c", subcore_axis_name="s")


def _dispatch_rows(table, dest_flat, n_rows):
    t, c = table.shape
    n_slots = dest_flat.shape[0] // t
    per_w = t // SC_WORKERS
    assert per_w * SC_WORKERS == t and per_w % (2 * SC_WINDOW) == 0
    n_chunks = per_w // SC_WINDOW
    w = SC_WINDOW

    @functools.partial(
        pl.kernel, mesh=_sc_mesh(),
        out_type=jax.ShapeDtypeStruct((n_rows, c), table.dtype),
        scratch_types=[pltpu.VMEM((w,), I32)] * n_slots + [pltpu.VMEM((w, c), table.dtype)] * 2
        + [pltpu.SemaphoreType.DMA] * (n_slots + 2),
        name="dispatch_rows",
    )
    def k(table_hbm, dest_hbm, out_hbm, *scratch):
        idx = scratch[:n_slots]
        rows = scratch[n_slots:n_slots + 2]
        scatter_sems = scratch[n_slots + 2:2 * n_slots + 2]
        read_sems = scratch[2 * n_slots + 2:]
        wid = lax.axis_index("s") * SC_CORES + lax.axis_index("c")
        base = wid * per_w

        def off(chunk):
            return pl.multiple_of(base + chunk * w, w)

        def read(chunk, buf):
            return pltpu.make_async_copy(table_hbm.at[pl.ds(off(chunk), w)], rows[buf], read_sems[buf])

        def scatter(kk, buf):
            return pltpu.make_async_copy(rows[buf], out_hbm.at[idx[kk]], scatter_sems[kk])

        read(0, 0).start()

        @pl.loop(0, n_chunks // 2)
        def _(p):
            for buf in range(2):
                chunk = 2 * p + buf

                @pl.when(chunk + 1 < n_chunks)
                def _():
                    read(chunk + 1, 1 - buf).start()

                read(chunk, buf).wait()
                for kk in range(n_slots):
                    src = pl.multiple_of(kk * t + off(chunk), w)
                    pltpu.sync_copy(dest_hbm.at[pl.ds(src, w)], idx[kk])
                    scatter(kk, buf).start()
                for kk in range(n_slots):
                    scatter(kk, buf).wait()

    return k(table, dest_flat)


def _gather_rows(table, idx):
    n = idx.shape[0]
    c = table.shape[1]
    per_w = n // SC_WORKERS
    assert per_w * SC_WORKERS == n and per_w % (2 * SC_WINDOW) == 0
    n_chunks = per_w // SC_WINDOW
    w = SC_WINDOW

    @functools.partial(
        pl.kernel, mesh=_sc_mesh(),
        out_type=jax.ShapeDtypeStruct((n, c), table.dtype),
        scratch_types=[pltpu.VMEM((w,), I32)] * 2 + [pltpu.VMEM((w, c), table.dtype)] * 2
        + [pltpu.SemaphoreType.DMA] * 4,
        name="gather_rows",
    )
    def k(table_hbm, idx_hbm, out_hbm, idx_a, idx_b, rows_a, rows_b, g_a, g_b, w_a, w_b):
        idx, rows, gather_sems, write_sems = (idx_a, idx_b), (rows_a, rows_b), (g_a, g_b), (w_a, w_b)
        wid = lax.axis_index("s") * SC_CORES + lax.axis_index("c")
        base = wid * per_w

        def off(chunk):
            return pl.multiple_of(base + chunk * w, w)

        def gather(buf):
            return pltpu.make_async_copy(table_hbm.at[idx[buf]], rows[buf], gather_sems[buf])

        def write(chunk, buf):
            return pltpu.make_async_copy(rows[buf], out_hbm.at[pl.ds(off(chunk), w)], write_sems[buf])

        def start_gather(chunk, buf):
            pltpu.sync_copy(idx_hbm.at[pl.ds(off(chunk), w)], idx[buf])
            gather(buf).start()

        start_gather(0, 0)

        @pl.loop(0, n_chunks // 2)
        def _(p):
            for buf in range(2):
                chunk = 2 * p + buf

                @pl.when(chunk + 1 < n_chunks)
                def _():
                    @pl.when(chunk >= 1)
                    def _():
                        write(chunk - 1, 1 - buf).wait()
                    start_gather(chunk + 1, 1 - buf)

                gather(buf).wait()
                write(chunk, buf).start()

        write(n_chunks - 2, 0).wait()
        write(n_chunks - 1, 1).wait()

    return k(table, idx)


def _expert_kernel(be_ref, nused_ref, first_ref, slot_ref, next_ref,
                   xs_ref, wg_hbm, bg_ref, wu_hbm, bu_ref, wd_hbm, bd_ref,
                   ys_ref, wg_ref, wu_ref, wd_ref, stage_g, stage_u, stage_d, sems, *, layer):
    b = pl.program_id(0)
    used = b < nused_ref[0]
    weights = ((wg_hbm, stage_g), (wu_hbm, stage_u), (wd_hbm, stage_d))

    def fetch(expert, slot):
        return [pltpu.make_async_copy(hbm.at[layer, expert], stage.at[slot], sems.at[slot, n])
                for n, (hbm, stage) in enumerate(weights)]

    @pl.when(used & (first_ref[b] == 1))
    def _():
        slot = slot_ref[b]

        @pl.when(b == 0)
        def _():
            for cp in fetch(be_ref[0], slot):
                cp.start()

        for cp in fetch(be_ref[b], slot):
            cp.wait()
        wg_ref[...] = stage_g[slot].astype(BF16)
        wu_ref[...] = stage_u[slot].astype(BF16)
        wd_ref[...] = stage_d[slot].astype(BF16)

        @pl.when(next_ref[b] >= 0)
        def _():
            for cp in fetch(next_ref[b], 1 - slot):
                cp.start()

    @pl.when(used)
    def _():
        lo, hi = _unpack_halves(xs_ref[...])
        xb = jnp.concatenate([lo.astype(BF16), hi.astype(BF16)], axis=1)
        a = _dot(xb, wg_ref[...]) + bg_ref[...]
        u = _dot(xb, wu_ref[...]) + bu_ref[...]
        a = jnp.minimum(a, SWIGLU_LIMIT)
        u = jnp.clip(u, -SWIGLU_LIMIT, SWIGLU_LIMIT)
        y = (a * jax.nn.sigmoid(SWIGLU_ALPHA * a)) * (u + 1.0)
        out = _dot(y.astype(BF16), wd_ref[...]) + bd_ref[...]
        ys_ref[...] = _pack_halves(out[:, :HALF], out[:, HALF:])

    @pl.when(b >= nused_ref[0])
    def _():
        ys_ref[...] = jnp.zeros(ys_ref.shape, I32)


def _expert_ffn(xs, block_e, n_used, counts, layer, wg, bg, wu, bu, wd, bd):
    n_rows = xs.shape[0]
    n_blocks = n_rows // ROW_BLOCK

    blk = jnp.arange(n_blocks, dtype=I32)
    used = blk < n_used[0]
    first = used & ((blk == 0) | (block_e != jnp.roll(block_e, 1)))
    slot = (jnp.cumsum(first.astype(I32)) - 1) % 2
    eid = jnp.arange(N_EXPERTS, dtype=I32)
    later = (eid[None, :] > eid[:, None]) & (counts[None, :] > 0)
    next_of_expert = jnp.min(jnp.where(later, eid[None, :], N_EXPERTS), axis=1)
    next_of_expert = jnp.where(next_of_expert == N_EXPERTS, -1, next_of_expert)
    next_e = next_of_expert[block_e]

    def rows(b, *_):
        return (b, 0)

    def expert(b, be, *_):
        return (layer, be[b], 0, 0)

    hbm = pl.BlockSpec(memory_space=pl.ANY)
    grid_spec = pltpu.PrefetchScalarGridSpec(
        num_scalar_prefetch=5,
        grid=(n_blocks,),
        in_specs=[pl.BlockSpec((ROW_BLOCK, HALF), rows),
                  hbm, pl.BlockSpec((None, None, 1, D_EXPERT), expert),
                  hbm, pl.BlockSpec((None, None, 1, D_EXPERT), expert),
                  hbm, pl.BlockSpec((None, None, 1, D_MODEL), expert)],
        out_specs=pl.BlockSpec((ROW_BLOCK, HALF), rows),
        scratch_shapes=[pltpu.VMEM((D_MODEL, D_EXPERT), BF16), pltpu.VMEM((D_MODEL, D_EXPERT), BF16),
                        pltpu.VMEM((D_EXPERT, D_MODEL), BF16),
                        pltpu.VMEM((2, D_MODEL, D_EXPERT), F32), pltpu.VMEM((2, D_MODEL, D_EXPERT), F32),
                        pltpu.VMEM((2, D_EXPERT, D_MODEL), F32),
                        pltpu.SemaphoreType.DMA((2, 3))],
    )
    return pl.pallas_call(
        functools.partial(_expert_kernel, layer=layer),
        out_shape=jax.ShapeDtypeStruct((n_rows, HALF), I32),
        grid_spec=grid_spec,
        compiler_params=_cparams(("arbitrary",)),
        name="expert_ffn",
    )(block_e, n_used, first.astype(I32), slot.astype(I32), next_e.astype(I32),
      xs, wg, bg, wu, bu, wd, bd)


def _combine_kernel(x1_ref, yg_ref, wcol_ref, p_ref, gple_ref, wpg_ref, wpp_ref, gout_ref, o_ref,
                    *, final):
    x1 = x1_ref[...]
    acc_lo = x1[:, :HALF]
    acc_hi = x1[:, HALF:]
    wcol = wcol_ref[...]
    for kk in range(TOP_K):
        lo, hi = _unpack_halves(yg_ref[kk])
        wk = wcol[:, kk:kk + 1]
        acc_lo = acc_lo + wk * lo
        acc_hi = acc_hi + wk * hi
    x2 = jnp.concatenate([acc_lo, acc_hi], axis=1)
    gate = jax.nn.sigmoid(_dot(_rms(x2, gple_ref[...]).astype(BF16), wpg_ref[...]))
    x3 = x2 + gate * _dot(p_ref[...].astype(BF16), wpp_ref[...])
    o_ref[...] = _rms(x3, gout_ref[...]) if final else x3


def _combine_kernel_inplace(x1_ref, yg_ref, wcol_ref, p_ref, gple_ref, wpg_ref, wpp_ref, gout_ref,
                            prev_ref, o_ref, *, final):
    del prev_ref
    _combine_kernel(x1_ref, yg_ref, wcol_ref, p_ref, gple_ref, wpg_ref, wpp_ref, gout_ref, o_ref,
                    final=final)


def _combine(x1, yg, wcol, p3d, layer, gple, wpg, wpp, gout, final, part, prev):
    t = x1.shape[0]
    tm = TOKEN_TILE
    n_tiles = yg.shape[1] // tm
    first_tile = part * n_tiles

    def row(i, *_):
        return (first_tile + i, 0)

    def full(a):
        return pl.BlockSpec(a.shape, lambda i: (0, 0))

    in_specs = [pl.BlockSpec((tm, D_MODEL), row),
                pl.BlockSpec((TOP_K, tm, HALF), lambda i: (0, i, 0)),
                pl.BlockSpec((tm, HEAD_PAD), row),
                pl.BlockSpec((None, tm, PLE_DIM), lambda i: (layer, first_tile + i, 0)),
                full(gple), full(wpg), full(wpp), full(gout)]
    args = [x1, yg, wcol, p3d, gple, wpg, wpp, gout]
    kern = functools.partial(_combine_kernel, final=final)
    aliases = {}
    if prev is not None:
        in_specs.append(pl.BlockSpec(memory_space=pl.ANY))
        args.append(prev)
        aliases = {len(args) - 1: 0}
        kern = functools.partial(_combine_kernel_inplace, final=final)
    return pl.pallas_call(
        kern,
        out_shape=jax.ShapeDtypeStruct((t, D_MODEL), F32),
        grid=(n_tiles,),
        in_specs=in_specs,
        out_specs=pl.BlockSpec((tm, D_MODEL), row),
        input_output_aliases=aliases,
        compiler_params=_cparams(("parallel",)),
        name="combine_ple",
    )(*args)


def _rope_tables(seq):
    inv_freq = ROPE_THETA ** (-jnp.arange(HALF_ROPE, dtype=F32) * 2.0 / MLA_ROPE)
    ang = jnp.arange(seq, dtype=F32)[:, None] * inv_freq[None, :]
    cos, sin = jnp.cos(ang), jnp.sin(ang)
    ones = jnp.ones((seq, MLA_NOPE), F32)
    zeros16 = jnp.zeros((seq, HALF_ROPE), F32)
    zeros64 = jnp.zeros((seq, MLA_NOPE), F32)
    tail = jnp.ones((seq, HEAD_PAD - MLA_NOPE - MLA_ROPE), F32)
    ztail = jnp.zeros_like(tail)
    cos_t = jnp.concatenate([ones, cos, cos, tail], axis=1)
    sina_t = jnp.concatenate([zeros64, zeros16, sin, ztail], axis=1)
    sinb_t = jnp.concatenate([zeros64, -sin, zeros16, ztail], axis=1)
    return cos_t, sina_t, sinb_t


def _prep_mixer_weights(w_in, w_uq, w_ukv):
    c0 = MLA_Q_LORA + MLA_KV_LORA
    c1 = c0 + MLA_ROPE
    c2 = c1 + len(DIL_GROUPS) * DIL_COLS
    kr_pad = jnp.pad(w_in[:, c0:c1], ((0, 0), (MLA_NOPE, HEAD_PAD - MLA_NOPE - MLA_ROPE)))
    wmla = jnp.concatenate([w_in[:, :c0], kr_pad], axis=1).astype(BF16)
    col = np.arange(len(DIL_GROUPS) * DIL_COLS)
    q_scale = np.where(col % DIL_COLS < DIL_OUT, DIL_HEAD_DIM ** -0.5, 1.0).astype(np.float32)
    wdil = (w_in[:, c1:c2] * q_scale[None, :]).astype(BF16)
    wgate = w_in[:, c2:].astype(BF16)
    pad = HEAD_PAD - MLA_NOPE - MLA_ROPE
    wuq_h = w_uq.reshape(MLA_Q_LORA, MLA_HEADS, MLA_NOPE + MLA_ROPE)
    wuq = jnp.pad(wuq_h, ((0, 0), (0, 0), (0, pad))).reshape(MLA_Q_LORA, MLA_HEADS * HEAD_PAD).astype(BF16)
    wukv_h = w_ukv.reshape(MLA_KV_LORA, MLA_HEADS, MLA_NOPE + MLA_V)
    wuk = jnp.pad(wukv_h[:, :, :MLA_NOPE], ((0, 0), (0, 0), (0, HEAD_PAD - MLA_NOPE)))
    wuk = wuk.reshape(MLA_KV_LORA, MLA_HEADS * HEAD_PAD).astype(BF16)
    wuv = wukv_h[:, :, MLA_NOPE:].reshape(MLA_KV_LORA, MLA_HEADS * MLA_V).astype(BF16)
    return wmla, wdil, wgate, wuq, wuk, wuv


def kernel(x, p, attn_norm, w_in, q_norm, w_uq, kv_norm, w_ukv, w_branch_a, w_branch_b, w_out, ffn_norm, w_router, b_router, w_gate, b_gate, w_up, b_up, w_down, b_down, ple_norm, w_ple_gate, w_ple_proj, final_norm):
    b, s, d = x.shape
    depth = w_in.shape[0]
    t = b * s
    assert d == D_MODEL and s % (DIL_GROUPS[-1][0]) == 0 and t % (SC_WORKERS * SC_WINDOW) == 0
    n_assign = t * TOP_K
    n_blocks = -(-(n_assign + N_EXPERTS * (ROW_BLOCK - 1)) // ROW_BLOCK)
    n_rows = n_blocks * ROW_BLOCK
    cos_t, sina_t, sinb_t = _rope_tables(s)
    xc = x.reshape(t, d)
    for i in range(depth):
        wmla, wdil, wgate, wuq, wuk, wuv = _prep_mixer_weights(w_in[i], w_uq[i], w_ukv[i])
        q, k, vt, zd0, zd1, zd2 = _inproj(
            xc, s, attn_norm[i][None], wmla, wdil, q_norm[i][None], kv_norm[i][None],
            wuq, wuk, wuv, cos_t, sina_t, sinb_t)
        oa = _mla_attention(q.reshape(b, s, -1), k.reshape(b, s, -1), vt)
        ob = _dilated_attention(zd0, zd1, zd2, s)
        x1, hp, topi_t, topw_t, wcol = _merge(
            xc, oa.reshape(t, -1), ob, attn_norm[i][None], wgate,
            w_branch_a[i].astype(BF16), w_branch_b[i].astype(BF16), w_out[i].astype(BF16),
            ffn_norm[i][None], w_router[i].T.astype(BF16), b_router[i][:, None])
        dest_t, meta = _positions(topi_t)
        ends = meta[2, :N_EXPERTS]
        block_start = jnp.arange(n_blocks, dtype=I32) * ROW_BLOCK
        block_e = jnp.minimum(
            jnp.sum((ends[None, :] <= block_start[:, None]).astype(I32), axis=1), N_EXPERTS - 1)
        n_used = (ends[N_EXPERTS - 1:] // ROW_BLOCK).astype(I32)
        dest_flat = dest_t[:TOP_K].reshape(n_assign)
        xs = _dispatch_rows(hp, dest_flat, n_rows)
        ys = _expert_ffn(xs, block_e, n_used, meta[0, :N_EXPERTS], i,
                         w_gate, b_gate[:, :, None, :], w_up, b_up[:, :, None, :],
                         w_down, b_down[:, :, None, :])
        final = i == depth - 1
        gout = final_norm[None] if final else attn_norm[i][None]
        wpg, wpp = w_ple_gate[i].astype(BF16), w_ple_proj[i].astype(BF16)
        tp = t // COMBINE_PARTS
        xc = None
        for part in range(COMBINE_PARTS):
            dest_part = dest_t[:TOP_K, part * tp:(part + 1) * tp].reshape(TOP_K * tp)
            yg = _gather_rows(ys, dest_part).reshape(TOP_K, tp, HALF)
            xc = _combine(x1, yg, wcol, p.reshape(depth, t, PLE_DIM), i, ple_norm[i][None],
                          wpg, wpp, gout, final, part, xc)
    return xc.reshape(b, s, d)
```

```python
import functools
import math

import jax
import jax.numpy as jnp
import numpy as np
from jax import lax
from jax.experimental import pallas as pl
from jax.experimental.pallas import tpu as pltpu
from jax.experimental.pallas import tpu_sc as plsc

F32 = jnp.float32
BF16 = jnp.bfloat16
I32 = jnp.int32

D_MODEL = 1024
PLE_DIM = 256
NORM_EPS = 1e-6

MLA_HEADS = 8
MLA_Q_LORA = 384
MLA_KV_LORA = 256
MLA_NOPE = 64
MLA_ROPE = 32
MLA_V = 64
ROPE_THETA = 10000.0
HEAD_PAD = 128
HALF_ROPE = MLA_ROPE // 2

DIL_GROUPS = ((128, 1), (512, 4), (2048, 16))
DIL_HEADS = 4
DIL_HEAD_DIM = 64
DIL_STEPS = 128
DIL_COLS = 3 * DIL_HEADS * DIL_HEAD_DIM
DIL_OUT = DIL_HEADS * DIL_HEAD_DIM
DIL_UNROLL = 8

N_EXPERTS = 32
TOP_K = 4
D_EXPERT = 1024
SWIGLU_LIMIT = 7.0
SWIGLU_ALPHA = 1.702
ROW_BLOCK = 512

TOKEN_TILE = 512
POSITION_TILE = 4096
POSITION_SUBTILE = 512
COMBINE_PARTS = 2
ATTN_TQ = 1024
ATTN_TK = 1024
ATTN_HEADS = 8
ATTN_SUB = 1024
ATTN_KEY_CHUNKS = 2
HALF = D_MODEL // 2
NEG = -1e30
SPECULATION_HEADROOM = 60.0
HI_MASK = -65536

SC_CORES = 2
SC_SUBCORES = 16
SC_WORKERS = SC_CORES * SC_SUBCORES
SC_WINDOW = 64

VMEM_LIMIT = 56 * 1024 * 1024


def _cparams(sem):
    return pltpu.CompilerParams(dimension_semantics=sem, vmem_limit_bytes=VMEM_LIMIT)


def _rms(x, g):
    return x * lax.rsqrt(jnp.mean(x * x, axis=-1, keepdims=True) + NORM_EPS) * g


def _dot(a, b):
    return jnp.dot(a, b, preferred_element_type=F32)


def _dot_nt(a, b):
    return lax.dot_general(a, b, (((1,), (1,)), ((), ())), preferred_element_type=F32)


def _pack_halves(lo, hi):
    lo_i = lax.bitcast_convert_type(lo.astype(BF16).astype(F32), I32)
    hi_i = lax.bitcast_convert_type(hi.astype(BF16).astype(F32), I32)
    return (hi_i & HI_MASK) | lax.shift_right_logical(lo_i, 16)


def _unpack_halves(w):
    lo = lax.bitcast_convert_type(lax.shift_left(w, 16), F32)
    hi = lax.bitcast_convert_type(w & HI_MASK, F32)
    return lo, hi


def _inproj_kernel(x_ref, g_ref, wmla_ref, wdil_ref, qn_ref, kvn_ref, wuq_ref, wuk_ref,
                   wuv_ref, cos_ref, sina_ref, sinb_ref,
                   q_ref, k_ref, vt_ref, zd0_ref, zd1_ref, zd2_ref, zs_sc):
    h = _rms(x_ref[...], g_ref[...]).astype(BF16)
    zm = _dot(h, wmla_ref[...])
    cq = _rms(zm[:, :MLA_Q_LORA], qn_ref[...]).astype(BF16)
    ckv = _rms(zm[:, MLA_Q_LORA:MLA_Q_LORA + MLA_KV_LORA], kvn_ref[...]).astype(BF16)
    kr = zm[:, MLA_Q_LORA + MLA_KV_LORA:]
    cos, sina, sinb = cos_ref[...], sina_ref[...], sinb_ref[...]

    def rope(t):
        return (t * cos + pltpu.roll(t, HALF_ROPE, 1) * sina
                + pltpu.roll(t, HEAD_PAD - HALF_ROPE, 1) * sinb)

    kr_rot = rope(kr)
    qraw = _dot(cq, wuq_ref[...])
    kraw = _dot(ckv, wuk_ref[...])
    vt_ref[...] = _dot(ckv, wuv_ref[...]).T.astype(BF16)
    scale = (MLA_NOPE + MLA_ROPE) ** -0.5 * math.log2(math.e)
    for hd in range(MLA_HEADS):
        sl = slice(hd * HEAD_PAD, (hd + 1) * HEAD_PAD)
        q_ref[:, sl] = (rope(qraw[:, sl]) * scale).astype(BF16)
        k_ref[:, sl] = (kraw[:, sl] + kr_rot).astype(BF16)
    tm = x_ref.shape[0]
    for gi, zd_ref in enumerate((zd0_ref, zd1_ref, zd2_ref)):
        z = _dot(h, wdil_ref[:, gi * DIL_COLS:(gi + 1) * DIL_COLS])
        window, dil = DIL_GROUPS[gi]
        if dil == 1:
            zd_ref[...] = z.astype(BF16)
            continue
        n_col = DIL_COLS // HEAD_PAD
        for c in range(n_col):
            zs_sc[c] = z[:, c * HEAD_PAD:(c + 1) * HEAD_PAD]
        rows = tm // dil
        part = pl.program_id(0) % (window // tm)
        for r in range(dil):
            dst = pl.ds(pl.multiple_of(r * DIL_STEPS + part * rows, rows), rows)
            for c in range(n_col):
                chunk = zs_sc[c, pl.ds(r, rows, stride=dil), :]
                zd_ref[dst, c * HEAD_PAD:(c + 1) * HEAD_PAD] = chunk.astype(BF16)


def _inproj(x2d, seq, g, wmla, wdil, qn, kvn, wuq, wuk, wuv, cos_t, sina_t, sinb_t):
    t = x2d.shape[0]
    tm = TOKEN_TILE
    n_seq_tiles = seq // tm

    def row(i):
        return (i, 0)

    def const(i):
        return (0, 0)

    def pos(i):
        return (i % n_seq_tiles, 0)

    def full(a):
        return pl.BlockSpec(a.shape, const)

    def vt_block(i):
        return (i // n_seq_tiles, 0, i % n_seq_tiles)

    def rows(width):
        return pl.BlockSpec((tm, width), row)

    def unit_rows(window):
        return pl.BlockSpec((window, DIL_COLS), lambda i: (i // (window // tm), 0))

    out_shape = [
        jax.ShapeDtypeStruct((t, MLA_HEADS * HEAD_PAD), BF16),
        jax.ShapeDtypeStruct((t, MLA_HEADS * HEAD_PAD), BF16),
        jax.ShapeDtypeStruct((t // seq, MLA_HEADS * MLA_V, seq), BF16),
        jax.ShapeDtypeStruct((t, DIL_COLS), BF16),
        jax.ShapeDtypeStruct((t, DIL_COLS), BF16),
        jax.ShapeDtypeStruct((t, DIL_COLS), BF16),
    ]
    return pl.pallas_call(
        _inproj_kernel,
        out_shape=out_shape,
        grid=(t // tm,),
        in_specs=[pl.BlockSpec((tm, D_MODEL), row), full(g), full(wmla), full(wdil),
                  full(qn), full(kvn), full(wuq), full(wuk), full(wuv),
                  pl.BlockSpec((tm, HEAD_PAD), pos), pl.BlockSpec((tm, HEAD_PAD), pos),
                  pl.BlockSpec((tm, HEAD_PAD), pos)],
        out_specs=[rows(D_MODEL), rows(D_MODEL),
                   pl.BlockSpec((None, MLA_HEADS * MLA_V, tm), vt_block),
                   rows(DIL_COLS)] + [unit_rows(window) for window, _ in DIL_GROUPS[1:]],
        scratch_shapes=[pltpu.VMEM((DIL_COLS // HEAD_PAD, tm, HEAD_PAD), F32)],
        compiler_params=_cparams(("arbitrary",)),
        name="inproj",
    )(x2d, g, wmla, wdil, qn, kvn, wuq, wuk, wuv, cos_t, sina_t, sinb_t)


def _mla_kernel(qi_ref, kj_ref, q_ref, k_ref, vt_ref, o_ref, m_sc, l_sc, acc_sc, redo_sc):
    p = pl.program_id(2)
    i = qi_ref[p]
    j = kj_ref[p]
    tq = q_ref.shape[0]
    tk = k_ref.shape[0]

    @pl.when(j == 0)
    def _():
        m_sc[...] = jnp.full(m_sc.shape, NEG, F32)
        l_sc[...] = jnp.zeros(l_sc.shape, F32)
        acc_sc[...] = jnp.zeros(acc_sc.shape, F32)

    ratio = tq // tk
    sub = ATTN_SUB

    def step(diagonal, speculative, first=False):
        chains = [(hh, c) for hh in range(ATTN_HEADS) for c in range(tq // sub)]
        ones_rows = (lax.broadcasted_iota(I32, (16, tk), 0) == 0).astype(BF16)
        state = {}
        for hh, c in chains:
            cs = slice(c * sub, (c + 1) * sub)
            if first and speculative:
                sl = slice(hh * HEAD_PAD, (hh + 1) * HEAD_PAD)
                s0 = _dot_nt(k_ref[:8, sl], q_ref[cs, sl])[:1]
                state[hh, c] = (s0, jnp.zeros((1, sub), F32), jnp.zeros((MLA_V, sub), F32))
            else:
                state[hh, c] = (m_sc[hh, :, cs], l_sc[hh, :, cs], acc_sc[hh, :, cs])
        new_state = {}
        within = None
        def scores(hh, c, rows, q0=0):
            sl = slice(hh * HEAD_PAD, (hh + 1) * HEAD_PAD)
            st = _dot_nt(k_ref[rows, sl], q_ref[c * sub + q0:(c + 1) * sub, sl])
            if diagonal:
                shape = (rows.stop - rows.start, sub - q0)
                key = lax.broadcasted_iota(I32, shape, 0) + (j * tk + rows.start)
                qry = lax.broadcasted_iota(I32, shape, 1) + (i * tq + c * sub + q0)
                st = jnp.where(qry >= key, st, NEG)
            return st

        def values(hh, rows):
            ones_row = (lax.broadcasted_iota(I32, (16, rows.stop - rows.start), 0) == 0).astype(BF16)
            return jnp.concatenate([vt_ref[hh * MLA_V:(hh + 1) * MLA_V, rows], ones_row], axis=0)

        for hh, c in chains:
            m_prev, l_prev, acc_prev = state[hh, c]
            if speculative:
                kc = tk // ATTN_KEY_CHUNKS
                m_blk, pv = None, None
                for n in range(ATTN_KEY_CHUNKS):
                    rows = slice(n * kc, (n + 1) * kc)
                    q0 = n * kc if (diagonal and tq == tk and sub == tq) else 0
                    st = scores(hh, c, rows, q0)
                    m_part = jnp.max(st, axis=0, keepdims=True)
                    part = _dot(values(hh, rows), jnp.exp2(st - m_prev[:, q0:]).astype(BF16))
                    if q0:
                        m_part = jnp.concatenate([jnp.full((1, q0), NEG, F32), m_part], axis=1)
                        part = jnp.concatenate([jnp.zeros((part.shape[0], q0), F32), part], axis=1)
                    m_blk = m_part if m_blk is None else jnp.maximum(m_blk, m_part)
                    pv = part if pv is None else pv + part
                m_new = jnp.maximum(m_prev, m_blk)
                alpha = jnp.exp2(m_prev - m_new)
                l_new = alpha * (l_prev + pv[MLA_V:MLA_V + 1])
                acc_new = alpha * (acc_prev + pv[:MLA_V])
                ok = jnp.max(m_blk - m_prev) <= SPECULATION_HEADROOM
                within = ok if within is None else (within & ok)
            else:
                rows = slice(0, tk)
                st = scores(hh, c, rows)
                m_new = jnp.maximum(m_prev, jnp.max(st, axis=0, keepdims=True))
                alpha = jnp.exp2(m_prev - m_new)
                pv = _dot(values(hh, rows), jnp.exp2(st - m_new).astype(BF16))
                l_new = alpha * l_prev + pv[MLA_V:MLA_V + 1]
                acc_new = alpha * acc_prev + pv[:MLA_V]
            new_state[hh, c] = (m_new, l_new, acc_new)

        def commit():
            for hh, c in chains:
                cs = slice(c * sub, (c + 1) * sub)
                m_sc[hh, :, cs], l_sc[hh, :, cs], acc_sc[hh, :, cs] = new_state[hh, c]

        if speculative:
            pl.when(within)(commit)
            redo_sc[0] = jnp.logical_not(within).astype(I32)
        else:
            commit()

    redo_sc[0] = 0
    on_diagonal = j >= ratio * i

    @pl.when((j == 0) & on_diagonal)
    def _():
        step(True, True, first=True)

    @pl.when((j == 0) & jnp.logical_not(on_diagonal))
    def _():
        step(False, True, first=True)

    @pl.when((j > 0) & jnp.logical_not(on_diagonal))
    def _():
        step(False, True)

    @pl.when((j > 0) & on_diagonal)
    def _():
        step(True, True)

    @pl.when(redo_sc[0] != 0)
    def _():
        step(True, False)

    @pl.when(j == ratio * i + (ratio - 1))
    def _():
        ot = jnp.concatenate([acc_sc[hh] / l_sc[hh] for hh in range(ATTN_HEADS)], axis=0)
        o_ref[...] = ot.T.astype(BF16)


def _mla_attention(q, k, vt):
    b, s, _ = q.shape
    tq, tk, nh = ATTN_TQ, ATTN_TK, ATTN_HEADS
    ratio = tq // tk
    nq = s // tq
    pairs = [(i, j) for i in range(nq) for j in range(ratio * (i + 1))]
    qi = jnp.asarray([p[0] for p in pairs], I32)
    kj = jnp.asarray([p[1] for p in pairs], I32)
    grid_spec = pltpu.PrefetchScalarGridSpec(
        num_scalar_prefetch=2,
        grid=(b, MLA_HEADS // nh, len(pairs)),
        in_specs=[
            pl.BlockSpec((None, tq, nh * HEAD_PAD), lambda bb, hp, p, qi, kj: (bb, qi[p], hp)),
            pl.BlockSpec((None, tk, nh * HEAD_PAD), lambda bb, hp, p, qi, kj: (bb, kj[p], hp)),
            pl.BlockSpec((None, nh * MLA_V, tk), lambda bb, hp, p, qi, kj: (bb, hp, kj[p])),
        ],
        out_specs=pl.BlockSpec((None, tq, nh * MLA_V), lambda bb, hp, p, qi, kj: (bb, qi[p], hp)),
        scratch_shapes=[pltpu.VMEM((nh, 1, tq), F32), pltpu.VMEM((nh, 1, tq), F32),
                        pltpu.VMEM((nh, MLA_V, tq), F32), pltpu.SMEM((1,), I32)],
    )
    return pl.pallas_call(
        _mla_kernel,
        out_shape=jax.ShapeDtypeStruct((b, s, MLA_HEADS * MLA_V), BF16),
        grid_spec=grid_spec,
        compiler_params=_cparams(("parallel", "parallel", "arbitrary")),
        name="mla_attention",
    )(qi, kj, q, k, vt)


def _alibi_slopes(n):
    def pow2(m):
        start = 2.0 ** (-8.0 / m)
        return [start ** (i + 1) for i in range(m)]
    if math.log2(n).is_integer():
        s = pow2(n)
    else:
        c = 2 ** int(math.floor(math.log2(n)))
        s = pow2(c) + pow2(2 * c)[0::2][: n - c]
    return np.array(sorted(s, reverse=True), dtype=np.float32)


def _dilated_block(cur, prev, bias4, first):
    n = DIL_STEPS
    hw = DIL_OUT
    q = cur[:, :hw]
    kk = jnp.concatenate([prev[:, hw:2 * hw], cur[:, hw:2 * hw]], axis=0)
    vv = jnp.concatenate([prev[:, 2 * hw:], cur[:, 2 * hw:]], axis=0)
    head_of_lane = lax.broadcasted_iota(I32, (n, hw), 1) // DIL_HEAD_DIM
    zero = jnp.zeros_like(q)
    q4 = jnp.concatenate([jnp.where(head_of_lane == h, q, zero) for h in range(DIL_HEADS)], axis=0)
    s4 = _dot_nt(q4, kk) + bias4
    if first is not None:
        ki = lax.broadcasted_iota(I32, (DIL_HEADS * n, 2 * n), 1)
        s4 = jnp.where(first & (ki < n), NEG, s4)
    m4 = jnp.max(s4, axis=1, keepdims=True)
    p4 = jnp.exp(s4 - m4).astype(BF16)
    l4 = _dot(p4, jnp.ones((2 * n, HEAD_PAD), BF16))
    pv4 = _dot(p4, vv)
    m4 = jnp.broadcast_to(m4, (DIL_HEADS * n, HEAD_PAD))

    def rows(a, h):
        return a[h * n:(h + 1) * n]

    o_un = rows(pv4, DIL_HEADS - 1)
    for h in range(DIL_HEADS - 2, -1, -1):
        o_un = jnp.where(head_of_lane == h, rows(pv4, h), o_un)
    low = lax.broadcasted_iota(I32, (n, HEAD_PAD), 1) < DIL_HEAD_DIM

    def per_lane(a):
        return jnp.concatenate([jnp.where(low, rows(a, 0), rows(a, 1)),
                                jnp.where(low, rows(a, 2), rows(a, 3))], axis=1)

    l_sel = per_lane(l4)
    return o_un / l_sel, per_lane(m4) + jnp.log(l_sel)


def _dilated_kernel(c0_ref, h0_ref, c1_ref, h1_ref, c2_ref, h2_ref, ob_ref, o_sc, l_sc, bias_sc,
                    *, slopes):
    u = pl.program_id(1)
    n = DIL_STEPS
    unit = ob_ref.shape[0]
    n_sb = unit // n
    first = u == 0
    qi = lax.broadcasted_iota(I32, (n, 2 * n), 0)
    ki = lax.broadcasted_iota(I32, (n, 2 * n), 1)
    dist = qi + n - ki
    valid = (dist >= 0) & (dist <= n)
    distf = dist.astype(F32)
    for gi in range(len(DIL_GROUPS)):
        for h in range(DIL_HEADS):
            bias_sc[gi, h * n:(h + 1) * n, :] = jnp.where(valid, -slopes[gi][h] * distf, NEG)

    def rows_of(ref, sb):
        return ref[pl.ds(pl.multiple_of(sb * n, n), n), :]

    def emit(gi, start, stride, o, lse):
        if isinstance(start, int):
            idx = pl.ds(start, n)
        elif stride == 1:
            idx = pl.ds(pl.multiple_of(start, n), n)
        else:
            idx = pl.ds(start, n, stride=stride)
        for half in range(DIL_OUT // HEAD_PAD):
            ls = slice(half * HEAD_PAD, (half + 1) * HEAD_PAD)
            o_sc[gi, half, idx, :] = o[:, ls]
            l_sc[gi, half, idx, :] = lse[:, ls]

    emit(0, 0, 1, *_dilated_block(c0_ref[:n, :], h0_ref[...], bias_sc[0], first))

    def g0_body(sb, carry):
        emit(0, sb * n, 1, *_dilated_block(rows_of(c0_ref, sb), rows_of(c0_ref, sb - 1), bias_sc[0], None))
        return carry

    lax.fori_loop(1, n_sb, g0_body, 0, unroll=DIL_UNROLL)

    d1 = DIL_GROUPS[1][1]

    def g1_head(r, carry):
        emit(1, r, d1, *_dilated_block(rows_of(c1_ref, r), rows_of(h1_ref, r), bias_sc[1], first))
        return carry

    def g1_body(sb, carry):
        start = (sb // d1) * (d1 * n) + sb % d1
        emit(1, start, d1, *_dilated_block(rows_of(c1_ref, sb), rows_of(c1_ref, sb - d1), bias_sc[1], None))
        return carry

    lax.fori_loop(0, d1, g1_head, 0, unroll=DIL_UNROLL)
    lax.fori_loop(d1, n_sb, g1_body, 0, unroll=DIL_UNROLL)

    d2 = DIL_GROUPS[2][1]

    def g2_body(r, carry):
        emit(2, r, d2, *_dilated_block(rows_of(c2_ref, r), rows_of(h2_ref, r), bias_sc[2], first))
        return carry

    lax.fori_loop(0, n_sb, g2_body, 0, unroll=DIL_UNROLL)

    def merge_body(c, carry):
        idx = pl.ds(pl.multiple_of(c * n, n), n)
        for half in range(DIL_OUT // HEAD_PAD):
            l0, l1, l2 = l_sc[0, half, idx, :], l_sc[1, half, idx, :], l_sc[2, half, idx, :]
            lmax = jnp.maximum(jnp.maximum(l0, l1), l2)
            e0, e1, e2 = jnp.exp(l0 - lmax), jnp.exp(l1 - lmax), jnp.exp(l2 - lmax)
            ob = (e0 * o_sc[0, half, idx, :] + e1 * o_sc[1, half, idx, :]
                  + e2 * o_sc[2, half, idx, :]) / (e0 + e1 + e2)
            ob_ref[idx, half * HEAD_PAD:(half + 1) * HEAD_PAD] = ob.astype(BF16)
        return carry

    lax.fori_loop(0, n_sb, merge_body, 0)


def _dilated_attention(zd0, zd1, zd2, seq):
    t = zd0.shape[0]
    unit = DIL_GROUPS[-1][0]
    upb = seq // unit
    n = DIL_STEPS
    u1 = DIL_GROUPS[1][0]
    all_slopes = _alibi_slopes(len(DIL_GROUPS) * DIL_HEADS).reshape(len(DIL_GROUPS), DIL_HEADS)
    slopes = tuple(tuple(float(x) * dil for x in all_slopes[gi]) for gi, (_, dil) in enumerate(DIL_GROUPS))

    def cur(bb, u):
        return (bb * upb + u, 0)

    def halo(rows):
        per_unit = unit // rows
        return lambda bb, u: ((bb * upb) * per_unit + jnp.maximum(u * per_unit - 1, 0), 0)

    return pl.pallas_call(
        functools.partial(_dilated_kernel, slopes=slopes),
        out_shape=jax.ShapeDtypeStruct((t, DIL_OUT), BF16),
        grid=(t // seq, upb),
        in_specs=[pl.BlockSpec((unit, DIL_COLS), cur), pl.BlockSpec((n, DIL_COLS), halo(n)),
                  pl.BlockSpec((unit, DIL_COLS), cur), pl.BlockSpec((u1, DIL_COLS), halo(u1)),
                  pl.BlockSpec((unit, DIL_COLS), cur), pl.BlockSpec((unit, DIL_COLS), halo(unit))],
        out_specs=pl.BlockSpec((unit, DIL_OUT), cur),
        scratch_shapes=[pltpu.VMEM((len(DIL_GROUPS), DIL_OUT // HEAD_PAD, unit, HEAD_PAD), F32),
                        pltpu.VMEM((len(DIL_GROUPS), DIL_OUT // HEAD_PAD, unit, HEAD_PAD), F32),
                        pltpu.VMEM((len(DIL_GROUPS), DIL_HEADS * n, 2 * n), F32)],
        compiler_params=_cparams(("parallel", "arbitrary")),
        name="dilated_attention",
    )(zd0, zd0, zd1, zd1, zd2, zd2)


def _merge_kernel(x_ref, oa_ref, ob_ref, ga_ref, wgate_ref,
                  wa_ref, wb_ref, wo_ref, g_ref, wr_ref, br_ref,
                  x1_ref, hp_ref, topi_ref, topw_ref, wcol_ref):
    tm = x_ref.shape[0]
    x = x_ref[...]
    h = _rms(x, ga_ref[...]).astype(BF16)
    mixed = (jax.nn.sigmoid(_dot(h, wgate_ref[:, :D_MODEL])) * _dot(oa_ref[...], wa_ref[...])
             + jax.nn.sigmoid(_dot(h, wgate_ref[:, D_MODEL:])) * _dot(ob_ref[...], wb_ref[...]))
    x1 = x + _dot(mixed.astype(BF16), wo_ref[...])
    x1_ref[...] = x1
    h2 = _rms(x1, g_ref[...])
    hp_ref[...] = _pack_halves(h2[:, :HALF], h2[:, HALF:])

    logits = _dot_nt(wr_ref[...], h2.astype(BF16)) + br_ref[...]
    eidx = lax.broadcasted_iota(I32, (N_EXPERTS, tm), 0)
    vals, idxs = [], []
    for _ in range(TOP_K):
        m = jnp.max(logits, axis=0, keepdims=True)
        idx = jnp.min(jnp.where(logits == m, eidx, N_EXPERTS), axis=0, keepdims=True)
        vals.append(m)
        idxs.append(idx)
        logits = jnp.where(eidx == idx, -jnp.inf, logits)
    exps = [jnp.exp(vk - vals[0]) for vk in vals]
    den = exps[0] + exps[1] + exps[2] + exps[3]
    row8 = lax.broadcasted_iota(I32, (8, tm), 0)
    row128 = lax.broadcasted_iota(I32, (HEAD_PAD, tm), 0)
    topi = jnp.zeros((8, tm), I32)
    topw = jnp.zeros((8, tm), F32)
    wide = jnp.zeros((HEAD_PAD, tm), F32)
    for kk in range(TOP_K):
        wk = exps[kk] / den
        topi = jnp.where(row8 == kk, idxs[kk], topi)
        topw = jnp.where(row8 == kk, wk, topw)
        wide = jnp.where(row128 == kk, wk, wide)
    topi_ref[...] = topi
    topw_ref[...] = topw
    wcol_ref[...] = wide.T


def _merge(x2d, oa, ob, g_attn, wgate, wa, wb, wo, g, wr_t, br_col):
    t = x2d.shape[0]
    tm = TOKEN_TILE

    def row(i):
        return (i, 0)

    def col(i):
        return (0, i)

    def full(a):
        return pl.BlockSpec(a.shape, lambda i: (0, 0))

    def rows(width):
        return pl.BlockSpec((tm, width), row)

    out_shape = [
        jax.ShapeDtypeStruct((t, D_MODEL), F32),
        jax.ShapeDtypeStruct((t, HALF), I32),
        jax.ShapeDtypeStruct((8, t), I32),
        jax.ShapeDtypeStruct((8, t), F32),
        jax.ShapeDtypeStruct((t, HEAD_PAD), F32),
    ]
    return pl.pallas_call(
        _merge_kernel,
        out_shape=out_shape,
        grid=(t // tm,),
        in_specs=[rows(D_MODEL), rows(MLA_HEADS * MLA_V), rows(DIL_OUT), full(g_attn), full(wgate)]
        + [full(wa), full(wb), full(wo), full(g), full(wr_t), full(br_col)],
        out_specs=[rows(D_MODEL), rows(HALF), pl.BlockSpec((8, tm), col), pl.BlockSpec((8, tm), col),
                   rows(HEAD_PAD)],
        compiler_params=_cparams(("parallel",)),
        name="merge_router",
    )(x2d, oa, ob, g_attn, wgate, wa, wb, wo, g, wr_t, br_col)


def _positions_kernel(topi_ref, dest_ref, meta_ref, cnt_sc, carry_sc, start_sc):
    ps = pl.program_id(0)
    i = pl.program_id(1)
    tm = POSITION_SUBTILE
    n_sub = topi_ref.shape[1] // tm
    eidx = lax.broadcasted_iota(I32, (N_EXPERTS, tm), 0)

    def hits_of(sb):
        topi = topi_ref[:, sb * tm:(sb + 1) * tm]
        return [eidx == topi[kk:kk + 1, :] for kk in range(TOP_K)]

    def members(hits):
        return hits[0] | hits[1] | hits[2] | hits[3]

    @pl.when((ps == 0) & (i == 0))
    def _():
        cnt_sc[...] = jnp.zeros(cnt_sc.shape, F32)

    @pl.when(ps == 0)
    def _():
        total = cnt_sc[...]
        for sb in range(n_sub):
            total = total + jnp.sum(members(hits_of(sb)).astype(F32), axis=1, keepdims=True)
        cnt_sc[...] = total

    @pl.when((ps == 1) & (i == 0))
    def _():
        cnt = cnt_sc[...].astype(I32)
        shift = ROW_BLOCK.bit_length() - 1
        padded = lax.shift_left(lax.shift_right_logical(cnt + (ROW_BLOCK - 1), shift), shift)
        sub = lax.broadcasted_iota(I32, (N_EXPERTS, HEAD_PAD), 0)
        lane = lax.broadcasted_iota(I32, (N_EXPERTS, HEAD_PAD), 1)
        padded_row = jnp.sum(jnp.where(sub == lane, padded, 0), axis=0, keepdims=True)
        start = jnp.sum(jnp.where(lane < sub, padded_row, 0), axis=1, keepdims=True)
        start_sc[...] = start.astype(F32)
        carry_sc[...] = jnp.zeros(carry_sc.shape, F32)
        cnt_row = jnp.sum(jnp.where(sub == lane, cnt, 0), axis=0, keepdims=True)
        start_row = jnp.sum(jnp.where(sub == lane, start, 0), axis=0, keepdims=True)
        row8 = lax.broadcasted_iota(I32, (8, HEAD_PAD), 0)
        meta = jnp.where(row8 == 0, cnt_row, 0)
        meta = jnp.where(row8 == 1, start_row, meta)
        meta = jnp.where(row8 == 2, start_row + padded_row, meta)
        meta_ref[...] = meta

    @pl.when(ps == 1)
    def _():
        tr = lax.broadcasted_iota(I32, (tm, tm), 0)
        tc = lax.broadcasted_iota(I32, (tm, tm), 1)
        before = (tr < tc).astype(BF16)
        row8 = lax.broadcasted_iota(I32, (8, tm), 0)
        offset = carry_sc[...] + start_sc[...]
        for sb in range(n_sub):
            hits = hits_of(sb)
            member = members(hits)
            base = _dot(member.astype(BF16), before) + offset
            dest = jnp.zeros((8, tm), I32)
            for kk in range(TOP_K):
                dk = jnp.sum(jnp.where(hits[kk], base, 0.0), axis=0, keepdims=True).astype(I32)
                dest = jnp.where(row8 == kk, dk, dest)
            dest_ref[:, sb * tm:(sb + 1) * tm] = dest
            offset = offset + jnp.sum(member.astype(F32), axis=1, keepdims=True)
        carry_sc[...] = offset - start_sc[...]


def _positions(topi_t):
    t = topi_t.shape[1]
    tm = min(POSITION_TILE, t)
    return pl.pallas_call(
        _positions_kernel,
        out_shape=[jax.ShapeDtypeStruct((8, t), I32), jax.ShapeDtypeStruct((8, HEAD_PAD), I32)],
        grid=(2, t // tm),
        in_specs=[pl.BlockSpec((8, tm), lambda ps, i: (0, i))],
        out_specs=[pl.BlockSpec((8, tm), lambda ps, i: (0, i * ps)),
                   pl.BlockSpec((8, HEAD_PAD), lambda ps, i: (0, 0))],
        scratch_shapes=[pltpu.VMEM((N_EXPERTS, 1), F32)] * 3,
        compiler_params=_cparams(("arbitrary", "arbitrary")),
        name="routing_positions",
    )(topi_t)


def _sc_mesh():
    return plsc.VectorSubcoreMesh(core_axis_name="c", subcore_axis_name="s")


def _dispatch_rows(table, dest_flat, n_rows):
    t, c = table.shape
    n_slots = dest_flat.shape[0] // t
    per_w = t // SC_WORKERS
    assert per_w * SC_WORKERS == t and per_w % (2 * SC_WINDOW) == 0
    n_chunks = per_w // SC_WINDOW
    w = SC_WINDOW

    @functools.partial(
        pl.kernel, mesh=_sc_mesh(),
        out_type=jax.ShapeDtypeStruct((n_rows, c), table.dtype),
        scratch_types=[pltpu.VMEM((w,), I32)] * n_slots + [pltpu.VMEM((w, c), table.dtype)] * 2
        + [pltpu.SemaphoreType.DMA] * (n_slots + 2),
        name="dispatch_rows",
    )
    def k(table_hbm, dest_hbm, out_hbm, *scratch):
        idx = scratch[:n_slots]
        rows = scratch[n_slots:n_slots + 2]
        scatter_sems = scratch[n_slots + 2:2 * n_slots + 2]
        read_sems = scratch[2 * n_slots + 2:]
        wid = lax.axis_index("s") * SC_CORES + lax.axis_index("c")
        base = wid * per_w

        def off(chunk):
            return pl.multiple_of(base + chunk * w, w)

        def read(chunk, buf):
            return pltpu.make_async_copy(table_hbm.at[pl.ds(off(chunk), w)], rows[buf], read_sems[buf])

        def scatter(kk, buf):
            return pltpu.make_async_copy(rows[buf], out_hbm.at[idx[kk]], scatter_sems[kk])

        read(0, 0).start()

        @pl.loop(0, n_chunks // 2)
        def _(p):
            for buf in range(2):
                chunk = 2 * p + buf

                @pl.when(chunk + 1 < n_chunks)
                def _():
                    read(chunk + 1, 1 - buf).start()

                read(chunk, buf).wait()
                for kk in range(n_slots):
                    src = pl.multiple_of(kk * t + off(chunk), w)
                    pltpu.sync_copy(dest_hbm.at[pl.ds(src, w)], idx[kk])
                    scatter(kk, buf).start()
                for kk in range(n_slots):
                    scatter(kk, buf).wait()

    return k(table, dest_flat)


def _gather_rows(table, idx):
    n = idx.shape[0]
    c = table.shape[1]
    per_w = n // SC_WORKERS
    assert per_w * SC_WORKERS == n and per_w % (2 * SC_WINDOW) == 0
    n_chunks = per_w // SC_WINDOW
    w = SC_WINDOW

    @functools.partial(
        pl.kernel, mesh=_sc_mesh(),
        out_type=jax.ShapeDtypeStruct((n, c), table.dtype),
        scratch_types=[pltpu.VMEM((w,), I32)] * 2 + [pltpu.VMEM((w, c), table.dtype)] * 2
        + [pltpu.SemaphoreType.DMA] * 4,
        name="gather_rows",
    )
    def k(table_hbm, idx_hbm, out_hbm, idx_a, idx_b, rows_a, rows_b, g_a, g_b, w_a, w_b):
        idx, rows, gather_sems, write_sems = (idx_a, idx_b), (rows_a, rows_b), (g_a, g_b), (w_a, w_b)
        wid = lax.axis_index("s") * SC_CORES + lax.axis_index("c")
        base = wid * per_w

        def off(chunk):
            return pl.multiple_of(base + chunk * w, w)

        def gather(buf):
            return pltpu.make_async_copy(table_hbm.at[idx[buf]], rows[buf], gather_sems[buf])

        def write(chunk, buf):
            return pltpu.make_async_copy(rows[buf], out_hbm.at[pl.ds(off(chunk), w)], write_sems[buf])

        def start_gather(chunk, buf):
            pltpu.sync_copy(idx_hbm.at[pl.ds(off(chunk), w)], idx[buf])
            gather(buf).start()

        start_gather(0, 0)

        @pl.loop(0, n_chunks // 2)
        def _(p):
            for buf in range(2):
                chunk = 2 * p + buf

                @pl.when(chunk + 1 < n_chunks)
                def _():
                    @pl.when(chunk >= 1)
                    def _():
                        write(chunk - 1, 1 - buf).wait()
                    start_gather(chunk + 1, 1 - buf)

                gather(buf).wait()
                write(chunk, buf).start()

        write(n_chunks - 2, 0).wait()
        write(n_chunks - 1, 1).wait()

    return k(table, idx)


def _expert_kernel(be_ref, nused_ref, first_ref, slot_ref, next_ref,
                   xs_ref, wg_hbm, bg_ref, wu_hbm, bu_ref, wd_hbm, bd_ref,
                   ys_ref, wg_ref, wu_ref, wd_ref, stage_g, stage_u, stage_d, sems, *, layer):
    b = pl.program_id(0)
    used = b < nused_ref[0]
    weights = ((wg_hbm, stage_g), (wu_hbm, stage_u), (wd_hbm, stage_d))

    def fetch(expert, slot):
        return [pltpu.make_async_copy(hbm.at[layer, expert], stage.at[slot], sems.at[slot, n])
                for n, (hbm, stage) in enumerate(weights)]

    @pl.when(used & (first_ref[b] == 1))
    def _():
        slot = slot_ref[b]

        @pl.when(b == 0)
        def _():
            for cp in fetch(be_ref[0], slot):
                cp.start()

        for cp in fetch(be_ref[b], slot):
            cp.wait()
        wg_ref[...] = stage_g[slot].astype(BF16)
        wu_ref[...] = stage_u[slot].astype(BF16)
        wd_ref[...] = stage_d[slot].astype(BF16)

        @pl.when(next_ref[b] >= 0)
        def _():
            for cp in fetch(next_ref[b], 1 - slot):
                cp.start()

    @pl.when(used)
    def _():
        lo, hi = _unpack_halves(xs_ref[...])
        xb = jnp.concatenate([lo.astype(BF16), hi.astype(BF16)], axis=1)
        a = _dot(xb, wg_ref[...]) + bg_ref[...]
        u = _dot(xb, wu_ref[...]) + bu_ref[...]
        a = jnp.minimum(a, SWIGLU_LIMIT)
        u = jnp.clip(u, -SWIGLU_LIMIT, SWIGLU_LIMIT)
        y = (a * jax.nn.sigmoid(SWIGLU_ALPHA * a)) * (u + 1.0)
        out = _dot(y.astype(BF16), wd_ref[...]) + bd_ref[...]
        ys_ref[...] = _pack_halves(out[:, :HALF], out[:, HALF:])

    @pl.when(b >= nused_ref[0])
    def _():
        ys_ref[...] = jnp.zeros(ys_ref.shape, I32)


def _expert_ffn(xs, block_e, n_used, counts, layer, wg, bg, wu, bu, wd, bd):
    n_rows = xs.shape[0]
    n_blocks = n_rows // ROW_BLOCK

    blk = jnp.arange(n_blocks, dtype=I32)
    used = blk < n_used[0]
    first = used & ((blk == 0) | (block_e != jnp.roll(block_e, 1)))
    slot = (jnp.cumsum(first.astype(I32)) - 1) % 2
    eid = jnp.arange(N_EXPERTS, dtype=I32)
    later = (eid[None, :] > eid[:, None]) & (counts[None, :] > 0)
    next_of_expert = jnp.min(jnp.where(later, eid[None, :], N_EXPERTS), axis=1)
    next_of_expert = jnp.where(next_of_expert == N_EXPERTS, -1, next_of_expert)
    next_e = jnp.sum(jnp.where(block_e[:, None] == eid[None, :], next_of_expert[None, :], 0), axis=1)

    def rows(b, *_):
        return (b, 0)

    def expert(b, be, *_):
        return (layer, be[b], 0, 0)

    hbm = pl.BlockSpec(memory_space=pl.ANY)
    grid_spec = pltpu.PrefetchScalarGridSpec(
        num_scalar_prefetch=5,
        grid=(n_blocks,),
        in_specs=[pl.BlockSpec((ROW_BLOCK, HALF), rows),
                  hbm, pl.BlockSpec((None, None, 1, D_EXPERT), expert),
                  hbm, pl.BlockSpec((None, None, 1, D_EXPERT), expert),
                  hbm, pl.BlockSpec((None, None, 1, D_MODEL), expert)],
        out_specs=pl.BlockSpec((ROW_BLOCK, HALF), rows),
        scratch_shapes=[pltpu.VMEM((D_MODEL, D_EXPERT), BF16), pltpu.VMEM((D_MODEL, D_EXPERT), BF16),
                        pltpu.VMEM((D_EXPERT, D_MODEL), BF16),
                        pltpu.VMEM((2, D_MODEL, D_EXPERT), F32), pltpu.VMEM((2, D_MODEL, D_EXPERT), F32),
                        pltpu.VMEM((2, D_EXPERT, D_MODEL), F32),
                        pltpu.SemaphoreType.DMA((2, 3))],
    )
    return pl.pallas_call(
        functools.partial(_expert_kernel, layer=layer),
        out_shape=jax.ShapeDtypeStruct((n_rows, HALF), I32),
        grid_spec=grid_spec,
        compiler_params=_cparams(("arbitrary",)),
        name="expert_ffn",
    )(block_e, n_used, first.astype(I32), slot.astype(I32), next_e.astype(I32),
      xs, wg, bg, wu, bu, wd, bd)


def _combine_kernel(x1_ref, yg_ref, wcol_ref, p_ref, gple_ref, wpg_ref, wpp_ref, gout_ref, o_ref,
                    *, final):
    x1 = x1_ref[...]
    acc_lo = x1[:, :HALF]
    acc_hi = x1[:, HALF:]
    wcol = wcol_ref[...]
    for kk in range(TOP_K):
        lo, hi = _unpack_halves(yg_ref[kk])
        wk = wcol[:, kk:kk + 1]
        acc_lo = acc_lo + wk * lo
        acc_hi = acc_hi + wk * hi
    x2 = jnp.concatenate([acc_lo, acc_hi], axis=1)
    gate = jax.nn.sigmoid(_dot(_rms(x2, gple_ref[...]).astype(BF16), wpg_ref[...]))
    x3 = x2 + gate * _dot(p_ref[...].astype(BF16), wpp_ref[...])
    o_ref[...] = _rms(x3, gout_ref[...]) if final else x3


def _combine_kernel_inplace(x1_ref, yg_ref, wcol_ref, p_ref, gple_ref, wpg_ref, wpp_ref, gout_ref,
                            prev_ref, o_ref, *, final):
    del prev_ref
    _combine_kernel(x1_ref, yg_ref, wcol_ref, p_ref, gple_ref, wpg_ref, wpp_ref, gout_ref, o_ref,
                    final=final)


def _combine(x1, yg, wcol, p3d, layer, gple, wpg, wpp, gout, final, part, prev):
    t = x1.shape[0]
    tm = TOKEN_TILE
    n_tiles = yg.shape[1] // tm
    first_tile = part * n_tiles

    def row(i, *_):
        return (first_tile + i, 0)

    def full(a):
        return pl.BlockSpec(a.shape, lambda i: (0, 0))

    in_specs = [pl.BlockSpec((tm, D_MODEL), row),
                pl.BlockSpec((TOP_K, tm, HALF), lambda i: (0, i, 0)),
                pl.BlockSpec((tm, HEAD_PAD), row),
                pl.BlockSpec((None, tm, PLE_DIM), lambda i: (layer, first_tile + i, 0)),
                full(gple), full(wpg), full(wpp), full(gout)]
    args = [x1, yg, wcol, p3d, gple, wpg, wpp, gout]
    kern = functools.partial(_combine_kernel, final=final)
    aliases = {}
    if prev is not None:
        in_specs.append(pl.BlockSpec(memory_space=pl.ANY))
        args.append(prev)
        aliases = {len(args) - 1: 0}
        kern = functools.partial(_combine_kernel_inplace, final=final)
    return pl.pallas_call(
        kern,
        out_shape=jax.ShapeDtypeStruct((t, D_MODEL), F32),
        grid=(n_tiles,),
        in_specs=in_specs,
        out_specs=pl.BlockSpec((tm, D_MODEL), row),
        input_output_aliases=aliases,
        compiler_params=_cparams(("parallel",)),
        name="combine_ple",
    )(*args)


def _rope_tables(seq):
    inv_freq = ROPE_THETA ** (-jnp.arange(HALF_ROPE, dtype=F32) * 2.0 / MLA_ROPE)
    ang = jnp.arange(seq, dtype=F32)[:, None] * inv_freq[None, :]
    cos, sin = jnp.cos(ang), jnp.sin(ang)
    ones = jnp.ones((seq, MLA_NOPE), F32)
    zeros16 = jnp.zeros((seq, HALF_ROPE), F32)
    zeros64 = jnp.zeros((seq, MLA_NOPE), F32)
    tail = jnp.ones((seq, HEAD_PAD - MLA_NOPE - MLA_ROPE), F32)
    ztail = jnp.zeros_like(tail)
    cos_t = jnp.concatenate([ones, cos, cos, tail], axis=1)
    sina_t = jnp.concatenate([zeros64, zeros16, sin, ztail], axis=1)
    sinb_t = jnp.concatenate([zeros64, -sin, zeros16, ztail], axis=1)
    return cos_t, sina_t, sinb_t


def _prep_mixer_weights(w_in, w_uq, w_ukv):
    c0 = MLA_Q_LORA + MLA_KV_LORA
    c1 = c0 + MLA_ROPE
    c2 = c1 + len(DIL_GROUPS) * DIL_COLS
    kr_pad = jnp.pad(w_in[:, c0:c1], ((0, 0), (MLA_NOPE, HEAD_PAD - MLA_NOPE - MLA_ROPE)))
    wmla = jnp.concatenate([w_in[:, :c0], kr_pad], axis=1).astype(BF16)
    col = np.arange(len(DIL_GROUPS) * DIL_COLS)
    q_scale = np.where(col % DIL_COLS < DIL_OUT, DIL_HEAD_DIM ** -0.5, 1.0).astype(np.float32)
    wdil = (w_in[:, c1:c2] * q_scale[None, :]).astype(BF16)
    wgate = w_in[:, c2:].astype(BF16)
    pad = HEAD_PAD - MLA_NOPE - MLA_ROPE
    wuq_h = w_uq.reshape(MLA_Q_LORA, MLA_HEADS, MLA_NOPE + MLA_ROPE)
    wuq = jnp.pad(wuq_h, ((0, 0), (0, 0), (0, pad))).reshape(MLA_Q_LORA, MLA_HEADS * HEAD_PAD).astype(BF16)
    wukv_h = w_ukv.reshape(MLA_KV_LORA, MLA_HEADS, MLA_NOPE + MLA_V)
    wuk = jnp.pad(wukv_h[:, :, :MLA_NOPE], ((0, 0), (0, 0), (0, HEAD_PAD - MLA_NOPE)))
    wuk = wuk.reshape(MLA_KV_LORA, MLA_HEADS * HEAD_PAD).astype(BF16)
    wuv = wukv_h[:, :, MLA_NOPE:].reshape(MLA_KV_LORA, MLA_HEADS * MLA_V).astype(BF16)
    return wmla, wdil, wgate, wuq, wuk, wuv


def kernel(x, p, attn_norm, w_in, q_norm, w_uq, kv_norm, w_ukv, w_branch_a, w_branch_b, w_out, ffn_norm, w_router, b_router, w_gate, b_gate, w_up, b_up, w_down, b_down, ple_norm, w_ple_gate, w_ple_proj, final_norm):
    b, s, d = x.shape
    depth = w_in.shape[0]
    t = b * s
    assert d == D_MODEL and s % (DIL_GROUPS[-1][0]) == 0 and t % (SC_WORKERS * SC_WINDOW) == 0
    n_assign = t * TOP_K
    n_blocks = -(-(n_assign + N_EXPERTS * (ROW_BLOCK - 1)) // ROW_BLOCK)
    n_rows = n_blocks * ROW_BLOCK
    cos_t, sina_t, sinb_t = _rope_tables(s)
    xc = x.reshape(t, d)
    for i in range(depth):
        wmla, wdil, wgate, wuq, wuk, wuv = _prep_mixer_weights(w_in[i], w_uq[i], w_ukv[i])
        q, k, vt, zd0, zd1, zd2 = _inproj(
            xc, s, attn_norm[i][None], wmla, wdil, q_norm[i][None], kv_norm[i][None],
            wuq, wuk, wuv, cos_t, sina_t, sinb_t)
        oa = _mla_attention(q.reshape(b, s, -1), k.reshape(b, s, -1), vt)
        ob = _dilated_attention(zd0, zd1, zd2, s)
        x1, hp, topi_t, topw_t, wcol = _merge(
            xc, oa.reshape(t, -1), ob, attn_norm[i][None], wgate,
            w_branch_a[i].astype(BF16), w_branch_b[i].astype(BF16), w_out[i].astype(BF16),
            ffn_norm[i][None], w_router[i].T.astype(BF16), b_router[i][:, None])
        dest_t, meta = _positions(topi_t)
        ends = meta[2, :N_EXPERTS]
        block_start = jnp.arange(n_blocks, dtype=I32) * ROW_BLOCK
        block_e = jnp.minimum(
            jnp.sum((ends[None, :] <= block_start[:, None]).astype(I32), axis=1), N_EXPERTS - 1)
        n_used = (ends[N_EXPERTS - 1:] // ROW_BLOCK).astype(I32)
        dest_flat = dest_t[:TOP_K].reshape(n_assign)
        xs = _dispatch_rows(hp, dest_flat, n_rows)
        ys = _expert_ffn(xs, block_e, n_used, meta[0, :N_EXPERTS], i,
                         w_gate, b_gate[:, :, None, :], w_up, b_up[:, :, None, :],
                         w_down, b_down[:, :, None, :])
        final = i == depth - 1
        gout = final_norm[None] if final else attn_norm[i][None]
        wpg, wpp = w_ple_gate[i].astype(BF16), w_ple_proj[i].astype(BF16)
        tp = t // COMBINE_PARTS
        xc = None
        for part in range(COMBINE_PARTS):
            dest_part = dest_t[:TOP_K, part * tp:(part + 1) * tp].reshape(TOP_K * tp)
            yg = _gather_rows(ys, dest_part).reshape(TOP_K, tp, HALF)
            xc = _combine(x1, yg, wcol, p.reshape(depth, t, PLE_DIM), i, ple_norm[i][None],
                          wpg, wpp, gout, final, part, xc)
    return xc.reshape(b, s, d)
```

```python
import functools
import math

import jax
import jax.numpy as jnp
import numpy as np
from jax import lax
from jax.experimental import pallas as pl
from jax.experimental.pallas import tpu as pltpu
from jax.experimental.pallas import tpu_sc as plsc

F32 = jnp.float32
BF16 = jnp.bfloat16
I32 = jnp.int32

D_MODEL = 1024
PLE_DIM = 256
NORM_EPS = 1e-6

MLA_HEADS = 8
MLA_Q_LORA = 384
MLA_KV_LORA = 256
MLA_NOPE = 64
MLA_ROPE = 32
MLA_V = 64
ROPE_THETA = 10000.0
HEAD_PAD = 128
HALF_ROPE = MLA_ROPE // 2

DIL_GROUPS = ((128, 1), (512, 4), (2048, 16))
DIL_HEADS = 4
DIL_HEAD_DIM = 64
DIL_STEPS = 128
DIL_COLS = 3 * DIL_HEADS * DIL_HEAD_DIM
DIL_OUT = DIL_HEADS * DIL_HEAD_DIM
DIL_UNROLL = 8

N_EXPERTS = 32
TOP_K = 4
D_EXPERT = 1024
SWIGLU_LIMIT = 7.0
SWIGLU_ALPHA = 1.702
ROW_BLOCK = 512

TOKEN_TILE = 512
POSITION_TILE = 4096
POSITION_SUBTILE = 512
COMBINE_PARTS = 2
ATTN_TQ = 1024
ATTN_TK = 1024
ATTN_HEADS = 4
ATTN_SUB = 1024
ATTN_KEY_CHUNKS = 2
HALF = D_MODEL // 2
NEG = -1e30
SPECULATION_HEADROOM = 60.0
HI_MASK = -65536

SC_CORES = 2
SC_SUBCORES = 16
SC_WORKERS = SC_CORES * SC_SUBCORES
SC_WINDOW = 64

VMEM_LIMIT = 56 * 1024 * 1024


def _cparams(sem):
    return pltpu.CompilerParams(dimension_semantics=sem, vmem_limit_bytes=VMEM_LIMIT)


def _rms(x, g):
    return x * lax.rsqrt(jnp.mean(x * x, axis=-1, keepdims=True) + NORM_EPS) * g


def _dot(a, b):
    return jnp.dot(a, b, preferred_element_type=F32)


def _dot_nt(a, b):
    return lax.dot_general(a, b, (((1,), (1,)), ((), ())), preferred_element_type=F32)


def _pack_halves(lo, hi):
    lo_i = lax.bitcast_convert_type(lo.astype(BF16).astype(F32), I32)
    hi_i = lax.bitcast_convert_type(hi.astype(BF16).astype(F32), I32)
    return (hi_i & HI_MASK) | lax.shift_right_logical(lo_i, 16)


def _unpack_halves(w):
    lo = lax.bitcast_convert_type(lax.shift_left(w, 16), F32)
    hi = lax.bitcast_convert_type(w & HI_MASK, F32)
    return lo, hi


def _inproj_kernel(x_ref, g_ref, wmla_ref, wdil_ref, qn_ref, kvn_ref, wuq_ref, wuk_ref,
                   wuv_ref, cos_ref, sina_ref, sinb_ref,
                   q_ref, k_ref, vt_ref, zd0_ref, zd1_ref, zd2_ref, zs_sc):
    h = _rms(x_ref[...], g_ref[...]).astype(BF16)
    zm = _dot(h, wmla_ref[...])
    cq = _rms(zm[:, :MLA_Q_LORA], qn_ref[...]).astype(BF16)
    ckv = _rms(zm[:, MLA_Q_LORA:MLA_Q_LORA + MLA_KV_LORA], kvn_ref[...]).astype(BF16)
    kr = zm[:, MLA_Q_LORA + MLA_KV_LORA:]
    cos, sina, sinb = cos_ref[...], sina_ref[...], sinb_ref[...]

    def rope(t):
        return (t * cos + pltpu.roll(t, HALF_ROPE, 1) * sina
                + pltpu.roll(t, HEAD_PAD - HALF_ROPE, 1) * sinb)

    kr_rot = rope(kr)
    qraw = _dot(cq, wuq_ref[...])
    kraw = _dot(ckv, wuk_ref[...])
    vt_ref[...] = _dot(ckv, wuv_ref[...]).T.astype(BF16)
    scale = (MLA_NOPE + MLA_ROPE) ** -0.5 * math.log2(math.e)
    for hd in range(MLA_HEADS):
        sl = slice(hd * HEAD_PAD, (hd + 1) * HEAD_PAD)
        q_ref[:, sl] = (rope(qraw[:, sl]) * scale).astype(BF16)
        k_ref[:, sl] = (kraw[:, sl] + kr_rot).astype(BF16)
    tm = x_ref.shape[0]
    for gi, zd_ref in enumerate((zd0_ref, zd1_ref, zd2_ref)):
        z = _dot(h, wdil_ref[:, gi * DIL_COLS:(gi + 1) * DIL_COLS])
        window, dil = DIL_GROUPS[gi]
        if dil == 1:
            zd_ref[...] = z.astype(BF16)
            continue
        n_col = DIL_COLS // HEAD_PAD
        for c in range(n_col):
            zs_sc[c] = z[:, c * HEAD_PAD:(c + 1) * HEAD_PAD]
        rows = tm // dil
        part = pl.program_id(0) % (window // tm)
        for r in range(dil):
            dst = pl.ds(pl.multiple_of(r * DIL_STEPS + part * rows, rows), rows)
            for c in range(n_col):
                chunk = zs_sc[c, pl.ds(r, rows, stride=dil), :]
                zd_ref[dst, c * HEAD_PAD:(c + 1) * HEAD_PAD] = chunk.astype(BF16)


def _inproj(x2d, seq, g, wmla, wdil, qn, kvn, wuq, wuk, wuv, cos_t, sina_t, sinb_t):
    t = x2d.shape[0]
    tm = TOKEN_TILE
    n_seq_tiles = seq // tm

    def row(i):
        return (i, 0)

    def const(i):
        return (0, 0)

    def pos(i):
        return (i % n_seq_tiles, 0)

    def full(a):
        return pl.BlockSpec(a.shape, const)

    def vt_block(i):
        return (i // n_seq_tiles, 0, i % n_seq_tiles)

    def rows(width):
        return pl.BlockSpec((tm, width), row)

    def unit_rows(window):
        return pl.BlockSpec((window, DIL_COLS), lambda i: (i // (window // tm), 0))

    out_shape = [
        jax.ShapeDtypeStruct((t, MLA_HEADS * HEAD_PAD), BF16),
        jax.ShapeDtypeStruct((t, MLA_HEADS * HEAD_PAD), BF16),
        jax.ShapeDtypeStruct((t // seq, MLA_HEADS * MLA_V, seq), BF16),
        jax.ShapeDtypeStruct((t, DIL_COLS), BF16),
        jax.ShapeDtypeStruct((t, DIL_COLS), BF16),
        jax.ShapeDtypeStruct((t, DIL_COLS), BF16),
    ]
    return pl.pallas_call(
        _inproj_kernel,
        out_shape=out_shape,
        grid=(t // tm,),
        in_specs=[pl.BlockSpec((tm, D_MODEL), row), full(g), full(wmla), full(wdil),
                  full(qn), full(kvn), full(wuq), full(wuk), full(wuv),
                  pl.BlockSpec((tm, HEAD_PAD), pos), pl.BlockSpec((tm, HEAD_PAD), pos),
                  pl.BlockSpec((tm, HEAD_PAD), pos)],
        out_specs=[rows(D_MODEL), rows(D_MODEL),
                   pl.BlockSpec((None, MLA_HEADS * MLA_V, tm), vt_block),
                   rows(DIL_COLS)] + [unit_rows(window) for window, _ in DIL_GROUPS[1:]],
        scratch_shapes=[pltpu.VMEM((DIL_COLS // HEAD_PAD, tm, HEAD_PAD), F32)],
        compiler_params=_cparams(("arbitrary",)),
        name="inproj",
    )(x2d, g, wmla, wdil, qn, kvn, wuq, wuk, wuv, cos_t, sina_t, sinb_t)


def _mla_kernel(qi_ref, kj_ref, q_ref, k_ref, vt_ref, o_ref, m_sc, l_sc, acc_sc, redo_sc):
    p = pl.program_id(2)
    i = qi_ref[p]
    j = kj_ref[p]
    tq = q_ref.shape[0]
    tk = k_ref.shape[0]

    @pl.when(j == 0)
    def _():
        m_sc[...] = jnp.full(m_sc.shape, NEG, F32)
        l_sc[...] = jnp.zeros(l_sc.shape, F32)
        acc_sc[...] = jnp.zeros(acc_sc.shape, F32)

    ratio = tq // tk
    sub = ATTN_SUB

    def step(diagonal, speculative):
        chains = [(hh, c) for hh in range(ATTN_HEADS) for c in range(tq // sub)]
        ones_rows = (lax.broadcasted_iota(I32, (16, tk), 0) == 0).astype(BF16)
        state = {}
        for hh, c in chains:
            cs = slice(c * sub, (c + 1) * sub)
            state[hh, c] = (m_sc[hh, :, cs], l_sc[hh, :, cs], acc_sc[hh, :, cs])
        new_state = {}
        within = None
        def scores(hh, c, rows, q0=0):
            sl = slice(hh * HEAD_PAD, (hh + 1) * HEAD_PAD)
            st = _dot_nt(k_ref[rows, sl], q_ref[c * sub + q0:(c + 1) * sub, sl])
            if diagonal:
                shape = (rows.stop - rows.start, sub - q0)
                key = lax.broadcasted_iota(I32, shape, 0) + (j * tk + rows.start)
                qry = lax.broadcasted_iota(I32, shape, 1) + (i * tq + c * sub + q0)
                st = jnp.where(qry >= key, st, NEG)
            return st

        def values(hh, rows):
            ones_row = (lax.broadcasted_iota(I32, (16, rows.stop - rows.start), 0) == 0).astype(BF16)
            return jnp.concatenate([vt_ref[hh * MLA_V:(hh + 1) * MLA_V, rows], ones_row], axis=0)

        for hh, c in chains:
            m_prev, l_prev, acc_prev = state[hh, c]
            if speculative:
                kc = tk // ATTN_KEY_CHUNKS
                m_blk, pv = None, None
                for n in range(ATTN_KEY_CHUNKS):
                    rows = slice(n * kc, (n + 1) * kc)
                    q0 = n * kc if (diagonal and tq == tk and sub == tq) else 0
                    st = scores(hh, c, rows, q0)
                    if n == 0:
                        m_prev = jnp.where(j == 0, st[:1], m_prev)
                    m_part = jnp.max(st, axis=0, keepdims=True)
                    part = _dot(values(hh, rows), jnp.exp2(st - m_prev[:, q0:]).astype(BF16))
                    if q0:
                        m_part = jnp.concatenate([jnp.full((1, q0), NEG, F32), m_part], axis=1)
                        part = jnp.concatenate([jnp.zeros((part.shape[0], q0), F32), part], axis=1)
                    m_blk = m_part if m_blk is None else jnp.maximum(m_blk, m_part)
                    pv = part if pv is None else pv + part
                m_new = jnp.maximum(m_prev, m_blk)
                alpha = jnp.exp2(m_prev - m_new)
                l_new = alpha * (l_prev + pv[MLA_V:MLA_V + 1])
                acc_new = alpha * (acc_prev + pv[:MLA_V])
                ok = jnp.max(m_blk - m_prev) <= SPECULATION_HEADROOM
                within = ok if within is None else (within & ok)
            else:
                rows = slice(0, tk)
                st = scores(hh, c, rows)
                m_new = jnp.maximum(m_prev, jnp.max(st, axis=0, keepdims=True))
                alpha = jnp.exp2(m_prev - m_new)
                pv = _dot(values(hh, rows), jnp.exp2(st - m_new).astype(BF16))
                l_new = alpha * l_prev + pv[MLA_V:MLA_V + 1]
                acc_new = alpha * acc_prev + pv[:MLA_V]
            new_state[hh, c] = (m_new, l_new, acc_new)

        def commit():
            for hh, c in chains:
                cs = slice(c * sub, (c + 1) * sub)
                m_sc[hh, :, cs], l_sc[hh, :, cs], acc_sc[hh, :, cs] = new_state[hh, c]

        if speculative:
            pl.when(within)(commit)
            redo_sc[0] = jnp.logical_not(within).astype(I32)
        else:
            commit()

    redo_sc[0] = 0
    on_diagonal = j >= ratio * i

    @pl.when(jnp.logical_not(on_diagonal))
    def _():
        step(False, True)

    @pl.when(on_diagonal)
    def _():
        step(True, True)

    @pl.when(redo_sc[0] != 0)
    def _():
        step(True, False)

    @pl.when(j == ratio * i + (ratio - 1))
    def _():
        ot = jnp.concatenate([acc_sc[hh] / l_sc[hh] for hh in range(ATTN_HEADS)], axis=0)
        o_ref[...] = ot.T.astype(BF16)


def _mla_attention(q, k, vt):
    b, s, _ = q.shape
    tq, tk, nh = ATTN_TQ, ATTN_TK, ATTN_HEADS
    ratio = tq // tk
    nq = s // tq
    pairs = [(i, j) for i in range(nq) for j in range(ratio * (i + 1))]
    qi = jnp.asarray([p[0] for p in pairs], I32)
    kj = jnp.asarray([p[1] for p in pairs], I32)
    grid_spec = pltpu.PrefetchScalarGridSpec(
        num_scalar_prefetch=2,
        grid=(b, MLA_HEADS // nh, len(pairs)),
        in_specs=[
            pl.BlockSpec((None, tq, nh * HEAD_PAD), lambda bb, hp, p, qi, kj: (bb, qi[p], hp)),
            pl.BlockSpec((None, tk, nh * HEAD_PAD), lambda bb, hp, p, qi, kj: (bb, kj[p], hp)),
            pl.BlockSpec((None, nh * MLA_V, tk), lambda bb, hp, p, qi, kj: (bb, hp, kj[p])),
        ],
        out_specs=pl.BlockSpec((None, tq, nh * MLA_V), lambda bb, hp, p, qi, kj: (bb, qi[p], hp)),
        scratch_shapes=[pltpu.VMEM((nh, 1, tq), F32), pltpu.VMEM((nh, 1, tq), F32),
                        pltpu.VMEM((nh, MLA_V, tq), F32), pltpu.SMEM((1,), I32)],
    )
    return pl.pallas_call(
        _mla_kernel,
        out_shape=jax.ShapeDtypeStruct((b, s, MLA_HEADS * MLA_V), BF16),
        grid_spec=grid_spec,
        compiler_params=_cparams(("parallel", "parallel", "arbitrary")),
        name="mla_attention",
    )(qi, kj, q, k, vt)


def _alibi_slopes(n):
    def pow2(m):
        start = 2.0 ** (-8.0 / m)
        return [start ** (i + 1) for i in range(m)]
    if math.log2(n).is_integer():
        s = pow2(n)
    else:
        c = 2 ** int(math.floor(math.log2(n)))
        s = pow2(c) + pow2(2 * c)[0::2][: n - c]
    return np.array(sorted(s, reverse=True), dtype=np.float32)


def _dilated_block(cur, prev, bias4, first):
    n = DIL_STEPS
    hw = DIL_OUT
    q = cur[:, :hw]
    kk = jnp.concatenate([prev[:, hw:2 * hw], cur[:, hw:2 * hw]], axis=0)
    vv = jnp.concatenate([prev[:, 2 * hw:], cur[:, 2 * hw:]], axis=0)
    head_of_lane = lax.broadcasted_iota(I32, (n, hw), 1) // DIL_HEAD_DIM
    zero = jnp.zeros_like(q)
    q4 = jnp.concatenate([jnp.where(head_of_lane == h, q, zero) for h in range(DIL_HEADS)], axis=0)
    s4 = _dot_nt(q4, kk) + bias4
    if first is not None:
        ki = lax.broadcasted_iota(I32, (DIL_HEADS * n, 2 * n), 1)
        s4 = jnp.where(first & (ki < n), NEG, s4)
    m4 = jnp.max(s4, axis=1, keepdims=True)
    p4 = jnp.exp(s4 - m4).astype(BF16)
    l4 = _dot(p4, jnp.ones((2 * n, HEAD_PAD), BF16))
    pv4 = _dot(p4, vv)
    m4 = jnp.broadcast_to(m4, (DIL_HEADS * n, HEAD_PAD))

    def rows(a, h):
        return a[h * n:(h + 1) * n]

    o_un = rows(pv4, DIL_HEADS - 1)
    for h in range(DIL_HEADS - 2, -1, -1):
        o_un = jnp.where(head_of_lane == h, rows(pv4, h), o_un)
    low = lax.broadcasted_iota(I32, (n, HEAD_PAD), 1) < DIL_HEAD_DIM

    def per_lane(a):
        return jnp.concatenate([jnp.where(low, rows(a, 0), rows(a, 1)),
                                jnp.where(low, rows(a, 2), rows(a, 3))], axis=1)

    l_sel = per_lane(l4)
    return o_un / l_sel, per_lane(m4) + jnp.log(l_sel)


def _dilated_kernel(c0_ref, h0_ref, c1_ref, h1_ref, c2_ref, h2_ref, ob_ref, o_sc, l_sc, bias_sc,
                    *, slopes):
    u = pl.program_id(1)
    n = DIL_STEPS
    unit = ob_ref.shape[0]
    n_sb = unit // n
    first = u == 0
    qi = lax.broadcasted_iota(I32, (n, 2 * n), 0)
    ki = lax.broadcasted_iota(I32, (n, 2 * n), 1)
    dist = qi + n - ki
    valid = (dist >= 0) & (dist <= n)
    distf = dist.astype(F32)
    for gi in range(len(DIL_GROUPS)):
        for h in range(DIL_HEADS):
            bias_sc[gi, h * n:(h + 1) * n, :] = jnp.where(valid, -slopes[gi][h] * distf, NEG)

    def rows_of(ref, sb):
        return ref[pl.ds(pl.multiple_of(sb * n, n), n), :]

    def emit(gi, start, stride, o, lse):
        if isinstance(start, int):
            idx = pl.ds(start, n)
        elif stride == 1:
            idx = pl.ds(pl.multiple_of(start, n), n)
        else:
            idx = pl.ds(start, n, stride=stride)
        for half in range(DIL_OUT // HEAD_PAD):
            ls = slice(half * HEAD_PAD, (half + 1) * HEAD_PAD)
            o_sc[gi, half, idx, :] = o[:, ls]
            l_sc[gi, half, idx, :] = lse[:, ls]

    emit(0, 0, 1, *_dilated_block(c0_ref[:n, :], h0_ref[...], bias_sc[0], first))

    def g0_body(sb, carry):
        emit(0, sb * n, 1, *_dilated_block(rows_of(c0_ref, sb), rows_of(c0_ref, sb - 1), bias_sc[0], None))
        return carry

    lax.fori_loop(1, n_sb, g0_body, 0, unroll=DIL_UNROLL)

    d1 = DIL_GROUPS[1][1]

    def g1_head(r, carry):
        emit(1, r, d1, *_dilated_block(rows_of(c1_ref, r), rows_of(h1_ref, r), bias_sc[1], first))
        return carry

    def g1_body(sb, carry):
        start = (sb // d1) * (d1 * n) + sb % d1
        emit(1, start, d1, *_dilated_block(rows_of(c1_ref, sb), rows_of(c1_ref, sb - d1), bias_sc[1], None))
        return carry

    lax.fori_loop(0, d1, g1_head, 0, unroll=DIL_UNROLL)
    lax.fori_loop(d1, n_sb, g1_body, 0, unroll=DIL_UNROLL)

    d2 = DIL_GROUPS[2][1]

    def g2_body(r, carry):
        emit(2, r, d2, *_dilated_block(rows_of(c2_ref, r), rows_of(h2_ref, r), bias_sc[2], first))
        return carry

    lax.fori_loop(0, n_sb, g2_body, 0, unroll=DIL_UNROLL)

    def merge_body(c, carry):
        idx = pl.ds(pl.multiple_of(c * n, n), n)
        for half in range(DIL_OUT // HEAD_PAD):
            l0, l1, l2 = l_sc[0, half, idx, :], l_sc[1, half, idx, :], l_sc[2, half, idx, :]
            lmax = jnp.maximum(jnp.maximum(l0, l1), l2)
            e0, e1, e2 = jnp.exp(l0 - lmax), jnp.exp(l1 - lmax), jnp.exp(l2 - lmax)
            ob = (e0 * o_sc[0, half, idx, :] + e1 * o_sc[1, half, idx, :]
                  + e2 * o_sc[2, half, idx, :]) / (e0 + e1 + e2)
            ob_ref[idx, half * HEAD_PAD:(half + 1) * HEAD_PAD] = ob.astype(BF16)
        return carry

    lax.fori_loop(0, n_sb, merge_body, 0)


def _dilated_attention(zd0, zd1, zd2, seq):
    t = zd0.shape[0]
    unit = DIL_GROUPS[-1][0]
    upb = seq // unit
    n = DIL_STEPS
    u1 = DIL_GROUPS[1][0]
    all_slopes = _alibi_slopes(len(DIL_GROUPS) * DIL_HEADS).reshape(len(DIL_GROUPS), DIL_HEADS)
    slopes = tuple(tuple(float(x) * dil for x in all_slopes[gi]) for gi, (_, dil) in enumerate(DIL_GROUPS))

    def cur(bb, u):
        return (bb * upb + u, 0)

    def halo(rows):
        per_unit = unit // rows
        return lambda bb, u: ((bb * upb) * per_unit + jnp.maximum(u * per_unit - 1, 0), 0)

    return pl.pallas_call(
        functools.partial(_dilated_kernel, slopes=slopes),
        out_shape=jax.ShapeDtypeStruct((t, DIL_OUT), BF16),
        grid=(t // seq, upb),
        in_specs=[pl.BlockSpec((unit, DIL_COLS), cur), pl.BlockSpec((n, DIL_COLS), halo(n)),
                  pl.BlockSpec((unit, DIL_COLS), cur), pl.BlockSpec((u1, DIL_COLS), halo(u1)),
                  pl.BlockSpec((unit, DIL_COLS), cur), pl.BlockSpec((unit, DIL_COLS), halo(unit))],
        out_specs=pl.BlockSpec((unit, DIL_OUT), cur),
        scratch_shapes=[pltpu.VMEM((len(DIL_GROUPS), DIL_OUT // HEAD_PAD, unit, HEAD_PAD), F32),
                        pltpu.VMEM((len(DIL_GROUPS), DIL_OUT // HEAD_PAD, unit, HEAD_PAD), F32),
                        pltpu.VMEM((len(DIL_GROUPS), DIL_HEADS * n, 2 * n), F32)],
        compiler_params=_cparams(("parallel", "arbitrary")),
        name="dilated_attention",
    )(zd0, zd0, zd1, zd1, zd2, zd2)


def _merge_kernel(x_ref, oa_ref, ob_ref, ga_ref, wgate_ref,
                  wa_ref, wb_ref, wo_ref, g_ref, wr_ref, br_ref,
                  x1_ref, hp_ref, topi_ref, topw_ref, wcol_ref):
    tm = x_ref.shape[0]
    x = x_ref[...]
    h = _rms(x, ga_ref[...]).astype(BF16)
    mixed = (jax.nn.sigmoid(_dot(h, wgate_ref[:, :D_MODEL])) * _dot(oa_ref[...], wa_ref[...])
             + jax.nn.sigmoid(_dot(h, wgate_ref[:, D_MODEL:])) * _dot(ob_ref[...], wb_ref[...]))
    x1 = x + _dot(mixed.astype(BF16), wo_ref[...])
    x1_ref[...] = x1
    h2 = _rms(x1, g_ref[...])
    hp_ref[...] = _pack_halves(h2[:, :HALF], h2[:, HALF:])

    logits = _dot_nt(wr_ref[...], h2.astype(BF16)) + br_ref[...]
    eidx = lax.broadcasted_iota(I32, (N_EXPERTS, tm), 0)
    vals, idxs = [], []
    for _ in range(TOP_K):
        m = jnp.max(logits, axis=0, keepdims=True)
        idx = jnp.min(jnp.where(logits == m, eidx, N_EXPERTS), axis=0, keepdims=True)
        vals.append(m)
        idxs.append(idx)
        logits = jnp.where(eidx == idx, -jnp.inf, logits)
    exps = [jnp.exp(vk - vals[0]) for vk in vals]
    den = exps[0] + exps[1] + exps[2] + exps[3]
    row8 = lax.broadcasted_iota(I32, (8, tm), 0)
    row128 = lax.broadcasted_iota(I32, (HEAD_PAD, tm), 0)
    topi = jnp.zeros((8, tm), I32)
    topw = jnp.zeros((8, tm), F32)
    wide = jnp.zeros((HEAD_PAD, tm), F32)
    for kk in range(TOP_K):
        wk = exps[kk] / den
        topi = jnp.where(row8 == kk, idxs[kk], topi)
        topw = jnp.where(row8 == kk, wk, topw)
        wide = jnp.where(row128 == kk, wk, wide)
    topi_ref[...] = topi
    topw_ref[...] = topw
    wcol_ref[...] = wide.T


def _merge(x2d, oa, ob, g_attn, wgate, wa, wb, wo, g, wr_t, br_col):
    t = x2d.shape[0]
    tm = TOKEN_TILE

    def row(i):
        return (i, 0)

    def col(i):
        return (0, i)

    def full(a):
        return pl.BlockSpec(a.shape, lambda i: (0, 0))

    def rows(width):
        return pl.BlockSpec((tm, width), row)

    out_shape = [
        jax.ShapeDtypeStruct((t, D_MODEL), F32),
        jax.ShapeDtypeStruct((t, HALF), I32),
        jax.ShapeDtypeStruct((8, t), I32),
        jax.ShapeDtypeStruct((8, t), F32),
        jax.ShapeDtypeStruct((t, HEAD_PAD), F32),
    ]
    return pl.pallas_call(
        _merge_kernel,
        out_shape=out_shape,
        grid=(t // tm,),
        in_specs=[rows(D_MODEL), rows(MLA_HEADS * MLA_V), rows(DIL_OUT), full(g_attn), full(wgate)]
        + [full(wa), full(wb), full(wo), full(g), full(wr_t), full(br_col)],
        out_specs=[rows(D_MODEL), rows(HALF), pl.BlockSpec((8, tm), col), pl.BlockSpec((8, tm), col),
                   rows(HEAD_PAD)],
        compiler_params=_cparams(("parallel",)),
        name="merge_router",
    )(x2d, oa, ob, g_attn, wgate, wa, wb, wo, g, wr_t, br_col)


def _positions_kernel(topi_ref, dest_ref, meta_ref, cnt_sc, carry_sc, start_sc):
    ps = pl.program_id(0)
    i = pl.program_id(1)
    tm = POSITION_SUBTILE
    n_sub = topi_ref.shape[1] // tm
    eidx = lax.broadcasted_iota(I32, (N_EXPERTS, tm), 0)

    def hits_of(sb):
        topi = topi_ref[:, sb * tm:(sb + 1) * tm]
        return [eidx == topi[kk:kk + 1, :] for kk in range(TOP_K)]

    def members(hits):
        return hits[0] | hits[1] | hits[2] | hits[3]

    @pl.when((ps == 0) & (i == 0))
    def _():
        cnt_sc[...] = jnp.zeros(cnt_sc.shape, F32)

    @pl.when(ps == 0)
    def _():
        total = cnt_sc[...]
        for sb in range(n_sub):
            total = total + jnp.sum(members(hits_of(sb)).astype(F32), axis=1, keepdims=True)
        cnt_sc[...] = total

    @pl.when((ps == 1) & (i == 0))
    def _():
        cnt = cnt_sc[...].astype(I32)
        shift = ROW_BLOCK.bit_length() - 1
        padded = lax.shift_left(lax.shift_right_logical(cnt + (ROW_BLOCK - 1), shift), shift)
        sub = lax.broadcasted_iota(I32, (N_EXPERTS, HEAD_PAD), 0)
        lane = lax.broadcasted_iota(I32, (N_EXPERTS, HEAD_PAD), 1)
        padded_row = jnp.sum(jnp.where(sub == lane, padded, 0), axis=0, keepdims=True)
        start = jnp.sum(jnp.where(lane < sub, padded_row, 0), axis=1, keepdims=True)
        start_sc[...] = start.astype(F32)
        carry_sc[...] = jnp.zeros(carry_sc.shape, F32)
        cnt_row = jnp.sum(jnp.where(sub == lane, cnt, 0), axis=0, keepdims=True)
        start_row = jnp.sum(jnp.where(sub == lane, start, 0), axis=0, keepdims=True)
        row8 = lax.broadcasted_iota(I32, (8, HEAD_PAD), 0)
        meta = jnp.where(row8 == 0, cnt_row, 0)
        meta = jnp.where(row8 == 1, start_row, meta)
        meta = jnp.where(row8 == 2, start_row + padded_row, meta)
        meta_ref[...] = meta

    @pl.when(ps == 1)
    def _():
        tr = lax.broadcasted_iota(I32, (tm, tm), 0)
        tc = lax.broadcasted_iota(I32, (tm, tm), 1)
        before = (tr < tc).astype(BF16)
        row8 = lax.broadcasted_iota(I32, (8, tm), 0)
        offset = carry_sc[...] + start_sc[...]
        for sb in range(n_sub):
            hits = hits_of(sb)
            member = members(hits)
            base = _dot(member.astype(BF16), before) + offset
            dest = jnp.zeros((8, tm), I32)
            for kk in range(TOP_K):
                dk = jnp.sum(jnp.where(hits[kk], base, 0.0), axis=0, keepdims=True).astype(I32)
                dest = jnp.where(row8 == kk, dk, dest)
            dest_ref[:, sb * tm:(sb + 1) * tm] = dest
            offset = offset + jnp.sum(member.astype(F32), axis=1, keepdims=True)
        carry_sc[...] = offset - start_sc[...]


def _positions(topi_t):
    t = topi_t.shape[1]
    tm = min(POSITION_TILE, t)
    return pl.pallas_call(
        _positions_kernel,
        out_shape=[jax.ShapeDtypeStruct((8, t), I32), jax.ShapeDtypeStruct((8, HEAD_PAD), I32)],
        grid=(2, t // tm),
        in_specs=[pl.BlockSpec((8, tm), lambda ps, i: (0, i))],
        out_specs=[pl.BlockSpec((8, tm), lambda ps, i: (0, i * ps)),
                   pl.BlockSpec((8, HEAD_PAD), lambda ps, i: (0, 0))],
        scratch_shapes=[pltpu.VMEM((N_EXPERTS, 1), F32)] * 3,
        compiler_params=_cparams(("arbitrary", "arbitrary")),
        name="routing_positions",
    )(topi_t)


def _sc_mesh():
    return plsc.VectorSubcoreMesh(core_axis_name="c", subcore_axis_name="s")


def _dispatch_rows(table, dest_flat, n_rows):
    t, c = table.shape
    n_slots = dest_flat.shape[0] // t
    per_w = t // SC_WORKERS
    assert per_w * SC_WORKERS == t and per_w % (2 * SC_WINDOW) == 0
    n_chunks = per_w // SC_WINDOW
    w = SC_WINDOW

    @functools.partial(
        pl.kernel, mesh=_sc_mesh(),
        out_type=jax.ShapeDtypeStruct((n_rows, c), table.dtype),
        scratch_types=[pltpu.VMEM((w,), I32)] * n_slots + [pltpu.VMEM((w, c), table.dtype)] * 2
        + [pltpu.SemaphoreType.DMA] * (n_slots + 2),
        name="dispatch_rows",
    )
    def k(table_hbm, dest_hbm, out_hbm, *scratch):
        idx = scratch[:n_slots]
        rows = scratch[n_slots:n_slots + 2]
        scatter_sems = scratch[n_slots + 2:2 * n_slots + 2]
        read_sems = scratch[2 * n_slots + 2:]
        wid = lax.axis_index("s") * SC_CORES + lax.axis_index("c")
        base = wid * per_w

        def off(chunk):
            return pl.multiple_of(base + chunk * w, w)

        def read(chunk, buf):
            return pltpu.make_async_copy(table_hbm.at[pl.ds(off(chunk), w)], rows[buf], read_sems[buf])

        def scatter(kk, buf):
            return pltpu.make_async_copy(rows[buf], out_hbm.at[idx[kk]], scatter_sems[kk])

        read(0, 0).start()

        @pl.loop(0, n_chunks // 2)
        def _(p):
            for buf in range(2):
                chunk = 2 * p + buf

                @pl.when(chunk + 1 < n_chunks)
                def _():
                    read(chunk + 1, 1 - buf).start()

                read(chunk, buf).wait()
                for kk in range(n_slots):
                    src = pl.multiple_of(kk * t + off(chunk), w)
                    pltpu.sync_copy(dest_hbm.at[pl.ds(src, w)], idx[kk])
                    scatter(kk, buf).start()
                for kk in range(n_slots):
                    scatter(kk, buf).wait()

    return k(table, dest_flat)


def _gather_rows(table, idx):
    n = idx.shape[0]
    c = table.shape[1]
    per_w = n // SC_WORKERS
    assert per_w * SC_WORKERS == n and per_w % (2 * SC_WINDOW) == 0
    n_chunks = per_w // SC_WINDOW
    w = SC_WINDOW

    @functools.partial(
        pl.kernel, mesh=_sc_mesh(),
        out_type=jax.ShapeDtypeStruct((n, c), table.dtype),
        scratch_types=[pltpu.VMEM((w,), I32)] * 2 + [pltpu.VMEM((w, c), table.dtype)] * 2
        + [pltpu.SemaphoreType.DMA] * 4,
        name="gather_rows",
    )
    def k(table_hbm, idx_hbm, out_hbm, idx_a, idx_b, rows_a, rows_b, g_a, g_b, w_a, w_b):
        idx, rows, gather_sems, write_sems = (idx_a, idx_b), (rows_a, rows_b), (g_a, g_b), (w_a, w_b)
        wid = lax.axis_index("s") * SC_CORES + lax.axis_index("c")
        base = wid * per_w

        def off(chunk):
            return pl.multiple_of(base + chunk * w, w)

        def gather(buf):
            return pltpu.make_async_copy(table_hbm.at[idx[buf]], rows[buf], gather_sems[buf])

        def write(chunk, buf):
            return pltpu.make_async_copy(rows[buf], out_hbm.at[pl.ds(off(chunk), w)], write_sems[buf])

        def start_gather(chunk, buf):
            pltpu.sync_copy(idx_hbm.at[pl.ds(off(chunk), w)], idx[buf])
            gather(buf).start()

        start_gather(0, 0)

        @pl.loop(0, n_chunks // 2)
        def _(p):
            for buf in range(2):
                chunk = 2 * p + buf

                @pl.when(chunk + 1 < n_chunks)
                def _():
                    @pl.when(chunk >= 1)
                    def _():
                        write(chunk - 1, 1 - buf).wait()
                    start_gather(chunk + 1, 1 - buf)

                gather(buf).wait()
                write(chunk, buf).start()

        write(n_chunks - 2, 0).wait()
        write(n_chunks - 1, 1).wait()

    return k(table, idx)


def _expert_kernel(be_ref, nused_ref, first_ref, slot_ref, next_ref,
                   xs_ref, wg_hbm, bg_ref, wu_hbm, bu_ref, wd_hbm, bd_ref,
                   ys_ref, wg_ref, wu_ref, wd_ref, stage_g, stage_u, stage_d, sems, *, layer):
    b = pl.program_id(0)
    used = b < nused_ref[0]
    weights = ((wg_hbm, stage_g), (wu_hbm, stage_u), (wd_hbm, stage_d))

    def fetch(expert, slot):
        return [pltpu.make_async_copy(hbm.at[layer, expert], stage.at[slot], sems.at[slot, n])
                for n, (hbm, stage) in enumerate(weights)]

    @pl.when(used & (first_ref[b] == 1))
    def _():
        slot = slot_ref[b]

        @pl.when(b == 0)
        def _():
            for cp in fetch(be_ref[0], slot):
                cp.start()

        for cp in fetch(be_ref[b], slot):
            cp.wait()
        wg_ref[...] = stage_g[slot].astype(BF16)
        wu_ref[...] = stage_u[slot].astype(BF16)
        wd_ref[...] = stage_d[slot].astype(BF16)

        @pl.when(next_ref[b] >= 0)
        def _():
            for cp in fetch(next_ref[b], 1 - slot):
                cp.start()

    @pl.when(used)
    def _():
        lo, hi = _unpack_halves(xs_ref[...])
        xb = jnp.concatenate([lo.astype(BF16), hi.astype(BF16)], axis=1)
        a = _dot(xb, wg_ref[...]) + bg_ref[...]
        u = _dot(xb, wu_ref[...]) + bu_ref[...]
        a = jnp.minimum(a, SWIGLU_LIMIT)
        u = jnp.clip(u, -SWIGLU_LIMIT, SWIGLU_LIMIT)
        y = (a * jax.nn.sigmoid(SWIGLU_ALPHA * a)) * (u + 1.0)
        out = _dot(y.astype(BF16), wd_ref[...]) + bd_ref[...]
        ys_ref[...] = _pack_halves(out[:, :HALF], out[:, HALF:])

    @pl.when(b >= nused_ref[0])
    def _():
        ys_ref[...] = jnp.zeros(ys_ref.shape, I32)


def _expert_ffn(xs, block_e, n_used, counts, layer, wg, bg, wu, bu, wd, bd):
    n_rows = xs.shape[0]
    n_blocks = n_rows // ROW_BLOCK

    blk = jnp.arange(n_blocks, dtype=I32)
    used = blk < n_used[0]
    first = used & ((blk == 0) | (block_e != jnp.roll(block_e, 1)))
    slot = (jnp.cumsum(first.astype(I32)) - 1) % 2
    eid = jnp.arange(N_EXPERTS, dtype=I32)
    later = (eid[None, :] > eid[:, None]) & (counts[None, :] > 0)
    next_of_expert = jnp.min(jnp.where(later, eid[None, :], N_EXPERTS), axis=1)
    next_of_expert = jnp.where(next_of_expert == N_EXPERTS, -1, next_of_expert)
    next_e = jnp.sum(jnp.where(block_e[:, None] == eid[None, :], next_of_expert[None, :], 0), axis=1)

    def rows(b, *_):
        return (b, 0)

    def expert(b, be, *_):
        return (layer, be[b], 0, 0)

    hbm = pl.BlockSpec(memory_space=pl.ANY)
    grid_spec = pltpu.PrefetchScalarGridSpec(
        num_scalar_prefetch=5,
        grid=(n_blocks,),
        in_specs=[pl.BlockSpec((ROW_BLOCK, HALF), rows),
                  hbm, pl.BlockSpec((None, None, 1, D_EXPERT), expert),
                  hbm, pl.BlockSpec((None, None, 1, D_EXPERT), expert),
                  hbm, pl.BlockSpec((None, None, 1, D_MODEL), expert)],
        out_specs=pl.BlockSpec((ROW_BLOCK, HALF), rows),
        scratch_shapes=[pltpu.VMEM((D_MODEL, D_EXPERT), BF16), pltpu.VMEM((D_MODEL, D_EXPERT), BF16),
                        pltpu.VMEM((D_EXPERT, D_MODEL), BF16),
                        pltpu.VMEM((2, D_MODEL, D_EXPERT), F32), pltpu.VMEM((2, D_MODEL, D_EXPERT), F32),
                        pltpu.VMEM((2, D_EXPERT, D_MODEL), F32),
                        pltpu.SemaphoreType.DMA((2, 3))],
    )
    return pl.pallas_call(
        functools.partial(_expert_kernel, layer=layer),
        out_shape=jax.ShapeDtypeStruct((n_rows, HALF), I32),
        grid_spec=grid_spec,
        compiler_params=_cparams(("arbitrary",)),
        name="expert_ffn",
    )(block_e, n_used, first.astype(I32), slot.astype(I32), next_e.astype(I32),
      xs, wg, bg, wu, bu, wd, bd)


def _combine_kernel(x1_ref, yg_ref, wcol_ref, p_ref, gple_ref, wpg_ref, wpp_ref, gout_ref, o_ref,
                    *, final):
    x1 = x1_ref[...]
    acc_lo = x1[:, :HALF]
    acc_hi = x1[:, HALF:]
    wcol = wcol_ref[...]
    for kk in range(TOP_K):
        lo, hi = _unpack_halves(yg_ref[kk])
        wk = wcol[:, kk:kk + 1]
        acc_lo = acc_lo + wk * lo
        acc_hi = acc_hi + wk * hi
    x2 = jnp.concatenate([acc_lo, acc_hi], axis=1)
    gate = jax.nn.sigmoid(_dot(_rms(x2, gple_ref[...]).astype(BF16), wpg_ref[...]))
    x3 = x2 + gate * _dot(p_ref[...].astype(BF16), wpp_ref[...])
    o_ref[...] = _rms(x3, gout_ref[...]) if final else x3


def _combine_kernel_inplace(x1_ref, yg_ref, wcol_ref, p_ref, gple_ref, wpg_ref, wpp_ref, gout_ref,
                            prev_ref, o_ref, *, final):
    del prev_ref
    _combine_kernel(x1_ref, yg_ref, wcol_ref, p_ref, gple_ref, wpg_ref, wpp_ref, gout_ref, o_ref,
                    final=final)


def _combine(x1, yg, wcol, p3d, layer, gple, wpg, wpp, gout, final, part, prev):
    t = x1.shape[0]
    tm = TOKEN_TILE
    n_tiles = yg.shape[1] // tm
    first_tile = part * n_tiles

    def row(i, *_):
        return (first_tile + i, 0)

    def full(a):
        return pl.BlockSpec(a.shape, lambda i: (0, 0))

    in_specs = [pl.BlockSpec((tm, D_MODEL), row),
                pl.BlockSpec((TOP_K, tm, HALF), lambda i: (0, i, 0)),
                pl.BlockSpec((tm, HEAD_PAD), row),
                pl.BlockSpec((None, tm, PLE_DIM), lambda i: (layer, first_tile + i, 0)),
                full(gple), full(wpg), full(wpp), full(gout)]
    args = [x1, yg, wcol, p3d, gple, wpg, wpp, gout]
    kern = functools.partial(_combine_kernel, final=final)
    aliases = {}
    if prev is not None:
        in_specs.append(pl.BlockSpec(memory_space=pl.ANY))
        args.append(prev)
        aliases = {len(args) - 1: 0}
        kern = functools.partial(_combine_kernel_inplace, final=final)
    return pl.pallas_call(
        kern,
        out_shape=jax.ShapeDtypeStruct((t, D_MODEL), F32),
        grid=(n_tiles,),
        in_specs=in_specs,
        out_specs=pl.BlockSpec((tm, D_MODEL), row),
        input_output_aliases=aliases,
        compiler_params=_cparams(("parallel",)),
        name="combine_ple",
    )(*args)


def _rope_tables(seq):
    inv_freq = ROPE_THETA ** (-jnp.arange(HALF_ROPE, dtype=F32) * 2.0 / MLA_ROPE)
    ang = jnp.arange(seq, dtype=F32)[:, None] * inv_freq[None, :]
    cos, sin = jnp.cos(ang), jnp.sin(ang)
    ones = jnp.ones((seq, MLA_NOPE), F32)
    zeros16 = jnp.zeros((seq, HALF_ROPE), F32)
    zeros64 = jnp.zeros((seq, MLA_NOPE), F32)
    tail = jnp.ones((seq, HEAD_PAD - MLA_NOPE - MLA_ROPE), F32)
    ztail = jnp.zeros_like(tail)
    cos_t = jnp.concatenate([ones, cos, cos, tail], axis=1)
    sina_t = jnp.concatenate([zeros64, zeros16, sin, ztail], axis=1)
    sinb_t = jnp.concatenate([zeros64, -sin, zeros16, ztail], axis=1)
    return cos_t, sina_t, sinb_t


def _prep_mixer_weights(w_in, w_uq, w_ukv):
    c0 = MLA_Q_LORA + MLA_KV_LORA
    c1 = c0 + MLA_ROPE
    c2 = c1 + len(DIL_GROUPS) * DIL_COLS
    kr_pad = jnp.pad(w_in[:, c0:c1], ((0, 0), (MLA_NOPE, HEAD_PAD - MLA_NOPE - MLA_ROPE)))
    wmla = jnp.concatenate([w_in[:, :c0], kr_pad], axis=1).astype(BF16)
    col = np.arange(len(DIL_GROUPS) * DIL_COLS)
    q_scale = np.where(col % DIL_COLS < DIL_OUT, DIL_HEAD_DIM ** -0.5, 1.0).astype(np.float32)
    wdil = (w_in[:, c1:c2] * q_scale[None, :]).astype(BF16)
    wgate = w_in[:, c2:].astype(BF16)
    pad = HEAD_PAD - MLA_NOPE - MLA_ROPE
    wuq_h = w_uq.reshape(MLA_Q_LORA, MLA_HEADS, MLA_NOPE + MLA_ROPE)
    wuq = jnp.pad(wuq_h, ((0, 0), (0, 0), (0, pad))).reshape(MLA_Q_LORA, MLA_HEADS * HEAD_PAD).astype(BF16)
    wukv_h = w_ukv.reshape(MLA_KV_LORA, MLA_HEADS, MLA_NOPE + MLA_V)
    wuk = jnp.pad(wukv_h[:, :, :MLA_NOPE], ((0, 0), (0, 0), (0, HEAD_PAD - MLA_NOPE)))
    wuk = wuk.reshape(MLA_KV_LORA, MLA_HEADS * HEAD_PAD).astype(BF16)
    wuv = wukv_h[:, :, MLA_NOPE:].reshape(MLA_KV_LORA, MLA_HEADS * MLA_V).astype(BF16)
    return wmla, wdil, wgate, wuq, wuk, wuv


def kernel(x, p, attn_norm, w_in, q_norm, w_uq, kv_norm, w_ukv, w_branch_a, w_branch_b, w_out, ffn_norm, w_router, b_router, w_gate, b_gate, w_up, b_up, w_down, b_down, ple_norm, w_ple_gate, w_ple_proj, final_norm):
    b, s, d = x.shape
    depth = w_in.shape[0]
    t = b * s
    assert d == D_MODEL and s % (DIL_GROUPS[-1][0]) == 0 and t % (SC_WORKERS * SC_WINDOW) == 0
    n_assign = t * TOP_K
    n_blocks = -(-(n_assign + N_EXPERTS * (ROW_BLOCK - 1)) // ROW_BLOCK)
    n_rows = n_blocks * ROW_BLOCK
    cos_t, sina_t, sinb_t = _rope_tables(s)
    xc = x.reshape(t, d)
    for i in range(depth):
        wmla, wdil, wgate, wuq, wuk, wuv = _prep_mixer_weights(w_in[i], w_uq[i], w_ukv[i])
        q, k, vt, zd0, zd1, zd2 = _inproj(
            xc, s, attn_norm[i][None], wmla, wdil, q_norm[i][None], kv_norm[i][None],
            wuq, wuk, wuv, cos_t, sina_t, sinb_t)
        oa = _mla_attention(q.reshape(b, s, -1), k.reshape(b, s, -1), vt)
        ob = _dilated_attention(zd0, zd1, zd2, s)
        x1, hp, topi_t, topw_t, wcol = _merge(
            xc, oa.reshape(t, -1), ob, attn_norm[i][None], wgate,
            w_branch_a[i].astype(BF16), w_branch_b[i].astype(BF16), w_out[i].astype(BF16),
            ffn_norm[i][None], w_router[i].T.astype(BF16), b_router[i][:, None])
        dest_t, meta = _positions(topi_t)
        ends = meta[2, :N_EXPERTS]
        block_start = jnp.arange(n_blocks, dtype=I32) * ROW_BLOCK
        block_e = jnp.minimum(
            jnp.sum((ends[None, :] <= block_start[:, None]).astype(I32), axis=1), N_EXPERTS - 1)
        n_used = (ends[N_EXPERTS - 1:] // ROW_BLOCK).astype(I32)
        dest_flat = dest_t[:TOP_K].reshape(n_assign)
        xs = _dispatch_rows(hp, dest_flat, n_rows)
        ys = _expert_ffn(xs, block_e, n_used, meta[0, :N_EXPERTS], i,
                         w_gate, b_gate[:, :, None, :], w_up, b_up[:, :, None, :],
                         w_down, b_down[:, :, None, :])
        final = i == depth - 1
        gout = final_norm[None] if final else attn_norm[i][None]
        wpg, wpp = w_ple_gate[i].astype(BF16), w_ple_proj[i].astype(BF16)
        tp = t // COMBINE_PARTS
        xc = None
        for part in range(COMBINE_PARTS):
            dest_part = dest_t[:TOP_K, part * tp:(part + 1) * tp].reshape(TOP_K * tp)
            yg = _gather_rows(ys, dest_part).reshape(TOP_K, tp, HALF)
            xc = _combine(x1, yg, wcol, p.reshape(depth, t, PLE_DIM), i, ple_norm[i][None],
                          wpg, wpp, gout, final, part, xc)
    return xc.reshape(b, s, d)
```

```python
import functools
import math

import jax
import jax.numpy as jnp
import numpy as np
from jax import lax
from jax.experimental import pallas as pl
from jax.experimental.pallas import tpu as pltpu
from jax.experimental.pallas import tpu_sc as plsc

F32 = jnp.float32
BF16 = jnp.bfloat16
I32 = jnp.int32

D_MODEL = 1024
PLE_DIM = 256
NORM_EPS = 1e-6

MLA_HEADS = 8
MLA_Q_LORA = 384
MLA_KV_LORA = 256
MLA_NOPE = 64
MLA_ROPE = 32
MLA_V = 64
ROPE_THETA = 10000.0
HEAD_PAD = 128
HALF_ROPE = MLA_ROPE // 2

DIL_GROUPS = ((128, 1), (512, 4), (2048, 16))
DIL_HEADS = 4
DIL_HEAD_DIM = 64
DIL_STEPS = 128
DIL_COLS = 3 * DIL_HEADS * DIL_HEAD_DIM
DIL_OUT = DIL_HEADS * DIL_HEAD_DIM
DIL_UNROLL = 8

N_EXPERTS = 32
TOP_K = 4
D_EXPERT = 1024
SWIGLU_LIMIT = 7.0
SWIGLU_ALPHA = 1.702
ROW_BLOCK = 512

TOKEN_TILE = 512
POSITION_TILE = 4096
POSITION_SUBTILE = 512
COMBINE_PARTS = 2
ATTN_TQ = 1024
ATTN_TK = 1024
ATTN_HEADS = 4
ATTN_SUB = 1024
ATTN_KEY_CHUNKS = 2
HALF = D_MODEL // 2
NEG = -1e30
SPECULATION_HEADROOM = 60.0
HI_MASK = -65536

SC_CORES = 2
SC_SUBCORES = 16
SC_WORKERS = SC_CORES * SC_SUBCORES
SC_WINDOW = 64

VMEM_LIMIT = 56 * 1024 * 1024


def _cparams(sem):
    return pltpu.CompilerParams(dimension_semantics=sem, vmem_limit_bytes=VMEM_LIMIT)


def _rms(x, g):
    return x * lax.rsqrt(jnp.mean(x * x, axis=-1, keepdims=True) + NORM_EPS) * g


def _dot(a, b):
    return jnp.dot(a, b, preferred_element_type=F32)


def _dot_nt(a, b):
    return lax.dot_general(a, b, (((1,), (1,)), ((), ())), preferred_element_type=F32)


def _pack_halves(lo, hi):
    lo_i = lax.bitcast_convert_type(lo.astype(BF16).astype(F32), I32)
    hi_i = lax.bitcast_convert_type(hi.astype(BF16).astype(F32), I32)
    return (hi_i & HI_MASK) | lax.shift_right_logical(lo_i, 16)


def _unpack_halves(w):
    lo = lax.bitcast_convert_type(lax.shift_left(w, 16), F32)
    hi = lax.bitcast_convert_type(w & HI_MASK, F32)
    return lo, hi


def _inproj_kernel(x_ref, g_ref, wmla_ref, wdil_ref, qn_ref, kvn_ref, wuq_ref, wuk_ref,
                   wuv_ref, cos_ref, sina_ref, sinb_ref,
                   q_ref, k_ref, vt_ref, zd0_ref, zd1_ref, zd2_ref, zs_sc):
    h = _rms(x_ref[...], g_ref[...]).astype(BF16)
    zm = _dot(h, wmla_ref[...])
    cq = _rms(zm[:, :MLA_Q_LORA], qn_ref[...]).astype(BF16)
    ckv = _rms(zm[:, MLA_Q_LORA:MLA_Q_LORA + MLA_KV_LORA], kvn_ref[...]).astype(BF16)
    kr = zm[:, MLA_Q_LORA + MLA_KV_LORA:]
    cos, sina, sinb = cos_ref[...], sina_ref[...], sinb_ref[...]

    def rope(t):
        return (t * cos + pltpu.roll(t, HALF_ROPE, 1) * sina
                + pltpu.roll(t, HEAD_PAD - HALF_ROPE, 1) * sinb)

    kr_rot = rope(kr)
    qraw = _dot(cq, wuq_ref[...])
    kraw = _dot(ckv, wuk_ref[...])
    vt_ref[...] = _dot(ckv, wuv_ref[...]).T.astype(BF16)
    scale = (MLA_NOPE + MLA_ROPE) ** -0.5 * math.log2(math.e)
    for hd in range(MLA_HEADS):
        sl = slice(hd * HEAD_PAD, (hd + 1) * HEAD_PAD)
        q_ref[:, sl] = (rope(qraw[:, sl]) * scale).astype(BF16)
        k_ref[:, sl] = (kraw[:, sl] + kr_rot).astype(BF16)
    tm = x_ref.shape[0]
    for gi, zd_ref in enumerate((zd0_ref, zd1_ref, zd2_ref)):
        z = _dot(h, wdil_ref[:, gi * DIL_COLS:(gi + 1) * DIL_COLS])
        window, dil = DIL_GROUPS[gi]
        if dil == 1:
            zd_ref[...] = z.astype(BF16)
            continue
        n_col = DIL_COLS // HEAD_PAD
        for c in range(n_col):
            zs_sc[c] = z[:, c * HEAD_PAD:(c + 1) * HEAD_PAD]
        rows = tm // dil
        part = pl.program_id(0) % (window // tm)
        for r in range(dil):
            dst = pl.ds(pl.multiple_of(r * DIL_STEPS + part * rows, rows), rows)
            for c in range(n_col):
                chunk = zs_sc[c, pl.ds(r, rows, stride=dil), :]
                zd_ref[dst, c * HEAD_PAD:(c + 1) * HEAD_PAD] = chunk.astype(BF16)


def _inproj(x2d, seq, g, wmla, wdil, qn, kvn, wuq, wuk, wuv, cos_t, sina_t, sinb_t):
    t = x2d.shape[0]
    tm = TOKEN_TILE
    n_seq_tiles = seq // tm

    def row(i):
        return (i, 0)

    def const(i):
        return (0, 0)

    def pos(i):
        return (i % n_seq_tiles, 0)

    def full(a):
        return pl.BlockSpec(a.shape, const)

    def vt_block(i):
        return (i // n_seq_tiles, 0, i % n_seq_tiles)

    def rows(width):
        return pl.BlockSpec((tm, width), row)

    def unit_rows(window):
        return pl.BlockSpec((window, DIL_COLS), lambda i: (i // (window // tm), 0))

    out_shape = [
        jax.ShapeDtypeStruct((t, MLA_HEADS * HEAD_PAD), BF16),
        jax.ShapeDtypeStruct((t, MLA_HEADS * HEAD_PAD), BF16),
        jax.ShapeDtypeStruct((t // seq, MLA_HEADS * MLA_V, seq), BF16),
        jax.ShapeDtypeStruct((t, DIL_COLS), BF16),
        jax.ShapeDtypeStruct((t, DIL_COLS), BF16),
        jax.ShapeDtypeStruct((t, DIL_COLS), BF16),
    ]
    return pl.pallas_call(
        _inproj_kernel,
        out_shape=out_shape,
        grid=(t // tm,),
        in_specs=[pl.BlockSpec((tm, D_MODEL), row), full(g), full(wmla), full(wdil),
                  full(qn), full(kvn), full(wuq), full(wuk), full(wuv),
                  pl.BlockSpec((tm, HEAD_PAD), pos), pl.BlockSpec((tm, HEAD_PAD), pos),
                  pl.BlockSpec((tm, HEAD_PAD), pos)],
        out_specs=[rows(D_MODEL), rows(D_MODEL),
                   pl.BlockSpec((None, MLA_HEADS * MLA_V, tm), vt_block),
                   rows(DIL_COLS)] + [unit_rows(window) for window, _ in DIL_GROUPS[1:]],
        scratch_shapes=[pltpu.VMEM((DIL_COLS // HEAD_PAD, tm, HEAD_PAD), F32)],
        compiler_params=_cparams(("arbitrary",)),
        name="inproj",
    )(x2d, g, wmla, wdil, qn, kvn, wuq, wuk, wuv, cos_t, sina_t, sinb_t)


def _mla_kernel(qi_ref, kj_ref, q_ref, k_ref, vt_ref, o_ref, m_sc, l_sc, acc_sc, redo_sc):
    p = pl.program_id(2)
    i = qi_ref[p]
    j = kj_ref[p]
    tq = q_ref.shape[0]
    tk = k_ref.shape[0]

    @pl.when(j == 0)
    def _():
        m_sc[...] = jnp.full(m_sc.shape, NEG, F32)
        l_sc[...] = jnp.zeros(l_sc.shape, F32)
        acc_sc[...] = jnp.zeros(acc_sc.shape, F32)

    ratio = tq // tk
    sub = ATTN_SUB

    def step(diagonal, speculative):
        chains = [(hh, c) for hh in range(ATTN_HEADS) for c in range(tq // sub)]
        ones_rows = (lax.broadcasted_iota(I32, (16, tk), 0) == 0).astype(BF16)
        state = {}
        for hh, c in chains:
            cs = slice(c * sub, (c + 1) * sub)
            state[hh, c] = (m_sc[hh, :, cs], l_sc[hh, :, cs], acc_sc[hh, :, cs])
        new_state = {}
        within = None
        def scores(hh, c, rows, q0=0):
            sl = slice(hh * HEAD_PAD, (hh + 1) * HEAD_PAD)
            st = _dot_nt(k_ref[rows, sl], q_ref[c * sub + q0:(c + 1) * sub, sl])
            if diagonal:
                shape = (rows.stop - rows.start, sub - q0)
                key = lax.broadcasted_iota(I32, shape, 0) + (j * tk + rows.start)
                qry = lax.broadcasted_iota(I32, shape, 1) + (i * tq + c * sub + q0)
                st = jnp.where(qry >= key, st, NEG)
            return st

        def values(hh, rows):
            ones_row = (lax.broadcasted_iota(I32, (16, rows.stop - rows.start), 0) == 0).astype(BF16)
            return jnp.concatenate([vt_ref[hh * MLA_V:(hh + 1) * MLA_V, rows], ones_row], axis=0)

        for hh, c in chains:
            m_prev, l_prev, acc_prev = state[hh, c]
            if speculative:
                kc = tk // ATTN_KEY_CHUNKS
                m_blk, pv = None, None
                for n in range(ATTN_KEY_CHUNKS):
                    rows = slice(n * kc, (n + 1) * kc)
                    q0 = n * kc if (diagonal and tq == tk and sub == tq) else 0
                    st = scores(hh, c, rows, q0)
                    if n == 0:
                        m_prev = jnp.where(j == 0, st[:1], m_prev)
                    m_part = jnp.max(st, axis=0, keepdims=True)
                    part = _dot(values(hh, rows), jnp.exp2(st - m_prev[:, q0:]).astype(BF16))
                    if q0:
                        m_part = jnp.concatenate([jnp.full((1, q0), NEG, F32), m_part], axis=1)
                        part = jnp.concatenate([jnp.zeros((part.shape[0], q0), F32), part], axis=1)
                    m_blk = m_part if m_blk is None else jnp.maximum(m_blk, m_part)
                    pv = part if pv is None else pv + part
                m_new = jnp.maximum(m_prev, m_blk)
                alpha = jnp.exp2(m_prev - m_new)
                l_new = alpha * (l_prev + pv[MLA_V:MLA_V + 1])
                acc_new = alpha * (acc_prev + pv[:MLA_V])
                ok = jnp.max(m_blk - m_prev) <= SPECULATION_HEADROOM
                within = ok if within is None else (within & ok)
            else:
                rows = slice(0, tk)
                st = scores(hh, c, rows)
                m_new = jnp.maximum(m_prev, jnp.max(st, axis=0, keepdims=True))
                alpha = jnp.exp2(m_prev - m_new)
                pv = _dot(values(hh, rows), jnp.exp2(st - m_new).astype(BF16))
                l_new = alpha * l_prev + pv[MLA_V:MLA_V + 1]
                acc_new = alpha * acc_prev + pv[:MLA_V]
            new_state[hh, c] = (m_new, l_new, acc_new)

        def commit():
            for hh, c in chains:
                cs = slice(c * sub, (c + 1) * sub)
                m_sc[hh, :, cs], l_sc[hh, :, cs], acc_sc[hh, :, cs] = new_state[hh, c]

        if speculative:
            pl.when(within)(commit)
            redo_sc[0] = jnp.logical_not(within).astype(I32)
        else:
            commit()

    redo_sc[0] = 0
    on_diagonal = j >= ratio * i

    @pl.when(jnp.logical_not(on_diagonal))
    def _():
        step(False, True)

    @pl.when(on_diagonal)
    def _():
        step(True, True)

    @pl.when(redo_sc[0] != 0)
    def _():
        step(True, False)

    @pl.when(j == ratio * i + (ratio - 1))
    def _():
        ot = jnp.concatenate([acc_sc[hh] / l_sc[hh] for hh in range(ATTN_HEADS)], axis=0)
        o_ref[...] = ot.T.astype(BF16)


def _mla_attention(q, k, vt):
    b, s, _ = q.shape
    tq, tk, nh = ATTN_TQ, ATTN_TK, ATTN_HEADS
    ratio = tq // tk
    nq = s // tq
    pairs = [(i, j) for i in range(nq) for j in range(ratio * (i + 1))]
    qi = jnp.asarray([p[0] for p in pairs], I32)
    kj = jnp.asarray([p[1] for p in pairs], I32)
    grid_spec = pltpu.PrefetchScalarGridSpec(
        num_scalar_prefetch=2,
        grid=(b, MLA_HEADS // nh, len(pairs)),
        in_specs=[
            pl.BlockSpec((None, tq, nh * HEAD_PAD), lambda bb, hp, p, qi, kj: (bb, qi[p], hp)),
            pl.BlockSpec((None, tk, nh * HEAD_PAD), lambda bb, hp, p, qi, kj: (bb, kj[p], hp)),
            pl.BlockSpec((None, nh * MLA_V, tk), lambda bb, hp, p, qi, kj: (bb, hp, kj[p])),
        ],
        out_specs=pl.BlockSpec((None, tq, nh * MLA_V), lambda bb, hp, p, qi, kj: (bb, qi[p], hp)),
        scratch_shapes=[pltpu.VMEM((nh, 1, tq), F32), pltpu.VMEM((nh, 1, tq), F32),
                        pltpu.VMEM((nh, MLA_V, tq), F32), pltpu.SMEM((1,), I32)],
    )
    return pl.pallas_call(
        _mla_kernel,
        out_shape=jax.ShapeDtypeStruct((b, s, MLA_HEADS * MLA_V), BF16),
        grid_spec=grid_spec,
        compiler_params=_cparams(("parallel", "parallel", "arbitrary")),
        name="mla_attention",
    )(qi, kj, q, k, vt)


def _alibi_slopes(n):
    def pow2(m):
        start = 2.0 ** (-8.0 / m)
        return [start ** (i + 1) for i in range(m)]
    if math.log2(n).is_integer():
        s = pow2(n)
    else:
        c = 2 ** int(math.floor(math.log2(n)))
        s = pow2(c) + pow2(2 * c)[0::2][: n - c]
    return np.array(sorted(s, reverse=True), dtype=np.float32)


def _dilated_block(cur, prev, bias4, first):
    n = DIL_STEPS
    hw = DIL_OUT
    q = cur[:, :hw]
    kk = jnp.concatenate([prev[:, hw:2 * hw], cur[:, hw:2 * hw]], axis=0)
    vv = jnp.concatenate([prev[:, 2 * hw:], cur[:, 2 * hw:]], axis=0)
    head_of_lane = lax.broadcasted_iota(I32, (n, hw), 1) // DIL_HEAD_DIM
    zero = jnp.zeros_like(q)
    q4 = jnp.concatenate([jnp.where(head_of_lane == h, q, zero) for h in range(DIL_HEADS)], axis=0)
    s4 = _dot_nt(q4, kk) + bias4
    if first is not None:
        ki = lax.broadcasted_iota(I32, (DIL_HEADS * n, 2 * n), 1)
        s4 = jnp.where(first & (ki < n), NEG, s4)
    m4 = jnp.max(s4, axis=1, keepdims=True)
    p4 = jnp.exp(s4 - m4).astype(BF16)
    l4 = _dot(p4, jnp.ones((2 * n, HEAD_PAD), BF16))
    pv4 = _dot(p4, vv)
    m4 = jnp.broadcast_to(m4, (DIL_HEADS * n, HEAD_PAD))

    def rows(a, h):
        return a[h * n:(h + 1) * n]

    o_un = rows(pv4, DIL_HEADS - 1)
    for h in range(DIL_HEADS - 2, -1, -1):
        o_un = jnp.where(head_of_lane == h, rows(pv4, h), o_un)
    low = lax.broadcasted_iota(I32, (n, HEAD_PAD), 1) < DIL_HEAD_DIM

    def per_lane(a):
        return jnp.concatenate([jnp.where(low, rows(a, 0), rows(a, 1)),
                                jnp.where(low, rows(a, 2), rows(a, 3))], axis=1)

    l_sel = per_lane(l4)
    return o_un / l_sel, per_lane(m4) + jnp.log(l_sel)


def _dilated_kernel(c0_ref, h0_ref, c1_ref, h1_ref, c2_ref, h2_ref, ob_ref, o_sc, l_sc, bias_sc,
                    *, slopes):
    u = pl.program_id(1)
    n = DIL_STEPS
    unit = ob_ref.shape[0]
    n_sb = unit // n
    first = u == 0
    qi = lax.broadcasted_iota(I32, (n, 2 * n), 0)
    ki = lax.broadcasted_iota(I32, (n, 2 * n), 1)
    dist = qi + n - ki
    valid = (dist >= 0) & (dist <= n)
    distf = dist.astype(F32)
    for gi in range(len(DIL_GROUPS)):
        for h in range(DIL_HEADS):
            bias_sc[gi, h * n:(h + 1) * n, :] = jnp.where(valid, -slopes[gi][h] * distf, NEG)

    def rows_of(ref, sb):
        return ref[pl.ds(pl.multiple_of(sb * n, n), n), :]

    def emit(gi, start, stride, o, lse):
        if isinstance(start, int):
            idx = pl.ds(start, n)
        elif stride == 1:
            idx = pl.ds(pl.multiple_of(start, n), n)
        else:
            idx = pl.ds(start, n, stride=stride)
        for half in range(DIL_OUT // HEAD_PAD):
            ls = slice(half * HEAD_PAD, (half + 1) * HEAD_PAD)
            o_sc[gi, half, idx, :] = o[:, ls]
            l_sc[gi, half, idx, :] = lse[:, ls]

    emit(0, 0, 1, *_dilated_block(c0_ref[:n, :], h0_ref[...], bias_sc[0], first))

    def g0_body(sb, carry):
        emit(0, sb * n, 1, *_dilated_block(rows_of(c0_ref, sb), rows_of(c0_ref, sb - 1), bias_sc[0], None))
        return carry

    lax.fori_loop(1, n_sb, g0_body, 0, unroll=DIL_UNROLL)

    d1 = DIL_GROUPS[1][1]

    def g1_head(r, carry):
        emit(1, r, d1, *_dilated_block(rows_of(c1_ref, r), rows_of(h1_ref, r), bias_sc[1], first))
        return carry

    def g1_body(sb, carry):
        start = (sb // d1) * (d1 * n) + sb % d1
        emit(1, start, d1, *_dilated_block(rows_of(c1_ref, sb), rows_of(c1_ref, sb - d1), bias_sc[1], None))
        return carry

    lax.fori_loop(0, d1, g1_head, 0, unroll=DIL_UNROLL)
    lax.fori_loop(d1, n_sb, g1_body, 0, unroll=DIL_UNROLL)

    d2 = DIL_GROUPS[2][1]

    def g2_body(r, carry):
        emit(2, r, d2, *_dilated_block(rows_of(c2_ref, r), rows_of(h2_ref, r), bias_sc[2], first))
        return carry

    lax.fori_loop(0, n_sb, g2_body, 0, unroll=DIL_UNROLL)

    def merge_body(c, carry):
        idx = pl.ds(pl.multiple_of(c * n, n), n)
        for half in range(DIL_OUT // HEAD_PAD):
            l0, l1, l2 = l_sc[0, half, idx, :], l_sc[1, half, idx, :], l_sc[2, half, idx, :]
            lmax = jnp.maximum(jnp.maximum(l0, l1), l2)
            e0, e1, e2 = jnp.exp(l0 - lmax), jnp.exp(l1 - lmax), jnp.exp(l2 - lmax)
            ob = (e0 * o_sc[0, half, idx, :] + e1 * o_sc[1, half, idx, :]
                  + e2 * o_sc[2, half, idx, :]) / (e0 + e1 + e2)
            ob_ref[idx, half * HEAD_PAD:(half + 1) * HEAD_PAD] = ob.astype(BF16)
        return carry

    lax.fori_loop(0, n_sb, merge_body, 0)


def _dilated_attention(zd0, zd1, zd2, seq):
    t = zd0.shape[0]
    unit = DIL_GROUPS[-1][0]
    upb = seq // unit
    n = DIL_STEPS
    u1 = DIL_GROUPS[1][0]
    all_slopes = _alibi_slopes(len(DIL_GROUPS) * DIL_HEADS).reshape(len(DIL_GROUPS), DIL_HEADS)
    slopes = tuple(tuple(float(x) * dil for x in all_slopes[gi]) for gi, (_, dil) in enumerate(DIL_GROUPS))

    def cur(bb, u):
        return (bb * upb + u, 0)

    def halo(rows):
        per_unit = unit // rows
        return lambda bb, u: ((bb * upb) * per_unit + jnp.maximum(u * per_unit - 1, 0), 0)

    return pl.pallas_call(
        functools.partial(_dilated_kernel, slopes=slopes),
        out_shape=jax.ShapeDtypeStruct((t, DIL_OUT), BF16),
        grid=(t // seq, upb),
        in_specs=[pl.BlockSpec((unit, DIL_COLS), cur), pl.BlockSpec((n, DIL_COLS), halo(n)),
                  pl.BlockSpec((unit, DIL_COLS), cur), pl.BlockSpec((u1, DIL_COLS), halo(u1)),
                  pl.BlockSpec((unit, DIL_COLS), cur), pl.BlockSpec((unit, DIL_COLS), halo(unit))],
        out_specs=pl.BlockSpec((unit, DIL_OUT), cur),
        scratch_shapes=[pltpu.VMEM((len(DIL_GROUPS), DIL_OUT // HEAD_PAD, unit, HEAD_PAD), F32),
                        pltpu.VMEM((len(DIL_GROUPS), DIL_OUT // HEAD_PAD, unit, HEAD_PAD), F32),
                        pltpu.VMEM((len(DIL_GROUPS), DIL_HEADS * n, 2 * n), F32)],
        compiler_params=_cparams(("parallel", "arbitrary")),
        name="dilated_attention",
    )(zd0, zd0, zd1, zd1, zd2, zd2)


def _merge_kernel(x_ref, oa_ref, ob_ref, ga_ref, wgate_ref,
                  wa_ref, wb_ref, wo_ref, g_ref, wr_ref, br_ref,
                  x1_ref, hp_ref, topi_ref, wcol_ref):
    tm = x_ref.shape[0]
    x = x_ref[...]
    h = _rms(x, ga_ref[...]).astype(BF16)
    mixed = (jax.nn.sigmoid(_dot(h, wgate_ref[:, :D_MODEL])) * _dot(oa_ref[...], wa_ref[...])
             + jax.nn.sigmoid(_dot(h, wgate_ref[:, D_MODEL:])) * _dot(ob_ref[...], wb_ref[...]))
    x1 = x + _dot(mixed.astype(BF16), wo_ref[...])
    x1_ref[...] = x1
    h2 = _rms(x1, g_ref[...])
    hp_ref[...] = _pack_halves(h2[:, :HALF], h2[:, HALF:])

    logits = _dot_nt(wr_ref[...], h2.astype(BF16)) + br_ref[...]
    eidx = lax.broadcasted_iota(I32, (N_EXPERTS, tm), 0)
    vals, idxs = [], []
    for _ in range(TOP_K):
        m = jnp.max(logits, axis=0, keepdims=True)
        idx = jnp.min(jnp.where(logits == m, eidx, N_EXPERTS), axis=0, keepdims=True)
        vals.append(m)
        idxs.append(idx)
        logits = jnp.where(eidx == idx, -jnp.inf, logits)
    exps = [jnp.exp(vk - vals[0]) for vk in vals]
    den = exps[0] + exps[1] + exps[2] + exps[3]
    row8 = lax.broadcasted_iota(I32, (8, tm), 0)
    row128 = lax.broadcasted_iota(I32, (HEAD_PAD, tm), 0)
    topi = jnp.zeros((8, tm), I32)
    wide = jnp.zeros((HEAD_PAD, tm), F32)
    for kk in range(TOP_K):
        topi = jnp.where(row8 == kk, idxs[kk], topi)
        wide = jnp.where(row128 == kk, exps[kk] / den, wide)
    topi_ref[...] = topi
    wcol_ref[...] = wide.T


def _merge(x2d, oa, ob, g_attn, wgate, wa, wb, wo, g, wr_t, br_col):
    t = x2d.shape[0]
    tm = TOKEN_TILE

    def row(i):
        return (i, 0)

    def col(i):
        return (0, i)

    def full(a):
        return pl.BlockSpec(a.shape, lambda i: (0, 0))

    def rows(width):
        return pl.BlockSpec((tm, width), row)

    out_shape = [
        jax.ShapeDtypeStruct((t, D_MODEL), F32),
        jax.ShapeDtypeStruct((t, HALF), I32),
        jax.ShapeDtypeStruct((8, t), I32),
        jax.ShapeDtypeStruct((t, HEAD_PAD), F32),
    ]
    return pl.pallas_call(
        _merge_kernel,
        out_shape=out_shape,
        grid=(t // tm,),
        in_specs=[rows(D_MODEL), rows(MLA_HEADS * MLA_V), rows(DIL_OUT), full(g_attn), full(wgate)]
        + [full(wa), full(wb), full(wo), full(g), full(wr_t), full(br_col)],
        out_specs=[rows(D_MODEL), rows(HALF), pl.BlockSpec((8, tm), col), rows(HEAD_PAD)],
        compiler_params=_cparams(("parallel",)),
        name="merge_router",
    )(x2d, oa, ob, g_attn, wgate, wa, wb, wo, g, wr_t, br_col)


def _positions_kernel(topi_ref, dest_ref, meta_ref, cnt_sc, carry_sc, start_sc):
    ps = pl.program_id(0)
    i = pl.program_id(1)
    tm = POSITION_SUBTILE
    n_sub = topi_ref.shape[1] // tm
    eidx = lax.broadcasted_iota(I32, (N_EXPERTS, tm), 0)

    def hits_of(sb):
        topi = topi_ref[:, sb * tm:(sb + 1) * tm]
        return [eidx == topi[kk:kk + 1, :] for kk in range(TOP_K)]

    def members(hits):
        return hits[0] | hits[1] | hits[2] | hits[3]

    @pl.when((ps == 0) & (i == 0))
    def _():
        cnt_sc[...] = jnp.zeros(cnt_sc.shape, F32)

    @pl.when(ps == 0)
    def _():
        total = cnt_sc[...]
        for sb in range(n_sub):
            total = total + jnp.sum(members(hits_of(sb)).astype(F32), axis=1, keepdims=True)
        cnt_sc[...] = total

    @pl.when((ps == 1) & (i == 0))
    def _():
        cnt = cnt_sc[...].astype(I32)
        shift = ROW_BLOCK.bit_length() - 1
        padded = lax.shift_left(lax.shift_right_logical(cnt + (ROW_BLOCK - 1), shift), shift)
        sub = lax.broadcasted_iota(I32, (N_EXPERTS, HEAD_PAD), 0)
        lane = lax.broadcasted_iota(I32, (N_EXPERTS, HEAD_PAD), 1)
        padded_row = jnp.sum(jnp.where(sub == lane, padded, 0), axis=0, keepdims=True)
        start = jnp.sum(jnp.where(lane < sub, padded_row, 0), axis=1, keepdims=True)
        start_sc[...] = start.astype(F32)
        carry_sc[...] = jnp.zeros(carry_sc.shape, F32)
        cnt_row = jnp.sum(jnp.where(sub == lane, cnt, 0), axis=0, keepdims=True)
        start_row = jnp.sum(jnp.where(sub == lane, start, 0), axis=0, keepdims=True)
        row8 = lax.broadcasted_iota(I32, (8, HEAD_PAD), 0)
        meta = jnp.where(row8 == 0, cnt_row, 0)
        meta = jnp.where(row8 == 1, start_row, meta)
        meta = jnp.where(row8 == 2, start_row + padded_row, meta)
        meta_ref[...] = meta

    @pl.when(ps == 1)
    def _():
        tr = lax.broadcasted_iota(I32, (tm, tm), 0)
        tc = lax.broadcasted_iota(I32, (tm, tm), 1)
        before = (tr < tc).astype(BF16)
        row8 = lax.broadcasted_iota(I32, (8, tm), 0)
        offset = carry_sc[...] + start_sc[...]
        for sb in range(n_sub):
            hits = hits_of(sb)
            member = members(hits)
            base = _dot(member.astype(BF16), before) + offset
            dest = jnp.zeros((8, tm), I32)
            for kk in range(TOP_K):
                dk = jnp.sum(jnp.where(hits[kk], base, 0.0), axis=0, keepdims=True).astype(I32)
                dest = jnp.where(row8 == kk, dk, dest)
            dest_ref[:, sb * tm:(sb + 1) * tm] = dest
            offset = offset + jnp.sum(member.astype(F32), axis=1, keepdims=True)
        carry_sc[...] = offset - start_sc[...]


def _positions(topi_t):
    t = topi_t.shape[1]
    tm = min(POSITION_TILE, t)
    return pl.pallas_call(
        _positions_kernel,
        out_shape=[jax.ShapeDtypeStruct((8, t), I32), jax.ShapeDtypeStruct((8, HEAD_PAD), I32)],
        grid=(2, t // tm),
        in_specs=[pl.BlockSpec((8, tm), lambda ps, i: (0, i))],
        out_specs=[pl.BlockSpec((8, tm), lambda ps, i: (0, i * ps)),
                   pl.BlockSpec((8, HEAD_PAD), lambda ps, i: (0, 0))],
        scratch_shapes=[pltpu.VMEM((N_EXPERTS, 1), F32)] * 3,
        compiler_params=_cparams(("arbitrary", "arbitrary")),
        name="routing_positions",
    )(topi_t)


def _sc_mesh():
    return plsc.VectorSubcoreMesh(core_axis_name="c", subcore_axis_name="s")


def _dispatch_rows(table, dest_flat, n_rows):
    t, c = table.shape
    n_slots = dest_flat.shape[0] // t
    per_w = t // SC_WORKERS
    assert per_w * SC_WORKERS == t and per_w % (2 * SC_WINDOW) == 0
    n_chunks = per_w // SC_WINDOW
    w = SC_WINDOW

    @functools.partial(
        pl.kernel, mesh=_sc_mesh(),
        out_type=jax.ShapeDtypeStruct((n_rows, c), table.dtype),
        scratch_types=[pltpu.VMEM((w,), I32)] * n_slots + [pltpu.VMEM((w, c), table.dtype)] * 2
        + [pltpu.SemaphoreType.DMA] * (n_slots + 2),
        name="dispatch_rows",
    )
    def k(table_hbm, dest_hbm, out_hbm, *scratch):
        idx = scratch[:n_slots]
        rows = scratch[n_slots:n_slots + 2]
        scatter_sems = scratch[n_slots + 2:2 * n_slots + 2]
        read_sems = scratch[2 * n_slots + 2:]
        wid = lax.axis_index("s") * SC_CORES + lax.axis_index("c")
        base = wid * per_w

        def off(chunk):
            return pl.multiple_of(base + chunk * w, w)

        def read(chunk, buf):
            return pltpu.make_async_copy(table_hbm.at[pl.ds(off(chunk), w)], rows[buf], read_sems[buf])

        def scatter(kk, buf):
            return pltpu.make_async_copy(rows[buf], out_hbm.at[idx[kk]], scatter_sems[kk])

        read(0, 0).start()

        @pl.loop(0, n_chunks // 2)
        def _(p):
            for buf in range(2):
                chunk = 2 * p + buf

                @pl.when(chunk + 1 < n_chunks)
                def _():
                    read(chunk + 1, 1 - buf).start()

                read(chunk, buf).wait()
                for kk in range(n_slots):
                    src = pl.multiple_of(kk * t + off(chunk), w)
                    pltpu.sync_copy(dest_hbm.at[pl.ds(src, w)], idx[kk])
                    scatter(kk, buf).start()
                for kk in range(n_slots):
                    scatter(kk, buf).wait()

    return k(table, dest_flat)


def _gather_rows(table, idx):
    n = idx.shape[0]
    c = table.shape[1]
    per_w = n // SC_WORKERS
    assert per_w * SC_WORKERS == n and per_w % (2 * SC_WINDOW) == 0
    n_chunks = per_w // SC_WINDOW
    w = SC_WINDOW

    @functools.partial(
        pl.kernel, mesh=_sc_mesh(),
        out_type=jax.ShapeDtypeStruct((n, c), table.dtype),
        scratch_types=[pltpu.VMEM((w,), I32)] * 2 + [pltpu.VMEM((w, c), table.dtype)] * 2
        + [pltpu.SemaphoreType.DMA] * 4,
        name="gather_rows",
    )
    def k(table_hbm, idx_hbm, out_hbm, idx_a, idx_b, rows_a, rows_b, g_a, g_b, w_a, w_b):
        idx, rows, gather_sems, write_sems = (idx_a, idx_b), (rows_a, rows_b), (g_a, g_b), (w_a, w_b)
        wid = lax.axis_index("s") * SC_CORES + lax.axis_index("c")
        base = wid * per_w

        def off(chunk):
            return pl.multiple_of(base + chunk * w, w)

        def gather(buf):
            return pltpu.make_async_copy(table_hbm.at[idx[buf]], rows[buf], gather_sems[buf])

        def write(chunk, buf):
            return pltpu.make_async_copy(rows[buf], out_hbm.at[pl.ds(off(chunk), w)], write_sems[buf])

        def start_gather(chunk, buf):
            pltpu.sync_copy(idx_hbm.at[pl.ds(off(chunk), w)], idx[buf])
            gather(buf).start()

        start_gather(0, 0)

        @pl.loop(0, n_chunks // 2)
        def _(p):
            for buf in range(2):
                chunk = 2 * p + buf

                @pl.when(chunk + 1 < n_chunks)
                def _():
                    @pl.when(chunk >= 1)
                    def _():
                        write(chunk - 1, 1 - buf).wait()
                    start_gather(chunk + 1, 1 - buf)

                gather(buf).wait()
                write(chunk, buf).start()

        write(n_chunks - 2, 0).wait()
        write(n_chunks - 1, 1).wait()

    return k(table, idx)


def _expert_kernel(be_ref, nused_ref, first_ref, slot_ref, next_ref,
                   xs_ref, wg_hbm, bg_ref, wu_hbm, bu_ref, wd_hbm, bd_ref,
                   ys_ref, wg_ref, wu_ref, wd_ref, stage_g, stage_u, stage_d, sems, *, layer):
    b = pl.program_id(0)
    used = b < nused_ref[0]
    weights = ((wg_hbm, stage_g), (wu_hbm, stage_u), (wd_hbm, stage_d))

    def fetch(expert, slot):
        return [pltpu.make_async_copy(hbm.at[layer, expert], stage.at[slot], sems.at[slot, n])
                for n, (hbm, stage) in enumerate(weights)]

    @pl.when(used & (first_ref[b] == 1))
    def _():
        slot = slot_ref[b]

        @pl.when(b == 0)
        def _():
            for cp in fetch(be_ref[0], slot):
                cp.start()

        for cp in fetch(be_ref[b], slot):
            cp.wait()
        wg_ref[...] = stage_g[slot].astype(BF16)
        wu_ref[...] = stage_u[slot].astype(BF16)
        wd_ref[...] = stage_d[slot].astype(BF16)

        @pl.when(next_ref[b] >= 0)
        def _():
            for cp in fetch(next_ref[b], 1 - slot):
                cp.start()

    @pl.when(used)
    def _():
        lo, hi = _unpack_halves(xs_ref[...])
        xb = jnp.concatenate([lo.astype(BF16), hi.astype(BF16)], axis=1)
        a = _dot(xb, wg_ref[...]) + bg_ref[...]
        u = _dot(xb, wu_ref[...]) + bu_ref[...]
        a = jnp.minimum(a, SWIGLU_LIMIT)
        u = jnp.clip(u, -SWIGLU_LIMIT, SWIGLU_LIMIT)
        y = (a * jax.nn.sigmoid(SWIGLU_ALPHA * a)) * (u + 1.0)
        out = _dot(y.astype(BF16), wd_ref[...]) + bd_ref[...]
        ys_ref[...] = _pack_halves(out[:, :HALF], out[:, HALF:])

    @pl.when(b >= nused_ref[0])
    def _():
        ys_ref[...] = jnp.zeros(ys_ref.shape, I32)


def _expert_ffn(xs, block_e, n_used, counts, layer, wg, bg, wu, bu, wd, bd):
    n_rows = xs.shape[0]
    n_blocks = n_rows // ROW_BLOCK

    blk = jnp.arange(n_blocks, dtype=I32)
    used = blk < n_used[0]
    first = used & ((blk == 0) | (block_e != jnp.roll(block_e, 1)))
    slot = (jnp.cumsum(first.astype(I32)) - 1) % 2
    eid = jnp.arange(N_EXPERTS, dtype=I32)
    later = (eid[None, :] > eid[:, None]) & (counts[None, :] > 0)
    next_of_expert = jnp.min(jnp.where(later, eid[None, :], N_EXPERTS), axis=1)
    next_of_expert = jnp.where(next_of_expert == N_EXPERTS, -1, next_of_expert)
    next_e = jnp.sum(jnp.where(block_e[:, None] == eid[None, :], next_of_expert[None, :], 0), axis=1)

    def rows(b, *_):
        return (b, 0)

    def expert(b, be, *_):
        return (layer, be[b], 0, 0)

    hbm = pl.BlockSpec(memory_space=pl.ANY)
    grid_spec = pltpu.PrefetchScalarGridSpec(
        num_scalar_prefetch=5,
        grid=(n_blocks,),
        in_specs=[pl.BlockSpec((ROW_BLOCK, HALF), rows),
                  hbm, pl.BlockSpec((None, None, 1, D_EXPERT), expert),
                  hbm, pl.BlockSpec((None, None, 1, D_EXPERT), expert),
                  hbm, pl.BlockSpec((None, None, 1, D_MODEL), expert)],
        out_specs=pl.BlockSpec((ROW_BLOCK, HALF), rows),
        scratch_shapes=[pltpu.VMEM((D_MODEL, D_EXPERT), BF16), pltpu.VMEM((D_MODEL, D_EXPERT), BF16),
                        pltpu.VMEM((D_EXPERT, D_MODEL), BF16),
                        pltpu.VMEM((2, D_MODEL, D_EXPERT), F32), pltpu.VMEM((2, D_MODEL, D_EXPERT), F32),
                        pltpu.VMEM((2, D_EXPERT, D_MODEL), F32),
                        pltpu.SemaphoreType.DMA((2, 3))],
    )
    return pl.pallas_call(
        functools.partial(_expert_kernel, layer=layer),
        out_shape=jax.ShapeDtypeStruct((n_rows, HALF), I32),
        grid_spec=grid_spec,
        compiler_params=_cparams(("arbitrary",)),
        name="expert_ffn",
    )(block_e, n_used, first.astype(I32), slot.astype(I32), next_e.astype(I32),
      xs, wg, bg, wu, bu, wd, bd)


def _combine_kernel(x1_ref, yg_ref, wcol_ref, p_ref, gple_ref, wpg_ref, wpp_ref, gout_ref, o_ref,
                    *, final):
    x1 = x1_ref[...]
    acc_lo = x1[:, :HALF]
    acc_hi = x1[:, HALF:]
    wcol = wcol_ref[...]
    for kk in range(TOP_K):
        lo, hi = _unpack_halves(yg_ref[kk])
        wk = wcol[:, kk:kk + 1]
        acc_lo = acc_lo + wk * lo
        acc_hi = acc_hi + wk * hi
    x2 = jnp.concatenate([acc_lo, acc_hi], axis=1)
    gate = jax.nn.sigmoid(_dot(_rms(x2, gple_ref[...]).astype(BF16), wpg_ref[...]))
    x3 = x2 + gate * _dot(p_ref[...].astype(BF16), wpp_ref[...])
    o_ref[...] = _rms(x3, gout_ref[...]) if final else x3


def _combine_kernel_inplace(x1_ref, yg_ref, wcol_ref, p_ref, gple_ref, wpg_ref, wpp_ref, gout_ref,
                            prev_ref, o_ref, *, final):
    del prev_ref
    _combine_kernel(x1_ref, yg_ref, wcol_ref, p_ref, gple_ref, wpg_ref, wpp_ref, gout_ref, o_ref,
                    final=final)


def _combine(x1, yg, wcol, p3d, layer, gple, wpg, wpp, gout, final, part, prev):
    t = x1.shape[0]
    tm = TOKEN_TILE
    n_tiles = yg.shape[1] // tm
    first_tile = part * n_tiles

    def row(i, *_):
        return (first_tile + i, 0)

    def full(a):
        return pl.BlockSpec(a.shape, lambda i: (0, 0))

    in_specs = [pl.BlockSpec((tm, D_MODEL), row),
                pl.BlockSpec((TOP_K, tm, HALF), lambda i: (0, i, 0)),
                pl.BlockSpec((tm, HEAD_PAD), row),
                pl.BlockSpec((None, tm, PLE_DIM), lambda i: (layer, first_tile + i, 0)),
                full(gple), full(wpg), full(wpp), full(gout)]
    args = [x1, yg, wcol, p3d, gple, wpg, wpp, gout]
    kern = functools.partial(_combine_kernel, final=final)
    aliases = {}
    if prev is not None:
        in_specs.append(pl.BlockSpec(memory_space=pl.ANY))
        args.append(prev)
        aliases = {len(args) - 1: 0}
        kern = functools.partial(_combine_kernel_inplace, final=final)
    return pl.pallas_call(
        kern,
        out_shape=jax.ShapeDtypeStruct((t, D_MODEL), F32),
        grid=(n_tiles,),
        in_specs=in_specs,
        out_specs=pl.BlockSpec((tm, D_MODEL), row),
        input_output_aliases=aliases,
        compiler_params=_cparams(("parallel",)),
        name="combine_ple",
    )(*args)


def _rope_tables(seq):
    inv_freq = ROPE_THETA ** (-jnp.arange(HALF_ROPE, dtype=F32) * 2.0 / MLA_ROPE)
    ang = jnp.arange(seq, dtype=F32)[:, None] * inv_freq[None, :]
    cos, sin = jnp.cos(ang), jnp.sin(ang)
    ones = jnp.ones((seq, MLA_NOPE), F32)
    zeros16 = jnp.zeros((seq, HALF_ROPE), F32)
    zeros64 = jnp.zeros((seq, MLA_NOPE), F32)
    tail = jnp.ones((seq, HEAD_PAD - MLA_NOPE - MLA_ROPE), F32)
    ztail = jnp.zeros_like(tail)
    cos_t = jnp.concatenate([ones, cos, cos, tail], axis=1)
    sina_t = jnp.concatenate([zeros64, zeros16, sin, ztail], axis=1)
    sinb_t = jnp.concatenate([zeros64, -sin, zeros16, ztail], axis=1)
    return cos_t, sina_t, sinb_t


def _prep_mixer_weights(w_in, w_uq, w_ukv):
    c0 = MLA_Q_LORA + MLA_KV_LORA
    c1 = c0 + MLA_ROPE
    c2 = c1 + len(DIL_GROUPS) * DIL_COLS
    kr_pad = jnp.pad(w_in[:, c0:c1], ((0, 0), (MLA_NOPE, HEAD_PAD - MLA_NOPE - MLA_ROPE)))
    wmla = jnp.concatenate([w_in[:, :c0], kr_pad], axis=1).astype(BF16)
    col = np.arange(len(DIL_GROUPS) * DIL_COLS)
    q_scale = np.where(col % DIL_COLS < DIL_OUT, DIL_HEAD_DIM ** -0.5, 1.0).astype(np.float32)
    wdil = (w_in[:, c1:c2] * q_scale[None, :]).astype(BF16)
    wgate = w_in[:, c2:].astype(BF16)
    pad = HEAD_PAD - MLA_NOPE - MLA_ROPE
    wuq_h = w_uq.reshape(MLA_Q_LORA, MLA_HEADS, MLA_NOPE + MLA_ROPE)
    wuq = jnp.pad(wuq_h, ((0, 0), (0, 0), (0, pad))).reshape(MLA_Q_LORA, MLA_HEADS * HEAD_PAD).astype(BF16)
    wukv_h = w_ukv.reshape(MLA_KV_LORA, MLA_HEADS, MLA_NOPE + MLA_V)
    wuk = jnp.pad(wukv_h[:, :, :MLA_NOPE], ((0, 0), (0, 0), (0, HEAD_PAD - MLA_NOPE)))
    wuk = wuk.reshape(MLA_KV_LORA, MLA_HEADS * HEAD_PAD).astype(BF16)
    wuv = wukv_h[:, :, MLA_NOPE:].reshape(MLA_KV_LORA, MLA_HEADS * MLA_V).astype(BF16)
    return wmla, wdil, wgate, wuq, wuk, wuv


def kernel(x, p, attn_norm, w_in, q_norm, w_uq, kv_norm, w_ukv, w_branch_a, w_branch_b, w_out, ffn_norm, w_router, b_router, w_gate, b_gate, w_up, b_up, w_down, b_down, ple_norm, w_ple_gate, w_ple_proj, final_norm):
    b, s, d = x.shape
    depth = w_in.shape[0]
    t = b * s
    assert d == D_MODEL and s % (DIL_GROUPS[-1][0]) == 0 and t % (SC_WORKERS * SC_WINDOW) == 0
    n_assign = t * TOP_K
    n_blocks = -(-(n_assign + N_EXPERTS * (ROW_BLOCK - 1)) // ROW_BLOCK)
    n_rows = n_blocks * ROW_BLOCK
    cos_t, sina_t, sinb_t = _rope_tables(s)
    xc = x.reshape(t, d)
    for i in range(depth):
        wmla, wdil, wgate, wuq, wuk, wuv = _prep_mixer_weights(w_in[i], w_uq[i], w_ukv[i])
        q, k, vt, zd0, zd1, zd2 = _inproj(
            xc, s, attn_norm[i][None], wmla, wdil, q_norm[i][None], kv_norm[i][None],
            wuq, wuk, wuv, cos_t, sina_t, sinb_t)
        oa = _mla_attention(q.reshape(b, s, -1), k.reshape(b, s, -1), vt)
        ob = _dilated_attention(zd0, zd1, zd2, s)
        x1, hp, topi_t, wcol = _merge(
            xc, oa.reshape(t, -1), ob, attn_norm[i][None], wgate,
            w_branch_a[i].astype(BF16), w_branch_b[i].astype(BF16), w_out[i].astype(BF16),
            ffn_norm[i][None], w_router[i].T.astype(BF16), b_router[i][:, None])
        dest_t, meta = _positions(topi_t)
        ends = meta[2, :N_EXPERTS]
        block_start = jnp.arange(n_blocks, dtype=I32) * ROW_BLOCK
        block_e = jnp.minimum(
            jnp.sum((ends[None, :] <= block_start[:, None]).astype(I32), axis=1), N_EXPERTS - 1)
        n_used = (ends[N_EXPERTS - 1:] // ROW_BLOCK).astype(I32)
        dest_flat = dest_t[:TOP_K].reshape(n_assign)
        xs = _dispatch_rows(hp, dest_flat, n_rows)
        ys = _expert_ffn(xs, block_e, n_used, meta[0, :N_EXPERTS], i,
                         w_gate, b_gate[:, :, None, :], w_up, b_up[:, :, None, :],
                         w_down, b_down[:, :, None, :])
        final = i == depth - 1
        gout = final_norm[None] if final else attn_norm[i][None]
        wpg, wpp = w_ple_gate[i].astype(BF16), w_ple_proj[i].astype(BF16)
        tp = t // COMBINE_PARTS
        xc = None
        for part in range(COMBINE_PARTS):
            dest_part = dest_t[:TOP_K, part * tp:(part + 1) * tp].reshape(TOP_K * tp)
            yg = _gather_rows(ys, dest_part).reshape(TOP_K, tp, HALF)
            xc = _combine(x1, yg, wcol, p.reshape(depth, t, PLE_DIM), i, ple_norm[i][None],
                          wpg, wpp, gout, final, part, xc)
    return xc.reshape(b, s, d)
```

```python
import functools
import math

import jax
import jax.numpy as jnp
import numpy as np
from jax import lax
from jax.experimental import pallas as pl
from jax.experimental.pallas import tpu as pltpu
from jax.experimental.pallas import tpu_sc as plsc

F32 = jnp.float32
BF16 = jnp.bfloat16
I32 = jnp.int32

D_MODEL = 1024
PLE_DIM = 256
NORM_EPS = 1e-6

MLA_HEADS = 8
MLA_Q_LORA = 384
MLA_KV_LORA = 256
MLA_NOPE = 64
MLA_ROPE = 32
MLA_V = 64
ROPE_THETA = 10000.0
HEAD_PAD = 128
HALF_ROPE = MLA_ROPE // 2

DIL_GROUPS = ((128, 1), (512, 4), (2048, 16))
DIL_HEADS = 4
DIL_HEAD_DIM = 64
DIL_STEPS = 128
DIL_COLS = 3 * DIL_HEADS * DIL_HEAD_DIM
DIL_OUT = DIL_HEADS * DIL_HEAD_DIM
DIL_UNROLL = 8

N_EXPERTS = 32
TOP_K = 4
D_EXPERT = 1024
SWIGLU_LIMIT = 7.0
SWIGLU_ALPHA = 1.702
ROW_BLOCK = 512

TOKEN_TILE = 512
POSITION_TILE = 4096
POSITION_SUBTILE = 512
COMBINE_PARTS = 1
ATTN_TQ = 1024
ATTN_TK = 1024
ATTN_HEADS = 4
ATTN_SUB = 1024
ATTN_KEY_CHUNKS = 2
HALF = D_MODEL // 2
NEG = -1e30
SPECULATION_HEADROOM = 60.0
HI_MASK = -65536

SC_CORES = 2
SC_SUBCORES = 16
SC_WORKERS = SC_CORES * SC_SUBCORES
SC_WINDOW = 64

VMEM_LIMIT = 56 * 1024 * 1024


def _cparams(sem):
    return pltpu.CompilerParams(dimension_semantics=sem, vmem_limit_bytes=VMEM_LIMIT)


def _rms(x, g):
    return x * lax.rsqrt(jnp.mean(x * x, axis=-1, keepdims=True) + NORM_EPS) * g


def _dot(a, b):
    return jnp.dot(a, b, preferred_element_type=F32)


def _dot_nt(a, b):
    return lax.dot_general(a, b, (((1,), (1,)), ((), ())), preferred_element_type=F32)


def _pack_halves(lo, hi):
    lo_i = lax.bitcast_convert_type(lo.astype(BF16).astype(F32), I32)
    hi_i = lax.bitcast_convert_type(hi.astype(BF16).astype(F32), I32)
    return (hi_i & HI_MASK) | lax.shift_right_logical(lo_i, 16)


def _unpack_halves(w):
    lo = lax.bitcast_convert_type(lax.shift_left(w, 16), F32)
    hi = lax.bitcast_convert_type(w & HI_MASK, F32)
    return lo, hi


def _inproj_kernel(x_ref, g_ref, wmla_ref, wdil_ref, qn_ref, kvn_ref, wuq_ref, wuk_ref,
                   wuv_ref, cos_ref, sina_ref, sinb_ref,
                   q_ref, k_ref, vt_ref, zd0_ref, zd1_ref, zd2_ref, zs_sc):
    h = _rms(x_ref[...], g_ref[...]).astype(BF16)
    zm = _dot(h, wmla_ref[...])
    cq = _rms(zm[:, :MLA_Q_LORA], qn_ref[...]).astype(BF16)
    ckv = _rms(zm[:, MLA_Q_LORA:MLA_Q_LORA + MLA_KV_LORA], kvn_ref[...]).astype(BF16)
    kr = zm[:, MLA_Q_LORA + MLA_KV_LORA:]
    cos, sina, sinb = cos_ref[...], sina_ref[...], sinb_ref[...]

    def rope(t):
        return (t * cos + pltpu.roll(t, HALF_ROPE, 1) * sina
                + pltpu.roll(t, HEAD_PAD - HALF_ROPE, 1) * sinb)

    kr_rot = rope(kr)
    qraw = _dot(cq, wuq_ref[...])
    kraw = _dot(ckv, wuk_ref[...])
    vt_ref[...] = _dot(ckv, wuv_ref[...]).T.astype(BF16)
    scale = (MLA_NOPE + MLA_ROPE) ** -0.5 * math.log2(math.e)
    for hd in range(MLA_HEADS):
        sl = slice(hd * HEAD_PAD, (hd + 1) * HEAD_PAD)
        q_ref[:, sl] = (rope(qraw[:, sl]) * scale).astype(BF16)
        k_ref[:, sl] = (kraw[:, sl] + kr_rot).astype(BF16)
    tm = x_ref.shape[0]
    for gi, zd_ref in enumerate((zd0_ref, zd1_ref, zd2_ref)):
        z = _dot(h, wdil_ref[:, gi * DIL_COLS:(gi + 1) * DIL_COLS])
        window, dil = DIL_GROUPS[gi]
        if dil == 1:
            zd_ref[...] = z.astype(BF16)
            continue
        n_col = DIL_COLS // HEAD_PAD
        for c in range(n_col):
            zs_sc[c] = z[:, c * HEAD_PAD:(c + 1) * HEAD_PAD]
        rows = tm // dil
        part = pl.program_id(0) % (window // tm)
        for r in range(dil):
            dst = pl.ds(pl.multiple_of(r * DIL_STEPS + part * rows, rows), rows)
            for c in range(n_col):
                chunk = zs_sc[c, pl.ds(r, rows, stride=dil), :]
                zd_ref[dst, c * HEAD_PAD:(c + 1) * HEAD_PAD] = chunk.astype(BF16)


def _inproj(x2d, seq, g, wmla, wdil, qn, kvn, wuq, wuk, wuv, cos_t, sina_t, sinb_t):
    t = x2d.shape[0]
    tm = TOKEN_TILE
    n_seq_tiles = seq // tm

    def row(i):
        return (i, 0)

    def const(i):
        return (0, 0)

    def pos(i):
        return (i % n_seq_tiles, 0)

    def full(a):
        return pl.BlockSpec(a.shape, const)

    def vt_block(i):
        return (i // n_seq_tiles, 0, i % n_seq_tiles)

    def rows(width):
        return pl.BlockSpec((tm, width), row)

    def unit_rows(window):
        return pl.BlockSpec((window, DIL_COLS), lambda i: (i // (window // tm), 0))

    out_shape = [
        jax.ShapeDtypeStruct((t, MLA_HEADS * HEAD_PAD), BF16),
        jax.ShapeDtypeStruct((t, MLA_HEADS * HEAD_PAD), BF16),
        jax.ShapeDtypeStruct((t // seq, MLA_HEADS * MLA_V, seq), BF16),
        jax.ShapeDtypeStruct((t, DIL_COLS), BF16),
        jax.ShapeDtypeStruct((t, DIL_COLS), BF16),
        jax.ShapeDtypeStruct((t, DIL_COLS), BF16),
    ]
    return pl.pallas_call(
        _inproj_kernel,
        out_shape=out_shape,
        grid=(t // tm,),
        in_specs=[pl.BlockSpec((tm, D_MODEL), row), full(g), full(wmla), full(wdil),
                  full(qn), full(kvn), full(wuq), full(wuk), full(wuv),
                  pl.BlockSpec((tm, HEAD_PAD), pos), pl.BlockSpec((tm, HEAD_PAD), pos),
                  pl.BlockSpec((tm, HEAD_PAD), pos)],
        out_specs=[rows(D_MODEL), rows(D_MODEL),
                   pl.BlockSpec((None, MLA_HEADS * MLA_V, tm), vt_block),
                   rows(DIL_COLS)] + [unit_rows(window) for window, _ in DIL_GROUPS[1:]],
        scratch_shapes=[pltpu.VMEM((DIL_COLS // HEAD_PAD, tm, HEAD_PAD), F32)],
        compiler_params=_cparams(("arbitrary",)),
        name="inproj",
    )(x2d, g, wmla, wdil, qn, kvn, wuq, wuk, wuv, cos_t, sina_t, sinb_t)


def _mla_kernel(qi_ref, kj_ref, q_ref, k_ref, vt_ref, o_ref, m_sc, l_sc, acc_sc, redo_sc):
    p = pl.program_id(2)
    i = qi_ref[p]
    j = kj_ref[p]
    tq = q_ref.shape[0]
    tk = k_ref.shape[0]

    @pl.when(j == 0)
    def _():
        m_sc[...] = jnp.full(m_sc.shape, NEG, F32)
        l_sc[...] = jnp.zeros(l_sc.shape, F32)
        acc_sc[...] = jnp.zeros(acc_sc.shape, F32)

    ratio = tq // tk
    sub = ATTN_SUB

    def step(diagonal, speculative):
        chains = [(hh, c) for hh in range(ATTN_HEADS) for c in range(tq // sub)]
        ones_rows = (lax.broadcasted_iota(I32, (16, tk), 0) == 0).astype(BF16)
        state = {}
        for hh, c in chains:
            cs = slice(c * sub, (c + 1) * sub)
            state[hh, c] = (m_sc[hh, :, cs], l_sc[hh, :, cs], acc_sc[hh, :, cs])
        new_state = {}
        within = None
        def scores(hh, c, rows, q0=0):
            sl = slice(hh * HEAD_PAD, (hh + 1) * HEAD_PAD)
            st = _dot_nt(k_ref[rows, sl], q_ref[c * sub + q0:(c + 1) * sub, sl])
            if diagonal:
                shape = (rows.stop - rows.start, sub - q0)
                key = lax.broadcasted_iota(I32, shape, 0) + (j * tk + rows.start)
                qry = lax.broadcasted_iota(I32, shape, 1) + (i * tq + c * sub + q0)
                st = jnp.where(qry >= key, st, NEG)
            return st

        def values(hh, rows):
            ones_row = (lax.broadcasted_iota(I32, (16, rows.stop - rows.start), 0) == 0).astype(BF16)
            return jnp.concatenate([vt_ref[hh * MLA_V:(hh + 1) * MLA_V, rows], ones_row], axis=0)

        for hh, c in chains:
            m_prev, l_prev, acc_prev = state[hh, c]
            if speculative:
                kc = tk // ATTN_KEY_CHUNKS
                m_blk, pv = None, None
                for n in range(ATTN_KEY_CHUNKS):
                    rows = slice(n * kc, (n + 1) * kc)
                    q0 = n * kc if (diagonal and tq == tk and sub == tq) else 0
                    st = scores(hh, c, rows, q0)
                    if n == 0:
                        m_prev = jnp.where(j == 0, st[:1], m_prev)
                    m_part = jnp.max(st, axis=0, keepdims=True)
                    part = _dot(values(hh, rows), jnp.exp2(st - m_prev[:, q0:]).astype(BF16))
                    if q0:
                        m_part = jnp.concatenate([jnp.full((1, q0), NEG, F32), m_part], axis=1)
                        part = jnp.concatenate([jnp.zeros((part.shape[0], q0), F32), part], axis=1)
                    m_blk = m_part if m_blk is None else jnp.maximum(m_blk, m_part)
                    pv = part if pv is None else pv + part
                m_new = jnp.maximum(m_prev, m_blk)
                alpha = jnp.exp2(m_prev - m_new)
                l_new = alpha * (l_prev + pv[MLA_V:MLA_V + 1])
                acc_new = alpha * (acc_prev + pv[:MLA_V])
                ok = jnp.max(m_blk - m_prev) <= SPECULATION_HEADROOM
                within = ok if within is None else (within & ok)
            else:
                rows = slice(0, tk)
                st = scores(hh, c, rows)
                m_new = jnp.maximum(m_prev, jnp.max(st, axis=0, keepdims=True))
                alpha = jnp.exp2(m_prev - m_new)
                pv = _dot(values(hh, rows), jnp.exp2(st - m_new).astype(BF16))
                l_new = alpha * l_prev + pv[MLA_V:MLA_V + 1]
                acc_new = alpha * acc_prev + pv[:MLA_V]
            new_state[hh, c] = (m_new, l_new, acc_new)

        def commit():
            for hh, c in chains:
                cs = slice(c * sub, (c + 1) * sub)
                m_sc[hh, :, cs], l_sc[hh, :, cs], acc_sc[hh, :, cs] = new_state[hh, c]

        if speculative:
            pl.when(within)(commit)
            redo_sc[0] = jnp.logical_not(within).astype(I32)
        else:
            commit()

    redo_sc[0] = 0
    on_diagonal = j >= ratio * i

    @pl.when(jnp.logical_not(on_diagonal))
    def _():
        step(False, True)

    @pl.when(on_diagonal)
    def _():
        step(True, True)

    @pl.when(redo_sc[0] != 0)
    def _():
        step(True, False)

    @pl.when(j == ratio * i + (ratio - 1))
    def _():
        ot = jnp.concatenate([acc_sc[hh] / l_sc[hh] for hh in range(ATTN_HEADS)], axis=0)
        o_ref[...] = ot.T.astype(BF16)


def _mla_attention(q, k, vt):
    b, s, _ = q.shape
    tq, tk, nh = ATTN_TQ, ATTN_TK, ATTN_HEADS
    ratio = tq // tk
    nq = s // tq
    pairs = [(i, j) for i in range(nq) for j in range(ratio * (i + 1))]
    qi = jnp.asarray([p[0] for p in pairs], I32)
    kj = jnp.asarray([p[1] for p in pairs], I32)
    grid_spec = pltpu.PrefetchScalarGridSpec(
        num_scalar_prefetch=2,
        grid=(b, MLA_HEADS // nh, len(pairs)),
        in_specs=[
            pl.BlockSpec((None, tq, nh * HEAD_PAD), lambda bb, hp, p, qi, kj: (bb, qi[p], hp)),
            pl.BlockSpec((None, tk, nh * HEAD_PAD), lambda bb, hp, p, qi, kj: (bb, kj[p], hp)),
            pl.BlockSpec((None, nh * MLA_V, tk), lambda bb, hp, p, qi, kj: (bb, hp, kj[p])),
        ],
        out_specs=pl.BlockSpec((None, tq, nh * MLA_V), lambda bb, hp, p, qi, kj: (bb, qi[p], hp)),
        scratch_shapes=[pltpu.VMEM((nh, 1, tq), F32), pltpu.VMEM((nh, 1, tq), F32),
                        pltpu.VMEM((nh, MLA_V, tq), F32), pltpu.SMEM((1,), I32)],
    )
    return pl.pallas_call(
        _mla_kernel,
        out_shape=jax.ShapeDtypeStruct((b, s, MLA_HEADS * MLA_V), BF16),
        grid_spec=grid_spec,
        compiler_params=_cparams(("parallel", "parallel", "arbitrary")),
        name="mla_attention",
    )(qi, kj, q, k, vt)


def _alibi_slopes(n):
    def pow2(m):
        start = 2.0 ** (-8.0 / m)
        return [start ** (i + 1) for i in range(m)]
    if math.log2(n).is_integer():
        s = pow2(n)
    else:
        c = 2 ** int(math.floor(math.log2(n)))
        s = pow2(c) + pow2(2 * c)[0::2][: n - c]
    return np.array(sorted(s, reverse=True), dtype=np.float32)


def _dilated_block(cur, prev, bias4, first):
    n = DIL_STEPS
    hw = DIL_OUT
    q = cur[:, :hw]
    kk = jnp.concatenate([prev[:, hw:2 * hw], cur[:, hw:2 * hw]], axis=0)
    vv = jnp.concatenate([prev[:, 2 * hw:], cur[:, 2 * hw:]], axis=0)
    head_of_lane = lax.broadcasted_iota(I32, (n, hw), 1) // DIL_HEAD_DIM
    zero = jnp.zeros_like(q)
    q4 = jnp.concatenate([jnp.where(head_of_lane == h, q, zero) for h in range(DIL_HEADS)], axis=0)
    s4 = _dot_nt(q4, kk) + bias4
    if first is not None:
        ki = lax.broadcasted_iota(I32, (DIL_HEADS * n, 2 * n), 1)
        s4 = jnp.where(first & (ki < n), NEG, s4)
    m4 = jnp.max(s4, axis=1, keepdims=True)
    p4 = jnp.exp(s4 - m4).astype(BF16)
    l4 = jnp.broadcast_to(jnp.sum(p4.astype(F32), axis=1, keepdims=True), (DIL_HEADS * n, HEAD_PAD))
    pv4 = _dot(p4, vv)
    m4 = jnp.broadcast_to(m4, (DIL_HEADS * n, HEAD_PAD))

    def rows(a, h):
        return a[h * n:(h + 1) * n]

    o_un = rows(pv4, DIL_HEADS - 1)
    for h in range(DIL_HEADS - 2, -1, -1):
        o_un = jnp.where(head_of_lane == h, rows(pv4, h), o_un)
    low = lax.broadcasted_iota(I32, (n, HEAD_PAD), 1) < DIL_HEAD_DIM

    def per_lane(a):
        return jnp.concatenate([jnp.where(low, rows(a, 0), rows(a, 1)),
                                jnp.where(low, rows(a, 2), rows(a, 3))], axis=1)

    l_sel = per_lane(l4)
    return o_un / l_sel, per_lane(m4) + jnp.log(l_sel)


def _dilated_kernel(c0_ref, h0_ref, c1_ref, h1_ref, c2_ref, h2_ref, ob_ref, o_sc, l_sc, bias_sc,
                    *, slopes):
    u = pl.program_id(1)
    n = DIL_STEPS
    unit = ob_ref.shape[0]
    n_sb = unit // n
    first = u == 0
    qi = lax.broadcasted_iota(I32, (n, 2 * n), 0)
    ki = lax.broadcasted_iota(I32, (n, 2 * n), 1)
    dist = qi + n - ki
    valid = (dist >= 0) & (dist <= n)
    distf = dist.astype(F32)
    for gi in range(len(DIL_GROUPS)):
        for h in range(DIL_HEADS):
            bias_sc[gi, h * n:(h + 1) * n, :] = jnp.where(valid, -slopes[gi][h] * distf, NEG)

    def rows_of(ref, sb):
        return ref[pl.ds(pl.multiple_of(sb * n, n), n), :]

    def emit(gi, start, stride, o, lse):
        if isinstance(start, int):
            idx = pl.ds(start, n)
        elif stride == 1:
            idx = pl.ds(pl.multiple_of(start, n), n)
        else:
            idx = pl.ds(start, n, stride=stride)
        for half in range(DIL_OUT // HEAD_PAD):
            ls = slice(half * HEAD_PAD, (half + 1) * HEAD_PAD)
            o_sc[gi, half, idx, :] = o[:, ls]
            l_sc[gi, half, idx, :] = lse[:, ls]

    emit(0, 0, 1, *_dilated_block(c0_ref[:n, :], h0_ref[...], bias_sc[0], first))

    def g0_body(sb, carry):
        emit(0, sb * n, 1, *_dilated_block(rows_of(c0_ref, sb), rows_of(c0_ref, sb - 1), bias_sc[0], None))
        return carry

    lax.fori_loop(1, n_sb, g0_body, 0, unroll=DIL_UNROLL)

    d1 = DIL_GROUPS[1][1]

    def g1_head(r, carry):
        emit(1, r, d1, *_dilated_block(rows_of(c1_ref, r), rows_of(h1_ref, r), bias_sc[1], first))
        return carry

    def g1_body(sb, carry):
        start = (sb // d1) * (d1 * n) + sb % d1
        emit(1, start, d1, *_dilated_block(rows_of(c1_ref, sb), rows_of(c1_ref, sb - d1), bias_sc[1], None))
        return carry

    lax.fori_loop(0, d1, g1_head, 0, unroll=DIL_UNROLL)
    lax.fori_loop(d1, n_sb, g1_body, 0, unroll=DIL_UNROLL)

    d2 = DIL_GROUPS[2][1]

    def g2_body(r, carry):
        emit(2, r, d2, *_dilated_block(rows_of(c2_ref, r), rows_of(h2_ref, r), bias_sc[2], first))
        return carry

    lax.fori_loop(0, n_sb, g2_body, 0, unroll=DIL_UNROLL)

    def merge_body(c, carry):
        idx = pl.ds(pl.multiple_of(c * n, n), n)
        for half in range(DIL_OUT // HEAD_PAD):
            l0, l1, l2 = l_sc[0, half, idx, :], l_sc[1, half, idx, :], l_sc[2, half, idx, :]
            lmax = jnp.maximum(jnp.maximum(l0, l1), l2)
            e0, e1, e2 = jnp.exp(l0 - lmax), jnp.exp(l1 - lmax), jnp.exp(l2 - lmax)
            ob = (e0 * o_sc[0, half, idx, :] + e1 * o_sc[1, half, idx, :]
                  + e2 * o_sc[2, half, idx, :]) / (e0 + e1 + e2)
            ob_ref[idx, half * HEAD_PAD:(half + 1) * HEAD_PAD] = ob.astype(BF16)
        return carry

    lax.fori_loop(0, n_sb, merge_body, 0)


def _dilated_attention(zd0, zd1, zd2, seq):
    t = zd0.shape[0]
    unit = DIL_GROUPS[-1][0]
    upb = seq // unit
    n = DIL_STEPS
    u1 = DIL_GROUPS[1][0]
    all_slopes = _alibi_slopes(len(DIL_GROUPS) * DIL_HEADS).reshape(len(DIL_GROUPS), DIL_HEADS)
    slopes = tuple(tuple(float(x) * dil for x in all_slopes[gi]) for gi, (_, dil) in enumerate(DIL_GROUPS))

    def cur(bb, u):
        return (bb * upb + u, 0)

    def halo(rows):
        per_unit = unit // rows
        return lambda bb, u: ((bb * upb) * per_unit + jnp.maximum(u * per_unit - 1, 0), 0)

    return pl.pallas_call(
        functools.partial(_dilated_kernel, slopes=slopes),
        out_shape=jax.ShapeDtypeStruct((t, DIL_OUT), BF16),
        grid=(t // seq, upb),
        in_specs=[pl.BlockSpec((unit, DIL_COLS), cur), pl.BlockSpec((n, DIL_COLS), halo(n)),
                  pl.BlockSpec((unit, DIL_COLS), cur), pl.BlockSpec((u1, DIL_COLS), halo(u1)),
                  pl.BlockSpec((unit, DIL_COLS), cur), pl.BlockSpec((unit, DIL_COLS), halo(unit))],
        out_specs=pl.BlockSpec((unit, DIL_OUT), cur),
        scratch_shapes=[pltpu.VMEM((len(DIL_GROUPS), DIL_OUT // HEAD_PAD, unit, HEAD_PAD), F32),
                        pltpu.VMEM((len(DIL_GROUPS), DIL_OUT // HEAD_PAD, unit, HEAD_PAD), F32),
                        pltpu.VMEM((len(DIL_GROUPS), DIL_HEADS * n, 2 * n), F32)],
        compiler_params=_cparams(("parallel", "arbitrary")),
        name="dilated_attention",
    )(zd0, zd0, zd1, zd1, zd2, zd2)


def _merge_kernel(x_ref, oa_ref, ob_ref, ga_ref, wgate_ref,
                  wa_ref, wb_ref, wo_ref, g_ref, wr_ref, br_ref,
                  x1_ref, hp_ref, topi_ref, wcol_ref):
    tm = x_ref.shape[0]
    x = x_ref[...]
    h = _rms(x, ga_ref[...]).astype(BF16)
    mixed = (jax.nn.sigmoid(_dot(h, wgate_ref[:, :D_MODEL])) * _dot(oa_ref[...], wa_ref[...])
             + jax.nn.sigmoid(_dot(h, wgate_ref[:, D_MODEL:])) * _dot(ob_ref[...], wb_ref[...]))
    x1 = x + _dot(mixed.astype(BF16), wo_ref[...])
    x1_ref[...] = x1
    h2 = _rms(x1, g_ref[...])
    hp_ref[...] = _pack_halves(h2[:, :HALF], h2[:, HALF:])

    logits = _dot_nt(wr_ref[...], h2.astype(BF16)) + br_ref[...]
    eidx = lax.broadcasted_iota(I32, (N_EXPERTS, tm), 0)
    vals, idxs = [], []
    for _ in range(TOP_K):
        m = jnp.max(logits, axis=0, keepdims=True)
        idx = jnp.min(jnp.where(logits == m, eidx, N_EXPERTS), axis=0, keepdims=True)
        vals.append(m)
        idxs.append(idx)
        logits = jnp.where(eidx == idx, -jnp.inf, logits)
    exps = [jnp.exp(vk - vals[0]) for vk in vals]
    den = exps[0] + exps[1] + exps[2] + exps[3]
    row8 = lax.broadcasted_iota(I32, (8, tm), 0)
    row128 = lax.broadcasted_iota(I32, (HEAD_PAD, tm), 0)
    topi = jnp.zeros((8, tm), I32)
    wide = jnp.zeros((HEAD_PAD, tm), F32)
    for kk in range(TOP_K):
        topi = jnp.where(row8 == kk, idxs[kk], topi)
        wide = jnp.where(row128 == kk, exps[kk] / den, wide)
    topi_ref[...] = topi
    wcol_ref[...] = wide.T


def _merge(x2d, oa, ob, g_attn, wgate, wa, wb, wo, g, wr_t, br_col):
    t = x2d.shape[0]
    tm = TOKEN_TILE

    def row(i):
        return (i, 0)

    def col(i):
        return (0, i)

    def full(a):
        return pl.BlockSpec(a.shape, lambda i: (0, 0))

    def rows(width):
        return pl.BlockSpec((tm, width), row)

    out_shape = [
        jax.ShapeDtypeStruct((t, D_MODEL), F32),
        jax.ShapeDtypeStruct((t, HALF), I32),
        jax.ShapeDtypeStruct((8, t), I32),
        jax.ShapeDtypeStruct((t, HEAD_PAD), F32),
    ]
    return pl.pallas_call(
        _merge_kernel,
        out_shape=out_shape,
        grid=(t // tm,),
        in_specs=[rows(D_MODEL), rows(MLA_HEADS * MLA_V), rows(DIL_OUT), full(g_attn), full(wgate)]
        + [full(wa), full(wb), full(wo), full(g), full(wr_t), full(br_col)],
        out_specs=[rows(D_MODEL), rows(HALF), pl.BlockSpec((8, tm), col), rows(HEAD_PAD)],
        compiler_params=_cparams(("parallel",)),
        name="merge_router",
    )(x2d, oa, ob, g_attn, wgate, wa, wb, wo, g, wr_t, br_col)


def _positions_kernel(topi_ref, dest_ref, meta_ref, cnt_sc, carry_sc, start_sc):
    ps = pl.program_id(0)
    i = pl.program_id(1)
    tm = POSITION_SUBTILE
    n_sub = topi_ref.shape[1] // tm
    eidx = lax.broadcasted_iota(I32, (N_EXPERTS, tm), 0)

    def hits_of(sb):
        topi = topi_ref[:, sb * tm:(sb + 1) * tm]
        return [eidx == topi[kk:kk + 1, :] for kk in range(TOP_K)]

    def members(hits):
        return hits[0] | hits[1] | hits[2] | hits[3]

    @pl.when((ps == 0) & (i == 0))
    def _():
        cnt_sc[...] = jnp.zeros(cnt_sc.shape, F32)

    @pl.when(ps == 0)
    def _():
        total = cnt_sc[...]
        for sb in range(n_sub):
            total = total + jnp.sum(members(hits_of(sb)).astype(F32), axis=1, keepdims=True)
        cnt_sc[...] = total

    @pl.when((ps == 1) & (i == 0))
    def _():
        cnt = cnt_sc[...].astype(I32)
        shift = ROW_BLOCK.bit_length() - 1
        padded = lax.shift_left(lax.shift_right_logical(cnt + (ROW_BLOCK - 1), shift), shift)
        sub = lax.broadcasted_iota(I32, (N_EXPERTS, HEAD_PAD), 0)
        lane = lax.broadcasted_iota(I32, (N_EXPERTS, HEAD_PAD), 1)
        padded_row = jnp.sum(jnp.where(sub == lane, padded, 0), axis=0, keepdims=True)
        start = jnp.sum(jnp.where(lane < sub, padded_row, 0), axis=1, keepdims=True)
        start_sc[...] = start.astype(F32)
        carry_sc[...] = jnp.zeros(carry_sc.shape, F32)
        cnt_row = jnp.sum(jnp.where(sub == lane, cnt, 0), axis=0, keepdims=True)
        start_row = jnp.sum(jnp.where(sub == lane, start, 0), axis=0, keepdims=True)
        row8 = lax.broadcasted_iota(I32, (8, HEAD_PAD), 0)
        meta = jnp.where(row8 == 0, cnt_row, 0)
        meta = jnp.where(row8 == 1, start_row, meta)
        meta = jnp.where(row8 == 2, start_row + padded_row, meta)
        meta_ref[...] = meta

    @pl.when(ps == 1)
    def _():
        tr = lax.broadcasted_iota(I32, (tm, tm), 0)
        tc = lax.broadcasted_iota(I32, (tm, tm), 1)
        before = (tr < tc).astype(BF16)
        row8 = lax.broadcasted_iota(I32, (8, tm), 0)
        offset = carry_sc[...] + start_sc[...]
        for sb in range(n_sub):
            hits = hits_of(sb)
            member = members(hits)
            base = _dot(member.astype(BF16), before) + offset
            dest = jnp.zeros((8, tm), I32)
            for kk in range(TOP_K):
                dk = jnp.sum(jnp.where(hits[kk], base, 0.0), axis=0, keepdims=True).astype(I32)
                dest = jnp.where(row8 == kk, dk, dest)
            dest_ref[:, sb * tm:(sb + 1) * tm] = dest
            offset = offset + jnp.sum(member.astype(F32), axis=1, keepdims=True)
        carry_sc[...] = offset - start_sc[...]


def _positions(topi_t):
    t = topi_t.shape[1]
    tm = min(POSITION_TILE, t)
    return pl.pallas_call(
        _positions_kernel,
        out_shape=[jax.ShapeDtypeStruct((8, t), I32), jax.ShapeDtypeStruct((8, HEAD_PAD), I32)],
        grid=(2, t // tm),
        in_specs=[pl.BlockSpec((8, tm), lambda ps, i: (0, i))],
        out_specs=[pl.BlockSpec((8, tm), lambda ps, i: (0, i * ps)),
                   pl.BlockSpec((8, HEAD_PAD), lambda ps, i: (0, 0))],
        scratch_shapes=[pltpu.VMEM((N_EXPERTS, 1), F32)] * 3,
        compiler_params=_cparams(("arbitrary", "arbitrary")),
        name="routing_positions",
    )(topi_t)


def _sc_mesh():
    return plsc.VectorSubcoreMesh(core_axis_name="c", subcore_axis_name="s")


def _dispatch_rows(table, dest_flat, n_rows):
    t, c = table.shape
    n_slots = dest_flat.shape[0] // t
    per_w = t // SC_WORKERS
    assert per_w * SC_WORKERS == t and per_w % (2 * SC_WINDOW) == 0
    n_chunks = per_w // SC_WINDOW
    w = SC_WINDOW

    @functools.partial(
        pl.kernel, mesh=_sc_mesh(),
        out_type=jax.ShapeDtypeStruct((n_rows, c), table.dtype),
        scratch_types=[pltpu.VMEM((w,), I32)] * n_slots + [pltpu.VMEM((w, c), table.dtype)] * 2
        + [pltpu.SemaphoreType.DMA] * (n_slots + 2),
        name="dispatch_rows",
    )
    def k(table_hbm, dest_hbm, out_hbm, *scratch):
        idx = scratch[:n_slots]
        rows = scratch[n_slots:n_slots + 2]
        scatter_sems = scratch[n_slots + 2:2 * n_slots + 2]
        read_sems = scratch[2 * n_slots + 2:]
        wid = lax.axis_index("s") * SC_CORES + lax.axis_index("c")
        base = wid * per_w

        def off(chunk):
            return pl.multiple_of(base + chunk * w, w)

        def read(chunk, buf):
            return pltpu.make_async_copy(table_hbm.at[pl.ds(off(chunk), w)], rows[buf], read_sems[buf])

        def scatter(kk, buf):
            return pltpu.make_async_copy(rows[buf], out_hbm.at[idx[kk]], scatter_sems[kk])

        read(0, 0).start()

        @pl.loop(0, n_chunks // 2)
        def _(p):
            for buf in range(2):
                chunk = 2 * p + buf

                @pl.when(chunk + 1 < n_chunks)
                def _():
                    read(chunk + 1, 1 - buf).start()

                read(chunk, buf).wait()
                for kk in range(n_slots):
                    src = pl.multiple_of(kk * t + off(chunk), w)
                    pltpu.sync_copy(dest_hbm.at[pl.ds(src, w)], idx[kk])
                    scatter(kk, buf).start()
                for kk in range(n_slots):
                    scatter(kk, buf).wait()

    return k(table, dest_flat)


def _gather_rows(table, idx):
    n = idx.shape[0]
    c = table.shape[1]
    per_w = n // SC_WORKERS
    assert per_w * SC_WORKERS == n and per_w % (2 * SC_WINDOW) == 0
    n_chunks = per_w // SC_WINDOW
    w = SC_WINDOW

    @functools.partial(
        pl.kernel, mesh=_sc_mesh(),
        out_type=jax.ShapeDtypeStruct((n, c), table.dtype),
        scratch_types=[pltpu.VMEM((w,), I32)] * 2 + [pltpu.VMEM((w, c), table.dtype)] * 2
        + [pltpu.SemaphoreType.DMA] * 4,
        name="gather_rows",
    )
    def k(table_hbm, idx_hbm, out_hbm, idx_a, idx_b, rows_a, rows_b, g_a, g_b, w_a, w_b):
        idx, rows, gather_sems, write_sems = (idx_a, idx_b), (rows_a, rows_b), (g_a, g_b), (w_a, w_b)
        wid = lax.axis_index("s") * SC_CORES + lax.axis_index("c")
        base = wid * per_w

        def off(chunk):
            return pl.multiple_of(base + chunk * w, w)

        def gather(buf):
            return pltpu.make_async_copy(table_hbm.at[idx[buf]], rows[buf], gather_sems[buf])

        def write(chunk, buf):
            return pltpu.make_async_copy(rows[buf], out_hbm.at[pl.ds(off(chunk), w)], write_sems[buf])

        def start_gather(chunk, buf):
            pltpu.sync_copy(idx_hbm.at[pl.ds(off(chunk), w)], idx[buf])
            gather(buf).start()

        start_gather(0, 0)

        @pl.loop(0, n_chunks // 2)
        def _(p):
            for buf in range(2):
                chunk = 2 * p + buf

                @pl.when(chunk + 1 < n_chunks)
                def _():
                    @pl.when(chunk >= 1)
                    def _():
                        write(chunk - 1, 1 - buf).wait()
                    start_gather(chunk + 1, 1 - buf)

                gather(buf).wait()
                write(chunk, buf).start()

        write(n_chunks - 2, 0).wait()
        write(n_chunks - 1, 1).wait()

    return k(table, idx)


def _expert_kernel(be_ref, nused_ref, first_ref, slot_ref, next_ref,
                   xs_ref, wg_hbm, bg_ref, wu_hbm, bu_ref, wd_hbm, bd_ref,
                   ys_ref, wg_ref, wu_ref, wd_ref, stage_g, stage_u, stage_d, sems, *, layer):
    b = pl.program_id(0)
    used = b < nused_ref[0]
    weights = ((wg_hbm, stage_g), (wu_hbm, stage_u), (wd_hbm, stage_d))

    def fetch(expert, slot):
        return [pltpu.make_async_copy(hbm.at[layer, expert], stage.at[slot], sems.at[slot, n])
                for n, (hbm, stage) in enumerate(weights)]

    @pl.when(used & (first_ref[b] == 1))
    def _():
        slot = slot_ref[b]

        @pl.when(b == 0)
        def _():
            for cp in fetch(be_ref[0], slot):
                cp.start()

        for cp in fetch(be_ref[b], slot):
            cp.wait()
        wg_ref[...] = stage_g[slot].astype(BF16)
        wu_ref[...] = stage_u[slot].astype(BF16)
        wd_ref[...] = stage_d[slot].astype(BF16)

        @pl.when(next_ref[b] >= 0)
        def _():
            for cp in fetch(next_ref[b], 1 - slot):
                cp.start()

    @pl.when(used)
    def _():
        lo, hi = _unpack_halves(xs_ref[...])
        xb = jnp.concatenate([lo.astype(BF16), hi.astype(BF16)], axis=1)
        a = _dot(xb, wg_ref[...]) + bg_ref[...]
        u = _dot(xb, wu_ref[...]) + bu_ref[...]
        a = jnp.minimum(a, SWIGLU_LIMIT)
        u = jnp.clip(u, -SWIGLU_LIMIT, SWIGLU_LIMIT)
        y = (a * jax.nn.sigmoid(SWIGLU_ALPHA * a)) * (u + 1.0)
        out = _dot(y.astype(BF16), wd_ref[...]) + bd_ref[...]
        ys_ref[...] = _pack_halves(out[:, :HALF], out[:, HALF:])

    @pl.when(b >= nused_ref[0])
    def _():
        ys_ref[...] = jnp.zeros(ys_ref.shape, I32)


def _expert_ffn(xs, block_e, n_used, counts, layer, wg, bg, wu, bu, wd, bd):
    n_rows = xs.shape[0]
    n_blocks = n_rows // ROW_BLOCK

    blk = jnp.arange(n_blocks, dtype=I32)
    used = blk < n_used[0]
    first = used & ((blk == 0) | (block_e != jnp.roll(block_e, 1)))
    slot = (jnp.cumsum(first.astype(I32)) - 1) % 2
    eid = jnp.arange(N_EXPERTS, dtype=I32)
    later = (eid[None, :] > eid[:, None]) & (counts[None, :] > 0)
    next_of_expert = jnp.min(jnp.where(later, eid[None, :], N_EXPERTS), axis=1)
    next_of_expert = jnp.where(next_of_expert == N_EXPERTS, -1, next_of_expert)
    next_e = jnp.sum(jnp.where(block_e[:, None] == eid[None, :], next_of_expert[None, :], 0), axis=1)

    def rows(b, *_):
        return (b, 0)

    def expert(b, be, *_):
        return (layer, be[b], 0, 0)

    hbm = pl.BlockSpec(memory_space=pl.ANY)
    grid_spec = pltpu.PrefetchScalarGridSpec(
        num_scalar_prefetch=5,
        grid=(n_blocks,),
        in_specs=[pl.BlockSpec((ROW_BLOCK, HALF), rows),
                  hbm, pl.BlockSpec((None, None, 1, D_EXPERT), expert),
                  hbm, pl.BlockSpec((None, None, 1, D_EXPERT), expert),
                  hbm, pl.BlockSpec((None, None, 1, D_MODEL), expert)],
        out_specs=pl.BlockSpec((ROW_BLOCK, HALF), rows),
        scratch_shapes=[pltpu.VMEM((D_MODEL, D_EXPERT), BF16), pltpu.VMEM((D_MODEL, D_EXPERT), BF16),
                        pltpu.VMEM((D_EXPERT, D_MODEL), BF16),
                        pltpu.VMEM((2, D_MODEL, D_EXPERT), F32), pltpu.VMEM((2, D_MODEL, D_EXPERT), F32),
                        pltpu.VMEM((2, D_EXPERT, D_MODEL), F32),
                        pltpu.SemaphoreType.DMA((2, 3))],
    )
    return pl.pallas_call(
        functools.partial(_expert_kernel, layer=layer),
        out_shape=jax.ShapeDtypeStruct((n_rows, HALF), I32),
        grid_spec=grid_spec,
        compiler_params=_cparams(("arbitrary",)),
        name="expert_ffn",
    )(block_e, n_used, first.astype(I32), slot.astype(I32), next_e.astype(I32),
      xs, wg, bg, wu, bu, wd, bd)


def _combine_kernel(x1_ref, yg_ref, wcol_ref, p_ref, gple_ref, wpg_ref, wpp_ref, gout_ref, o_ref,
                    *, final):
    x1 = x1_ref[...]
    acc_lo = x1[:, :HALF]
    acc_hi = x1[:, HALF:]
    wcol = wcol_ref[...]
    for kk in range(TOP_K):
        lo, hi = _unpack_halves(yg_ref[kk])
        wk = wcol[:, kk:kk + 1]
        acc_lo = acc_lo + wk * lo
        acc_hi = acc_hi + wk * hi
    x2 = jnp.concatenate([acc_lo, acc_hi], axis=1)
    gate = jax.nn.sigmoid(_dot(_rms(x2, gple_ref[...]).astype(BF16), wpg_ref[...]))
    x3 = x2 + gate * _dot(p_ref[...].astype(BF16), wpp_ref[...])
    o_ref[...] = _rms(x3, gout_ref[...]) if final else x3


def _combine_kernel_inplace(x1_ref, yg_ref, wcol_ref, p_ref, gple_ref, wpg_ref, wpp_ref, gout_ref,
                            prev_ref, o_ref, *, final):
    del prev_ref
    _combine_kernel(x1_ref, yg_ref, wcol_ref, p_ref, gple_ref, wpg_ref, wpp_ref, gout_ref, o_ref,
                    final=final)


def _combine(x1, yg, wcol, p3d, layer, gple, wpg, wpp, gout, final, part, prev):
    t = x1.shape[0]
    tm = TOKEN_TILE
    n_tiles = yg.shape[1] // tm
    first_tile = part * n_tiles

    def row(i, *_):
        return (first_tile + i, 0)

    def full(a):
        return pl.BlockSpec(a.shape, lambda i: (0, 0))

    in_specs = [pl.BlockSpec((tm, D_MODEL), row),
                pl.BlockSpec((TOP_K, tm, HALF), lambda i: (0, i, 0)),
                pl.BlockSpec((tm, HEAD_PAD), row),
                pl.BlockSpec((None, tm, PLE_DIM), lambda i: (layer, first_tile + i, 0)),
                full(gple), full(wpg), full(wpp), full(gout)]
    args = [x1, yg, wcol, p3d, gple, wpg, wpp, gout]
    kern = functools.partial(_combine_kernel, final=final)
    aliases = {}
    if prev is not None:
        in_specs.append(pl.BlockSpec(memory_space=pl.ANY))
        args.append(prev)
        aliases = {len(args) - 1: 0}
        kern = functools.partial(_combine_kernel_inplace, final=final)
    return pl.pallas_call(
        kern,
        out_shape=jax.ShapeDtypeStruct((t, D_MODEL), F32),
        grid=(n_tiles,),
        in_specs=in_specs,
        out_specs=pl.BlockSpec((tm, D_MODEL), row),
        input_output_aliases=aliases,
        compiler_params=_cparams(("parallel",)),
        name="combine_ple",
    )(*args)


def _rope_tables(seq):
    inv_freq = ROPE_THETA ** (-jnp.arange(HALF_ROPE, dtype=F32) * 2.0 / MLA_ROPE)
    ang = jnp.arange(seq, dtype=F32)[:, None] * inv_freq[None, :]
    cos, sin = jnp.cos(ang), jnp.sin(ang)
    ones = jnp.ones((seq, MLA_NOPE), F32)
    zeros16 = jnp.zeros((seq, HALF_ROPE), F32)
    zeros64 = jnp.zeros((seq, MLA_NOPE), F32)
    tail = jnp.ones((seq, HEAD_PAD - MLA_NOPE - MLA_ROPE), F32)
    ztail = jnp.zeros_like(tail)
    cos_t = jnp.concatenate([ones, cos, cos, tail], axis=1)
    sina_t = jnp.concatenate([zeros64, zeros16, sin, ztail], axis=1)
    sinb_t = jnp.concatenate([zeros64, -sin, zeros16, ztail], axis=1)
    return cos_t, sina_t, sinb_t


def _prep_mixer_weights(w_in, w_uq, w_ukv):
    c0 = MLA_Q_LORA + MLA_KV_LORA
    c1 = c0 + MLA_ROPE
    c2 = c1 + len(DIL_GROUPS) * DIL_COLS
    kr_pad = jnp.pad(w_in[:, c0:c1], ((0, 0), (MLA_NOPE, HEAD_PAD - MLA_NOPE - MLA_ROPE)))
    wmla = jnp.concatenate([w_in[:, :c0], kr_pad], axis=1).astype(BF16)
    col = np.arange(len(DIL_GROUPS) * DIL_COLS)
    q_scale = np.where(col % DIL_COLS < DIL_OUT, DIL_HEAD_DIM ** -0.5, 1.0).astype(np.float32)
    wdil = (w_in[:, c1:c2] * q_scale[None, :]).astype(BF16)
    wgate = w_in[:, c2:].astype(BF16)
    pad = HEAD_PAD - MLA_NOPE - MLA_ROPE
    wuq_h = w_uq.reshape(MLA_Q_LORA, MLA_HEADS, MLA_NOPE + MLA_ROPE)
    wuq = jnp.pad(wuq_h, ((0, 0), (0, 0), (0, pad))).reshape(MLA_Q_LORA, MLA_HEADS * HEAD_PAD).astype(BF16)
    wukv_h = w_ukv.reshape(MLA_KV_LORA, MLA_HEADS, MLA_NOPE + MLA_V)
    wuk = jnp.pad(wukv_h[:, :, :MLA_NOPE], ((0, 0), (0, 0), (0, HEAD_PAD - MLA_NOPE)))
    wuk = wuk.reshape(MLA_KV_LORA, MLA_HEADS * HEAD_PAD).astype(BF16)
    wuv = wukv_h[:, :, MLA_NOPE:].reshape(MLA_KV_LORA, MLA_HEADS * MLA_V).astype(BF16)
    return wmla, wdil, wgate, wuq, wuk, wuv


def kernel(x, p, attn_norm, w_in, q_norm, w_uq, kv_norm, w_ukv, w_branch_a, w_branch_b, w_out, ffn_norm, w_router, b_router, w_gate, b_gate, w_up, b_up, w_down, b_down, ple_norm, w_ple_gate, w_ple_proj, final_norm):
    b, s, d = x.shape
    depth = w_in.shape[0]
    t = b * s
    assert d == D_MODEL and s % (DIL_GROUPS[-1][0]) == 0 and t % (SC_WORKERS * SC_WINDOW) == 0
    n_assign = t * TOP_K
    n_blocks = -(-(n_assign + N_EXPERTS * (ROW_BLOCK - 1)) // ROW_BLOCK)
    n_rows = n_blocks * ROW_BLOCK
    cos_t, sina_t, sinb_t = _rope_tables(s)
    xc = x.reshape(t, d)
    for i in range(depth):
        wmla, wdil, wgate, wuq, wuk, wuv = _prep_mixer_weights(w_in[i], w_uq[i], w_ukv[i])
        q, k, vt, zd0, zd1, zd2 = _inproj(
            xc, s, attn_norm[i][None], wmla, wdil, q_norm[i][None], kv_norm[i][None],
            wuq, wuk, wuv, cos_t, sina_t, sinb_t)
        oa = _mla_attention(q.reshape(b, s, -1), k.reshape(b, s, -1), vt)
        ob = _dilated_attention(zd0, zd1, zd2, s)
        x1, hp, topi_t, wcol = _merge(
            xc, oa.reshape(t, -1), ob, attn_norm[i][None], wgate,
            w_branch_a[i].astype(BF16), w_branch_b[i].astype(BF16), w_out[i].astype(BF16),
            ffn_norm[i][None], w_router[i].T.astype(BF16), b_router[i][:, None])
        dest_t, meta = _positions(topi_t)
        ends = meta[2, :N_EXPERTS]
        block_start = jnp.arange(n_blocks, dtype=I32) * ROW_BLOCK
        block_e = jnp.minimum(
            jnp.sum((ends[None, :] <= block_start[:, None]).astype(I32), axis=1), N_EXPERTS - 1)
        n_used = (ends[N_EXPERTS - 1:] // ROW_BLOCK).astype(I32)
        dest_flat = dest_t[:TOP_K].reshape(n_assign)
        xs = _dispatch_rows(hp, dest_flat, n_rows)
        ys = _expert_ffn(xs, block_e, n_used, meta[0, :N_EXPERTS], i,
                         w_gate, b_gate[:, :, None, :], w_up, b_up[:, :, None, :],
                         w_down, b_down[:, :, None, :])
        final = i == depth - 1
        gout = final_norm[None] if final else attn_norm[i][None]
        wpg, wpp = w_ple_gate[i].astype(BF16), w_ple_proj[i].astype(BF16)
        tp = t // COMBINE_PARTS
        xc = None
        for part in range(COMBINE_PARTS):
            dest_part = dest_t[:TOP_K, part * tp:(part + 1) * tp].reshape(TOP_K * tp)
            yg = _gather_rows(ys, dest_part).reshape(TOP_K, tp, HALF)
            xc = _combine(x1, yg, wcol, p.reshape(depth, t, PLE_DIM), i, ple_norm[i][None],
                          wpg, wpp, gout, final, part, xc)
    return xc.reshape(b, s, d)
```

```python
import functools
import math

import jax
import jax.numpy as jnp
import numpy as np
from jax import lax
from jax.experimental import pallas as pl
from jax.experimental.pallas import tpu as pltpu
from jax.experimental.pallas import tpu_sc as plsc

F32 = jnp.float32
BF16 = jnp.bfloat16
I32 = jnp.int32

D_MODEL = 1024
PLE_DIM = 256
NORM_EPS = 1e-6

MLA_HEADS = 8
MLA_Q_LORA = 384
MLA_KV_LORA = 256
MLA_NOPE = 64
MLA_ROPE = 32
MLA_V = 64
ROPE_THETA = 10000.0
HEAD_PAD = 128
HALF_ROPE = MLA_ROPE // 2

DIL_GROUPS = ((128, 1), (512, 4), (2048, 16))
DIL_HEADS = 4
DIL_HEAD_DIM = 64
DIL_STEPS = 128
DIL_COLS = 3 * DIL_HEADS * DIL_HEAD_DIM
DIL_OUT = DIL_HEADS * DIL_HEAD_DIM
DIL_UNROLL = 8

N_EXPERTS = 32
TOP_K = 4
D_EXPERT = 1024
SWIGLU_LIMIT = 7.0
SWIGLU_ALPHA = 1.702
ROW_BLOCK = 512

TOKEN_TILE = 512
POSITION_TILE = 4096
POSITION_SUBTILE = 512
COMBINE_PARTS = 2
ATTN_TQ = 1024
ATTN_TK = 1024
ATTN_HEADS = 4
ATTN_SUB = 1024
ATTN_KEY_CHUNKS = 2
HALF = D_MODEL // 2
NEG = -1e30
SPECULATION_HEADROOM = 60.0
HI_MASK = -65536

SC_CORES = 2
SC_SUBCORES = 16
SC_WORKERS = SC_CORES * SC_SUBCORES
SC_WINDOW = 64

VMEM_LIMIT = 56 * 1024 * 1024


def _cparams(sem):
    return pltpu.CompilerParams(dimension_semantics=sem, vmem_limit_bytes=VMEM_LIMIT)


def _rms(x, g):
    return x * lax.rsqrt(jnp.mean(x * x, axis=-1, keepdims=True) + NORM_EPS) * g


def _dot(a, b):
    return jnp.dot(a, b, preferred_element_type=F32)


def _dot_nt(a, b):
    return lax.dot_general(a, b, (((1,), (1,)), ((), ())), preferred_element_type=F32)


def _pack_halves(lo, hi):
    lo_i = lax.bitcast_convert_type(lo.astype(BF16).astype(F32), I32)
    hi_i = lax.bitcast_convert_type(hi.astype(BF16).astype(F32), I32)
    return (hi_i & HI_MASK) | lax.shift_right_logical(lo_i, 16)


def _unpack_halves(w):
    lo = lax.bitcast_convert_type(lax.shift_left(w, 16), F32)
    hi = lax.bitcast_convert_type(w & HI_MASK, F32)
    return lo, hi


def _inproj_kernel(x_ref, g_ref, wmla_ref, wdil_ref, qn_ref, kvn_ref, wuq_ref, wuk_ref,
                   wuv_ref, cos_ref, sina_ref, sinb_ref,
                   q_ref, k_ref, vt_ref, zd0_ref, zd1_ref, zd2_ref, zs_sc):
    h = _rms(x_ref[...], g_ref[...]).astype(BF16)
    zm = _dot(h, wmla_ref[...])
    cq = _rms(zm[:, :MLA_Q_LORA], qn_ref[...]).astype(BF16)
    ckv = _rms(zm[:, MLA_Q_LORA:MLA_Q_LORA + MLA_KV_LORA], kvn_ref[...]).astype(BF16)
    kr = zm[:, MLA_Q_LORA + MLA_KV_LORA:]
    cos, sina, sinb = cos_ref[...], sina_ref[...], sinb_ref[...]

    def rope(t):
        return (t * cos + pltpu.roll(t, HALF_ROPE, 1) * sina
                + pltpu.roll(t, HEAD_PAD - HALF_ROPE, 1) * sinb)

    kr_rot = rope(kr)
    qraw = _dot(cq, wuq_ref[...])
    kraw = _dot(ckv, wuk_ref[...])
    vt_ref[...] = _dot(ckv, wuv_ref[...]).T.astype(BF16)
    scale = (MLA_NOPE + MLA_ROPE) ** -0.5 * math.log2(math.e)
    for hd in range(MLA_HEADS):
        sl = slice(hd * HEAD_PAD, (hd + 1) * HEAD_PAD)
        q_ref[:, sl] = (rope(qraw[:, sl]) * scale).astype(BF16)
        k_ref[:, sl] = (kraw[:, sl] + kr_rot).astype(BF16)
    tm = x_ref.shape[0]
    for gi, zd_ref in enumerate((zd0_ref, zd1_ref, zd2_ref)):
        z = _dot(h, wdil_ref[:, gi * DIL_COLS:(gi + 1) * DIL_COLS])
        window, dil = DIL_GROUPS[gi]
        if dil == 1:
            zd_ref[...] = z.astype(BF16)
            continue
        n_col = DIL_COLS // HEAD_PAD
        for c in range(n_col):
            zs_sc[c] = z[:, c * HEAD_PAD:(c + 1) * HEAD_PAD]
        rows = tm // dil
        part = pl.program_id(0) % (window // tm)
        for r in range(dil):
            dst = pl.ds(pl.multiple_of(r * DIL_STEPS + part * rows, rows), rows)
            for c in range(n_col):
                chunk = zs_sc[c, pl.ds(r, rows, stride=dil), :]
                zd_ref[dst, c * HEAD_PAD:(c + 1) * HEAD_PAD] = chunk.astype(BF16)


def _inproj(x2d, seq, g, wmla, wdil, qn, kvn, wuq, wuk, wuv, cos_t, sina_t, sinb_t):
    t = x2d.shape[0]
    tm = TOKEN_TILE
    n_seq_tiles = seq // tm

    def row(i):
        return (i, 0)

    def const(i):
        return (0, 0)

    def pos(i):
        return (i % n_seq_tiles, 0)

    def full(a):
        return pl.BlockSpec(a.shape, const)

    def vt_block(i):
        return (i // n_seq_tiles, 0, i % n_seq_tiles)

    def rows(width):
        return pl.BlockSpec((tm, width), row)

    def unit_rows(window):
        return pl.BlockSpec((window, DIL_COLS), lambda i: (i // (window // tm), 0))

    out_shape = [
        jax.ShapeDtypeStruct((t, MLA_HEADS * HEAD_PAD), BF16),
        jax.ShapeDtypeStruct((t, MLA_HEADS * HEAD_PAD), BF16),
        jax.ShapeDtypeStruct((t // seq, MLA_HEADS * MLA_V, seq), BF16),
        jax.ShapeDtypeStruct((t, DIL_COLS), BF16),
        jax.ShapeDtypeStruct((t, DIL_COLS), BF16),
        jax.ShapeDtypeStruct((t, DIL_COLS), BF16),
    ]
    return pl.pallas_call(
        _inproj_kernel,
        out_shape=out_shape,
        grid=(t // tm,),
        in_specs=[pl.BlockSpec((tm, D_MODEL), row), full(g), full(wmla), full(wdil),
                  full(qn), full(kvn), full(wuq), full(wuk), full(wuv),
                  pl.BlockSpec((tm, HEAD_PAD), pos), pl.BlockSpec((tm, HEAD_PAD), pos),
                  pl.BlockSpec((tm, HEAD_PAD), pos)],
        out_specs=[rows(D_MODEL), rows(D_MODEL),
                   pl.BlockSpec((None, MLA_HEADS * MLA_V, tm), vt_block),
                   rows(DIL_COLS)] + [unit_rows(window) for window, _ in DIL_GROUPS[1:]],
        scratch_shapes=[pltpu.VMEM((DIL_COLS // HEAD_PAD, tm, HEAD_PAD), F32)],
        compiler_params=_cparams(("arbitrary",)),
        name="inproj",
    )(x2d, g, wmla, wdil, qn, kvn, wuq, wuk, wuv, cos_t, sina_t, sinb_t)


def _mla_kernel(qi_ref, kj_ref, q_ref, k_ref, vt_ref, o_ref, m_sc, l_sc, acc_sc, redo_sc):
    p = pl.program_id(2)
    i = qi_ref[p]
    j = kj_ref[p]
    tq = q_ref.shape[0]
    tk = k_ref.shape[0]

    @pl.when(j == 0)
    def _():
        m_sc[...] = jnp.full(m_sc.shape, NEG, F32)
        l_sc[...] = jnp.zeros(l_sc.shape, F32)
        acc_sc[...] = jnp.zeros(acc_sc.shape, F32)

    ratio = tq // tk
    sub = ATTN_SUB

    def step(diagonal, speculative):
        chains = [(hh, c) for hh in range(ATTN_HEADS) for c in range(tq // sub)]
        ones_rows = (lax.broadcasted_iota(I32, (16, tk), 0) == 0).astype(BF16)
        state = {}
        for hh, c in chains:
            cs = slice(c * sub, (c + 1) * sub)
            state[hh, c] = (m_sc[hh, :, cs], l_sc[hh, :, cs], acc_sc[hh, :, cs])
        new_state = {}
        within = None
        def scores(hh, c, rows, q0=0):
            sl = slice(hh * HEAD_PAD, (hh + 1) * HEAD_PAD)
            st = _dot_nt(k_ref[rows, sl], q_ref[c * sub + q0:(c + 1) * sub, sl])
            if diagonal:
                shape = (rows.stop - rows.start, sub - q0)
                key = lax.broadcasted_iota(I32, shape, 0) + (j * tk + rows.start)
                qry = lax.broadcasted_iota(I32, shape, 1) + (i * tq + c * sub + q0)
                st = jnp.where(qry >= key, st, NEG)
            return st

        def values(hh, rows):
            ones_row = (lax.broadcasted_iota(I32, (16, rows.stop - rows.start), 0) == 0).astype(BF16)
            return jnp.concatenate([vt_ref[hh * MLA_V:(hh + 1) * MLA_V, rows], ones_row], axis=0)

        for hh, c in chains:
            m_prev, l_prev, acc_prev = state[hh, c]
            if speculative:
                kc = tk // ATTN_KEY_CHUNKS
                m_blk, pv = None, None
                for n in range(ATTN_KEY_CHUNKS):
                    rows = slice(n * kc, (n + 1) * kc)
                    q0 = n * kc if (diagonal and tq == tk and sub == tq) else 0
                    st = scores(hh, c, rows, q0)
                    if n == 0:
                        m_prev = jnp.where(j == 0, st[:1], m_prev)
                    m_part = jnp.max(st, axis=0, keepdims=True)
                    part = _dot(values(hh, rows), jnp.exp2(st - m_prev[:, q0:]).astype(BF16))
                    if q0:
                        m_part = jnp.concatenate([jnp.full((1, q0), NEG, F32), m_part], axis=1)
                        part = jnp.concatenate([jnp.zeros((part.shape[0], q0), F32), part], axis=1)
                    m_blk = m_part if m_blk is None else jnp.maximum(m_blk, m_part)
                    pv = part if pv is None else pv + part
                m_new = jnp.maximum(m_prev, m_blk)
                alpha = jnp.exp2(m_prev - m_new)
                l_new = alpha * (l_prev + pv[MLA_V:MLA_V + 1])
                acc_new = alpha * (acc_prev + pv[:MLA_V])
                ok = jnp.max(m_blk - m_prev) <= SPECULATION_HEADROOM
                within = ok if within is None else (within & ok)
            else:
                rows = slice(0, tk)
                st = scores(hh, c, rows)
                m_new = jnp.maximum(m_prev, jnp.max(st, axis=0, keepdims=True))
                alpha = jnp.exp2(m_prev - m_new)
                pv = _dot(values(hh, rows), jnp.exp2(st - m_new).astype(BF16))
                l_new = alpha * l_prev + pv[MLA_V:MLA_V + 1]
                acc_new = alpha * acc_prev + pv[:MLA_V]
            new_state[hh, c] = (m_new, l_new, acc_new)

        def commit():
            for hh, c in chains:
                cs = slice(c * sub, (c + 1) * sub)
                m_sc[hh, :, cs], l_sc[hh, :, cs], acc_sc[hh, :, cs] = new_state[hh, c]

        if speculative:
            pl.when(within)(commit)
            redo_sc[0] = jnp.logical_not(within).astype(I32)
        else:
            commit()

    redo_sc[0] = 0
    on_diagonal = j >= ratio * i

    @pl.when(jnp.logical_not(on_diagonal))
    def _():
        step(False, True)

    @pl.when(on_diagonal)
    def _():
        step(True, True)

    @pl.when(redo_sc[0] != 0)
    def _():
        step(True, False)

    @pl.when(j == ratio * i + (ratio - 1))
    def _():
        ot = jnp.concatenate([acc_sc[hh] / l_sc[hh] for hh in range(ATTN_HEADS)], axis=0)
        o_ref[...] = ot.T.astype(BF16)


def _mla_attention(q, k, vt):
    b, s, _ = q.shape
    tq, tk, nh = ATTN_TQ, ATTN_TK, ATTN_HEADS
    ratio = tq // tk
    nq = s // tq
    pairs = [(i, j) for i in range(nq) for j in range(ratio * (i + 1))]
    qi = jnp.asarray([p[0] for p in pairs], I32)
    kj = jnp.asarray([p[1] for p in pairs], I32)
    grid_spec = pltpu.PrefetchScalarGridSpec(
        num_scalar_prefetch=2,
        grid=(b, MLA_HEADS // nh, len(pairs)),
        in_specs=[
            pl.BlockSpec((None, tq, nh * HEAD_PAD), lambda bb, hp, p, qi, kj: (bb, qi[p], hp)),
            pl.BlockSpec((None, tk, nh * HEAD_PAD), lambda bb, hp, p, qi, kj: (bb, kj[p], hp)),
            pl.BlockSpec((None, nh * MLA_V, tk), lambda bb, hp, p, qi, kj: (bb, hp, kj[p])),
        ],
        out_specs=pl.BlockSpec((None, tq, nh * MLA_V), lambda bb, hp, p, qi, kj: (bb, qi[p], hp)),
        scratch_shapes=[pltpu.VMEM((nh, 1, tq), F32), pltpu.VMEM((nh, 1, tq), F32),
                        pltpu.VMEM((nh, MLA_V, tq), F32), pltpu.SMEM((1,), I32)],
    )
    return pl.pallas_call(
        _mla_kernel,
        out_shape=jax.ShapeDtypeStruct((b, s, MLA_HEADS * MLA_V), BF16),
        grid_spec=grid_spec,
        compiler_params=_cparams(("parallel", "parallel", "arbitrary")),
        name="mla_attention",
    )(qi, kj, q, k, vt)


def _alibi_slopes(n):
    def pow2(m):
        start = 2.0 ** (-8.0 / m)
        return [start ** (i + 1) for i in range(m)]
    if math.log2(n).is_integer():
        s = pow2(n)
    else:
        c = 2 ** int(math.floor(math.log2(n)))
        s = pow2(c) + pow2(2 * c)[0::2][: n - c]
    return np.array(sorted(s, reverse=True), dtype=np.float32)


def _dilated_block(cur, prev, bias4, first, *, head_of_lane, low):
    n = DIL_STEPS
    hw = DIL_OUT
    q = cur[:, :hw]
    kk = jnp.concatenate([prev[:, hw:2 * hw], cur[:, hw:2 * hw]], axis=0)
    vv = jnp.concatenate([prev[:, 2 * hw:], cur[:, 2 * hw:]], axis=0)
    zero = jnp.zeros_like(q)
    q4 = jnp.concatenate([jnp.where(head_of_lane == h, q, zero) for h in range(DIL_HEADS)], axis=0)
    s4 = _dot_nt(q4, kk) + bias4
    if first is not None:
        ki = lax.broadcasted_iota(I32, (DIL_HEADS * n, 2 * n), 1)
        s4 = jnp.where(first & (ki < n), NEG, s4)
    m4 = jnp.max(s4, axis=1, keepdims=True)
    p4 = jnp.exp(s4 - m4).astype(BF16)
    l4 = jnp.broadcast_to(jnp.sum(p4.astype(F32), axis=1, keepdims=True), (DIL_HEADS * n, HEAD_PAD))
    pv4 = _dot(p4, vv)
    m4 = jnp.broadcast_to(m4, (DIL_HEADS * n, HEAD_PAD))

    def rows(a, h):
        return a[h * n:(h + 1) * n]

    o_un = rows(pv4, DIL_HEADS - 1)
    for h in range(DIL_HEADS - 2, -1, -1):
        o_un = jnp.where(head_of_lane == h, rows(pv4, h), o_un)
    def per_lane(a):
        return jnp.concatenate([jnp.where(low, rows(a, 0), rows(a, 1)),
                                jnp.where(low, rows(a, 2), rows(a, 3))], axis=1)

    l_sel = per_lane(l4)
    return o_un / l_sel, per_lane(m4) + jnp.log(l_sel)


def _dilated_kernel(c0_ref, h0_ref, c1_ref, h1_ref, c2_ref, h2_ref, ob_ref, o_sc, l_sc, bias_sc,
                    *, slopes):
    u = pl.program_id(1)
    n = DIL_STEPS
    unit = ob_ref.shape[0]
    n_sb = unit // n
    first = u == 0
    qi = lax.broadcasted_iota(I32, (n, 2 * n), 0)
    ki = lax.broadcasted_iota(I32, (n, 2 * n), 1)
    dist = qi + n - ki
    valid = (dist >= 0) & (dist <= n)
    distf = dist.astype(F32)
    for gi in range(len(DIL_GROUPS)):
        for h in range(DIL_HEADS):
            bias_sc[gi, h * n:(h + 1) * n, :] = jnp.where(valid, -slopes[gi][h] * distf, NEG)

    block = functools.partial(
        _dilated_block,
        head_of_lane=lax.broadcasted_iota(I32, (n, DIL_OUT), 1) // DIL_HEAD_DIM,
        low=lax.broadcasted_iota(I32, (n, HEAD_PAD), 1) < DIL_HEAD_DIM)

    def rows_of(ref, sb):
        return ref[pl.ds(pl.multiple_of(sb * n, n), n), :]

    def emit(gi, start, stride, o, lse):
        if isinstance(start, int):
            idx = pl.ds(start, n)
        elif stride == 1:
            idx = pl.ds(pl.multiple_of(start, n), n)
        else:
            idx = pl.ds(start, n, stride=stride)
        for half in range(DIL_OUT // HEAD_PAD):
            ls = slice(half * HEAD_PAD, (half + 1) * HEAD_PAD)
            o_sc[gi, half, idx, :] = o[:, ls]
            l_sc[gi, half, idx, :] = lse[:, ls]

    emit(0, 0, 1, *block(c0_ref[:n, :], h0_ref[...], bias_sc[0], first))

    def g0_body(sb, carry):
        emit(0, sb * n, 1, *block(rows_of(c0_ref, sb), rows_of(c0_ref, sb - 1), bias_sc[0], None))
        return carry

    lax.fori_loop(1, n_sb, g0_body, 0, unroll=DIL_UNROLL)

    d1 = DIL_GROUPS[1][1]

    def g1_head(r, carry):
        emit(1, r, d1, *block(rows_of(c1_ref, r), rows_of(h1_ref, r), bias_sc[1], first))
        return carry

    def g1_body(sb, carry):
        start = (sb // d1) * (d1 * n) + sb % d1
        emit(1, start, d1, *block(rows_of(c1_ref, sb), rows_of(c1_ref, sb - d1), bias_sc[1], None))
        return carry

    lax.fori_loop(0, d1, g1_head, 0, unroll=DIL_UNROLL)
    lax.fori_loop(d1, n_sb, g1_body, 0, unroll=DIL_UNROLL)

    d2 = DIL_GROUPS[2][1]

    def g2_body(r, carry):
        emit(2, r, d2, *block(rows_of(c2_ref, r), rows_of(h2_ref, r), bias_sc[2], first))
        return carry

    lax.fori_loop(0, n_sb, g2_body, 0, unroll=DIL_UNROLL)

    def merge_body(c, carry):
        idx = pl.ds(pl.multiple_of(c * n, n), n)
        for half in range(DIL_OUT // HEAD_PAD):
            l0, l1, l2 = l_sc[0, half, idx, :], l_sc[1, half, idx, :], l_sc[2, half, idx, :]
            lmax = jnp.maximum(jnp.maximum(l0, l1), l2)
            e0, e1, e2 = jnp.exp(l0 - lmax), jnp.exp(l1 - lmax), jnp.exp(l2 - lmax)
            ob = (e0 * o_sc[0, half, idx, :] + e1 * o_sc[1, half, idx, :]
                  + e2 * o_sc[2, half, idx, :]) / (e0 + e1 + e2)
            ob_ref[idx, half * HEAD_PAD:(half + 1) * HEAD_PAD] = ob.astype(BF16)
        return carry

    lax.fori_loop(0, n_sb, merge_body, 0)


def _dilated_attention(zd0, zd1, zd2, seq):
    t = zd0.shape[0]
    unit = DIL_GROUPS[-1][0]
    upb = seq // unit
    n = DIL_STEPS
    u1 = DIL_GROUPS[1][0]
    all_slopes = _alibi_slopes(len(DIL_GROUPS) * DIL_HEADS).reshape(len(DIL_GROUPS), DIL_HEADS)
    slopes = tuple(tuple(float(x) * dil for x in all_slopes[gi]) for gi, (_, dil) in enumerate(DIL_GROUPS))

    def cur(bb, u):
        return (bb * upb + u, 0)

    def halo(rows):
        per_unit = unit // rows
        return lambda bb, u: ((bb * upb) * per_unit + jnp.maximum(u * per_unit - 1, 0), 0)

    return pl.pallas_call(
        functools.partial(_dilated_kernel, slopes=slopes),
        out_shape=jax.ShapeDtypeStruct((t, DIL_OUT), BF16),
        grid=(t // seq, upb),
        in_specs=[pl.BlockSpec((unit, DIL_COLS), cur), pl.BlockSpec((n, DIL_COLS), halo(n)),
                  pl.BlockSpec((unit, DIL_COLS), cur), pl.BlockSpec((u1, DIL_COLS), halo(u1)),
                  pl.BlockSpec((unit, DIL_COLS), cur), pl.BlockSpec((unit, DIL_COLS), halo(unit))],
        out_specs=pl.BlockSpec((unit, DIL_OUT), cur),
        scratch_shapes=[pltpu.VMEM((len(DIL_GROUPS), DIL_OUT // HEAD_PAD, unit, HEAD_PAD), F32),
                        pltpu.VMEM((len(DIL_GROUPS), DIL_OUT // HEAD_PAD, unit, HEAD_PAD), F32),
                        pltpu.VMEM((len(DIL_GROUPS), DIL_HEADS * n, 2 * n), F32)],
        compiler_params=_cparams(("parallel", "arbitrary")),
        name="dilated_attention",
    )(zd0, zd0, zd1, zd1, zd2, zd2)


def _merge_kernel(x_ref, oa_ref, ob_ref, ga_ref, wgate_ref,
                  wa_ref, wb_ref, wo_ref, g_ref, wr_ref, br_ref,
                  x1_ref, hp_ref, topi_ref, wcol_ref):
    tm = x_ref.shape[0]
    x = x_ref[...]
    h = _rms(x, ga_ref[...]).astype(BF16)
    mixed = (jax.nn.sigmoid(_dot(h, wgate_ref[:, :D_MODEL])) * _dot(oa_ref[...], wa_ref[...])
             + jax.nn.sigmoid(_dot(h, wgate_ref[:, D_MODEL:])) * _dot(ob_ref[...], wb_ref[...]))
    x1 = x + _dot(mixed.astype(BF16), wo_ref[...])
    x1_ref[...] = x1
    h2 = _rms(x1, g_ref[...])
    hp_ref[...] = _pack_halves(h2[:, :HALF], h2[:, HALF:])

    logits = _dot_nt(wr_ref[...], h2.astype(BF16)) + br_ref[...]
    eidx = lax.broadcasted_iota(I32, (N_EXPERTS, tm), 0)
    vals, idxs = [], []
    for _ in range(TOP_K):
        m = jnp.max(logits, axis=0, keepdims=True)
        idx = jnp.min(jnp.where(logits == m, eidx, N_EXPERTS), axis=0, keepdims=True)
        vals.append(m)
        idxs.append(idx)
        logits = jnp.where(eidx == idx, -jnp.inf, logits)
    exps = [jnp.exp(vk - vals[0]) for vk in vals]
    den = exps[0] + exps[1] + exps[2] + exps[3]
    row8 = lax.broadcasted_iota(I32, (8, tm), 0)
    row128 = lax.broadcasted_iota(I32, (HEAD_PAD, tm), 0)
    topi = jnp.zeros((8, tm), I32)
    wide = jnp.zeros((HEAD_PAD, tm), F32)
    for kk in range(TOP_K):
        topi = jnp.where(row8 == kk, idxs[kk], topi)
        wide = jnp.where(row128 == kk, exps[kk] / den, wide)
    topi_ref[...] = topi
    wcol_ref[...] = wide.T


def _merge(x2d, oa, ob, g_attn, wgate, wa, wb, wo, g, wr_t, br_col):
    t = x2d.shape[0]
    tm = TOKEN_TILE

    def row(i):
        return (i, 0)

    def col(i):
        return (0, i)

    def full(a):
        return pl.BlockSpec(a.shape, lambda i: (0, 0))

    def rows(width):
        return pl.BlockSpec((tm, width), row)

    out_shape = [
        jax.ShapeDtypeStruct((t, D_MODEL), F32),
        jax.ShapeDtypeStruct((t, HALF), I32),
        jax.ShapeDtypeStruct((8, t), I32),
        jax.ShapeDtypeStruct((t, HEAD_PAD), F32),
    ]
    return pl.pallas_call(
        _merge_kernel,
        out_shape=out_shape,
        grid=(t // tm,),
        in_specs=[rows(D_MODEL), rows(MLA_HEADS * MLA_V), rows(DIL_OUT), full(g_attn), full(wgate)]
        + [full(wa), full(wb), full(wo), full(g), full(wr_t), full(br_col)],
        out_specs=[rows(D_MODEL), rows(HALF), pl.BlockSpec((8, tm), col), rows(HEAD_PAD)],
        compiler_params=_cparams(("parallel",)),
        name="merge_router",
    )(x2d, oa, ob, g_attn, wgate, wa, wb, wo, g, wr_t, br_col)


def _positions_kernel(topi_ref, dest_ref, meta_ref, cnt_sc, carry_sc, start_sc):
    ps = pl.program_id(0)
    i = pl.program_id(1)
    tm = POSITION_SUBTILE
    n_sub = topi_ref.shape[1] // tm
    eidx = lax.broadcasted_iota(I32, (N_EXPERTS, tm), 0)

    def hits_of(sb):
        topi = topi_ref[:, sb * tm:(sb + 1) * tm]
        return [eidx == topi[kk:kk + 1, :] for kk in range(TOP_K)]

    def members(hits):
        return hits[0] | hits[1] | hits[2] | hits[3]

    @pl.when((ps == 0) & (i == 0))
    def _():
        cnt_sc[...] = jnp.zeros(cnt_sc.shape, F32)

    @pl.when(ps == 0)
    def _():
        total = cnt_sc[...]
        for sb in range(n_sub):
            total = total + jnp.sum(members(hits_of(sb)).astype(F32), axis=1, keepdims=True)
        cnt_sc[...] = total

    @pl.when((ps == 1) & (i == 0))
    def _():
        cnt = cnt_sc[...].astype(I32)
        shift = ROW_BLOCK.bit_length() - 1
        padded = lax.shift_left(lax.shift_right_logical(cnt + (ROW_BLOCK - 1), shift), shift)
        sub = lax.broadcasted_iota(I32, (N_EXPERTS, HEAD_PAD), 0)
        lane = lax.broadcasted_iota(I32, (N_EXPERTS, HEAD_PAD), 1)
        padded_row = jnp.sum(jnp.where(sub == lane, padded, 0), axis=0, keepdims=True)
        start = jnp.sum(jnp.where(lane < sub, padded_row, 0), axis=1, keepdims=True)
        start_sc[...] = start.astype(F32)
        carry_sc[...] = jnp.zeros(carry_sc.shape, F32)
        cnt_row = jnp.sum(jnp.where(sub == lane, cnt, 0), axis=0, keepdims=True)
        start_row = jnp.sum(jnp.where(sub == lane, start, 0), axis=0, keepdims=True)
        row8 = lax.broadcasted_iota(I32, (8, HEAD_PAD), 0)
        meta = jnp.where(row8 == 0, cnt_row, 0)
        meta = jnp.where(row8 == 1, start_row, meta)
        meta = jnp.where(row8 == 2, start_row + padded_row, meta)
        meta_ref[...] = meta

    @pl.when(ps == 1)
    def _():
        tr = lax.broadcasted_iota(I32, (tm, tm), 0)
        tc = lax.broadcasted_iota(I32, (tm, tm), 1)
        before = (tr < tc).astype(BF16)
        row8 = lax.broadcasted_iota(I32, (8, tm), 0)
        offset = carry_sc[...] + start_sc[...]
        for sb in range(n_sub):
            hits = hits_of(sb)
            member = members(hits)
            base = _dot(member.astype(BF16), before) + offset
            dest = jnp.zeros((8, tm), I32)
            for kk in range(TOP_K):
                dk = jnp.sum(jnp.where(hits[kk], base, 0.0), axis=0, keepdims=True).astype(I32)
                dest = jnp.where(row8 == kk, dk, dest)
            dest_ref[:, sb * tm:(sb + 1) * tm] = dest
            offset = offset + jnp.sum(member.astype(F32), axis=1, keepdims=True)
        carry_sc[...] = offset - start_sc[...]


def _positions(topi_t):
    t = topi_t.shape[1]
    tm = min(POSITION_TILE, t)
    return pl.pallas_call(
        _positions_kernel,
        out_shape=[jax.ShapeDtypeStruct((8, t), I32), jax.ShapeDtypeStruct((8, HEAD_PAD), I32)],
        grid=(2, t // tm),
        in_specs=[pl.BlockSpec((8, tm), lambda ps, i: (0, i))],
        out_specs=[pl.BlockSpec((8, tm), lambda ps, i: (0, i * ps)),
                   pl.BlockSpec((8, HEAD_PAD), lambda ps, i: (0, 0))],
        scratch_shapes=[pltpu.VMEM((N_EXPERTS, 1), F32)] * 3,
        compiler_params=_cparams(("arbitrary", "arbitrary")),
        name="routing_positions",
    )(topi_t)


def _sc_mesh():
    return plsc.VectorSubcoreMesh(core_axis_name="c", subcore_axis_name="s")


def _dispatch_rows(table, dest_flat, n_rows):
    t, c = table.shape
    n_slots = dest_flat.shape[0] // t
    per_w = t // SC_WORKERS
    assert per_w * SC_WORKERS == t and per_w % (2 * SC_WINDOW) == 0
    n_chunks = per_w // SC_WINDOW
    w = SC_WINDOW

    @functools.partial(
        pl.kernel, mesh=_sc_mesh(),
        out_type=jax.ShapeDtypeStruct((n_rows, c), table.dtype),
        scratch_types=[pltpu.VMEM((w,), I32)] * n_slots + [pltpu.VMEM((w, c), table.dtype)] * 2
        + [pltpu.SemaphoreType.DMA] * (n_slots + 2),
        name="dispatch_rows",
    )
    def k(table_hbm, dest_hbm, out_hbm, *scratch):
        idx = scratch[:n_slots]
        rows = scratch[n_slots:n_slots + 2]
        scatter_sems = scratch[n_slots + 2:2 * n_slots + 2]
        read_sems = scratch[2 * n_slots + 2:]
        wid = lax.axis_index("s") * SC_CORES + lax.axis_index("c")
        base = wid * per_w

        def off(chunk):
            return pl.multiple_of(base + chunk * w, w)

        def read(chunk, buf):
            return pltpu.make_async_copy(table_hbm.at[pl.ds(off(chunk), w)], rows[buf], read_sems[buf])

        def scatter(kk, buf):
            return pltpu.make_async_copy(rows[buf], out_hbm.at[idx[kk]], scatter_sems[kk])

        read(0, 0).start()

        @pl.loop(0, n_chunks // 2)
        def _(p):
            for buf in range(2):
                chunk = 2 * p + buf

                @pl.when(chunk + 1 < n_chunks)
                def _():
                    read(chunk + 1, 1 - buf).start()

                read(chunk, buf).wait()
                for kk in range(n_slots):
                    src = pl.multiple_of(kk * t + off(chunk), w)
                    pltpu.sync_copy(dest_hbm.at[pl.ds(src, w)], idx[kk])
                    scatter(kk, buf).start()
                for kk in range(n_slots):
                    scatter(kk, buf).wait()

    return k(table, dest_flat)


def _gather_rows(table, idx):
    n = idx.shape[0]
    c = table.shape[1]
    per_w = n // SC_WORKERS
    assert per_w * SC_WORKERS == n and per_w % (2 * SC_WINDOW) == 0
    n_chunks = per_w // SC_WINDOW
    w = SC_WINDOW

    @functools.partial(
        pl.kernel, mesh=_sc_mesh(),
        out_type=jax.ShapeDtypeStruct((n, c), table.dtype),
        scratch_types=[pltpu.VMEM((w,), I32)] * 2 + [pltpu.VMEM((w, c), table.dtype)] * 2
        + [pltpu.SemaphoreType.DMA] * 4,
        name="gather_rows",
    )
    def k(table_hbm, idx_hbm, out_hbm, idx_a, idx_b, rows_a, rows_b, g_a, g_b, w_a, w_b):
        idx, rows, gather_sems, write_sems = (idx_a, idx_b), (rows_a, rows_b), (g_a, g_b), (w_a, w_b)
        wid = lax.axis_index("s") * SC_CORES + lax.axis_index("c")
        base = wid * per_w

        def off(chunk):
            return pl.multiple_of(base + chunk * w, w)

        def gather(buf):
            return pltpu.make_async_copy(table_hbm.at[idx[buf]], rows[buf], gather_sems[buf])

        def write(chunk, buf):
            return pltpu.make_async_copy(rows[buf], out_hbm.at[pl.ds(off(chunk), w)], write_sems[buf])

        def start_gather(chunk, buf):
            pltpu.sync_copy(idx_hbm.at[pl.ds(off(chunk), w)], idx[buf])
            gather(buf).start()

        start_gather(0, 0)

        @pl.loop(0, n_chunks // 2)
        def _(p):
            for buf in range(2):
                chunk = 2 * p + buf

                @pl.when(chunk + 1 < n_chunks)
                def _():
                    @pl.when(chunk >= 1)
                    def _():
                        write(chunk - 1, 1 - buf).wait()
                    start_gather(chunk + 1, 1 - buf)

                gather(buf).wait()
                write(chunk, buf).start()

        write(n_chunks - 2, 0).wait()
        write(n_chunks - 1, 1).wait()

    return k(table, idx)


def _expert_kernel(be_ref, nused_ref, first_ref, slot_ref, next_ref,
                   xs_ref, wg_hbm, bg_ref, wu_hbm, bu_ref, wd_hbm, bd_ref,
                   ys_ref, wg_ref, wu_ref, wd_ref, stage_g, stage_u, stage_d, sems, *, layer):
    b = pl.program_id(0)
    used = b < nused_ref[0]
    weights = ((wg_hbm, stage_g), (wu_hbm, stage_u), (wd_hbm, stage_d))

    def fetch(expert, slot):
        return [pltpu.make_async_copy(hbm.at[layer, expert], stage.at[slot], sems.at[slot, n])
                for n, (hbm, stage) in enumerate(weights)]

    @pl.when(used & (first_ref[b] == 1))
    def _():
        slot = slot_ref[b]

        @pl.when(b == 0)
        def _():
            for cp in fetch(be_ref[0], slot):
                cp.start()

        for cp in fetch(be_ref[b], slot):
            cp.wait()
        wg_ref[...] = stage_g[slot].astype(BF16)
        wu_ref[...] = stage_u[slot].astype(BF16)
        wd_ref[...] = stage_d[slot].astype(BF16)

        @pl.when(next_ref[b] >= 0)
        def _():
            for cp in fetch(next_ref[b], 1 - slot):
                cp.start()

    @pl.when(used)
    def _():
        lo, hi = _unpack_halves(xs_ref[...])
        xb = jnp.concatenate([lo.astype(BF16), hi.astype(BF16)], axis=1)
        a = _dot(xb, wg_ref[...]) + bg_ref[...]
        u = _dot(xb, wu_ref[...]) + bu_ref[...]
        a = jnp.minimum(a, SWIGLU_LIMIT)
        u = jnp.clip(u, -SWIGLU_LIMIT, SWIGLU_LIMIT)
        y = (a * jax.nn.sigmoid(SWIGLU_ALPHA * a)) * (u + 1.0)
        out = _dot(y.astype(BF16), wd_ref[...]) + bd_ref[...]
        ys_ref[...] = _pack_halves(out[:, :HALF], out[:, HALF:])

    @pl.when(b >= nused_ref[0])
    def _():
        ys_ref[...] = jnp.zeros(ys_ref.shape, I32)


def _expert_ffn(xs, block_e, n_used, counts, layer, wg, bg, wu, bu, wd, bd):
    n_rows = xs.shape[0]
    n_blocks = n_rows // ROW_BLOCK

    blk = jnp.arange(n_blocks, dtype=I32)
    used = blk < n_used[0]
    first = used & ((blk == 0) | (block_e != jnp.roll(block_e, 1)))
    slot = (jnp.cumsum(first.astype(I32)) - 1) % 2
    eid = jnp.arange(N_EXPERTS, dtype=I32)
    later = (eid[None, :] > eid[:, None]) & (counts[None, :] > 0)
    next_of_expert = jnp.min(jnp.where(later, eid[None, :], N_EXPERTS), axis=1)
    next_of_expert = jnp.where(next_of_expert == N_EXPERTS, -1, next_of_expert)
    next_e = jnp.sum(jnp.where(block_e[:, None] == eid[None, :], next_of_expert[None, :], 0), axis=1)

    def rows(b, *_):
        return (b, 0)

    def expert(b, be, *_):
        return (layer, be[b], 0, 0)

    hbm = pl.BlockSpec(memory_space=pl.ANY)
    grid_spec = pltpu.PrefetchScalarGridSpec(
        num_scalar_prefetch=5,
        grid=(n_blocks,),
        in_specs=[pl.BlockSpec((ROW_BLOCK, HALF), rows),
                  hbm, pl.BlockSpec((None, None, 1, D_EXPERT), expert),
                  hbm, pl.BlockSpec((None, None, 1, D_EXPERT), expert),
                  hbm, pl.BlockSpec((None, None, 1, D_MODEL), expert)],
        out_specs=pl.BlockSpec((ROW_BLOCK, HALF), rows),
        scratch_shapes=[pltpu.VMEM((D_MODEL, D_EXPERT), BF16), pltpu.VMEM((D_MODEL, D_EXPERT), BF16),
                        pltpu.VMEM((D_EXPERT, D_MODEL), BF16),
                        pltpu.VMEM((2, D_MODEL, D_EXPERT), F32), pltpu.VMEM((2, D_MODEL, D_EXPERT), F32),
                        pltpu.VMEM((2, D_EXPERT, D_MODEL), F32),
                        pltpu.SemaphoreType.DMA((2, 3))],
    )
    return pl.pallas_call(
        functools.partial(_expert_kernel, layer=layer),
        out_shape=jax.ShapeDtypeStruct((n_rows, HALF), I32),
        grid_spec=grid_spec,
        compiler_params=_cparams(("arbitrary",)),
        name="expert_ffn",
    )(block_e, n_used, first.astype(I32), slot.astype(I32), next_e.astype(I32),
      xs, wg, bg, wu, bu, wd, bd)


def _combine_kernel(x1_ref, yg_ref, wcol_ref, p_ref, gple_ref, wpg_ref, wpp_ref, gout_ref, o_ref,
                    *, final):
    x1 = x1_ref[...]
    acc_lo = x1[:, :HALF]
    acc_hi = x1[:, HALF:]
    wcol = wcol_ref[...]
    for kk in range(TOP_K):
        lo, hi = _unpack_halves(yg_ref[kk])
        wk = wcol[:, kk:kk + 1]
        acc_lo = acc_lo + wk * lo
        acc_hi = acc_hi + wk * hi
    x2 = jnp.concatenate([acc_lo, acc_hi], axis=1)
    gate = jax.nn.sigmoid(_dot(_rms(x2, gple_ref[...]).astype(BF16), wpg_ref[...]))
    x3 = x2 + gate * _dot(p_ref[...].astype(BF16), wpp_ref[...])
    o_ref[...] = _rms(x3, gout_ref[...]) if final else x3


def _combine_kernel_inplace(x1_ref, yg_ref, wcol_ref, p_ref, gple_ref, wpg_ref, wpp_ref, gout_ref,
                            prev_ref, o_ref, *, final):
    del prev_ref
    _combine_kernel(x1_ref, yg_ref, wcol_ref, p_ref, gple_ref, wpg_ref, wpp_ref, gout_ref, o_ref,
                    final=final)


def _combine(x1, yg, wcol, p3d, layer, gple, wpg, wpp, gout, final, part, prev):
    t = x1.shape[0]
    tm = TOKEN_TILE
    n_tiles = yg.shape[1] // tm
    first_tile = part * n_tiles

    def row(i, *_):
        return (first_tile + i, 0)

    def full(a):
        return pl.BlockSpec(a.shape, lambda i: (0, 0))

    in_specs = [pl.BlockSpec((tm, D_MODEL), row),
                pl.BlockSpec((TOP_K, tm, HALF), lambda i: (0, i, 0)),
                pl.BlockSpec((tm, HEAD_PAD), row),
                pl.BlockSpec((None, tm, PLE_DIM), lambda i: (layer, first_tile + i, 0)),
                full(gple), full(wpg), full(wpp), full(gout)]
    args = [x1, yg, wcol, p3d, gple, wpg, wpp, gout]
    kern = functools.partial(_combine_kernel, final=final)
    aliases = {}
    if prev is not None:
        in_specs.append(pl.BlockSpec(memory_space=pl.ANY))
        args.append(prev)
        aliases = {len(args) - 1: 0}
        kern = functools.partial(_combine_kernel_inplace, final=final)
    return pl.pallas_call(
        kern,
        out_shape=jax.ShapeDtypeStruct((t, D_MODEL), F32),
        grid=(n_tiles,),
        in_specs=in_specs,
        out_specs=pl.BlockSpec((tm, D_MODEL), row),
        input_output_aliases=aliases,
        compiler_params=_cparams(("parallel",)),
        name="combine_ple",
    )(*args)


def _rope_tables(seq):
    inv_freq = ROPE_THETA ** (-jnp.arange(HALF_ROPE, dtype=F32) * 2.0 / MLA_ROPE)
    ang = jnp.arange(seq, dtype=F32)[:, None] * inv_freq[None, :]
    cos, sin = jnp.cos(ang), jnp.sin(ang)
    ones = jnp.ones((seq, MLA_NOPE), F32)
    zeros16 = jnp.zeros((seq, HALF_ROPE), F32)
    zeros64 = jnp.zeros((seq, MLA_NOPE), F32)
    tail = jnp.ones((seq, HEAD_PAD - MLA_NOPE - MLA_ROPE), F32)
    ztail = jnp.zeros_like(tail)
    cos_t = jnp.concatenate([ones, cos, cos, tail], axis=1)
    sina_t = jnp.concatenate([zeros64, zeros16, sin, ztail], axis=1)
    sinb_t = jnp.concatenate([zeros64, -sin, zeros16, ztail], axis=1)
    return cos_t, sina_t, sinb_t


def _prep_mixer_weights(w_in, w_uq, w_ukv):
    c0 = MLA_Q_LORA + MLA_KV_LORA
    c1 = c0 + MLA_ROPE
    c2 = c1 + len(DIL_GROUPS) * DIL_COLS
    kr_pad = jnp.pad(w_in[:, c0:c1], ((0, 0), (MLA_NOPE, HEAD_PAD - MLA_NOPE - MLA_ROPE)))
    wmla = jnp.concatenate([w_in[:, :c0], kr_pad], axis=1).astype(BF16)
    col = np.arange(len(DIL_GROUPS) * DIL_COLS)
    q_scale = np.where(col % DIL_COLS < DIL_OUT, DIL_HEAD_DIM ** -0.5, 1.0).astype(np.float32)
    wdil = (w_in[:, c1:c2] * q_scale[None, :]).astype(BF16)
    wgate = w_in[:, c2:].astype(BF16)
    pad = HEAD_PAD - MLA_NOPE - MLA_ROPE
    wuq_h = w_uq.reshape(MLA_Q_LORA, MLA_HEADS, MLA_NOPE + MLA_ROPE)
    wuq = jnp.pad(wuq_h, ((0, 0), (0, 0), (0, pad))).reshape(MLA_Q_LORA, MLA_HEADS * HEAD_PAD).astype(BF16)
    wukv_h = w_ukv.reshape(MLA_KV_LORA, MLA_HEADS, MLA_NOPE + MLA_V)
    wuk = jnp.pad(wukv_h[:, :, :MLA_NOPE], ((0, 0), (0, 0), (0, HEAD_PAD - MLA_NOPE)))
    wuk = wuk.reshape(MLA_KV_LORA, MLA_HEADS * HEAD_PAD).astype(BF16)
    wuv = wukv_h[:, :, MLA_NOPE:].reshape(MLA_KV_LORA, MLA_HEADS * MLA_V).astype(BF16)
    return wmla, wdil, wgate, wuq, wuk, wuv


def kernel(x, p, attn_norm, w_in, q_norm, w_uq, kv_norm, w_ukv, w_branch_a, w_branch_b, w_out, ffn_norm, w_router, b_router, w_gate, b_gate, w_up, b_up, w_down, b_down, ple_norm, w_ple_gate, w_ple_proj, final_norm):
    b, s, d = x.shape
    depth = w_in.shape[0]
    t = b * s
    assert d == D_MODEL and s % (DIL_GROUPS[-1][0]) == 0 and t % (SC_WORKERS * SC_WINDOW) == 0
    n_assign = t * TOP_K
    n_blocks = -(-(n_assign + N_EXPERTS * (ROW_BLOCK - 1)) // ROW_BLOCK)
    n_rows = n_blocks * ROW_BLOCK
    cos_t, sina_t, sinb_t = _rope_tables(s)
    xc = x.reshape(t, d)
    for i in range(depth):
        wmla, wdil, wgate, wuq, wuk, wuv = _prep_mixer_weights(w_in[i], w_uq[i], w_ukv[i])
        q, k, vt, zd0, zd1, zd2 = _inproj(
            xc, s, attn_norm[i][None], wmla, wdil, q_norm[i][None], kv_norm[i][None],
            wuq, wuk, wuv, cos_t, sina_t, sinb_t)
        oa = _mla_attention(q.reshape(b, s, -1), k.reshape(b, s, -1), vt)
        ob = _dilated_attention(zd0, zd1, zd2, s)
        x1, hp, topi_t, wcol = _merge(
            xc, oa.reshape(t, -1), ob, attn_norm[i][None], wgate,
            w_branch_a[i].astype(BF16), w_branch_b[i].astype(BF16), w_out[i].astype(BF16),
            ffn_norm[i][None], w_router[i].T.astype(BF16), b_router[i][:, None])
        dest_t, meta = _positions(topi_t)
        ends = meta[2, :N_EXPERTS]
        block_start = jnp.arange(n_blocks, dtype=I32) * ROW_BLOCK
        block_e = jnp.minimum(
            jnp.sum((ends[None, :] <= block_start[:, None]).astype(I32), axis=1), N_EXPERTS - 1)
        n_used = (ends[N_EXPERTS - 1:] // ROW_BLOCK).astype(I32)
        dest_flat = dest_t[:TOP_K].reshape(n_assign)
        xs = _dispatch_rows(hp, dest_flat, n_rows)
        ys = _expert_ffn(xs, block_e, n_used, meta[0, :N_EXPERTS], i,
                         w_gate, b_gate[:, :, None, :], w_up, b_up[:, :, None, :],
                         w_down, b_down[:, :, None, :])
        final = i == depth - 1
        gout = final_norm[None] if final else attn_norm[i][None]
        wpg, wpp = w_ple_gate[i].astype(BF16), w_ple_proj[i].astype(BF16)
        tp = t // COMBINE_PARTS
        xc = None
        for part in range(COMBINE_PARTS):
            dest_part = dest_t[:TOP_K, part * tp:(part + 1) * tp].reshape(TOP_K * tp)
            yg = _gather_rows(ys, dest_part).reshape(TOP_K, tp, HALF)
            xc = _combine(x1, yg, wcol, p.reshape(depth, t, PLE_DIM), i, ple_norm[i][None],
                          wpg, wpp, gout, final, part, xc)
    return xc.reshape(b, s, d)
```

```python
import functools
import math

import jax
import jax.numpy as jnp
import numpy as np
from jax import lax
from jax.experimental import pallas as pl
from jax.experimental.pallas import tpu as pltpu
from jax.experimental.pallas import tpu_sc as plsc

F32 = jnp.float32
BF16 = jnp.bfloat16
I32 = jnp.int32

D_MODEL = 1024
PLE_DIM = 256
NORM_EPS = 1e-6

MLA_HEADS = 8
MLA_Q_LORA = 384
MLA_KV_LORA = 256
MLA_NOPE = 64
MLA_ROPE = 32
MLA_V = 64
ROPE_THETA = 10000.0
HEAD_PAD = 128
HALF_ROPE = MLA_ROPE // 2

DIL_GROUPS = ((128, 1), (512, 4), (2048, 16))
DIL_HEADS = 4
DIL_HEAD_DIM = 64
DIL_STEPS = 128
DIL_COLS = 3 * DIL_HEADS * DIL_HEAD_DIM
DIL_OUT = DIL_HEADS * DIL_HEAD_DIM
DIL_UNROLL = 8

N_EXPERTS = 32
TOP_K = 4
D_EXPERT = 1024
SWIGLU_LIMIT = 7.0
SWIGLU_ALPHA = 1.702
ROW_BLOCK = 512

TOKEN_TILE = 512
POSITION_TILE = 4096
POSITION_SUBTILE = 512
COMBINE_PARTS = 2
ATTN_TQ = 1024
ATTN_TK = 1024
ATTN_HEADS = 4
ATTN_SUB = 1024
ATTN_KEY_CHUNKS = 2
HALF = D_MODEL // 2
NEG = -1e30
SPECULATION_HEADROOM = 60.0
HI_MASK = -65536

SC_CORES = 2
SC_SUBCORES = 16
SC_WORKERS = SC_CORES * SC_SUBCORES
SC_WINDOW = 64

VMEM_LIMIT = 56 * 1024 * 1024


def _cparams(sem):
    return pltpu.CompilerParams(dimension_semantics=sem, vmem_limit_bytes=VMEM_LIMIT)


def _rms(x, g):
    return x * lax.rsqrt(jnp.mean(x * x, axis=-1, keepdims=True) + NORM_EPS) * g


def _dot(a, b):
    return jnp.dot(a, b, preferred_element_type=F32)


def _dot_nt(a, b):
    return lax.dot_general(a, b, (((1,), (1,)), ((), ())), preferred_element_type=F32)


def _pack_halves(lo, hi):
    lo_i = lax.bitcast_convert_type(lo.astype(BF16).astype(F32), I32)
    hi_i = lax.bitcast_convert_type(hi.astype(BF16).astype(F32), I32)
    return (hi_i & HI_MASK) | lax.shift_right_logical(lo_i, 16)


def _unpack_halves(w):
    lo = lax.bitcast_convert_type(lax.shift_left(w, 16), F32)
    hi = lax.bitcast_convert_type(w & HI_MASK, F32)
    return lo, hi


def _inproj_kernel(x_ref, g_ref, wmla_ref, wdil_ref, qn_ref, kvn_ref, wuq_ref, wuk_ref,
                   wuv_ref, cos_ref, sina_ref, sinb_ref,
                   q_ref, k_ref, vt_ref, zd0_ref, zd1_ref, zd2_ref, zs_sc):
    h = _rms(x_ref[...], g_ref[...]).astype(BF16)
    zm = _dot(h, wmla_ref[...])
    cq = _rms(zm[:, :MLA_Q_LORA], qn_ref[...]).astype(BF16)
    ckv = _rms(zm[:, MLA_Q_LORA:MLA_Q_LORA + MLA_KV_LORA], kvn_ref[...]).astype(BF16)
    kr = zm[:, MLA_Q_LORA + MLA_KV_LORA:]
    cos, sina, sinb = cos_ref[...], sina_ref[...], sinb_ref[...]

    def rope(t):
        return (t * cos + pltpu.roll(t, HALF_ROPE, 1) * sina
                + pltpu.roll(t, HEAD_PAD - HALF_ROPE, 1) * sinb)

    kr_rot = rope(kr)
    qraw = _dot(cq, wuq_ref[...])
    kraw = _dot(ckv, wuk_ref[...])
    vt_ref[...] = _dot(ckv, wuv_ref[...]).T.astype(BF16)
    scale = (MLA_NOPE + MLA_ROPE) ** -0.5 * math.log2(math.e)
    for hd in range(MLA_HEADS):
        sl = slice(hd * HEAD_PAD, (hd + 1) * HEAD_PAD)
        q_ref[:, sl] = (rope(qraw[:, sl]) * scale).astype(BF16)
        k_ref[:, sl] = (kraw[:, sl] + kr_rot).astype(BF16)
    tm = x_ref.shape[0]
    for gi, zd_ref in enumerate((zd0_ref, zd1_ref, zd2_ref)):
        z = _dot(h, wdil_ref[:, gi * DIL_COLS:(gi + 1) * DIL_COLS])
        window, dil = DIL_GROUPS[gi]
        if dil == 1:
            zd_ref[...] = z.astype(BF16)
            continue
        n_col = DIL_COLS // HEAD_PAD
        for c in range(n_col):
            zs_sc[c] = z[:, c * HEAD_PAD:(c + 1) * HEAD_PAD]
        rows = tm // dil
        part = pl.program_id(0) % (window // tm)
        for r in range(dil):
            dst = pl.ds(pl.multiple_of(r * DIL_STEPS + part * rows, rows), rows)
            for c in range(n_col):
                chunk = zs_sc[c, pl.ds(r, rows, stride=dil), :]
                zd_ref[dst, c * HEAD_PAD:(c + 1) * HEAD_PAD] = chunk.astype(BF16)


def _inproj(x2d, seq, g, wmla, wdil, qn, kvn, wuq, wuk, wuv, cos_t, sina_t, sinb_t):
    t = x2d.shape[0]
    tm = TOKEN_TILE
    n_seq_tiles = seq // tm

    def row(i):
        return (i, 0)

    def const(i):
        return (0, 0)

    def pos(i):
        return (i % n_seq_tiles, 0)

    def full(a):
        return pl.BlockSpec(a.shape, const)

    def vt_block(i):
        return (i // n_seq_tiles, 0, i % n_seq_tiles)

    def rows(width):
        return pl.BlockSpec((tm, width), row)

    def unit_rows(window):
        return pl.BlockSpec((window, DIL_COLS), lambda i: (i // (window // tm), 0))

    out_shape = [
        jax.ShapeDtypeStruct((t, MLA_HEADS * HEAD_PAD), BF16),
        jax.ShapeDtypeStruct((t, MLA_HEADS * HEAD_PAD), BF16),
        jax.ShapeDtypeStruct((t // seq, MLA_HEADS * MLA_V, seq), BF16),
        jax.ShapeDtypeStruct((t, DIL_COLS), BF16),
        jax.ShapeDtypeStruct((t, DIL_COLS), BF16),
        jax.ShapeDtypeStruct((t, DIL_COLS), BF16),
    ]
    return pl.pallas_call(
        _inproj_kernel,
        out_shape=out_shape,
        grid=(t // tm,),
        in_specs=[pl.BlockSpec((tm, D_MODEL), row), full(g), full(wmla), full(wdil),
                  full(qn), full(kvn), full(wuq), full(wuk), full(wuv),
                  pl.BlockSpec((tm, HEAD_PAD), pos), pl.BlockSpec((tm, HEAD_PAD), pos),
                  pl.BlockSpec((tm, HEAD_PAD), pos)],
        out_specs=[rows(D_MODEL), rows(D_MODEL),
                   pl.BlockSpec((None, MLA_HEADS * MLA_V, tm), vt_block),
                   rows(DIL_COLS)] + [unit_rows(window) for window, _ in DIL_GROUPS[1:]],
        scratch_shapes=[pltpu.VMEM((DIL_COLS // HEAD_PAD, tm, HEAD_PAD), F32)],
        compiler_params=_cparams(("arbitrary",)),
        name="inproj",
    )(x2d, g, wmla, wdil, qn, kvn, wuq, wuk, wuv, cos_t, sina_t, sinb_t)


def _mla_kernel(qi_ref, kj_ref, q_ref, k_ref, vt_ref, o_ref, m_sc, l_sc, acc_sc, redo_sc):
    p = pl.program_id(2)
    i = qi_ref[p]
    j = kj_ref[p]
    tq = q_ref.shape[0]
    tk = k_ref.shape[0]

    @pl.when(j == 0)
    def _():
        m_sc[...] = jnp.full(m_sc.shape, NEG, F32)
        l_sc[...] = jnp.zeros(l_sc.shape, F32)
        acc_sc[...] = jnp.zeros(acc_sc.shape, F32)

    ratio = tq // tk
    sub = ATTN_SUB

    def step(diagonal, speculative):
        chains = [(hh, c) for hh in range(ATTN_HEADS) for c in range(tq // sub)]
        ones_rows = (lax.broadcasted_iota(I32, (16, tk), 0) == 0).astype(BF16)
        state = {}
        for hh, c in chains:
            cs = slice(c * sub, (c + 1) * sub)
            state[hh, c] = (m_sc[hh, :, cs], l_sc[hh, :, cs], acc_sc[hh, :, cs])
        new_state = {}
        within = None
        def scores(hh, c, rows, q0=0):
            sl = slice(hh * HEAD_PAD, (hh + 1) * HEAD_PAD)
            st = _dot_nt(k_ref[rows, sl], q_ref[c * sub + q0:(c + 1) * sub, sl])
            if diagonal:
                shape = (rows.stop - rows.start, sub - q0)
                key = lax.broadcasted_iota(I32, shape, 0) + (j * tk + rows.start)
                qry = lax.broadcasted_iota(I32, shape, 1) + (i * tq + c * sub + q0)
                st = jnp.where(qry >= key, st, NEG)
            return st

        def values(hh, rows):
            ones_row = (lax.broadcasted_iota(I32, (16, rows.stop - rows.start), 0) == 0).astype(BF16)
            return jnp.concatenate([vt_ref[hh * MLA_V:(hh + 1) * MLA_V, rows], ones_row], axis=0)

        for hh, c in chains:
            m_prev, l_prev, acc_prev = state[hh, c]
            if speculative:
                kc = tk // ATTN_KEY_CHUNKS
                m_blk, pv = None, None
                for n in range(ATTN_KEY_CHUNKS):
                    rows = slice(n * kc, (n + 1) * kc)
                    q0 = n * kc if (diagonal and tq == tk and sub == tq) else 0
                    st = scores(hh, c, rows, q0)
                    if n == 0:
                        m_prev = jnp.where(j == 0, st[:1], m_prev)
                    m_part = jnp.max(st, axis=0, keepdims=True)
                    part = _dot(values(hh, rows), jnp.exp2(st - m_prev[:, q0:]).astype(BF16))
                    if q0:
                        m_part = jnp.concatenate([jnp.full((1, q0), NEG, F32), m_part], axis=1)
                        part = jnp.concatenate([jnp.zeros((part.shape[0], q0), F32), part], axis=1)
                    m_blk = m_part if m_blk is None else jnp.maximum(m_blk, m_part)
                    pv = part if pv is None else pv + part
                m_new = jnp.maximum(m_prev, m_blk)
                alpha = jnp.exp2(m_prev - m_new)
                l_new = alpha * (l_prev + pv[MLA_V:MLA_V + 1])
                acc_new = alpha * (acc_prev + pv[:MLA_V])
                ok = jnp.max(m_blk - m_prev) <= SPECULATION_HEADROOM
                within = ok if within is None else (within & ok)
            else:
                rows = slice(0, tk)
                st = scores(hh, c, rows)
                m_new = jnp.maximum(m_prev, jnp.max(st, axis=0, keepdims=True))
                alpha = jnp.exp2(m_prev - m_new)
                pv = _dot(values(hh, rows), jnp.exp2(st - m_new).astype(BF16))
                l_new = alpha * l_prev + pv[MLA_V:MLA_V + 1]
                acc_new = alpha * acc_prev + pv[:MLA_V]
            new_state[hh, c] = (m_new, l_new, acc_new)

        def commit():
            for hh, c in chains:
                cs = slice(c * sub, (c + 1) * sub)
                m_sc[hh, :, cs], l_sc[hh, :, cs], acc_sc[hh, :, cs] = new_state[hh, c]

        if speculative:
            pl.when(within)(commit)
            redo_sc[0] = jnp.logical_not(within).astype(I32)
        else:
            commit()

    redo_sc[0] = 0
    on_diagonal = j >= ratio * i

    @pl.when(jnp.logical_not(on_diagonal))
    def _():
        step(False, True)

    @pl.when(on_diagonal)
    def _():
        step(True, True)

    @pl.when(redo_sc[0] != 0)
    def _():
        step(True, False)

    @pl.when(j == ratio * i + (ratio - 1))
    def _():
        ot = jnp.concatenate([acc_sc[hh] / l_sc[hh] for hh in range(ATTN_HEADS)], axis=0)
        o_ref[...] = ot.T.astype(BF16)


def _mla_attention(q, k, vt):
    b, s, _ = q.shape
    tq, tk, nh = ATTN_TQ, ATTN_TK, ATTN_HEADS
    ratio = tq // tk
    nq = s // tq
    pairs = [(i, j) for i in range(nq) for j in range(ratio * (i + 1))]
    qi = jnp.asarray([p[0] for p in pairs], I32)
    kj = jnp.asarray([p[1] for p in pairs], I32)
    grid_spec = pltpu.PrefetchScalarGridSpec(
        num_scalar_prefetch=2,
        grid=(b, MLA_HEADS // nh, len(pairs)),
        in_specs=[
            pl.BlockSpec((None, tq, nh * HEAD_PAD), lambda bb, hp, p, qi, kj: (bb, qi[p], hp)),
            pl.BlockSpec((None, tk, nh * HEAD_PAD), lambda bb, hp, p, qi, kj: (bb, kj[p], hp)),
            pl.BlockSpec((None, nh * MLA_V, tk), lambda bb, hp, p, qi, kj: (bb, hp, kj[p])),
        ],
        out_specs=pl.BlockSpec((None, tq, nh * MLA_V), lambda bb, hp, p, qi, kj: (bb, qi[p], hp)),
        scratch_shapes=[pltpu.VMEM((nh, 1, tq), F32), pltpu.VMEM((nh, 1, tq), F32),
                        pltpu.VMEM((nh, MLA_V, tq), F32), pltpu.SMEM((1,), I32)],
    )
    return pl.pallas_call(
        _mla_kernel,
        out_shape=jax.ShapeDtypeStruct((b, s, MLA_HEADS * MLA_V), BF16),
        grid_spec=grid_spec,
        compiler_params=_cparams(("parallel", "parallel", "arbitrary")),
        name="mla_attention",
    )(qi, kj, q, k, vt)


def _alibi_slopes(n):
    def pow2(m):
        start = 2.0 ** (-8.0 / m)
        return [start ** (i + 1) for i in range(m)]
    if math.log2(n).is_integer():
        s = pow2(n)
    else:
        c = 2 ** int(math.floor(math.log2(n)))
        s = pow2(c) + pow2(2 * c)[0::2][: n - c]
    return np.array(sorted(s, reverse=True), dtype=np.float32)


def _dilated_block(cur, prev, bias4, first, *, head_of_lane, low):
    n = DIL_STEPS
    hw = DIL_OUT
    q = cur[:, :hw]
    kk = jnp.concatenate([prev[:, hw:2 * hw], cur[:, hw:2 * hw]], axis=0)
    vv = jnp.concatenate([prev[:, 2 * hw:], cur[:, 2 * hw:]], axis=0)
    zero = jnp.zeros_like(q)
    q4 = jnp.concatenate([jnp.where(head_of_lane == h, q, zero) for h in range(DIL_HEADS)], axis=0)
    s4 = _dot_nt(q4, kk) + bias4
    if first is not None:
        ki = lax.broadcasted_iota(I32, (DIL_HEADS * n, 2 * n), 1)
        s4 = jnp.where(first & (ki < n), NEG, s4)
    m4 = jnp.max(s4, axis=1, keepdims=True)
    p4 = jnp.exp(s4 - m4).astype(BF16)
    l4 = jnp.broadcast_to(jnp.sum(p4.astype(F32), axis=1, keepdims=True), (DIL_HEADS * n, HEAD_PAD))
    pv4 = _dot(p4, vv)
    m4 = jnp.broadcast_to(m4, (DIL_HEADS * n, HEAD_PAD))

    def rows(a, h):
        return a[h * n:(h + 1) * n]

    o_un = rows(pv4, DIL_HEADS - 1)
    for h in range(DIL_HEADS - 2, -1, -1):
        o_un = jnp.where(head_of_lane == h, rows(pv4, h), o_un)
    def per_lane(a):
        return jnp.concatenate([jnp.where(low, rows(a, 0), rows(a, 1)),
                                jnp.where(low, rows(a, 2), rows(a, 3))], axis=1)

    l_sel = per_lane(l4)
    return o_un / l_sel, per_lane(m4) + jnp.log(l_sel)


def _dilated_kernel(c0_ref, h0_ref, c1_ref, h1_ref, c2_ref, h2_ref, ob_ref, o_sc, l_sc, bias_sc,
                    *, slopes):
    u = pl.program_id(1)
    n = DIL_STEPS
    unit = ob_ref.shape[0]
    n_sb = unit // n
    first = u == 0
    qi = lax.broadcasted_iota(I32, (n, 2 * n), 0)
    ki = lax.broadcasted_iota(I32, (n, 2 * n), 1)
    dist = qi + n - ki
    valid = (dist >= 0) & (dist <= n)
    distf = dist.astype(F32)
    for gi in range(len(DIL_GROUPS)):
        for h in range(DIL_HEADS):
            bias_sc[gi, h * n:(h + 1) * n, :] = jnp.where(valid, -slopes[gi][h] * distf, NEG)

    block = functools.partial(
        _dilated_block,
        head_of_lane=lax.broadcasted_iota(I32, (n, DIL_OUT), 1) // DIL_HEAD_DIM,
        low=lax.broadcasted_iota(I32, (n, HEAD_PAD), 1) < DIL_HEAD_DIM)

    def rows_of(ref, sb):
        return ref[pl.ds(pl.multiple_of(sb * n, n), n), :]

    def emit(gi, start, stride, o, lse):
        if isinstance(start, int):
            idx = pl.ds(start, n)
        elif stride == 1:
            idx = pl.ds(pl.multiple_of(start, n), n)
        else:
            idx = pl.ds(start, n, stride=stride)
        for half in range(DIL_OUT // HEAD_PAD):
            ls = slice(half * HEAD_PAD, (half + 1) * HEAD_PAD)
            o_sc[gi, half, idx, :] = o[:, ls]
            l_sc[gi, half, idx, :] = lse[:, ls]

    emit(0, 0, 1, *block(c0_ref[:n, :], h0_ref[...], bias_sc[0], first))

    def g0_body(sb, carry):
        emit(0, sb * n, 1, *block(rows_of(c0_ref, sb), rows_of(c0_ref, sb - 1), bias_sc[0], None))
        return carry

    lax.fori_loop(1, n_sb, g0_body, 0, unroll=DIL_UNROLL)

    d1 = DIL_GROUPS[1][1]

    def g1_head(r, carry):
        emit(1, r, d1, *block(rows_of(c1_ref, r), rows_of(h1_ref, r), bias_sc[1], first))
        return carry

    def g1_body(sb, carry):
        start = (sb // d1) * (d1 * n) + sb % d1
        emit(1, start, d1, *block(rows_of(c1_ref, sb), rows_of(c1_ref, sb - d1), bias_sc[1], None))
        return carry

    lax.fori_loop(0, d1, g1_head, 0, unroll=DIL_UNROLL)
    lax.fori_loop(d1, n_sb, g1_body, 0, unroll=DIL_UNROLL)

    d2 = DIL_GROUPS[2][1]

    def g2_body(r, carry):
        emit(2, r, d2, *block(rows_of(c2_ref, r), rows_of(h2_ref, r), bias_sc[2], first))
        return carry

    lax.fori_loop(0, n_sb, g2_body, 0, unroll=DIL_UNROLL)

    def merge_body(c, carry):
        idx = pl.ds(pl.multiple_of(c * n, n), n)
        for half in range(DIL_OUT // HEAD_PAD):
            l0, l1, l2 = l_sc[0, half, idx, :], l_sc[1, half, idx, :], l_sc[2, half, idx, :]
            lmax = jnp.maximum(jnp.maximum(l0, l1), l2)
            e0, e1, e2 = jnp.exp(l0 - lmax), jnp.exp(l1 - lmax), jnp.exp(l2 - lmax)
            ob = (e0 * o_sc[0, half, idx, :] + e1 * o_sc[1, half, idx, :]
                  + e2 * o_sc[2, half, idx, :]) / (e0 + e1 + e2)
            ob_ref[idx, half * HEAD_PAD:(half + 1) * HEAD_PAD] = ob.astype(BF16)
        return carry

    lax.fori_loop(0, n_sb, merge_body, 0)


def _dilated_attention(zd0, zd1, zd2, seq):
    t = zd0.shape[0]
    unit = DIL_GROUPS[-1][0]
    upb = seq // unit
    n = DIL_STEPS
    u1 = DIL_GROUPS[1][0]
    all_slopes = _alibi_slopes(len(DIL_GROUPS) * DIL_HEADS).reshape(len(DIL_GROUPS), DIL_HEADS)
    slopes = tuple(tuple(float(x) * dil for x in all_slopes[gi]) for gi, (_, dil) in enumerate(DIL_GROUPS))

    def cur(bb, u):
        return (bb * upb + u, 0)

    def halo(rows):
        per_unit = unit // rows
        return lambda bb, u: ((bb * upb) * per_unit + jnp.maximum(u * per_unit - 1, 0), 0)

    return pl.pallas_call(
        functools.partial(_dilated_kernel, slopes=slopes),
        out_shape=jax.ShapeDtypeStruct((t, DIL_OUT), BF16),
        grid=(t // seq, upb),
        in_specs=[pl.BlockSpec((unit, DIL_COLS), cur), pl.BlockSpec((n, DIL_COLS), halo(n)),
                  pl.BlockSpec((unit, DIL_COLS), cur), pl.BlockSpec((u1, DIL_COLS), halo(u1)),
                  pl.BlockSpec((unit, DIL_COLS), cur), pl.BlockSpec((unit, DIL_COLS), halo(unit))],
        out_specs=pl.BlockSpec((unit, DIL_OUT), cur),
        scratch_shapes=[pltpu.VMEM((len(DIL_GROUPS), DIL_OUT // HEAD_PAD, unit, HEAD_PAD), F32),
                        pltpu.VMEM((len(DIL_GROUPS), DIL_OUT // HEAD_PAD, unit, HEAD_PAD), F32),
                        pltpu.VMEM((len(DIL_GROUPS), DIL_HEADS * n, 2 * n), F32)],
        compiler_params=_cparams(("parallel", "arbitrary")),
        name="dilated_attention",
    )(zd0, zd0, zd1, zd1, zd2, zd2)


def _merge_kernel(x_ref, oa_ref, ob_ref, ga_ref, wgate_ref,
                  wa_ref, wb_ref, wo_ref, g_ref, wr_ref, br_ref,
                  x1_ref, hp_ref, topi_ref, wcol_ref):
    tm = x_ref.shape[0]
    x = x_ref[...]
    h = _rms(x, ga_ref[...]).astype(BF16)
    mixed = (jax.nn.sigmoid(_dot(h, wgate_ref[:, :D_MODEL])) * _dot(oa_ref[...], wa_ref[...])
             + jax.nn.sigmoid(_dot(h, wgate_ref[:, D_MODEL:])) * _dot(ob_ref[...], wb_ref[...]))
    x1 = x + _dot(mixed.astype(BF16), wo_ref[...])
    x1_ref[...] = x1
    h2 = _rms(x1, g_ref[...])
    hp_ref[...] = _pack_halves(h2[:, :HALF], h2[:, HALF:])

    logits = _dot_nt(wr_ref[...], h2.astype(BF16)) + br_ref[...]
    eidx = lax.broadcasted_iota(I32, (N_EXPERTS, tm), 0)
    vals, idxs = [], []
    for _ in range(TOP_K):
        m = jnp.max(logits, axis=0, keepdims=True)
        idx = jnp.min(jnp.where(logits == m, eidx, N_EXPERTS), axis=0, keepdims=True)
        vals.append(m)
        idxs.append(idx)
        logits = jnp.where(eidx == idx, -jnp.inf, logits)
    exps = [jnp.exp(vk - vals[0]) for vk in vals]
    den = exps[0] + exps[1] + exps[2] + exps[3]
    row8 = lax.broadcasted_iota(I32, (8, tm), 0)
    row128 = lax.broadcasted_iota(I32, (HEAD_PAD, tm), 0)
    topi = jnp.zeros((8, tm), I32)
    wide = jnp.zeros((HEAD_PAD, tm), F32)
    for kk in range(TOP_K):
        topi = jnp.where(row8 == kk, idxs[kk], topi)
        wide = jnp.where(row128 == kk, exps[kk] / den, wide)
    topi_ref[...] = topi
    wcol_ref[...] = wide.T


def _merge(x2d, oa, ob, g_attn, wgate, wa, wb, wo, g, wr_t, br_col):
    t = x2d.shape[0]
    tm = TOKEN_TILE

    def row(i):
        return (i, 0)

    def col(i):
        return (0, i)

    def full(a):
        return pl.BlockSpec(a.shape, lambda i: (0, 0))

    def rows(width):
        return pl.BlockSpec((tm, width), row)

    out_shape = [
        jax.ShapeDtypeStruct((t, D_MODEL), F32),
        jax.ShapeDtypeStruct((t, HALF), I32),
        jax.ShapeDtypeStruct((8, t), I32),
        jax.ShapeDtypeStruct((t, HEAD_PAD), F32),
    ]
    return pl.pallas_call(
        _merge_kernel,
        out_shape=out_shape,
        grid=(t // tm,),
        in_specs=[rows(D_MODEL), rows(MLA_HEADS * MLA_V), rows(DIL_OUT), full(g_attn), full(wgate)]
        + [full(wa), full(wb), full(wo), full(g), full(wr_t), full(br_col)],
        out_specs=[rows(D_MODEL), rows(HALF), pl.BlockSpec((8, tm), col), rows(HEAD_PAD)],
        compiler_params=_cparams(("parallel",)),
        name="merge_router",
    )(x2d, oa, ob, g_attn, wgate, wa, wb, wo, g, wr_t, br_col)


def _positions_kernel(topi_ref, dest_ref, meta_ref, cnt_sc, carry_sc, start_sc):
    ps = pl.program_id(0)
    i = pl.program_id(1)
    tm = POSITION_SUBTILE
    n_sub = topi_ref.shape[1] // tm
    eidx = lax.broadcasted_iota(I32, (N_EXPERTS, tm), 0)

    def hits_of(sb):
        topi = topi_ref[:, sb * tm:(sb + 1) * tm]
        return [eidx == topi[kk:kk + 1, :] for kk in range(TOP_K)]

    def members(hits):
        return hits[0] | hits[1] | hits[2] | hits[3]

    @pl.when((ps == 0) & (i == 0))
    def _():
        cnt_sc[...] = jnp.zeros(cnt_sc.shape, F32)

    @pl.when(ps == 0)
    def _():
        total = cnt_sc[...]
        for sb in range(n_sub):
            total = total + jnp.sum(members(hits_of(sb)).astype(F32), axis=1, keepdims=True)
        cnt_sc[...] = total

    @pl.when((ps == 1) & (i == 0))
    def _():
        cnt = cnt_sc[...].astype(I32)
        shift = ROW_BLOCK.bit_length() - 1
        padded = lax.shift_left(lax.shift_right_logical(cnt + (ROW_BLOCK - 1), shift), shift)
        sub = lax.broadcasted_iota(I32, (N_EXPERTS, HEAD_PAD), 0)
        lane = lax.broadcasted_iota(I32, (N_EXPERTS, HEAD_PAD), 1)
        padded_row = jnp.sum(jnp.where(sub == lane, padded, 0), axis=0, keepdims=True)
        start = jnp.sum(jnp.where(lane < sub, padded_row, 0), axis=1, keepdims=True)
        start_sc[...] = start.astype(F32)
        carry_sc[...] = jnp.zeros(carry_sc.shape, F32)
        cnt_row = jnp.sum(jnp.where(sub == lane, cnt, 0), axis=0, keepdims=True)
        start_row = jnp.sum(jnp.where(sub == lane, start, 0), axis=0, keepdims=True)
        row8 = lax.broadcasted_iota(I32, (8, HEAD_PAD), 0)
        meta = jnp.where(row8 == 0, cnt_row, 0)
        meta = jnp.where(row8 == 1, start_row, meta)
        meta = jnp.where(row8 == 2, start_row + padded_row, meta)
        meta_ref[...] = meta

    @pl.when(ps == 1)
    def _():
        tr = lax.broadcasted_iota(I32, (tm, tm), 0)
        tc = lax.broadcasted_iota(I32, (tm, tm), 1)
        before = (tr < tc).astype(BF16)
        row8 = lax.broadcasted_iota(I32, (8, tm), 0)
        offset = carry_sc[...] + start_sc[...]
        for sb in range(n_sub):
            hits = hits_of(sb)
            member = members(hits)
            base = _dot(member.astype(BF16), before) + offset
            dest = jnp.zeros((8, tm), I32)
            for kk in range(TOP_K):
                dk = jnp.sum(jnp.where(hits[kk], base, 0.0), axis=0, keepdims=True).astype(I32)
                dest = jnp.where(row8 == kk, dk, dest)
            dest_ref[:, sb * tm:(sb + 1) * tm] = dest
            offset = offset + jnp.sum(member.astype(F32), axis=1, keepdims=True)
        carry_sc[...] = offset - start_sc[...]


def _positions(topi_t):
    t = topi_t.shape[1]
    tm = min(POSITION_TILE, t)
    return pl.pallas_call(
        _positions_kernel,
        out_shape=[jax.ShapeDtypeStruct((8, t), I32), jax.ShapeDtypeStruct((8, HEAD_PAD), I32)],
        grid=(2, t // tm),
        in_specs=[pl.BlockSpec((8, tm), lambda ps, i: (0, i))],
        out_specs=[pl.BlockSpec((8, tm), lambda ps, i: (0, i * ps)),
                   pl.BlockSpec((8, HEAD_PAD), lambda ps, i: (0, 0))],
        scratch_shapes=[pltpu.VMEM((N_EXPERTS, 1), F32)] * 3,
        compiler_params=_cparams(("arbitrary", "arbitrary")),
        name="routing_positions",
    )(topi_t)


def _sc_mesh():
    return plsc.VectorSubcoreMesh(core_axis_name="c", subcore_axis_name="s")


def _dispatch_rows(table, dest_flat, n_rows):
    t, c = table.shape
    n_slots = dest_flat.shape[0] // t
    per_w = t // SC_WORKERS
    assert per_w * SC_WORKERS == t and per_w % (2 * SC_WINDOW) == 0
    n_chunks = per_w // SC_WINDOW
    w = SC_WINDOW

    @functools.partial(
        pl.kernel, mesh=_sc_mesh(),
        out_type=jax.ShapeDtypeStruct((n_rows, c), table.dtype),
        scratch_types=[pltpu.VMEM((n_chunks, w), I32)] * n_slots + [pltpu.VMEM((w, c), table.dtype)] * 2
        + [pltpu.SemaphoreType.DMA] * (n_slots + 2),
        name="dispatch_rows",
    )
    def k(table_hbm, dest_hbm, out_hbm, *scratch):
        idx = scratch[:n_slots]
        rows = scratch[n_slots:n_slots + 2]
        scatter_sems = scratch[n_slots + 2:2 * n_slots + 2]
        read_sems = scratch[2 * n_slots + 2:]
        wid = lax.axis_index("s") * SC_CORES + lax.axis_index("c")
        base = wid * per_w

        def off(chunk):
            return pl.multiple_of(base + chunk * w, w)

        def read(chunk, buf):
            return pltpu.make_async_copy(table_hbm.at[pl.ds(off(chunk), w)], rows[buf], read_sems[buf])

        def scatter(kk, chunk, buf):
            return pltpu.make_async_copy(rows[buf], out_hbm.at[idx[kk].at[chunk]], scatter_sems[kk])

        read(0, 0).start()
        for kk in range(n_slots):
            pltpu.sync_copy(dest_hbm.at[pl.ds(kk * (t // w) + wid * n_chunks, n_chunks)], idx[kk])

        @pl.loop(0, n_chunks // 2)
        def _(p):
            for buf in range(2):
                chunk = 2 * p + buf

                @pl.when(chunk + 1 < n_chunks)
                def _():
                    read(chunk + 1, 1 - buf).start()

                read(chunk, buf).wait()
                for kk in range(n_slots):
                    scatter(kk, chunk, buf).start()
                for kk in range(n_slots):
                    scatter(kk, chunk, buf).wait()

    return k(table, dest_flat.reshape(n_slots * t // w, w))


def _gather_rows(table, idx):
    n = idx.shape[0]
    c = table.shape[1]
    per_w = n // SC_WORKERS
    assert per_w * SC_WORKERS == n and per_w % (2 * SC_WINDOW) == 0
    n_chunks = per_w // SC_WINDOW
    w = SC_WINDOW

    @functools.partial(
        pl.kernel, mesh=_sc_mesh(),
        out_type=jax.ShapeDtypeStruct((n, c), table.dtype),
        scratch_types=[pltpu.VMEM((n_chunks, w), I32)] + [pltpu.VMEM((w, c), table.dtype)] * 2
        + [pltpu.SemaphoreType.DMA] * 4,
        name="gather_rows",
    )
    def k(table_hbm, idx_hbm, out_hbm, idx_all, rows_a, rows_b, g_a, g_b, w_a, w_b):
        rows, gather_sems, write_sems = (rows_a, rows_b), (g_a, g_b), (w_a, w_b)
        wid = lax.axis_index("s") * SC_CORES + lax.axis_index("c")
        base = wid * per_w

        def off(chunk):
            return pl.multiple_of(base + chunk * w, w)

        def gather(chunk, buf):
            return pltpu.make_async_copy(table_hbm.at[idx_all.at[chunk]], rows[buf], gather_sems[buf])

        def write(chunk, buf):
            return pltpu.make_async_copy(rows[buf], out_hbm.at[pl.ds(off(chunk), w)], write_sems[buf])

        def start_gather(chunk, buf):
            gather(chunk, buf).start()

        pltpu.sync_copy(idx_hbm.at[pl.ds(wid * n_chunks, n_chunks)], idx_all)
        start_gather(0, 0)

        @pl.loop(0, n_chunks // 2)
        def _(p):
            for buf in range(2):
                chunk = 2 * p + buf

                @pl.when(chunk + 1 < n_chunks)
                def _():
                    @pl.when(chunk >= 1)
                    def _():
                        write(chunk - 1, 1 - buf).wait()
                    start_gather(chunk + 1, 1 - buf)

                gather(chunk, buf).wait()
                write(chunk, buf).start()

        write(n_chunks - 2, 0).wait()
        write(n_chunks - 1, 1).wait()

    return k(table, idx.reshape(n // w, w))


def _expert_kernel(be_ref, nused_ref, first_ref, slot_ref, next_ref,
                   xs_ref, wg_hbm, bg_ref, wu_hbm, bu_ref, wd_hbm, bd_ref,
                   ys_ref, wg_ref, wu_ref, wd_ref, stage_g, stage_u, stage_d, sems, *, layer):
    b = pl.program_id(0)
    used = b < nused_ref[0]
    weights = ((wg_hbm, stage_g), (wu_hbm, stage_u), (wd_hbm, stage_d))

    def fetch(expert, slot):
        return [pltpu.make_async_copy(hbm.at[layer, expert], stage.at[slot], sems.at[slot, n])
                for n, (hbm, stage) in enumerate(weights)]

    @pl.when(used & (first_ref[b] == 1))
    def _():
        slot = slot_ref[b]

        @pl.when(b == 0)
        def _():
            for cp in fetch(be_ref[0], slot):
                cp.start()

        for cp in fetch(be_ref[b], slot):
            cp.wait()
        wg_ref[...] = stage_g[slot].astype(BF16)
        wu_ref[...] = stage_u[slot].astype(BF16)
        wd_ref[...] = stage_d[slot].astype(BF16)

        @pl.when(next_ref[b] >= 0)
        def _():
            for cp in fetch(next_ref[b], 1 - slot):
                cp.start()

    @pl.when(used)
    def _():
        lo, hi = _unpack_halves(xs_ref[...])
        xb = jnp.concatenate([lo.astype(BF16), hi.astype(BF16)], axis=1)
        a = _dot(xb, wg_ref[...]) + bg_ref[...]
        u = _dot(xb, wu_ref[...]) + bu_ref[...]
        a = jnp.minimum(a, SWIGLU_LIMIT)
        u = jnp.clip(u, -SWIGLU_LIMIT, SWIGLU_LIMIT)
        y = (a * jax.nn.sigmoid(SWIGLU_ALPHA * a)) * (u + 1.0)
        out = _dot(y.astype(BF16), wd_ref[...]) + bd_ref[...]
        ys_ref[...] = _pack_halves(out[:, :HALF], out[:, HALF:])

    @pl.when(b >= nused_ref[0])
    def _():
        ys_ref[...] = jnp.zeros(ys_ref.shape, I32)


def _expert_ffn(xs, block_e, n_used, counts, layer, wg, bg, wu, bu, wd, bd):
    n_rows = xs.shape[0]
    n_blocks = n_rows // ROW_BLOCK

    blk = jnp.arange(n_blocks, dtype=I32)
    used = blk < n_used[0]
    first = used & ((blk == 0) | (block_e != jnp.roll(block_e, 1)))
    slot = (jnp.cumsum(first.astype(I32)) - 1) % 2
    eid = jnp.arange(N_EXPERTS, dtype=I32)
    later = (eid[None, :] > eid[:, None]) & (counts[None, :] > 0)
    next_of_expert = jnp.min(jnp.where(later, eid[None, :], N_EXPERTS), axis=1)
    next_of_expert = jnp.where(next_of_expert == N_EXPERTS, -1, next_of_expert)
    next_e = jnp.sum(jnp.where(block_e[:, None] == eid[None, :], next_of_expert[None, :], 0), axis=1)

    def rows(b, *_):
        return (b, 0)

    def expert(b, be, *_):
        return (layer, be[b], 0, 0)

    hbm = pl.BlockSpec(memory_space=pl.ANY)
    grid_spec = pltpu.PrefetchScalarGridSpec(
        num_scalar_prefetch=5,
        grid=(n_blocks,),
        in_specs=[pl.BlockSpec((ROW_BLOCK, HALF), rows),
                  hbm, pl.BlockSpec((None, None, 1, D_EXPERT), expert),
                  hbm, pl.BlockSpec((None, None, 1, D_EXPERT), expert),
                  hbm, pl.BlockSpec((None, None, 1, D_MODEL), expert)],
        out_specs=pl.BlockSpec((ROW_BLOCK, HALF), rows),
        scratch_shapes=[pltpu.VMEM((D_MODEL, D_EXPERT), BF16), pltpu.VMEM((D_MODEL, D_EXPERT), BF16),
                        pltpu.VMEM((D_EXPERT, D_MODEL), BF16),
                        pltpu.VMEM((2, D_MODEL, D_EXPERT), F32), pltpu.VMEM((2, D_MODEL, D_EXPERT), F32),
                        pltpu.VMEM((2, D_EXPERT, D_MODEL), F32),
                        pltpu.SemaphoreType.DMA((2, 3))],
    )
    return pl.pallas_call(
        functools.partial(_expert_kernel, layer=layer),
        out_shape=jax.ShapeDtypeStruct((n_rows, HALF), I32),
        grid_spec=grid_spec,
        compiler_params=_cparams(("arbitrary",)),
        name="expert_ffn",
    )(block_e, n_used, first.astype(I32), slot.astype(I32), next_e.astype(I32),
      xs, wg, bg, wu, bu, wd, bd)


def _combine_kernel(x1_ref, yg_ref, wcol_ref, p_ref, gple_ref, wpg_ref, wpp_ref, gout_ref, o_ref,
                    *, final):
    x1 = x1_ref[...]
    acc_lo = x1[:, :HALF]
    acc_hi = x1[:, HALF:]
    wcol = wcol_ref[...]
    for kk in range(TOP_K):
        lo, hi = _unpack_halves(yg_ref[kk])
        wk = wcol[:, kk:kk + 1]
        acc_lo = acc_lo + wk * lo
        acc_hi = acc_hi + wk * hi
    x2 = jnp.concatenate([acc_lo, acc_hi], axis=1)
    gate = jax.nn.sigmoid(_dot(_rms(x2, gple_ref[...]).astype(BF16), wpg_ref[...]))
    x3 = x2 + gate * _dot(p_ref[...].astype(BF16), wpp_ref[...])
    o_ref[...] = _rms(x3, gout_ref[...]) if final else x3


def _combine_kernel_inplace(x1_ref, yg_ref, wcol_ref, p_ref, gple_ref, wpg_ref, wpp_ref, gout_ref,
                            prev_ref, o_ref, *, final):
    del prev_ref
    _combine_kernel(x1_ref, yg_ref, wcol_ref, p_ref, gple_ref, wpg_ref, wpp_ref, gout_ref, o_ref,
                    final=final)


def _combine(x1, yg, wcol, p3d, layer, gple, wpg, wpp, gout, final, part, prev):
    t = x1.shape[0]
    tm = TOKEN_TILE
    n_tiles = yg.shape[1] // tm
    first_tile = part * n_tiles

    def row(i, *_):
        return (first_tile + i, 0)

    def full(a):
        return pl.BlockSpec(a.shape, lambda i: (0, 0))

    in_specs = [pl.BlockSpec((tm, D_MODEL), row),
                pl.BlockSpec((TOP_K, tm, HALF), lambda i: (0, i, 0)),
                pl.BlockSpec((tm, HEAD_PAD), row),
                pl.BlockSpec((None, tm, PLE_DIM), lambda i: (layer, first_tile + i, 0)),
                full(gple), full(wpg), full(wpp), full(gout)]
    args = [x1, yg, wcol, p3d, gple, wpg, wpp, gout]
    kern = functools.partial(_combine_kernel, final=final)
    aliases = {}
    if prev is not None:
        in_specs.append(pl.BlockSpec(memory_space=pl.ANY))
        args.append(prev)
        aliases = {len(args) - 1: 0}
        kern = functools.partial(_combine_kernel_inplace, final=final)
    return pl.pallas_call(
        kern,
        out_shape=jax.ShapeDtypeStruct((t, D_MODEL), F32),
        grid=(n_tiles,),
        in_specs=in_specs,
        out_specs=pl.BlockSpec((tm, D_MODEL), row),
        input_output_aliases=aliases,
        compiler_params=_cparams(("parallel",)),
        name="combine_ple",
    )(*args)


def _rope_tables(seq):
    inv_freq = ROPE_THETA ** (-jnp.arange(HALF_ROPE, dtype=F32) * 2.0 / MLA_ROPE)
    ang = jnp.arange(seq, dtype=F32)[:, None] * inv_freq[None, :]
    cos, sin = jnp.cos(ang), jnp.sin(ang)
    ones = jnp.ones((seq, MLA_NOPE), F32)
    zeros16 = jnp.zeros((seq, HALF_ROPE), F32)
    zeros64 = jnp.zeros((seq, MLA_NOPE), F32)
    tail = jnp.ones((seq, HEAD_PAD - MLA_NOPE - MLA_ROPE), F32)
    ztail = jnp.zeros_like(tail)
    cos_t = jnp.concatenate([ones, cos, cos, tail], axis=1)
    sina_t = jnp.concatenate([zeros64, zeros16, sin, ztail], axis=1)
    sinb_t = jnp.concatenate([zeros64, -sin, zeros16, ztail], axis=1)
    return cos_t, sina_t, sinb_t


def _prep_mixer_weights(w_in, w_uq, w_ukv):
    c0 = MLA_Q_LORA + MLA_KV_LORA
    c1 = c0 + MLA_ROPE
    c2 = c1 + len(DIL_GROUPS) * DIL_COLS
    kr_pad = jnp.pad(w_in[:, c0:c1], ((0, 0), (MLA_NOPE, HEAD_PAD - MLA_NOPE - MLA_ROPE)))
    wmla = jnp.concatenate([w_in[:, :c0], kr_pad], axis=1).astype(BF16)
    col = np.arange(len(DIL_GROUPS) * DIL_COLS)
    q_scale = np.where(col % DIL_COLS < DIL_OUT, DIL_HEAD_DIM ** -0.5, 1.0).astype(np.float32)
    wdil = (w_in[:, c1:c2] * q_scale[None, :]).astype(BF16)
    wgate = w_in[:, c2:].astype(BF16)
    pad = HEAD_PAD - MLA_NOPE - MLA_ROPE
    wuq_h = w_uq.reshape(MLA_Q_LORA, MLA_HEADS, MLA_NOPE + MLA_ROPE)
    wuq = jnp.pad(wuq_h, ((0, 0), (0, 0), (0, pad))).reshape(MLA_Q_LORA, MLA_HEADS * HEAD_PAD).astype(BF16)
    wukv_h = w_ukv.reshape(MLA_KV_LORA, MLA_HEADS, MLA_NOPE + MLA_V)
    wuk = jnp.pad(wukv_h[:, :, :MLA_NOPE], ((0, 0), (0, 0), (0, HEAD_PAD - MLA_NOPE)))
    wuk = wuk.reshape(MLA_KV_LORA, MLA_HEADS * HEAD_PAD).astype(BF16)
    wuv = wukv_h[:, :, MLA_NOPE:].reshape(MLA_KV_LORA, MLA_HEADS * MLA_V).astype(BF16)
    return wmla, wdil, wgate, wuq, wuk, wuv


def kernel(x, p, attn_norm, w_in, q_norm, w_uq, kv_norm, w_ukv, w_branch_a, w_branch_b, w_out, ffn_norm, w_router, b_router, w_gate, b_gate, w_up, b_up, w_down, b_down, ple_norm, w_ple_gate, w_ple_proj, final_norm):
    b, s, d = x.shape
    depth = w_in.shape[0]
    t = b * s
    assert d == D_MODEL and s % (DIL_GROUPS[-1][0]) == 0 and t % (SC_WORKERS * SC_WINDOW) == 0
    n_assign = t * TOP_K
    n_blocks = -(-(n_assign + N_EXPERTS * (ROW_BLOCK - 1)) // ROW_BLOCK)
    n_rows = n_blocks * ROW_BLOCK
    cos_t, sina_t, sinb_t = _rope_tables(s)
    xc = x.reshape(t, d)
    for i in range(depth):
        wmla, wdil, wgate, wuq, wuk, wuv = _prep_mixer_weights(w_in[i], w_uq[i], w_ukv[i])
        q, k, vt, zd0, zd1, zd2 = _inproj(
            xc, s, attn_norm[i][None], wmla, wdil, q_norm[i][None], kv_norm[i][None],
            wuq, wuk, wuv, cos_t, sina_t, sinb_t)
        oa = _mla_attention(q.reshape(b, s, -1), k.reshape(b, s, -1), vt)
        ob = _dilated_attention(zd0, zd1, zd2, s)
        x1, hp, topi_t, wcol = _merge(
            xc, oa.reshape(t, -1), ob, attn_norm[i][None], wgate,
            w_branch_a[i].astype(BF16), w_branch_b[i].astype(BF16), w_out[i].astype(BF16),
            ffn_norm[i][None], w_router[i].T.astype(BF16), b_router[i][:, None])
        dest_t, meta = _positions(topi_t)
        ends = meta[2, :N_EXPERTS]
        block_start = jnp.arange(n_blocks, dtype=I32) * ROW_BLOCK
        block_e = jnp.minimum(
            jnp.sum((ends[None, :] <= block_start[:, None]).astype(I32), axis=1), N_EXPERTS - 1)
        n_used = (ends[N_EXPERTS - 1:] // ROW_BLOCK).astype(I32)
        dest_flat = dest_t[:TOP_K].reshape(n_assign)
        xs = _dispatch_rows(hp, dest_flat, n_rows)
        ys = _expert_ffn(xs, block_e, n_used, meta[0, :N_EXPERTS], i,
                         w_gate, b_gate[:, :, None, :], w_up, b_up[:, :, None, :],
                         w_down, b_down[:, :, None, :])
        final = i == depth - 1
        gout = final_norm[None] if final else attn_norm[i][None]
        wpg, wpp = w_ple_gate[i].astype(BF16), w_ple_proj[i].astype(BF16)
        tp = t // COMBINE_PARTS
        xc = None
        for part in range(COMBINE_PARTS):
            dest_part = dest_t[:TOP_K, part * tp:(part + 1) * tp].reshape(TOP_K * tp)
            yg = _gather_rows(ys, dest_part).reshape(TOP_K, tp, HALF)
            xc = _combine(x1, yg, wcol, p.reshape(depth, t, PLE_DIM), i, ple_norm[i][None],
                          wpg, wpp, gout, final, part, xc)
    return xc.reshape(b, s, d)
```

```python
import functools
import math

import jax
import jax.numpy as jnp
import numpy as np
from jax import lax
from jax.experimental import pallas as pl
from jax.experimental.pallas import tpu as pltpu
from jax.experimental.pallas import tpu_sc as plsc

F32 = jnp.float32
BF16 = jnp.bfloat16
I32 = jnp.int32

D_MODEL = 1024
PLE_DIM = 256
NORM_EPS = 1e-6

MLA_HEADS = 8
MLA_Q_LORA = 384
MLA_KV_LORA = 256
MLA_NOPE = 64
MLA_ROPE = 32
MLA_V = 64
ROPE_THETA = 10000.0
HEAD_PAD = 128
HALF_ROPE = MLA_ROPE // 2

DIL_GROUPS = ((128, 1), (512, 4), (2048, 16))
DIL_HEADS = 4
DIL_HEAD_DIM = 64
DIL_STEPS = 128
DIL_COLS = 3 * DIL_HEADS * DIL_HEAD_DIM
DIL_OUT = DIL_HEADS * DIL_HEAD_DIM
DIL_UNROLL = 8

N_EXPERTS = 32
TOP_K = 4
D_EXPERT = 1024
SWIGLU_LIMIT = 7.0
SWIGLU_ALPHA = 1.702
ROW_BLOCK = 512

TOKEN_TILE = 512
POSITION_TILE = 4096
POSITION_SUBTILE = 512
COMBINE_PARTS = 2
ATTN_TQ = 1024
ATTN_TK = 1024
ATTN_HEADS = 4
ATTN_SUB = 1024
ATTN_KEY_CHUNKS = 2
HALF = D_MODEL // 2
NEG = -1e30
SPECULATION_HEADROOM = 60.0
HI_MASK = -65536

SC_CORES = 2
SC_SUBCORES = 16
SC_WORKERS = SC_CORES * SC_SUBCORES
SC_WINDOW = 64

VMEM_LIMIT = 56 * 1024 * 1024


def _cparams(sem):
    return pltpu.CompilerParams(dimension_semantics=sem, vmem_limit_bytes=VMEM_LIMIT)


def _rms(x, g):
    return x * lax.rsqrt(jnp.mean(x * x, axis=-1, keepdims=True) + NORM_EPS) * g


def _dot(a, b):
    return jnp.dot(a, b, preferred_element_type=F32)


def _dot_nt(a, b):
    return lax.dot_general(a, b, (((1,), (1,)), ((), ())), preferred_element_type=F32)


def _pack_halves(lo, hi):
    lo_i = lax.bitcast_convert_type(lo.astype(BF16).astype(F32), I32)
    hi_i = lax.bitcast_convert_type(hi.astype(BF16).astype(F32), I32)
    return (hi_i & HI_MASK) | lax.shift_right_logical(lo_i, 16)


def _unpack_halves(w):
    lo = lax.bitcast_convert_type(lax.shift_left(w, 16), F32)
    hi = lax.bitcast_convert_type(w & HI_MASK, F32)
    return lo, hi


def _inproj_kernel(x_ref, g_ref, wmla_ref, wdil_ref, qn_ref, kvn_ref, wuq_ref, wuk_ref,
                   wuv_ref, cos_ref, sina_ref, sinb_ref,
                   q_ref, k_ref, vt_ref, zd0_ref, zd1_ref, zd2_ref, zs_sc):
    h = _rms(x_ref[...], g_ref[...]).astype(BF16)
    zm = _dot(h, wmla_ref[...])
    cq = _rms(zm[:, :MLA_Q_LORA], qn_ref[...]).astype(BF16)
    ckv = _rms(zm[:, MLA_Q_LORA:MLA_Q_LORA + MLA_KV_LORA], kvn_ref[...]).astype(BF16)
    kr = zm[:, MLA_Q_LORA + MLA_KV_LORA:]
    cos, sina, sinb = cos_ref[...], sina_ref[...], sinb_ref[...]

    def rope(t):
        return (t * cos + pltpu.roll(t, HALF_ROPE, 1) * sina
                + pltpu.roll(t, HEAD_PAD - HALF_ROPE, 1) * sinb)

    kr_rot = rope(kr)
    qraw = _dot(cq, wuq_ref[...])
    kraw = _dot(ckv, wuk_ref[...])
    vt_ref[...] = _dot(ckv, wuv_ref[...]).T.astype(BF16)
    scale = (MLA_NOPE + MLA_ROPE) ** -0.5 * math.log2(math.e)
    for hd in range(MLA_HEADS):
        sl = slice(hd * HEAD_PAD, (hd + 1) * HEAD_PAD)
        q_ref[:, sl] = (rope(qraw[:, sl]) * scale).astype(BF16)
        k_ref[:, sl] = (kraw[:, sl] + kr_rot).astype(BF16)
    tm = x_ref.shape[0]
    for gi, zd_ref in enumerate((zd0_ref, zd1_ref, zd2_ref)):
        z = _dot(h, wdil_ref[:, gi * DIL_COLS:(gi + 1) * DIL_COLS])
        window, dil = DIL_GROUPS[gi]
        if dil == 1:
            zd_ref[...] = z.astype(BF16)
            continue
        n_col = DIL_COLS // HEAD_PAD
        for c in range(n_col):
            zs_sc[c] = z[:, c * HEAD_PAD:(c + 1) * HEAD_PAD]
        rows = tm // dil
        part = pl.program_id(0) % (window // tm)
        for r in range(dil):
            dst = pl.ds(pl.multiple_of(r * DIL_STEPS + part * rows, rows), rows)
            for c in range(n_col):
                chunk = zs_sc[c, pl.ds(r, rows, stride=dil), :]
                zd_ref[dst, c * HEAD_PAD:(c + 1) * HEAD_PAD] = chunk.astype(BF16)


def _inproj(x2d, seq, g, wmla, wdil, qn, kvn, wuq, wuk, wuv, cos_t, sina_t, sinb_t):
    t = x2d.shape[0]
    tm = TOKEN_TILE
    n_seq_tiles = seq // tm

    def row(i):
        return (i, 0)

    def const(i):
        return (0, 0)

    def pos(i):
        return (i % n_seq_tiles, 0)

    def full(a):
        return pl.BlockSpec(a.shape, const)

    def vt_block(i):
        return (i // n_seq_tiles, 0, i % n_seq_tiles)

    def rows(width):
        return pl.BlockSpec((tm, width), row)

    def unit_rows(window):
        return pl.BlockSpec((window, DIL_COLS), lambda i: (i // (window // tm), 0))

    out_shape = [
        jax.ShapeDtypeStruct((t, MLA_HEADS * HEAD_PAD), BF16),
        jax.ShapeDtypeStruct((t, MLA_HEADS * HEAD_PAD), BF16),
        jax.ShapeDtypeStruct((t // seq, MLA_HEADS * MLA_V, seq), BF16),
        jax.ShapeDtypeStruct((t, DIL_COLS), BF16),
        jax.ShapeDtypeStruct((t, DIL_COLS), BF16),
        jax.ShapeDtypeStruct((t, DIL_COLS), BF16),
    ]
    return pl.pallas_call(
        _inproj_kernel,
        out_shape=out_shape,
        grid=(t // tm,),
        in_specs=[pl.BlockSpec((tm, D_MODEL), row), full(g), full(wmla), full(wdil),
                  full(qn), full(kvn), full(wuq), full(wuk), full(wuv),
                  pl.BlockSpec((tm, HEAD_PAD), pos), pl.BlockSpec((tm, HEAD_PAD), pos),
                  pl.BlockSpec((tm, HEAD_PAD), pos)],
        out_specs=[rows(D_MODEL), rows(D_MODEL),
                   pl.BlockSpec((None, MLA_HEADS * MLA_V, tm), vt_block),
                   rows(DIL_COLS)] + [unit_rows(window) for window, _ in DIL_GROUPS[1:]],
        scratch_shapes=[pltpu.VMEM((DIL_COLS // HEAD_PAD, tm, HEAD_PAD), F32)],
        compiler_params=_cparams(("arbitrary",)),
        name="inproj",
    )(x2d, g, wmla, wdil, qn, kvn, wuq, wuk, wuv, cos_t, sina_t, sinb_t)


def _mla_kernel(qi_ref, kj_ref, q_ref, k_ref, vt_ref, o_ref, m_sc, l_sc, acc_sc, redo_sc):
    p = pl.program_id(2)
    i = qi_ref[p]
    j = kj_ref[p]
    tq = q_ref.shape[0]
    tk = k_ref.shape[0]

    @pl.when(j == 0)
    def _():
        m_sc[...] = jnp.full(m_sc.shape, NEG, F32)
        l_sc[...] = jnp.zeros(l_sc.shape, F32)
        acc_sc[...] = jnp.zeros(acc_sc.shape, F32)

    ratio = tq // tk
    sub = ATTN_SUB

    def step(diagonal, speculative):
        chains = [(hh, c) for hh in range(ATTN_HEADS) for c in range(tq // sub)]
        ones_rows = (lax.broadcasted_iota(I32, (16, tk), 0) == 0).astype(BF16)
        state = {}
        for hh, c in chains:
            cs = slice(c * sub, (c + 1) * sub)
            state[hh, c] = (m_sc[hh, :, cs], l_sc[hh, :, cs], acc_sc[hh, :, cs])
        new_state = {}
        within = None
        def scores(hh, c, rows, q0=0):
            sl = slice(hh * HEAD_PAD, (hh + 1) * HEAD_PAD)
            st = _dot_nt(k_ref[rows, sl], q_ref[c * sub + q0:(c + 1) * sub, sl])
            if diagonal:
                shape = (rows.stop - rows.start, sub - q0)
                key = lax.broadcasted_iota(I32, shape, 0) + (j * tk + rows.start)
                qry = lax.broadcasted_iota(I32, shape, 1) + (i * tq + c * sub + q0)
                st = jnp.where(qry >= key, st, NEG)
            return st

        def values(hh, rows):
            ones_row = (lax.broadcasted_iota(I32, (16, rows.stop - rows.start), 0) == 0).astype(BF16)
            return jnp.concatenate([vt_ref[hh * MLA_V:(hh + 1) * MLA_V, rows], ones_row], axis=0)

        for hh, c in chains:
            m_prev, l_prev, acc_prev = state[hh, c]
            if speculative:
                kc = tk // ATTN_KEY_CHUNKS
                m_blk, pv = None, None
                for n in range(ATTN_KEY_CHUNKS):
                    rows = slice(n * kc, (n + 1) * kc)
                    q0 = n * kc if (diagonal and tq == tk and sub == tq) else 0
                    st = scores(hh, c, rows, q0)
                    if n == 0:
                        m_prev = jnp.where(j == 0, st[:1], m_prev)
                    m_part = jnp.max(st, axis=0, keepdims=True)
                    part = _dot(values(hh, rows), jnp.exp2(st - m_prev[:, q0:]).astype(BF16))
                    if q0:
                        m_part = jnp.concatenate([jnp.full((1, q0), NEG, F32), m_part], axis=1)
                        part = jnp.concatenate([jnp.zeros((part.shape[0], q0), F32), part], axis=1)
                    m_blk = m_part if m_blk is None else jnp.maximum(m_blk, m_part)
                    pv = part if pv is None else pv + part
                m_new = jnp.maximum(m_prev, m_blk)
                alpha = jnp.exp2(m_prev - m_new)
                l_new = alpha * (l_prev + pv[MLA_V:MLA_V + 1])
                acc_new = alpha * (acc_prev + pv[:MLA_V])
                ok = jnp.max(m_blk - m_prev) <= SPECULATION_HEADROOM
                within = ok if within is None else (within & ok)
            else:
                rows = slice(0, tk)
                st = scores(hh, c, rows)
                m_new = jnp.maximum(m_prev, jnp.max(st, axis=0, keepdims=True))
                alpha = jnp.exp2(m_prev - m_new)
                pv = _dot(values(hh, rows), jnp.exp2(st - m_new).astype(BF16))
                l_new = alpha * l_prev + pv[MLA_V:MLA_V + 1]
                acc_new = alpha * acc_prev + pv[:MLA_V]
            new_state[hh, c] = (m_new, l_new, acc_new)

        def commit():
            for hh, c in chains:
                cs = slice(c * sub, (c + 1) * sub)
                m_sc[hh, :, cs], l_sc[hh, :, cs], acc_sc[hh, :, cs] = new_state[hh, c]

        if speculative:
            pl.when(within)(commit)
            redo_sc[0] = jnp.logical_not(within).astype(I32)
        else:
            commit()

    redo_sc[0] = 0
    on_diagonal = j >= ratio * i

    @pl.when(jnp.logical_not(on_diagonal))
    def _():
        step(False, True)

    @pl.when(on_diagonal)
    def _():
        step(True, True)

    @pl.when(redo_sc[0] != 0)
    def _():
        step(True, False)

    @pl.when(j == ratio * i + (ratio - 1))
    def _():
        ot = jnp.concatenate([acc_sc[hh] / l_sc[hh] for hh in range(ATTN_HEADS)], axis=0)
        o_ref[...] = ot.T.astype(BF16)


def _mla_attention(q, k, vt):
    b, s, _ = q.shape
    tq, tk, nh = ATTN_TQ, ATTN_TK, ATTN_HEADS
    ratio = tq // tk
    nq = s // tq
    pairs = [(i, j) for i in range(nq) for j in range(ratio * (i + 1))]
    qi = jnp.asarray([p[0] for p in pairs], I32)
    kj = jnp.asarray([p[1] for p in pairs], I32)
    grid_spec = pltpu.PrefetchScalarGridSpec(
        num_scalar_prefetch=2,
        grid=(b, MLA_HEADS // nh, len(pairs)),
        in_specs=[
            pl.BlockSpec((None, tq, nh * HEAD_PAD), lambda bb, hp, p, qi, kj: (bb, qi[p], hp)),
            pl.BlockSpec((None, tk, nh * HEAD_PAD), lambda bb, hp, p, qi, kj: (bb, kj[p], hp)),
            pl.BlockSpec((None, nh * MLA_V, tk), lambda bb, hp, p, qi, kj: (bb, hp, kj[p])),
        ],
        out_specs=pl.BlockSpec((None, tq, nh * MLA_V), lambda bb, hp, p, qi, kj: (bb, qi[p], hp)),
        scratch_shapes=[pltpu.VMEM((nh, 1, tq), F32), pltpu.VMEM((nh, 1, tq), F32),
                        pltpu.VMEM((nh, MLA_V, tq), F32), pltpu.SMEM((1,), I32)],
    )
    return pl.pallas_call(
        _mla_kernel,
        out_shape=jax.ShapeDtypeStruct((b, s, MLA_HEADS * MLA_V), BF16),
        grid_spec=grid_spec,
        compiler_params=_cparams(("parallel", "parallel", "arbitrary")),
        name="mla_attention",
    )(qi, kj, q, k, vt)


def _alibi_slopes(n):
    def pow2(m):
        start = 2.0 ** (-8.0 / m)
        return [start ** (i + 1) for i in range(m)]
    if math.log2(n).is_integer():
        s = pow2(n)
    else:
        c = 2 ** int(math.floor(math.log2(n)))
        s = pow2(c) + pow2(2 * c)[0::2][: n - c]
    return np.array(sorted(s, reverse=True), dtype=np.float32)


def _dilated_block(cur, prev, bias4, first, *, head_of_lane, low):
    n = DIL_STEPS
    hw = DIL_OUT
    q = cur[:, :hw]
    kk = jnp.concatenate([prev[:, hw:2 * hw], cur[:, hw:2 * hw]], axis=0)
    vv = jnp.concatenate([prev[:, 2 * hw:], cur[:, 2 * hw:]], axis=0)
    zero = jnp.zeros_like(q)
    q4 = jnp.concatenate([jnp.where(head_of_lane == h, q, zero) for h in range(DIL_HEADS)], axis=0)
    s4 = _dot_nt(q4, kk) + bias4
    if first is not None:
        ki = lax.broadcasted_iota(I32, (DIL_HEADS * n, 2 * n), 1)
        s4 = jnp.where(first & (ki < n), NEG, s4)
    m4 = jnp.max(s4, axis=1, keepdims=True)
    p4 = jnp.exp(s4 - m4).astype(BF16)
    l4 = jnp.broadcast_to(jnp.sum(p4.astype(F32), axis=1, keepdims=True), (DIL_HEADS * n, HEAD_PAD))
    pv4 = _dot(p4, vv)
    m4 = jnp.broadcast_to(m4, (DIL_HEADS * n, HEAD_PAD))

    def rows(a, h):
        return a[h * n:(h + 1) * n]

    o_un = rows(pv4, DIL_HEADS - 1)
    for h in range(DIL_HEADS - 2, -1, -1):
        o_un = jnp.where(head_of_lane == h, rows(pv4, h), o_un)
    def per_lane(a):
        return jnp.concatenate([jnp.where(low, rows(a, 0), rows(a, 1)),
                                jnp.where(low, rows(a, 2), rows(a, 3))], axis=1)

    l_sel = per_lane(l4)
    return o_un / l_sel, per_lane(m4) + jnp.log(l_sel)


def _dilated_kernel(c0_ref, h0_ref, c1_ref, h1_ref, c2_ref, h2_ref, ob_ref, o_sc, l_sc, bias_sc,
                    *, slopes):
    u = pl.program_id(1)
    n = DIL_STEPS
    unit = ob_ref.shape[0]
    n_sb = unit // n
    first = u == 0
    qi = lax.broadcasted_iota(I32, (n, 2 * n), 0)
    ki = lax.broadcasted_iota(I32, (n, 2 * n), 1)
    dist = qi + n - ki
    valid = (dist >= 0) & (dist <= n)
    distf = dist.astype(F32)
    for gi in range(len(DIL_GROUPS)):
        for h in range(DIL_HEADS):
            bias_sc[gi, h * n:(h + 1) * n, :] = jnp.where(valid, -slopes[gi][h] * distf, NEG)

    block = functools.partial(
        _dilated_block,
        head_of_lane=lax.broadcasted_iota(I32, (n, DIL_OUT), 1) // DIL_HEAD_DIM,
        low=lax.broadcasted_iota(I32, (n, HEAD_PAD), 1) < DIL_HEAD_DIM)

    def rows_of(ref, sb):
        return ref[pl.ds(pl.multiple_of(sb * n, n), n), :]

    def emit(gi, start, stride, o, lse):
        if isinstance(start, int):
            idx = pl.ds(start, n)
        elif stride == 1:
            idx = pl.ds(pl.multiple_of(start, n), n)
        else:
            idx = pl.ds(start, n, stride=stride)
        for half in range(DIL_OUT // HEAD_PAD):
            ls = slice(half * HEAD_PAD, (half + 1) * HEAD_PAD)
            o_sc[gi, half, idx, :] = o[:, ls]
            l_sc[gi, half, idx, :] = lse[:, ls]

    emit(0, 0, 1, *block(c0_ref[:n, :], h0_ref[...], bias_sc[0], first))

    def g0_body(sb, carry):
        emit(0, sb * n, 1, *block(rows_of(c0_ref, sb), rows_of(c0_ref, sb - 1), bias_sc[0], None))
        return carry

    lax.fori_loop(1, n_sb, g0_body, 0, unroll=DIL_UNROLL)

    d1 = DIL_GROUPS[1][1]

    def g1_head(r, carry):
        emit(1, r, d1, *block(rows_of(c1_ref, r), rows_of(h1_ref, r), bias_sc[1], first))
        return carry

    def g1_body(sb, carry):
        start = (sb // d1) * (d1 * n) + sb % d1
        emit(1, start, d1, *block(rows_of(c1_ref, sb), rows_of(c1_ref, sb - d1), bias_sc[1], None))
        return carry

    lax.fori_loop(0, d1, g1_head, 0, unroll=DIL_UNROLL)
    lax.fori_loop(d1, n_sb, g1_body, 0, unroll=DIL_UNROLL)

    d2 = DIL_GROUPS[2][1]

    def g2_body(r, carry):
        emit(2, r, d2, *block(rows_of(c2_ref, r), rows_of(h2_ref, r), bias_sc[2], first))
        return carry

    lax.fori_loop(0, n_sb, g2_body, 0, unroll=DIL_UNROLL)

    def merge_body(c, carry):
        idx = pl.ds(pl.multiple_of(c * n, n), n)
        for half in range(DIL_OUT // HEAD_PAD):
            l0, l1, l2 = l_sc[0, half, idx, :], l_sc[1, half, idx, :], l_sc[2, half, idx, :]
            lmax = jnp.maximum(jnp.maximum(l0, l1), l2)
            e0, e1, e2 = jnp.exp(l0 - lmax), jnp.exp(l1 - lmax), jnp.exp(l2 - lmax)
            ob = (e0 * o_sc[0, half, idx, :] + e1 * o_sc[1, half, idx, :]
                  + e2 * o_sc[2, half, idx, :]) / (e0 + e1 + e2)
            ob_ref[idx, half * HEAD_PAD:(half + 1) * HEAD_PAD] = ob.astype(BF16)
        return carry

    lax.fori_loop(0, n_sb, merge_body, 0)


def _dilated_attention(zd0, zd1, zd2, seq):
    t = zd0.shape[0]
    unit = DIL_GROUPS[-1][0]
    upb = seq // unit
    n = DIL_STEPS
    u1 = DIL_GROUPS[1][0]
    all_slopes = _alibi_slopes(len(DIL_GROUPS) * DIL_HEADS).reshape(len(DIL_GROUPS), DIL_HEADS)
    slopes = tuple(tuple(float(x) * dil for x in all_slopes[gi]) for gi, (_, dil) in enumerate(DIL_GROUPS))

    def cur(bb, u):
        return (bb * upb + u, 0)

    def halo(rows):
        per_unit = unit // rows
        return lambda bb, u: ((bb * upb) * per_unit + jnp.maximum(u * per_unit - 1, 0), 0)

    return pl.pallas_call(
        functools.partial(_dilated_kernel, slopes=slopes),
        out_shape=jax.ShapeDtypeStruct((t, DIL_OUT), BF16),
        grid=(t // seq, upb),
        in_specs=[pl.BlockSpec((unit, DIL_COLS), cur), pl.BlockSpec((n, DIL_COLS), halo(n)),
                  pl.BlockSpec((unit, DIL_COLS), cur), pl.BlockSpec((u1, DIL_COLS), halo(u1)),
                  pl.BlockSpec((unit, DIL_COLS), cur), pl.BlockSpec((unit, DIL_COLS), halo(unit))],
        out_specs=pl.BlockSpec((unit, DIL_OUT), cur),
        scratch_shapes=[pltpu.VMEM((len(DIL_GROUPS), DIL_OUT // HEAD_PAD, unit, HEAD_PAD), F32),
                        pltpu.VMEM((len(DIL_GROUPS), DIL_OUT // HEAD_PAD, unit, HEAD_PAD), F32),
                        pltpu.VMEM((len(DIL_GROUPS), DIL_HEADS * n, 2 * n), F32)],
        compiler_params=_cparams(("parallel", "arbitrary")),
        name="dilated_attention",
    )(zd0, zd0, zd1, zd1, zd2, zd2)


def _merge_kernel(x_ref, oa_ref, ob_ref, ga_ref, wgate_ref,
                  wa_ref, wb_ref, wo_ref, g_ref, wr_ref, br_ref,
                  x1_ref, hp_ref, topi_ref, wcol_ref):
    tm = x_ref.shape[0]
    x = x_ref[...]
    h = _rms(x, ga_ref[...]).astype(BF16)
    mixed = (jax.nn.sigmoid(_dot(h, wgate_ref[:, :D_MODEL])) * _dot(oa_ref[...], wa_ref[...])
             + jax.nn.sigmoid(_dot(h, wgate_ref[:, D_MODEL:])) * _dot(ob_ref[...], wb_ref[...]))
    x1 = x + _dot(mixed.astype(BF16), wo_ref[...])
    x1_ref[...] = x1
    h2 = _rms(x1, g_ref[...])
    hp_ref[...] = _pack_halves(h2[:, :HALF], h2[:, HALF:])

    logits = _dot_nt(wr_ref[...], h2.astype(BF16)) + br_ref[...]
    eidx = lax.broadcasted_iota(I32, (N_EXPERTS, tm), 0)
    vals, idxs = [], []
    for _ in range(TOP_K):
        m = jnp.max(logits, axis=0, keepdims=True)
        idx = jnp.min(jnp.where(logits == m, eidx, N_EXPERTS), axis=0, keepdims=True)
        vals.append(m)
        idxs.append(idx)
        logits = jnp.where(eidx == idx, -jnp.inf, logits)
    exps = [jnp.exp(vk - vals[0]) for vk in vals]
    den = exps[0] + exps[1] + exps[2] + exps[3]
    row8 = lax.broadcasted_iota(I32, (8, tm), 0)
    row128 = lax.broadcasted_iota(I32, (HEAD_PAD, tm), 0)
    topi = jnp.zeros((8, tm), I32)
    wide = jnp.zeros((HEAD_PAD, tm), F32)
    for kk in range(TOP_K):
        topi = jnp.where(row8 == kk, idxs[kk], topi)
        wide = jnp.where(row128 == kk, exps[kk] / den, wide)
    topi_ref[...] = topi
    wcol_ref[...] = wide.T


def _merge(x2d, oa, ob, g_attn, wgate, wa, wb, wo, g, wr_t, br_col):
    t = x2d.shape[0]
    tm = TOKEN_TILE

    def row(i):
        return (i, 0)

    def col(i):
        return (0, i)

    def full(a):
        return pl.BlockSpec(a.shape, lambda i: (0, 0))

    def rows(width):
        return pl.BlockSpec((tm, width), row)

    out_shape = [
        jax.ShapeDtypeStruct((t, D_MODEL), F32),
        jax.ShapeDtypeStruct((t, HALF), I32),
        jax.ShapeDtypeStruct((8, t), I32),
        jax.ShapeDtypeStruct((t, HEAD_PAD), F32),
    ]
    return pl.pallas_call(
        _merge_kernel,
        out_shape=out_shape,
        grid=(t // tm,),
        in_specs=[rows(D_MODEL), rows(MLA_HEADS * MLA_V), rows(DIL_OUT), full(g_attn), full(wgate)]
        + [full(wa), full(wb), full(wo), full(g), full(wr_t), full(br_col)],
        out_specs=[rows(D_MODEL), rows(HALF), pl.BlockSpec((8, tm), col), rows(HEAD_PAD)],
        compiler_params=_cparams(("parallel",)),
        name="merge_router",
    )(x2d, oa, ob, g_attn, wgate, wa, wb, wo, g, wr_t, br_col)


def _positions_kernel(topi_ref, dest_ref, meta_ref, cnt_sc, carry_sc, start_sc):
    ps = pl.program_id(0)
    i = pl.program_id(1)
    tm = POSITION_SUBTILE
    n_sub = topi_ref.shape[1] // tm
    eidx = lax.broadcasted_iota(I32, (N_EXPERTS, tm), 0)

    def hits_of(sb):
        topi = topi_ref[:, sb * tm:(sb + 1) * tm]
        return [eidx == topi[kk:kk + 1, :] for kk in range(TOP_K)]

    def members(hits):
        return hits[0] | hits[1] | hits[2] | hits[3]

    @pl.when((ps == 0) & (i == 0))
    def _():
        cnt_sc[...] = jnp.zeros(cnt_sc.shape, F32)

    @pl.when(ps == 0)
    def _():
        total = cnt_sc[...]
        for sb in range(n_sub):
            total = total + jnp.sum(members(hits_of(sb)).astype(F32), axis=1, keepdims=True)
        cnt_sc[...] = total

    @pl.when((ps == 1) & (i == 0))
    def _():
        cnt = cnt_sc[...].astype(I32)
        shift = ROW_BLOCK.bit_length() - 1
        padded = lax.shift_left(lax.shift_right_logical(cnt + (ROW_BLOCK - 1), shift), shift)
        sub = lax.broadcasted_iota(I32, (N_EXPERTS, HEAD_PAD), 0)
        lane = lax.broadcasted_iota(I32, (N_EXPERTS, HEAD_PAD), 1)
        padded_row = jnp.sum(jnp.where(sub == lane, padded, 0), axis=0, keepdims=True)
        start = jnp.sum(jnp.where(lane < sub, padded_row, 0), axis=1, keepdims=True)
        start_sc[...] = start.astype(F32)
        carry_sc[...] = jnp.zeros(carry_sc.shape, F32)
        cnt_row = jnp.sum(jnp.where(sub == lane, cnt, 0), axis=0, keepdims=True)
        start_row = jnp.sum(jnp.where(sub == lane, start, 0), axis=0, keepdims=True)
        row8 = lax.broadcasted_iota(I32, (8, HEAD_PAD), 0)
        meta = jnp.where(row8 == 0, cnt_row, 0)
        meta = jnp.where(row8 == 1, start_row, meta)
        meta = jnp.where(row8 == 2, start_row + padded_row, meta)
        meta_ref[...] = meta

    @pl.when(ps == 1)
    def _():
        tr = lax.broadcasted_iota(I32, (tm, tm), 0)
        tc = lax.broadcasted_iota(I32, (tm, tm), 1)
        before = (tr < tc).astype(BF16)
        row8 = lax.broadcasted_iota(I32, (8, tm), 0)
        offset = carry_sc[...] + start_sc[...]
        for sb in range(n_sub):
            hits = hits_of(sb)
            member = members(hits)
            base = _dot(member.astype(BF16), before) + offset
            dest = jnp.zeros((8, tm), I32)
            for kk in range(TOP_K):
                dk = jnp.sum(jnp.where(hits[kk], base, 0.0), axis=0, keepdims=True).astype(I32)
                dest = jnp.where(row8 == kk, dk, dest)
            dest_ref[:, sb * tm:(sb + 1) * tm] = dest
            offset = offset + jnp.sum(member.astype(F32), axis=1, keepdims=True)
        carry_sc[...] = offset - start_sc[...]


def _positions(topi_t):
    t = topi_t.shape[1]
    tm = min(POSITION_TILE, t)
    return pl.pallas_call(
        _positions_kernel,
        out_shape=[jax.ShapeDtypeStruct((8, t), I32), jax.ShapeDtypeStruct((8, HEAD_PAD), I32)],
        grid=(2, t // tm),
        in_specs=[pl.BlockSpec((8, tm), lambda ps, i: (0, i))],
        out_specs=[pl.BlockSpec((8, tm), lambda ps, i: (0, i * ps)),
                   pl.BlockSpec((8, HEAD_PAD), lambda ps, i: (0, 0))],
        scratch_shapes=[pltpu.VMEM((N_EXPERTS, 1), F32)] * 3,
        compiler_params=_cparams(("arbitrary", "arbitrary")),
        name="routing_positions",
    )(topi_t)


def _sc_mesh():
    return plsc.VectorSubcoreMesh(core_axis_name="c", subcore_axis_name="s")


def _dispatch_rows(table, dest_flat, n_rows):
    t, c = table.shape
    n_slots = dest_flat.shape[0] // t
    per_w = t // SC_WORKERS
    assert per_w * SC_WORKERS == t and per_w % (2 * SC_WINDOW) == 0
    n_chunks = per_w // SC_WINDOW
    w = SC_WINDOW

    @functools.partial(
        pl.kernel, mesh=_sc_mesh(),
        out_type=jax.ShapeDtypeStruct((n_rows, c), table.dtype),
        scratch_types=[pltpu.VMEM((n_chunks, w), I32)] * n_slots + [pltpu.VMEM((w, c), table.dtype)] * 2
        + [pltpu.SemaphoreType.DMA] * (n_slots + 2),
        name="dispatch_rows",
    )
    def k(table_hbm, dest_hbm, out_hbm, *scratch):
        idx = scratch[:n_slots]
        rows = scratch[n_slots:n_slots + 2]
        scatter_sems = scratch[n_slots + 2:2 * n_slots + 2]
        read_sems = scratch[2 * n_slots + 2:]
        wid = lax.axis_index("s") * SC_CORES + lax.axis_index("c")
        base = wid * per_w

        def off(chunk):
            return pl.multiple_of(base + chunk * w, w)

        def read(chunk, buf):
            return pltpu.make_async_copy(table_hbm.at[pl.ds(off(chunk), w)], rows[buf], read_sems[buf])

        def scatter(kk, chunk, buf):
            return pltpu.make_async_copy(rows[buf], out_hbm.at[idx[kk].at[chunk]], scatter_sems[kk])

        read(0, 0).start()
        for kk in range(n_slots):
            pltpu.sync_copy(dest_hbm.at[pl.ds(kk * (t // w) + wid * n_chunks, n_chunks)], idx[kk])

        @pl.loop(0, n_chunks // 2)
        def _(p):
            for buf in range(2):
                chunk = 2 * p + buf

                @pl.when(chunk + 1 < n_chunks)
                def _():
                    read(chunk + 1, 1 - buf).start()

                read(chunk, buf).wait()
                for kk in range(n_slots):
                    scatter(kk, chunk, buf).start()
                for kk in range(n_slots):
                    scatter(kk, chunk, buf).wait()

    return k(table, dest_flat.reshape(n_slots * t // w, w))


def _gather_rows(table, idx):
    n = idx.shape[0]
    c = table.shape[1]
    per_w = n // SC_WORKERS
    assert per_w * SC_WORKERS == n and per_w % (2 * SC_WINDOW) == 0
    n_chunks = per_w // SC_WINDOW
    w = SC_WINDOW

    @functools.partial(
        pl.kernel, mesh=_sc_mesh(),
        out_type=jax.ShapeDtypeStruct((n, c), table.dtype),
        scratch_types=[pltpu.VMEM((n_chunks, w), I32)] + [pltpu.VMEM((w, c), table.dtype)] * 2
        + [pltpu.SemaphoreType.DMA] * 4,
        name="gather_rows",
    )
    def k(table_hbm, idx_hbm, out_hbm, idx_all, rows_a, rows_b, g_a, g_b, w_a, w_b):
        rows, gather_sems, write_sems = (rows_a, rows_b), (g_a, g_b), (w_a, w_b)
        wid = lax.axis_index("s") * SC_CORES + lax.axis_index("c")
        base = wid * per_w

        def off(chunk):
            return pl.multiple_of(base + chunk * w, w)

        def gather(chunk, buf):
            return pltpu.make_async_copy(table_hbm.at[idx_all.at[chunk]], rows[buf], gather_sems[buf])

        def write(chunk, buf):
            return pltpu.make_async_copy(rows[buf], out_hbm.at[pl.ds(off(chunk), w)], write_sems[buf])

        def start_gather(chunk, buf):
            gather(chunk, buf).start()

        pltpu.sync_copy(idx_hbm.at[pl.ds(wid * n_chunks, n_chunks)], idx_all)
        start_gather(0, 0)

        @pl.loop(0, n_chunks // 2)
        def _(p):
            for buf in range(2):
                chunk = 2 * p + buf

                @pl.when(chunk + 1 < n_chunks)
                def _():
                    @pl.when(chunk >= 1)
                    def _():
                        write(chunk - 1, 1 - buf).wait()
                    start_gather(chunk + 1, 1 - buf)

                gather(chunk, buf).wait()
                write(chunk, buf).start()

        write(n_chunks - 2, 0).wait()
        write(n_chunks - 1, 1).wait()

    return k(table, idx.reshape(n // w, w))


def _expert_kernel(be_ref, nused_ref, first_ref, slot_ref, next_ref,
                   xs_ref, wg_hbm, bg_ref, wu_hbm, bu_ref, wd_hbm, bd_ref,
                   ys_ref, wg_ref, wu_ref, wd_ref, stage_g, stage_u, stage_d, sems, *, layer):
    b = pl.program_id(0)
    used = b < nused_ref[0]
    weights = ((wg_hbm, stage_g), (wu_hbm, stage_u), (wd_hbm, stage_d))

    def fetch(expert, slot):
        return [pltpu.make_async_copy(hbm.at[layer, expert], stage.at[slot], sems.at[slot, n])
                for n, (hbm, stage) in enumerate(weights)]

    @pl.when(used & (first_ref[b] == 1))
    def _():
        slot = slot_ref[b]

        @pl.when(b == 0)
        def _():
            for cp in fetch(be_ref[0], slot):
                cp.start()

        for cp in fetch(be_ref[b], slot):
            cp.wait()
        wg_ref[...] = stage_g[slot].astype(BF16)
        wu_ref[...] = stage_u[slot].astype(BF16)
        wd_ref[...] = stage_d[slot].astype(BF16)

        @pl.when(next_ref[b] >= 0)
        def _():
            for cp in fetch(next_ref[b], 1 - slot):
                cp.start()

    @pl.when(used)
    def _():
        lo, hi = _unpack_halves(xs_ref[...])
        xb = jnp.concatenate([lo.astype(BF16), hi.astype(BF16)], axis=1)
        a = _dot(xb, wg_ref[...]) + bg_ref[...]
        u = _dot(xb, wu_ref[...]) + bu_ref[...]
        a = jnp.minimum(a, SWIGLU_LIMIT)
        u = jnp.clip(u, -SWIGLU_LIMIT, SWIGLU_LIMIT)
        y = (a * jax.nn.sigmoid(SWIGLU_ALPHA * a)) * (u + 1.0)
        out = _dot(y.astype(BF16), wd_ref[...]) + bd_ref[...]
        ys_ref[...] = _pack_halves(out[:, :HALF], out[:, HALF:])

    @pl.when(b >= nused_ref[0])
    def _():
        ys_ref[...] = jnp.zeros(ys_ref.shape, I32)


def _expert_ffn(xs, block_e, n_used, counts, layer, wg, bg, wu, bu, wd, bd):
    n_rows = xs.shape[0]
    n_blocks = n_rows // ROW_BLOCK

    blk = jnp.arange(n_blocks, dtype=I32)
    used = blk < n_used[0]
    first = used & ((blk == 0) | (block_e != jnp.roll(block_e, 1)))
    slot = (jnp.cumsum(first.astype(I32)) - 1) % 2
    eid = jnp.arange(N_EXPERTS, dtype=I32)
    later = (eid[None, :] > eid[:, None]) & (counts[None, :] > 0)
    next_of_expert = jnp.min(jnp.where(later, eid[None, :], N_EXPERTS), axis=1)
    next_of_expert = jnp.where(next_of_expert == N_EXPERTS, -1, next_of_expert)
    next_e = jnp.sum(jnp.where(block_e[:, None] == eid[None, :], next_of_expert[None, :], 0), axis=1)

    def rows(b, *_):
        return (b, 0)

    def expert(b, be, *_):
        return (layer, be[b], 0, 0)

    hbm = pl.BlockSpec(memory_space=pl.ANY)
    grid_spec = pltpu.PrefetchScalarGridSpec(
        num_scalar_prefetch=5,
        grid=(n_blocks,),
        in_specs=[pl.BlockSpec((ROW_BLOCK, HALF), rows),
                  hbm, pl.BlockSpec((None, None, 1, D_EXPERT), expert),
                  hbm, pl.BlockSpec((None, None, 1, D_EXPERT), expert),
                  hbm, pl.BlockSpec((None, None, 1, D_MODEL), expert)],
        out_specs=pl.BlockSpec((ROW_BLOCK, HALF), rows),
        scratch_shapes=[pltpu.VMEM((D_MODEL, D_EXPERT), BF16), pltpu.VMEM((D_MODEL, D_EXPERT), BF16),
                        pltpu.VMEM((D_EXPERT, D_MODEL), BF16),
                        pltpu.VMEM((2, D_MODEL, D_EXPERT), F32), pltpu.VMEM((2, D_MODEL, D_EXPERT), F32),
                        pltpu.VMEM((2, D_EXPERT, D_MODEL), F32),
                        pltpu.SemaphoreType.DMA((2, 3))],
    )
    return pl.pallas_call(
        functools.partial(_expert_kernel, layer=layer),
        out_shape=jax.ShapeDtypeStruct((n_rows, HALF), I32),
        grid_spec=grid_spec,
        compiler_params=_cparams(("arbitrary",)),
        name="expert_ffn",
    )(block_e, n_used, first.astype(I32), slot.astype(I32), next_e.astype(I32),
      xs, wg, bg, wu, bu, wd, bd)


def _ple_project_kernel(p_ref, wpp_ref, o_ref):
    o_ref[...] = _dot(p_ref[...].astype(BF16), wpp_ref[...]).astype(BF16)


def _ple_project(p3d, layer, wpp):
    t = p3d.shape[1]
    tm = 2 * TOKEN_TILE
    return pl.pallas_call(
        _ple_project_kernel,
        out_shape=jax.ShapeDtypeStruct((t, D_MODEL), BF16),
        grid=(t // tm,),
        in_specs=[pl.BlockSpec((None, tm, PLE_DIM), lambda i: (layer, i, 0)),
                  pl.BlockSpec(wpp.shape, lambda i: (0, 0))],
        out_specs=pl.BlockSpec((tm, D_MODEL), lambda i: (i, 0)),
        compiler_params=_cparams(("parallel",)),
        name="ple_project",
    )(p3d, wpp)


def _combine_kernel(x1_ref, yg_ref, wcol_ref, proj_ref, gple_ref, wpg_ref, gout_ref, o_ref,
                    *, final):
    x1 = x1_ref[...]
    acc_lo = x1[:, :HALF]
    acc_hi = x1[:, HALF:]
    wcol = wcol_ref[...]
    for kk in range(TOP_K):
        lo, hi = _unpack_halves(yg_ref[kk])
        wk = wcol[:, kk:kk + 1]
        acc_lo = acc_lo + wk * lo
        acc_hi = acc_hi + wk * hi
    x2 = jnp.concatenate([acc_lo, acc_hi], axis=1)
    gate = jax.nn.sigmoid(_dot(_rms(x2, gple_ref[...]).astype(BF16), wpg_ref[...]))
    x3 = x2 + gate * proj_ref[...].astype(F32)
    o_ref[...] = _rms(x3, gout_ref[...]) if final else x3


def _combine_kernel_inplace(x1_ref, yg_ref, wcol_ref, proj_ref, gple_ref, wpg_ref, gout_ref,
                            prev_ref, o_ref, *, final):
    del prev_ref
    _combine_kernel(x1_ref, yg_ref, wcol_ref, proj_ref, gple_ref, wpg_ref, gout_ref, o_ref,
                    final=final)


def _combine(x1, yg, wcol, proj, gple, wpg, gout, final, part, prev):
    t = x1.shape[0]
    tm = TOKEN_TILE
    n_tiles = yg.shape[1] // tm
    first_tile = part * n_tiles

    def row(i, *_):
        return (first_tile + i, 0)

    def full(a):
        return pl.BlockSpec(a.shape, lambda i: (0, 0))

    in_specs = [pl.BlockSpec((tm, D_MODEL), row),
                pl.BlockSpec((TOP_K, tm, HALF), lambda i: (0, i, 0)),
                pl.BlockSpec((tm, HEAD_PAD), row),
                pl.BlockSpec((tm, D_MODEL), row),
                full(gple), full(wpg), full(gout)]
    args = [x1, yg, wcol, proj, gple, wpg, gout]
    kern = functools.partial(_combine_kernel, final=final)
    aliases = {}
    if prev is not None:
        in_specs.append(pl.BlockSpec(memory_space=pl.ANY))
        args.append(prev)
        aliases = {len(args) - 1: 0}
        kern = functools.partial(_combine_kernel_inplace, final=final)
    return pl.pallas_call(
        kern,
        out_shape=jax.ShapeDtypeStruct((t, D_MODEL), F32),
        grid=(n_tiles,),
        in_specs=in_specs,
        out_specs=pl.BlockSpec((tm, D_MODEL), row),
        input_output_aliases=aliases,
        compiler_params=_cparams(("parallel",)),
        name="combine_ple",
    )(*args)


def _rope_tables(seq):
    inv_freq = ROPE_THETA ** (-jnp.arange(HALF_ROPE, dtype=F32) * 2.0 / MLA_ROPE)
    ang = jnp.arange(seq, dtype=F32)[:, None] * inv_freq[None, :]
    cos, sin = jnp.cos(ang), jnp.sin(ang)
    ones = jnp.ones((seq, MLA_NOPE), F32)
    zeros16 = jnp.zeros((seq, HALF_ROPE), F32)
    zeros64 = jnp.zeros((seq, MLA_NOPE), F32)
    tail = jnp.ones((seq, HEAD_PAD - MLA_NOPE - MLA_ROPE), F32)
    ztail = jnp.zeros_like(tail)
    cos_t = jnp.concatenate([ones, cos, cos, tail], axis=1)
    sina_t = jnp.concatenate([zeros64, zeros16, sin, ztail], axis=1)
    sinb_t = jnp.concatenate([zeros64, -sin, zeros16, ztail], axis=1)
    return cos_t, sina_t, sinb_t


def _prep_mixer_weights(w_in, w_uq, w_ukv):
    c0 = MLA_Q_LORA + MLA_KV_LORA
    c1 = c0 + MLA_ROPE
    c2 = c1 + len(DIL_GROUPS) * DIL_COLS
    kr_pad = jnp.pad(w_in[:, c0:c1], ((0, 0), (MLA_NOPE, HEAD_PAD - MLA_NOPE - MLA_ROPE)))
    wmla = jnp.concatenate([w_in[:, :c0], kr_pad], axis=1).astype(BF16)
    col = np.arange(len(DIL_GROUPS) * DIL_COLS)
    q_scale = np.where(col % DIL_COLS < DIL_OUT, DIL_HEAD_DIM ** -0.5, 1.0).astype(np.float32)
    wdil = (w_in[:, c1:c2] * q_scale[None, :]).astype(BF16)
    wgate = w_in[:, c2:].astype(BF16)
    pad = HEAD_PAD - MLA_NOPE - MLA_ROPE
    wuq_h = w_uq.reshape(MLA_Q_LORA, MLA_HEADS, MLA_NOPE + MLA_ROPE)
    wuq = jnp.pad(wuq_h, ((0, 0), (0, 0), (0, pad))).reshape(MLA_Q_LORA, MLA_HEADS * HEAD_PAD).astype(BF16)
    wukv_h = w_ukv.reshape(MLA_KV_LORA, MLA_HEADS, MLA_NOPE + MLA_V)
    wuk = jnp.pad(wukv_h[:, :, :MLA_NOPE], ((0, 0), (0, 0), (0, HEAD_PAD - MLA_NOPE)))
    wuk = wuk.reshape(MLA_KV_LORA, MLA_HEADS * HEAD_PAD).astype(BF16)
    wuv = wukv_h[:, :, MLA_NOPE:].reshape(MLA_KV_LORA, MLA_HEADS * MLA_V).astype(BF16)
    return wmla, wdil, wgate, wuq, wuk, wuv


def kernel(x, p, attn_norm, w_in, q_norm, w_uq, kv_norm, w_ukv, w_branch_a, w_branch_b, w_out, ffn_norm, w_router, b_router, w_gate, b_gate, w_up, b_up, w_down, b_down, ple_norm, w_ple_gate, w_ple_proj, final_norm):
    b, s, d = x.shape
    depth = w_in.shape[0]
    t = b * s
    assert d == D_MODEL and s % (DIL_GROUPS[-1][0]) == 0 and t % (SC_WORKERS * SC_WINDOW) == 0
    n_assign = t * TOP_K
    n_blocks = -(-(n_assign + N_EXPERTS * (ROW_BLOCK - 1)) // ROW_BLOCK)
    n_rows = n_blocks * ROW_BLOCK
    cos_t, sina_t, sinb_t = _rope_tables(s)
    xc = x.reshape(t, d)
    for i in range(depth):
        wmla, wdil, wgate, wuq, wuk, wuv = _prep_mixer_weights(w_in[i], w_uq[i], w_ukv[i])
        q, k, vt, zd0, zd1, zd2 = _inproj(
            xc, s, attn_norm[i][None], wmla, wdil, q_norm[i][None], kv_norm[i][None],
            wuq, wuk, wuv, cos_t, sina_t, sinb_t)
        oa = _mla_attention(q.reshape(b, s, -1), k.reshape(b, s, -1), vt)
        ob = _dilated_attention(zd0, zd1, zd2, s)
        x1, hp, topi_t, wcol = _merge(
            xc, oa.reshape(t, -1), ob, attn_norm[i][None], wgate,
            w_branch_a[i].astype(BF16), w_branch_b[i].astype(BF16), w_out[i].astype(BF16),
            ffn_norm[i][None], w_router[i].T.astype(BF16), b_router[i][:, None])
        dest_t, meta = _positions(topi_t)
        ends = meta[2, :N_EXPERTS]
        block_start = jnp.arange(n_blocks, dtype=I32) * ROW_BLOCK
        block_e = jnp.minimum(
            jnp.sum((ends[None, :] <= block_start[:, None]).astype(I32), axis=1), N_EXPERTS - 1)
        n_used = (ends[N_EXPERTS - 1:] // ROW_BLOCK).astype(I32)
        dest_flat = dest_t[:TOP_K].reshape(n_assign)
        xs = _dispatch_rows(hp, dest_flat, n_rows)
        proj = _ple_project(p.reshape(depth, t, PLE_DIM), i, w_ple_proj[i].astype(BF16))
        ys = _expert_ffn(xs, block_e, n_used, meta[0, :N_EXPERTS], i,
                         w_gate, b_gate[:, :, None, :], w_up, b_up[:, :, None, :],
                         w_down, b_down[:, :, None, :])
        final = i == depth - 1
        gout = final_norm[None] if final else attn_norm[i][None]
        wpg = w_ple_gate[i].astype(BF16)
        tp = t // COMBINE_PARTS
        xc = None
        for part in range(COMBINE_PARTS):
            dest_part = dest_t[:TOP_K, part * tp:(part + 1) * tp].reshape(TOP_K * tp)
            yg = _gather_rows(ys, dest_part).reshape(TOP_K, tp, HALF)
            xc = _combine(x1, yg, wcol, proj, ple_norm[i][None], wpg, gout, final, part, xc)
    return xc.reshape(b, s, d)
```
